```python
import math
import jax
import jax.numpy as jnp
from jax import lax
import numpy as np

D_MODEL = 1024
BATCH = 4
SEQ = 4096
DEPTH = 1
DEC_BATCH = 32
DEC_SEQ = 1
PAST_LEN = 16384
PAGE_SIZE = 128

D_MIX = D_MODEL
HEAD_DIM = 64
D_NSA = D_MIX // 2
N_HEADS = D_NSA // HEAD_DIM
N_KV = N_HEADS // 4
HPG = N_HEADS // N_KV
CMP_BLOCK = 32
CMP_STRIDE = 16
CMP_HIDDEN = HEAD_DIM
SEL_BLOCK = 64
TOP_N = 16
WINDOW = 512
Q_BLOCK = 128
D_RET = D_MIX - D_NSA
N_RET_HEADS = 4
DV_RET = D_RET // N_RET_HEADS
DK_RET = DV_RET // 2
RET_CHUNK = 128
ROPE_BASE = 10000.0
EPS = 1e-6
NEG = -1e30
FORCE = 1e4
KV_W = N_KV * HEAD_DIM
COLS = (D_NSA, KV_W, KV_W, KV_W, KV_W, KV_W, KV_W, 3 * N_HEADS, D_NSA,
        N_RET_HEADS * DK_RET, N_RET_HEADS * DK_RET, D_RET, D_RET)
D_IN = sum(COLS)
SPLITS = tuple(int(v) for v in np.cumsum(COLS)[:-1])

kernel_name = 'nsa_retention_hybrid_step'


def rmsnorm(x, g):
    xf = x.astype(jnp.float32)
    y = xf * lax.rsqrt(jnp.mean(xf * xf, axis=-1, keepdims=True) + EPS)
    return (y * g.astype(jnp.float32)).astype(x.dtype)


def masked_softmax(s, mask):
    p = jax.nn.softmax(jnp.where(mask, s, NEG), axis=-1)
    return jnp.where(mask, p, 0.0)


def rotary(x, pos):
    xf = x.astype(jnp.float32)
    half = xf.shape[-1] // 2
    freqs = ROPE_BASE ** (-jnp.arange(half, dtype=jnp.float32) / half)
    ang = pos.astype(jnp.float32)[:, None] * freqs[None, :]
    cos = jnp.cos(ang)[None, :, None, :]
    sin = jnp.sin(ang)[None, :, None, :]
    x1, x2 = xf[..., :half], xf[..., half:]
    return jnp.concatenate([x1 * cos - x2 * sin, x1 * sin + x2 * cos], axis=-1)


def pad_rows(x, mult):
    extra = (-x.shape[1]) % mult
    return jnp.pad(x, ((0, 0), (0, extra)) + ((0, 0),) * (x.ndim - 2))


def gather_pages(cache, page_table):
    rows = cache[page_table]
    return rows.reshape(page_table.shape[0], -1, *cache.shape[2:])


def compress(x, pe, w1, w2):
    b, l = x.shape[:2]
    n_chunk = l // CMP_STRIDE
    span = CMP_BLOCK // CMP_STRIDE
    n_cmp = n_chunk - span + 1
    ch = x.reshape(b, n_chunk, CMP_STRIDE, N_KV, HEAD_DIM)
    blocks = jnp.concatenate([ch[:, o:o + n_cmp] for o in range(span)], axis=2)
    h = jax.nn.silu(jnp.einsum('bnlgd,ldf->bngf', blocks + pe[:, None, :], w1))
    return jnp.einsum('bngf,fd->bngd', h, w2)


def nsa_block(q, q_pos, kc, vc, ks_blk, vs_blk, kw, vw, kw_pos, gates):
    f32 = jnp.float32
    scale = HEAD_DIM ** -0.5
    q = q.astype(f32)
    b, nq = q.shape[:2]
    n_cmp, n_sel = kc.shape[1], ks_blk.shape[1]
    t = q_pos[:, None]
    c_end = jnp.arange(n_cmp, dtype=jnp.int32) * CMP_STRIDE + CMP_BLOCK - 1
    c_mask = (c_end[None, :] <= t)[None, :, None, None, :]
    s_c = jnp.einsum('bqghd,bngd->bqghn', q, kc.astype(f32)) * scale
    p_c = masked_softmax(s_c, c_mask)
    o_c = jnp.einsum('bqghn,bngd->bqghd', p_c, vc.astype(f32))
    ratio = SEL_BLOCK // CMP_STRIDE
    span = CMP_BLOCK // CMP_STRIDE
    imp = jnp.pad(p_c.sum(axis=3), ((0, 0), (0, 0), (0, 0), (span - 1, ratio * n_sel - n_cmp)))
    imp_t = imp[..., span - 1:span - 1 + ratio * n_sel]
    for n in range(1, span):
        imp_t = imp_t + imp[..., span - 1 - n:span - 1 - n + ratio * n_sel]
    imp_s = imp_t.reshape(b, nq, N_KV, n_sel, ratio).sum(axis=-1)
    blk = jnp.arange(n_sel, dtype=jnp.int32)[None, :]
    jt = (q_pos // SEL_BLOCK)[:, None]
    s_valid = blk * SEL_BLOCK <= t
    forced = (blk == 0) | (blk == jt) | (blk == jt - 1)
    score = jnp.where(s_valid[None, :, None, :], jnp.where(forced[None, :, None, :], FORCE, imp_s), NEG)
    k_eff = min(TOP_N, n_sel)
    _, idx = lax.top_k(score, k_eff)
    b_i = jnp.arange(b)[:, None, None, None]
    g_i = jnp.arange(N_KV)[None, None, :, None]
    ks_sel = ks_blk.transpose(0, 3, 1, 2, 4)[b_i, g_i, idx]
    vs_sel = vs_blk.transpose(0, 3, 1, 2, 4)[b_i, g_i, idx]
    kpos = idx[..., None] * SEL_BLOCK + jnp.arange(SEL_BLOCK, dtype=jnp.int32)
    s_mask = (kpos <= q_pos[None, :, None, None, None]).reshape(b, nq, N_KV, 1, k_eff * SEL_BLOCK)
    s_s = jnp.einsum('bqghd,bqgksd->bqghks', q, ks_sel.astype(f32)) * scale
    s_s = s_s.reshape(b, nq, N_KV, HPG, k_eff * SEL_BLOCK)
    p_s = masked_softmax(s_s, s_mask).reshape(b, nq, N_KV, HPG, k_eff, SEL_BLOCK)
    o_s = jnp.einsum('bqghks,bqgksd->bqghd', p_s, vs_sel.astype(f32))
    kp = kw_pos[None, :]
    w_mask = ((kp <= t) & (kp > t - WINDOW) & (kp >= 0))[None, :, None, None, :]
    s_w = jnp.einsum('bqghd,bwgd->bqghw', q, kw.astype(f32)) * scale
    p_w = masked_softmax(s_w, w_mask)
    o_w = jnp.einsum('bqghw,bwgd->bqghd', p_w, vw.astype(f32))
    g = gates.astype(f32)
    return g[..., 0:1] * o_c + g[..., 1:2] * o_s + g[..., 2:3] * o_w


def retention(q, k, v, s0):
    b, l = q.shape[:2]
    cl = math.gcd(l, RET_CHUNK)
    n = l // cl
    log_g = jnp.log(1.0 - 2.0 ** (-5.0 - jnp.arange(N_RET_HEADS, dtype=jnp.float32)))
    i = jnp.arange(cl, dtype=jnp.float32)
    diff = i[:, None] - i[None, :]
    d_in = jnp.where(diff >= 0, jnp.exp(log_g[:, None, None] * jnp.maximum(diff, 0.0)), 0.0)
    xi = jnp.exp(log_g[None, :] * (i[:, None] + 1.0))
    zeta = jnp.exp(log_g[None, :] * (cl - 1.0 - i[:, None]))
    g_c = jnp.exp(log_g * cl)

    def chunks(a):
        return a.reshape(b, n, cl, *a.shape[2:]).swapaxes(0, 1)

    def step(s, inp):
        qc, kc, vc = inp
        a = jnp.einsum('bihd,bjhd->bhij', qc, kc) * d_in
        o = jnp.einsum('bhij,bjhe->bihe', a, vc) + jnp.einsum('bihd,bhde->bihe', qc, s) * xi[None, :, :, None]
        s = s * g_c[None, :, None, None] + jnp.einsum('bjhd,bjhe->bhde', kc * zeta[None, :, :, None], vc)
        return s, o

    s, o = lax.scan(step, s0, (chunks(q), chunks(k), chunks(v)))
    return o.swapaxes(0, 1).reshape(b, l, N_RET_HEADS, DV_RET), s


def mix_inputs(x, c, g_norm, w_ada, b_ada, w_in, g_q, g_ks, g_kw):
    b, l = x.shape[:2]
    shift, scale, gate = jnp.split(jax.nn.silu(c) @ w_ada + b_ada, 3, axis=-1)
    h = rmsnorm(x, g_norm) * (1.0 + scale[:, None, :]) + shift[:, None, :]
    q, kc, vc, ks, vs, kw, vw, br, g_a, rq, rk, rv, g_r = jnp.split(h @ w_in, SPLITS, axis=-1)

    def kv(a):
        return a.reshape(b, l, N_KV, HEAD_DIM)

    q = rmsnorm(q.reshape(b, l, N_HEADS, HEAD_DIM), g_q).reshape(b, l, N_KV, HPG, HEAD_DIM)
    br = jax.nn.sigmoid(br).reshape(b, l, N_KV, HPG, 3)
    return (gate, q, kv(kc), kv(vc), rmsnorm(kv(ks), g_ks), kv(vs), rmsnorm(kv(kw), g_kw), kv(vw), br, g_a,
            rq.reshape(b, l, N_RET_HEADS, DK_RET), rk.reshape(b, l, N_RET_HEADS, DK_RET),
            rv.reshape(b, l, N_RET_HEADS, DV_RET), g_r)


def mix_outputs(x, gate, o_nsa, g_a, o_ret, g_r, g_ret, w_out):
    b, l = x.shape[:2]
    y_a = o_nsa.reshape(b, l, D_NSA).astype(x.dtype) * jax.nn.silu(g_a)
    y_r = rmsnorm(o_ret, g_ret).reshape(b, l, D_RET).astype(x.dtype) * jax.nn.silu(g_r)
    return x + gate[:, None, :] * (jnp.concatenate([y_a, y_r], axis=-1) @ w_out)


def prompt_layer(x, c, W):
    (g_norm, w_ada, b_ada, w_in, g_q, g_kc, g_ks, g_kw, pe_ck, w_ck1, w_ck2,
     pe_cv, w_cv1, w_cv2, g_ret, w_out) = W
    b, l = x.shape[:2]
    pos = jnp.arange(l, dtype=jnp.int32)
    (gate, q, kc_raw, vc_raw, ks, vs, kw, vw, br, g_a, rq, rk, rv, g_r) = mix_inputs(
        x, c, g_norm, w_ada, b_ada, w_in, g_q, g_ks, g_kw)
    kc = rmsnorm(compress(kc_raw, pe_ck, w_ck1, w_ck2), g_kc)
    vc = compress(vc_raw, pe_cv, w_cv1, w_cv2)
    ks_blk = ks.reshape(b, l // SEL_BLOCK, SEL_BLOCK, N_KV, HEAD_DIM)
    vs_blk = vs.reshape(b, l // SEL_BLOCK, SEL_BLOCK, N_KV, HEAD_DIM)
    pad_w = ((0, 0), (WINDOW, 0), (0, 0), (0, 0))
    kw_pad = jnp.pad(kw, pad_w)
    vw_pad = jnp.pad(vw, pad_w)
    qb = min(Q_BLOCK, l)

    def q_block(i):
        s0 = i * qb
        return nsa_block(lax.dynamic_slice_in_dim(q, s0, qb, axis=1),
                         s0 + jnp.arange(qb, dtype=jnp.int32), kc, vc, ks_blk, vs_blk,
                         lax.dynamic_slice_in_dim(kw_pad, s0, WINDOW + qb, axis=1),
                         lax.dynamic_slice_in_dim(vw_pad, s0, WINDOW + qb, axis=1),
                         s0 - WINDOW + jnp.arange(WINDOW + qb, dtype=jnp.int32),
                         lax.dynamic_slice_in_dim(br, s0, qb, axis=1))

    o_nsa = lax.map(q_block, jnp.arange(l // qb, dtype=jnp.int32))
    o_nsa = o_nsa.swapaxes(0, 1).reshape(b, l, N_KV, HPG, HEAD_DIM)
    s_init = jnp.zeros((b, N_RET_HEADS, DK_RET, DV_RET), jnp.float32)
    o_ret, s_ret = retention(rotary(rq, pos), rotary(rk, pos) * DK_RET ** -0.5, rv.astype(jnp.float32), s_init)
    y = mix_outputs(x, gate, o_nsa, g_a, o_ret, g_r, g_ret, w_out)
    keep = min(WINDOW, l)
    return y, (jnp.stack([kc_raw, vc_raw], axis=2), jnp.stack([ks, vs], axis=2),
               jnp.stack([kw[:, l - keep:], vw[:, l - keep:]], axis=2), s_ret.astype(x.dtype))


def sample_layer(x, c, cache_cmp, cache_slc, state_win, state_ret, page_table, W):
    (g_norm, w_ada, b_ada, w_in, g_q, g_kc, g_ks, g_kw, pe_ck, w_ck1, w_ck2,
     pe_cv, w_cv1, w_cv2, g_ret, w_out) = W
    b, l = x.shape[:2]
    pos = PAST_LEN + jnp.arange(l, dtype=jnp.int32)
    (gate, q, kc_raw, vc_raw, ks, vs, kw, vw, br, g_a, rq, rk, rv, g_r) = mix_inputs(
        x, c, g_norm, w_ada, b_ada, w_in, g_q, g_ks, g_kw)
    new_cmp = jnp.stack([kc_raw, vc_raw], axis=2)
    new_slc = jnp.stack([ks, vs], axis=2)
    full_cmp = pad_rows(jnp.concatenate(
        [gather_pages(cache_cmp, page_table).astype(x.dtype), new_cmp], axis=1), CMP_STRIDE)
    kc = rmsnorm(compress(full_cmp[:, :, 0], pe_ck, w_ck1, w_ck2), g_kc)
    vc = compress(full_cmp[:, :, 1], pe_cv, w_cv1, w_cv2)
    full_slc = pad_rows(jnp.concatenate(
        [gather_pages(cache_slc, page_table).astype(x.dtype), new_slc], axis=1), SEL_BLOCK)
    slc_blk = full_slc.reshape(b, -1, SEL_BLOCK, 2, N_KV, HEAD_DIM)
    wb = state_win.shape[1]
    win = jnp.concatenate([state_win.astype(x.dtype), jnp.stack([kw, vw], axis=2)], axis=1)
    win_pos = PAST_LEN - wb + jnp.arange(wb + l, dtype=jnp.int32)
    o_nsa = nsa_block(q, pos, kc, vc, slc_blk[:, :, :, 0], slc_blk[:, :, :, 1],
                      win[:, :, 0], win[:, :, 1], win_pos, br)
    o_ret, s_ret = retention(rotary(rq, pos), rotary(rk, pos) * DK_RET ** -0.5,
                             rv.astype(jnp.float32), state_ret.astype(jnp.float32))
    y = mix_outputs(x, gate, o_nsa, g_a, o_ret, g_r, g_ret, w_out)
    keep = min(WINDOW, wb + l)
    return y, (new_cmp, new_slc, win[:, wb + l - keep:], s_ret.astype(x.dtype))


def setup_inputs(seed: int = 0) -> dict:
    key = jax.random.key(seed)
    k = jax.random.split(key, 25)
    n_pages = PAST_LEN // PAGE_SIZE
    n_used = DEC_BATCH * n_pages
    n_phys = n_used + n_used // 4
    win_len = min(WINDOW, PAST_LEN)

    def nrm(kk, shape, s):
        return s * jax.random.normal(kk, shape, jnp.float32)

    def gain(kk, n):
        return 1.0 + nrm(kk, (DEPTH, n), 0.01)

    kv_page = (DEPTH, n_phys, PAGE_SIZE, 2, N_KV, HEAD_DIM)
    page_table = jax.random.permutation(k[8], n_phys)[:n_used].reshape(DEC_BATCH, n_pages).astype(jnp.int32)
    return {
        'x_prompt': nrm(k[0], (BATCH, SEQ, D_MODEL), 1.0),
        'x_sample': nrm(k[1], (DEC_BATCH, DEC_SEQ, D_MODEL), 1.0),
        'c_prompt': nrm(k[2], (BATCH, D_MODEL), 1.0),
        'c_sample': nrm(k[3], (DEC_BATCH, D_MODEL), 1.0),
        'cache_cmp': nrm(k[4], kv_page, 1.0),
        'cache_slc': nrm(k[5], kv_page, 1.0),
        'state_win': nrm(k[6], (DEPTH, DEC_BATCH, win_len, 2, N_KV, HEAD_DIM), 1.0),
        'state_ret': nrm(k[7], (DEPTH, DEC_BATCH, N_RET_HEADS, DK_RET, DV_RET), 0.5),
        'page_table': page_table,
        'g_norm': gain(k[9], D_MODEL),
        'w_ada': nrm(k[10], (DEPTH, D_MODEL, 3 * D_MODEL), 0.5 * D_MODEL ** -0.5),
        'b_ada': nrm(k[11], (DEPTH, 3 * D_MODEL), 0.02),
        'w_in': nrm(k[12], (DEPTH, D_MODEL, D_IN), D_MODEL ** -0.5),
        'g_q': gain(k[13], HEAD_DIM),
        'g_kc': gain(k[14], HEAD_DIM),
        'g_ks': gain(k[15], HEAD_DIM),
        'g_kw': gain(k[16], HEAD_DIM),
        'pe_ck': nrm(k[17], (DEPTH, CMP_BLOCK, HEAD_DIM), 0.1),
        'w_ck1': nrm(k[18], (DEPTH, CMP_BLOCK, HEAD_DIM, CMP_HIDDEN), (CMP_BLOCK * HEAD_DIM) ** -0.5),
        'w_ck2': nrm(k[19], (DEPTH, CMP_HIDDEN, HEAD_DIM), CMP_HIDDEN ** -0.5),
        'pe_cv': nrm(k[20], (DEPTH, CMP_BLOCK, HEAD_DIM), 0.1),
        'w_cv1': nrm(k[21], (DEPTH, CMP_BLOCK, HEAD_DIM, CMP_HIDDEN), (CMP_BLOCK * HEAD_DIM) ** -0.5),
        'w_cv2': nrm(k[22], (DEPTH, CMP_HIDDEN, HEAD_DIM), CMP_HIDDEN ** -0.5),
        'g_ret': gain(k[23], DV_RET),
        'w_out': nrm(k[24], (DEPTH, D_MIX, D_MODEL), D_MIX ** -0.5),
    }


def reference(x_prompt, x_sample, c_prompt, c_sample, cache_cmp, cache_slc, state_win, state_ret, page_table,
              g_norm, w_ada, b_ada, w_in, g_q, g_kc, g_ks, g_kw, pe_ck, w_ck1, w_ck2, pe_cv, w_cv1, w_cv2,
              g_ret, w_out):
    xp, xs = x_prompt, x_sample
    new_p, new_s = [], []
    for li in range(DEPTH):
        W = (g_norm[li], w_ada[li], b_ada[li], w_in[li], g_q[li], g_kc[li], g_ks[li], g_kw[li],
             pe_ck[li], w_ck1[li], w_ck2[li], pe_cv[li], w_cv1[li], w_cv2[li], g_ret[li], w_out[li])
        xp, st_p = prompt_layer(xp, c_prompt, W)
        xs, st_s = sample_layer(xs, c_sample, cache_cmp[li], cache_slc[li], state_win[li], state_ret[li],
                                page_table, W)
        new_p.append(st_p)
        new_s.append(st_s)

    def stack(group, j):
        return jnp.stack([st[j] for st in group])

    return (xp, xs, stack(new_p, 0), stack(new_p, 1), stack(new_p, 2), stack(new_p, 3),
            stack(new_s, 0), stack(new_s, 1), stack(new_s, 2), stack(new_s, 3))
```

```python
import functools
import math

import jax
import jax.numpy as jnp
import numpy as np
from jax import lax
from jax.experimental import pallas as pl
from jax.experimental.pallas import tpu as pltpu

D_MODEL = 1024
HEAD_DIM = 64
D_NSA = 512
N_HEADS = 8
N_KV = 2
HPG = 4
CMP_BLOCK = 32
CMP_STRIDE = 16
SEL_BLOCK = 64
TOP_N = 16
WINDOW = 512
D_RET = 512
N_RET_HEADS = 4
DV_RET = 128
DK_RET = 64
RET_CHUNK = 128
PAGE_SIZE = 128
ROPE_BASE = 10000.0
EPS = 1e-6
NEG = -1e30
FORCE = 1e4
BELOW_NEG = -3e38

KV_ROW = 2 * N_KV * HEAD_DIM
CHUNK_LANES = CMP_STRIDE * KV_ROW
LANES = 128
VMEM_LIMIT_BYTES = 56 * 1024 * 1024

F32 = jnp.float32
BF16 = jnp.bfloat16

O_Q = 0
O_CMP = O_Q + N_HEADS * LANES
O_SLC = O_CMP + KV_ROW
O_WIN = O_SLC + KV_ROW
O_BR = O_WIN + KV_ROW
O_GA = O_BR + LANES
O_RQ = O_GA + D_NSA
O_RK = O_RQ + 2 * N_RET_HEADS * DK_RET
O_RV = O_RK + 2 * N_RET_HEADS * DK_RET
O_GR = O_RV + D_RET
D_IN_PAD = O_GR + D_RET


def _cparams(*sem):
    return pltpu.CompilerParams(dimension_semantics=sem, vmem_limit_bytes=VMEM_LIMIT_BYTES)


def _dot(a, b):
    return jnp.dot(a, b, preferred_element_type=F32)


def _dot_nt(a, b):
    return lax.dot_general(a, b, (((1,), (1,)), ((), ())), preferred_element_type=F32)


def _dot_tn(a, b):
    return lax.dot_general(a, b, (((0,), (0,)), ((), ())), preferred_element_type=F32)


def _silu(x):
    return x * jax.nn.sigmoid(x)


def _split3_dot(p, m):
    p1 = p.astype(BF16)
    r1 = p - p1.astype(F32)
    p2 = r1.astype(BF16)
    p3 = (r1 - p2.astype(F32)).astype(BF16)
    return _dot(p1, m) + _dot(p2, m) + _dot(p3, m)


def _pair_rmsnorm(k, gain2):
    lo = lax.broadcasted_iota(jnp.int32, k.shape, 1) < HEAD_DIM
    sq = k * k
    s0 = jnp.sum(jnp.where(lo, sq, 0.0), axis=-1, keepdims=True)
    s1 = jnp.sum(jnp.where(lo, 0.0, sq), axis=-1, keepdims=True)
    ms = jnp.where(lo, s0, s1) * (1.0 / HEAD_DIM)
    return k * lax.rsqrt(ms + EPS) * gain2


def _mod_kernel(c_ref, w_ref, b_ref, o_ref):
    o_ref[...] = _dot(_silu(c_ref[...]).astype(BF16), w_ref[...]) + b_ref[...]


def _modulation(c_all, w_ada, b_ada):
    rows = c_all.shape[0]
    n = w_ada.shape[1]
    tn = 1024
    return pl.pallas_call(
        _mod_kernel,
        grid=(n // tn,),
        in_specs=[pl.BlockSpec((rows, D_MODEL), lambda j: (0, 0)),
                  pl.BlockSpec((D_MODEL, tn), lambda j: (0, j)),
                  pl.BlockSpec((1, tn), lambda j: (0, j))],
        out_specs=pl.BlockSpec((rows, tn), lambda j: (0, j)),
        out_shape=jax.ShapeDtypeStruct((rows, n), F32),
        compiler_params=_cparams("arbitrary"),
        name="adaln_modulation",
    )(c_all, w_ada.astype(BF16), b_ada.reshape(1, n))


def _rope_kernel(freq_ref, cos_ref, sin_ref, *, base, stride, rows):
    shape = (rows, LANES)
    r = lax.broadcasted_iota(jnp.int32, shape, 0)
    pos = (base + stride * (pl.program_id(0) * rows + r)).astype(F32)
    ang = pos * freq_ref[...]
    lane = lax.broadcasted_iota(jnp.int32, shape, 1)
    sign = jnp.where((lane & (DK_RET - 1)) < DK_RET // 2, -1.0, 1.0)
    cos_ref[...] = jnp.cos(ang)
    sin_ref[...] = jnp.sin(ang) * sign


def _rope_tables(n_rows, base, stride):
    half = DK_RET // 2
    freqs = ROPE_BASE ** (-jnp.arange(half, dtype=F32) / half)
    freq_row = jnp.tile(freqs, LANES // half).reshape(1, LANES)
    rows = min(n_rows, 512)
    return pl.pallas_call(
        functools.partial(_rope_kernel, base=base, stride=stride, rows=rows),
        grid=(n_rows // rows,),
        in_specs=[pl.BlockSpec((1, LANES), lambda i: (0, 0))],
        out_specs=[pl.BlockSpec((rows, LANES), lambda i: (i, 0))] * 2,
        out_shape=[jax.ShapeDtypeStruct((n_rows, LANES), F32)] * 2,
        compiler_params=_cparams("arbitrary"),
        name="rope_tables",
    )(freq_row)


def _proj_kernel(x_ref, scale_ref, shift_ref, gn_ref, w_ref, gq_ref, gks_ref, gkw_ref, cos_ref, sin_ref,
                 q_ref, cmp_ref, slc_ref, slcb_ref, win_ref, winb_ref, br_ref, ga_ref, rq_ref, rk_ref,
                 rv_ref, gr_ref):
    x = x_ref[...]
    ms = jnp.mean(x * x, axis=-1, keepdims=True)
    xn = x * lax.rsqrt(ms + EPS) * gn_ref[...]
    hb = (xn * (1.0 + scale_ref[0]) + shift_ref[0]).astype(BF16)

    def seg(start, width):
        return _dot(hb, w_ref[:, start:start + width])

    yq = seg(O_Q, N_HEADS * LANES)
    for s in range(N_HEADS):
        t = yq[:, s * LANES:(s + 1) * LANES]
        ms = jnp.sum(t * t, axis=-1, keepdims=True) * (1.0 / HEAD_DIM)
        q_ref[0, s] = (t * lax.rsqrt(ms + EPS) * gq_ref[s]).astype(BF16)

    cmp_ref[...] = seg(O_CMP, KV_ROW)

    y = seg(O_SLC, KV_ROW)
    y = jnp.concatenate([_pair_rmsnorm(y[:, :LANES], gks_ref[...]), y[:, LANES:]], axis=-1)
    slc_ref[...] = y
    slcb_ref[...] = y.astype(BF16)

    y = seg(O_WIN, KV_ROW)
    y = jnp.concatenate([_pair_rmsnorm(y[:, :LANES], gkw_ref[...]), y[:, LANES:]], axis=-1)
    win_ref[...] = y
    winb_ref[...] = y.astype(BF16)

    br_ref[...] = jax.nn.sigmoid(seg(O_BR, LANES))
    ga_ref[...] = _silu(seg(O_GA, D_NSA))

    cos2 = jnp.concatenate([cos_ref[...]] * 2, axis=-1)
    sin2 = jnp.concatenate([sin_ref[...]] * 2, axis=-1)
    nrot = N_RET_HEADS * DK_RET
    y = seg(O_RQ, 2 * nrot)
    rq_ref[...] = (y[:, :nrot] * cos2 + y[:, nrot:] * sin2).astype(BF16)
    y = seg(O_RK, 2 * nrot)
    rk_ref[...] = ((y[:, :nrot] * cos2 + y[:, nrot:] * sin2) * DK_RET ** -0.5).astype(BF16)
    rv_ref[...] = seg(O_RV, D_RET).astype(BF16)
    gr_ref[...] = _silu(seg(O_GR, D_RET))


def _swap_halves(w):
    k = w.shape[0]
    w4 = w.reshape(k, N_RET_HEADS, 2, DK_RET // 2)
    return w4[:, :, ::-1, :].reshape(k, N_RET_HEADS * DK_RET)


def _prep_proj_weights(w_in, g_q, g_ks, g_kw):
    k = w_in.shape[0]
    o = 0
    wq = w_in[:, o:o + D_NSA].reshape(k, N_KV, HPG, HEAD_DIM)
    o += D_NSA
    z = jnp.zeros_like(wq)
    wq = jnp.stack([jnp.concatenate([wq[:, 0], z[:, 0]], axis=-1),
                    jnp.concatenate([z[:, 1], wq[:, 1]], axis=-1)], axis=1).reshape(k, N_HEADS * LANES)
    w_kv = w_in[:, o:o + 3 * KV_ROW]
    o += 3 * KV_ROW
    n_br = 3 * N_HEADS
    w_br = jnp.pad(w_in[:, o:o + n_br], ((0, 0), (0, LANES - n_br)))
    o += n_br
    w_ga = w_in[:, o:o + D_NSA]
    o += D_NSA
    nrot = N_RET_HEADS * DK_RET
    w_rq = w_in[:, o:o + nrot]
    o += nrot
    w_rk = w_in[:, o:o + nrot]
    o += nrot
    w_rv = w_in[:, o:o + D_RET]
    o += D_RET
    w_gr = w_in[:, o:o + D_RET]
    w = jnp.concatenate([wq, w_kv, w_br, w_ga, w_rq, _swap_halves(w_rq), w_rk, _swap_halves(w_rk), w_rv, w_gr],
                        axis=1).astype(BF16)
    assert w.shape[1] == D_IN_PAD
    gz = jnp.zeros((HEAD_DIM,), F32)
    gq = g_q * HEAD_DIM ** -0.5
    gq_slots = jnp.stack([jnp.concatenate([gq, gz]) if s < HPG else jnp.concatenate([gz, gq])
                          for s in range(N_HEADS)]).reshape(N_HEADS, 1, LANES)
    return w, gq_slots, jnp.tile(g_ks, 2).reshape(1, LANES), jnp.tile(g_kw, 2).reshape(1, LANES)


def _projection(x2, scale, shift, g_norm, w, gq_slots, gks2, gkw2, cos_t, sin_t, *, n_batch, per_row_mod):
    m = x2.shape[0]
    seq = m // n_batch
    tm = min(seq, 256)
    nt = seq // tm
    if per_row_mod:
        mod_spec = pl.BlockSpec((1, tm, D_MODEL), lambda i: (0, i, 0))
    else:
        mod_spec = pl.BlockSpec((1, 1, D_MODEL), lambda i: (i // nt, 0, 0))

    def rows_spec(width):
        return pl.BlockSpec((tm, width), lambda i: (i, 0))

    def const_spec(shape):
        return pl.BlockSpec(shape, lambda i: (0,) * len(shape))

    widths = [(KV_ROW, F32), (KV_ROW, F32), (KV_ROW, BF16), (KV_ROW, F32), (KV_ROW, BF16), (LANES, F32),
              (D_NSA, F32), (N_RET_HEADS * DK_RET, BF16), (N_RET_HEADS * DK_RET, BF16), (D_RET, BF16),
              (D_RET, F32)]
    out_shape = [jax.ShapeDtypeStruct((n_batch, N_HEADS, seq, LANES), BF16)]
    out_specs = [pl.BlockSpec((1, N_HEADS, tm, LANES), lambda i: (i // nt, 0, i % nt, 0))]
    for width, dt in widths:
        out_shape.append(jax.ShapeDtypeStruct((m, width), dt))
        out_specs.append(rows_spec(width))
    return pl.pallas_call(
        _proj_kernel,
        grid=(m // tm,),
        in_specs=[rows_spec(D_MODEL), mod_spec, mod_spec, const_spec((1, D_MODEL)),
                  const_spec((D_MODEL, D_IN_PAD)), const_spec((N_HEADS, 1, LANES)), const_spec((1, LANES)),
                  const_spec((1, LANES)), pl.BlockSpec((tm, LANES), lambda i: (i % nt, 0)),
                  pl.BlockSpec((tm, LANES), lambda i: (i % nt, 0))],
        out_specs=out_specs,
        out_shape=out_shape,
        compiler_params=_cparams("arbitrary"),
        name="input_projection",
    )(x2, scale, shift, g_norm.reshape(1, D_MODEL), w, gq_slots, gks2, gkw2, cos_t, sin_t)


def _chunk_matmul_kernel(*refs, n_in):
    x_refs, w_ref, o_ref = refs[-2 - n_in:-2], refs[-2], refs[-1]
    if n_in == 1:
        x = x_refs[0][0]
    else:
        x = jnp.concatenate([r[0] for r in x_refs], axis=0)
    o_ref[0] = _dot(x.astype(BF16), w_ref[...])


def _chunk_matmul_dense(chunks, w1x):
    b, n, _ = chunks.shape
    return pl.pallas_call(
        functools.partial(_chunk_matmul_kernel, n_in=1),
        grid=(b,),
        in_specs=[pl.BlockSpec((1, n, CHUNK_LANES), lambda i: (i, 0, 0)),
                  pl.BlockSpec(w1x.shape, lambda i: (0, 0))],
        out_specs=pl.BlockSpec((1, n, w1x.shape[1]), lambda i: (i, 0, 0)),
        out_shape=jax.ShapeDtypeStruct((b, n, w1x.shape[1]), F32),
        compiler_params=_cparams("arbitrary"),
        name="compress_chunks_prompt",
    )(chunks, w1x)


def _chunk_matmul_paged(cache_pages, page_table, w1x, pages_per_step):
    b, n_pages = page_table.shape
    cpp = cache_pages.shape[1]
    p = pages_per_step
    in_specs = [pl.BlockSpec((1, cpp, CHUNK_LANES), lambda i, j, pt, k=k: (pt[i, j * p + k], 0, 0))
                for k in range(p)]
    in_specs.append(pl.BlockSpec(w1x.shape, lambda i, j, pt: (0, 0)))
    return pl.pallas_call(
        functools.partial(_chunk_matmul_kernel, n_in=p),
        grid_spec=pltpu.PrefetchScalarGridSpec(
            num_scalar_prefetch=1,
            grid=(b, n_pages // p),
            in_specs=in_specs,
            out_specs=pl.BlockSpec((1, p * cpp, w1x.shape[1]), lambda i, j, pt: (i, j, 0)),
        ),
        out_shape=jax.ShapeDtypeStruct((b, n_pages * cpp, w1x.shape[1]), F32),
        compiler_params=_cparams("arbitrary", "arbitrary"),
        name="compress_chunks_paged",
    )(page_table, *([cache_pages] * p), w1x)


def _compress_combine_kernel(a_ref, new_ref, pe_ref, w1_ref, w2_ref, gkc_ref, kc_ref, vc_ref, *, n):
    half = KV_ROW
    a = a_ref[0]
    pe_part = _dot(pe_ref[...], w1_ref[...])
    bias = pe_part[0:1, :half] + pe_part[1:2, half:]
    new8 = jnp.broadcast_to(new_ref[0], (8, KV_ROW)).astype(BF16)
    a_new = _dot(new8, w1_ref[0:KV_ROW, half:])[0:1]
    hi = pltpu.roll(a[:, half:], n - 1, axis=0)
    row = lax.broadcasted_iota(jnp.int32, (n, half), 0)
    hi = jnp.where(row == n - 1, a_new, hi)
    out = _dot(_silu(a[:, :half] + hi + bias).astype(BF16), w2_ref[...])
    kc_ref[0] = _pair_rmsnorm(out[:, :LANES], gkc_ref[...]).astype(BF16)
    vc_ref[0] = out[:, LANES:].astype(BF16)


def _compress_combine(a, new_rows, pe8, w1x, w2x, gkc2):
    b, n, _ = a.shape
    return pl.pallas_call(
        functools.partial(_compress_combine_kernel, n=n),
        grid=(b,),
        in_specs=[pl.BlockSpec((1, n, 2 * KV_ROW), lambda i: (i, 0, 0)),
                  pl.BlockSpec((1, 1, KV_ROW), lambda i: (i, 0, 0)),
                  pl.BlockSpec(pe8.shape, lambda i: (0, 0)),
                  pl.BlockSpec(w1x.shape, lambda i: (0, 0)),
                  pl.BlockSpec(w2x.shape, lambda i: (0, 0)),
                  pl.BlockSpec((1, LANES), lambda i: (0, 0))],
        out_specs=[pl.BlockSpec((1, n, LANES), lambda i: (i, 0, 0))] * 2,
        out_shape=[jax.ShapeDtypeStruct((b, n, LANES), BF16)] * 2,
        compiler_params=_cparams("arbitrary"),
        name="compress_combine",
    )(a, new_rows, pe8, w1x, w2x, gkc2)


def _prep_compress_weights(pe_ck, w_ck1, w_ck2, pe_cv, w_cv1, w_cv2, g_kc):
    eye = jnp.eye(N_KV, dtype=F32)
    w1 = jnp.stack([w_ck1, w_cv1]).reshape(2, 2, CMP_STRIDE, HEAD_DIM, HEAD_DIM)
    w1x = jnp.einsum('khldf,kq,gp->lkgdhqpf', w1, eye, eye).reshape(CHUNK_LANES, 2 * KV_ROW).astype(BF16)
    w2 = jnp.stack([w_ck2, w_cv2])
    w2x = jnp.einsum('kfd,kq,gp->kgfqpd', w2, eye, eye).reshape(KV_ROW, KV_ROW).astype(BF16)
    pe = jnp.stack([pe_ck, pe_cv]).reshape(2, 2, CMP_STRIDE, HEAD_DIM)
    pe_rows = jnp.broadcast_to(pe.transpose(1, 2, 0, 3)[:, :, :, None, :],
                               (2, CMP_STRIDE, 2, N_KV, HEAD_DIM)).reshape(2, CHUNK_LANES)
    pe8 = jnp.zeros((8, CHUNK_LANES), F32).at[:2].set(pe_rows).astype(BF16)
    return w1x, w2x, pe8, jnp.tile(g_kc, 2).reshape(1, LANES)


def _importance_matrix(n_cmp_pad, n_cmp, n_sel, n_sel_pad):
    ratio = SEL_BLOCK // CMP_STRIDE
    span = CMP_BLOCK // CMP_STRIDE
    m = np.zeros((n_cmp_pad, n_sel_pad), np.float32)
    for j in range(n_sel):
        for r in range(ratio):
            for s in range(span):
                n = ratio * j + r - s
                if 0 <= n < n_cmp:
                    m[n, j] += 1.0
    return jnp.asarray(m, BF16)


def _nsa_prompt_kernel(q_ref, kc_ref, vc_ref, slc_ref, win_ref, br_ref, ga_ref, mimp_ref, exp_ref, o_ref,
                       m_sc, l_sc, acc_sc, *, tq, tk, n_cmp, n_sel):
    qs = pl.program_id(1) * tq
    rows = HPG * tq
    ncp = kc_ref.shape[1]
    wlen = WINDOW + tq
    br = br_ref[...]
    lane_lo = lax.broadcasted_iota(jnp.int32, (tq, LANES), 1) < HEAD_DIM

    def row_pos(shape):
        return qs + (lax.broadcasted_iota(jnp.int32, shape, 0) & (tq - 1))

    for g in range(N_KV):
        q = q_ref[0, HPG * g:HPG * (g + 1)].reshape(rows, LANES)

        s = _dot_nt(q, kc_ref[0])
        col = lax.broadcasted_iota(jnp.int32, (rows, ncp), 1)
        mask = (col * CMP_STRIDE + (CMP_BLOCK - 1) <= row_pos((rows, ncp))) & (col < n_cmp)
        s = jnp.where(mask, s, NEG)
        e = jnp.exp(s - jnp.max(s, axis=-1, keepdims=True))
        p = jnp.where(mask, e, 0.0) * (1.0 / jnp.sum(e, axis=-1, keepdims=True))
        o_cmp = _dot(p.astype(BF16), vc_ref[0])
        p_sum = p[0:tq] + p[tq:2 * tq] + p[2 * tq:3 * tq] + p[3 * tq:4 * tq]
        imp = _split3_dot(p_sum, mimp_ref[...])

        blk = lax.broadcasted_iota(jnp.int32, (tq, LANES), 1)
        t = qs + lax.broadcasted_iota(jnp.int32, (tq, LANES), 0)
        jt = t // SEL_BLOCK
        forced = (blk == 0) | (blk == jt) | (blk == jt - 1)
        score = jnp.where(blk * SEL_BLOCK <= t, jnp.where(forced, FORCE, imp), NEG)
        score = jnp.where(blk < n_sel, score, BELOW_NEG)
        n_rank = ((n_sel + 7) // 8) * 8
        s_t = score.T[:n_rank]
        j_row = lax.broadcasted_iota(jnp.int32, (n_rank, tq), 0)
        beaten = jnp.zeros((n_rank, tq), F32)
        for i in range(n_sel):
            r = s_t[i:i + 1, :]
            beaten = beaten + jnp.where((r > s_t) | ((r == s_t) & (i < j_row)), 1.0, 0.0)
        sel_t = jnp.where(beaten < min(TOP_N, n_sel), 1.0, 0.0)
        if n_rank < LANES:
            sel_t = jnp.concatenate([sel_t, jnp.zeros((LANES - n_rank, tq), F32)], axis=0)
        sel = sel_t.T.astype(BF16)

        m_sc[...] = jnp.full((rows, LANES), NEG, F32)
        l_sc[...] = jnp.zeros((rows, LANES), F32)
        acc_sc[...] = jnp.zeros((rows, LANES), F32)

        def key_tile(kt, carry):
            k0 = pl.multiple_of(kt * tk, tk)
            k = slc_ref[0, pl.ds(k0, tk), 0:LANES]
            v = slc_ref[0, pl.ds(k0, tk), LANES:2 * LANES]
            s = _dot_nt(q, k)
            sel_keys = _dot(sel, exp_ref[kt])
            kpos = k0 + lax.broadcasted_iota(jnp.int32, (tq, tk), 1)
            tt = qs + lax.broadcasted_iota(jnp.int32, (tq, tk), 0)
            allowed = (sel_keys > 0.5) & (kpos <= tt)
            s = jnp.concatenate([jnp.where(allowed, s[h * tq:(h + 1) * tq], NEG) for h in range(HPG)], axis=0)
            m_prev = m_sc[...]
            m_next = jnp.maximum(m_prev, jnp.max(s, axis=-1, keepdims=True))
            alpha = jnp.exp(m_prev - m_next)
            p = jnp.exp(s - m_next[:, 0:1])
            l_sc[...] = alpha * l_sc[...] + jnp.sum(p, axis=-1, keepdims=True)
            acc_sc[...] = alpha * acc_sc[...] + _dot(p.astype(BF16), v)
            m_sc[...] = m_next
            return carry

        lax.fori_loop(0, qs // tk + 1, key_tile, 0)
        o_slc = acc_sc[...] * (1.0 / l_sc[...])

        w0 = pl.multiple_of(jnp.maximum(qs - WINDOW, 0), tq)
        k = win_ref[0, pl.ds(w0, wlen), 0:LANES]
        v = win_ref[0, pl.ds(w0, wlen), LANES:2 * LANES]
        s = _dot_nt(q, k)
        kpos = w0 + lax.broadcasted_iota(jnp.int32, (rows, wlen), 1)
        tt = row_pos((rows, wlen))
        s = jnp.where((kpos <= tt) & (kpos > tt - WINDOW), s, NEG)
        e = jnp.exp(s - jnp.max(s, axis=-1, keepdims=True))
        o_win = _dot(e.astype(BF16), v) * (1.0 / jnp.sum(e, axis=-1, keepdims=True))

        heads = []
        for h in range(HPG):
            rs = slice(h * tq, (h + 1) * tq)
            c = 3 * (HPG * g + h)
            heads.append(br[:, c:c + 1] * o_cmp[rs] + br[:, c + 1:c + 2] * o_slc[rs] + br[:, c + 2:c + 3] * o_win[rs])
        for hh in range(HPG // 2):
            even, odd = heads[2 * hh], heads[2 * hh + 1]
            if g == 0:
                odd = pltpu.roll(odd, HEAD_DIM, axis=1)
            else:
                even = pltpu.roll(even, HEAD_DIM, axis=1)
            c0 = (2 * g + hh) * LANES
            o_ref[:, c0:c0 + LANES] = (jnp.where(lane_lo, even, odd) * ga_ref[:, c0:c0 + LANES]).astype(BF16)


def _nsa_prompt(q_slots, kc, vc, slc_b, win_b, br, ga, *, n_batch, seq):
    tq = 128
    tk = min(512, seq)
    n_cmp = seq // CMP_STRIDE - (CMP_BLOCK // CMP_STRIDE) + 1
    n_sel = seq // SEL_BLOCK
    assert n_sel <= LANES and seq % tk == 0 and tk % tq == 0 and seq >= WINDOW + tq
    ncp = kc.shape[1]
    mimp = _importance_matrix(ncp, n_cmp, n_sel, LANES)
    key_blk = (jnp.arange(seq, dtype=jnp.int32) // SEL_BLOCK).reshape(seq // tk, 1, tk)
    expand = (key_blk == jnp.arange(LANES, dtype=jnp.int32).reshape(1, LANES, 1)).astype(BF16)
    nq = seq // tq
    rows = HPG * tq
    return pl.pallas_call(
        functools.partial(_nsa_prompt_kernel, tq=tq, tk=tk, n_cmp=n_cmp, n_sel=n_sel),
        grid=(n_batch, nq),
        in_specs=[pl.BlockSpec((1, N_HEADS, tq, LANES), lambda b, i: (b, 0, i, 0)),
                  pl.BlockSpec((1, ncp, LANES), lambda b, i: (b, 0, 0)),
                  pl.BlockSpec((1, ncp, LANES), lambda b, i: (b, 0, 0)),
                  pl.BlockSpec((1, seq, KV_ROW), lambda b, i: (b, 0, 0)),
                  pl.BlockSpec((1, seq, KV_ROW), lambda b, i: (b, 0, 0)),
                  pl.BlockSpec((tq, LANES), lambda b, i: (b * nq + i, 0)),
                  pl.BlockSpec((tq, D_NSA), lambda b, i: (b * nq + i, 0)),
                  pl.BlockSpec(mimp.shape, lambda b, i: (0, 0)),
                  pl.BlockSpec(expand.shape, lambda b, i: (0, 0, 0))],
        out_specs=pl.BlockSpec((tq, D_NSA), lambda b, i: (b * nq + i, 0)),
        out_shape=jax.ShapeDtypeStruct((n_batch * seq, D_NSA), BF16),
        scratch_shapes=[pltpu.VMEM((rows, LANES), F32)] * 3,
        compiler_params=_cparams("arbitrary", "arbitrary"),
        name="nsa_prompt",
    )(q_slots, kc, vc, slc_b.reshape(n_batch, seq, KV_ROW), win_b.reshape(n_batch, seq, KV_ROW), br, ga, mimp,
      expand)


def _retention_prompt_kernel(rq_ref, rk_ref, rv_ref, o_ref, s_ref, *, cl):
    @pl.when(pl.program_id(1) == 0)
    def _():
        s_ref[...] = jnp.zeros(s_ref.shape, F32)

    i_pos = lax.broadcasted_iota(jnp.int32, (cl, cl), 0)
    j_pos = lax.broadcasted_iota(jnp.int32, (cl, cl), 1)
    diff = (i_pos - j_pos).astype(F32)
    i_col = lax.broadcasted_iota(jnp.int32, (cl, 1), 0).astype(F32)
    lane_lo = lax.broadcasted_iota(jnp.int32, (cl, LANES), 1) < DK_RET
    for pair in range(N_RET_HEADS // 2):
        q2 = rq_ref[0, :, pair * LANES:(pair + 1) * LANES]
        k2 = rk_ref[0, :, pair * LANES:(pair + 1) * LANES]
        s_pair = jnp.concatenate([s_ref[0, 2 * pair], s_ref[0, 2 * pair + 1]], axis=0)
        for odd in range(2):
            h = 2 * pair + odd
            log_g = math.log(1.0 - 2.0 ** (-5.0 - h))
            keep = lane_lo if odd == 0 else jnp.logical_not(lane_lo)
            qh = jnp.where(keep, q2, jnp.zeros_like(q2))
            v = rv_ref[0, :, h * DV_RET:(h + 1) * DV_RET]
            decay = jnp.where(diff >= 0, jnp.exp(log_g * jnp.maximum(diff, 0.0)), 0.0)
            a = _dot_nt(qh, k2) * decay
            xi = jnp.exp(log_g * (i_col + 1.0))
            zeta = jnp.exp(log_g * (cl - 1.0 - i_col))
            o_ref[0, :, h * DV_RET:(h + 1) * DV_RET] = (_dot(a.astype(BF16), v)
                                                         + _dot(qh, s_pair.astype(BF16)) * xi)
            kz = (k2.astype(F32) * zeta).astype(BF16)
            u = _dot_tn(kz, v)[odd * DK_RET:(odd + 1) * DK_RET]
            s_ref[0, h] = s_ref[0, h] * math.exp(log_g * cl) + u


def _retention_prompt(rq, rk, rv, *, n_batch, seq):
    cl = math.gcd(seq, RET_CHUNK)
    nrot = N_RET_HEADS * DK_RET
    return pl.pallas_call(
        functools.partial(_retention_prompt_kernel, cl=cl),
        grid=(n_batch, seq // cl),
        in_specs=[pl.BlockSpec((1, cl, nrot), lambda b, i: (b, i, 0)),
                  pl.BlockSpec((1, cl, nrot), lambda b, i: (b, i, 0)),
                  pl.BlockSpec((1, cl, D_RET), lambda b, i: (b, i, 0))],
        out_specs=[pl.BlockSpec((1, cl, D_RET), lambda b, i: (b, i, 0)),
                   pl.BlockSpec((1, N_RET_HEADS, DK_RET, DV_RET), lambda b, i: (b, 0, 0, 0))],
        out_shape=[jax.ShapeDtypeStruct((n_batch, seq, D_RET), F32),
                   jax.ShapeDtypeStruct((n_batch, N_RET_HEADS, DK_RET, DV_RET), F32)],
        compiler_params=_cparams("arbitrary", "arbitrary"),
        name="retention_prompt",
    )(rq.reshape(n_batch, seq, nrot), rk.reshape(n_batch, seq, nrot), rv.reshape(n_batch, seq, D_RET))


def _retention_step_kernel(rq_ref, rk_ref, rv_ref, s0_ref, o_ref, s_ref):
    lane_lo = lax.broadcasted_iota(jnp.int32, (8, LANES), 1) < DK_RET
    diag = (lax.broadcasted_iota(jnp.int32, (LANES, LANES), 0)
            == lax.broadcasted_iota(jnp.int32, (LANES, LANES), 1))
    for pair in range(N_RET_HEADS // 2):
        q2 = jnp.broadcast_to(rq_ref[0, :, pair * LANES:(pair + 1) * LANES], (8, LANES))
        k_row = rk_ref[0, :, pair * LANES:(pair + 1) * LANES]
        k2 = jnp.broadcast_to(k_row, (8, LANES))
        k_col = jnp.sum(jnp.where(diag, jnp.broadcast_to(k_row, (LANES, LANES)), 0.0), axis=-1, keepdims=True)
        s_pair = jnp.concatenate([s0_ref[0, 2 * pair], s0_ref[0, 2 * pair + 1]], axis=0)
        for odd in range(2):
            h = 2 * pair + odd
            g = math.exp(math.log(1.0 - 2.0 ** (-5.0 - h)))
            keep = lane_lo if odd == 0 else jnp.logical_not(lane_lo)
            qh = jnp.where(keep, q2, 0.0)
            v_row = rv_ref[0, :, h * DV_RET:(h + 1) * DV_RET]
            qk = jnp.sum(qh * k2, axis=-1, keepdims=True)
            o = qk * v_row + _dot(qh, s_pair) * g
            o_ref[0, :, h * DV_RET:(h + 1) * DV_RET] = o[0:1]
            u = (k_col * v_row)[odd * DK_RET:(odd + 1) * DK_RET]
            s_ref[0, h] = s0_ref[0, h] * g + u


def _retention_step(rq, rk, rv, s0):
    b = rq.shape[0]
    nrot = N_RET_HEADS * DK_RET
    state_spec = pl.BlockSpec((1, N_RET_HEADS, DK_RET, DV_RET), lambda i: (i, 0, 0, 0))
    return pl.pallas_call(
        _retention_step_kernel,
        grid=(b,),
        in_specs=[pl.BlockSpec((1, 1, nrot), lambda i: (i, 0, 0)),
                  pl.BlockSpec((1, 1, nrot), lambda i: (i, 0, 0)),
                  pl.BlockSpec((1, 1, D_RET), lambda i: (i, 0, 0)),
                  state_spec],
        out_specs=[pl.BlockSpec((1, 1, D_RET), lambda i: (i, 0, 0)), state_spec],
        out_shape=[jax.ShapeDtypeStruct((b, 1, D_RET), F32),
                   jax.ShapeDtypeStruct((b, N_RET_HEADS, DK_RET, DV_RET), F32)],
        compiler_params=_cparams("arbitrary"),
        name="retention_step",
    )(rq.astype(F32).reshape(b, 1, nrot), rk.astype(F32).reshape(b, 1, nrot), rv.astype(F32).reshape(b, 1, D_RET),
      s0)


def _nsa_step1_kernel(q_ref, kc_ref, vc_ref, win_ref, new_ref, mimp_ref, oc_ref, ow_ref, idx_ref, *,
                      t, n_cmp, n_sel):
    q = q_ref[0]
    nq = q.shape[0]
    ncp = kc_ref.shape[1]
    nsp = mimp_ref.shape[1]

    s = _dot_nt(q, kc_ref[0])
    col = lax.broadcasted_iota(jnp.int32, (nq, ncp), 1)
    mask = (col * CMP_STRIDE + (CMP_BLOCK - 1) <= t) & (col < n_cmp)
    s = jnp.where(mask, s, NEG)
    e = jnp.exp(s - jnp.max(s, axis=-1, keepdims=True))
    p = jnp.where(mask, e, 0.0) * (1.0 / jnp.sum(e, axis=-1, keepdims=True))
    oc_ref[0] = _dot(p.astype(BF16), vc_ref[0])
    imp_rows = _split3_dot(p, mimp_ref[...])

    blk = lax.broadcasted_iota(jnp.int32, (1, nsp), 1)
    jt = t // SEL_BLOCK
    forced = (blk == 0) | (blk == jt) | (blk == jt - 1)
    scores = []
    for g in range(N_KV):
        imp = jnp.sum(imp_rows[HPG * g:HPG * (g + 1)], axis=0, keepdims=True)
        sc = jnp.where(blk * SEL_BLOCK <= t, jnp.where(forced, FORCE, imp), NEG)
        scores.append(jnp.where(blk < n_sel, sc, BELOW_NEG))
    s_rows = jnp.concatenate(scores + [jnp.full((LANES - N_KV, nsp), BELOW_NEG, F32)], axis=0)
    s_cols = s_rows.T
    a_idx = lax.broadcasted_iota(jnp.int32, (nsp, nsp), 0)
    c_idx = lax.broadcasted_iota(jnp.int32, (nsp, nsp), 1)
    rank_lane = lax.broadcasted_iota(jnp.int32, (nsp, LANES), 1).astype(F32)
    a_val = lax.broadcasted_iota(jnp.int32, (nsp, LANES), 0).astype(F32)
    rows_out = []
    for g in range(N_KV):
        s_col = s_cols[:, g:g + 1]
        s_row = s_rows[g:g + 1, :]
        beats = (s_row > s_col) | ((s_row == s_col) & (c_idx < a_idx))
        rank = jnp.sum(jnp.where(beats, 1.0, 0.0), axis=-1, keepdims=True)
        rows_out.append(jnp.sum(jnp.where(rank == rank_lane, a_val, 0.0), axis=0, keepdims=True))
    idx = jnp.concatenate(rows_out + [jnp.zeros((8 - N_KV, LANES), F32)], axis=0)
    idx_ref[0] = idx.astype(jnp.int32)

    wb = win_ref.shape[1]
    kw = win_ref[0, :, 0:LANES].astype(BF16)
    vw = win_ref[0, :, LANES:2 * LANES].astype(BF16)
    k_new = new_ref[0, :, 0:LANES]
    v_new = new_ref[0, :, LANES:2 * LANES]
    s = _dot_nt(q, kw)
    kpos = t - wb + lax.broadcasted_iota(jnp.int32, (nq, wb), 1)
    s = jnp.where((kpos > t - WINDOW) & (kpos >= 0), s, NEG)
    s_new = jnp.sum(q.astype(F32) * k_new, axis=-1, keepdims=True)
    m = jnp.maximum(jnp.max(s, axis=-1, keepdims=True), s_new)
    e = jnp.exp(s - m)
    e_new = jnp.exp(s_new - m)
    ow_ref[0] = (_dot(e.astype(BF16), vw) + e_new * v_new) * (1.0 / (jnp.sum(e, axis=-1, keepdims=True) + e_new))


def _nsa_step1(q16, kc, vc, state_win, new_win, mimp, *, t, n_cmp, n_sel):
    b, nq, _ = q16.shape
    ncp = kc.shape[1]
    wb = state_win.shape[1]

    def per_b(shape):
        return pl.BlockSpec((1,) + shape, lambda i: (i,) + (0,) * len(shape))

    return pl.pallas_call(
        functools.partial(_nsa_step1_kernel, t=t, n_cmp=n_cmp, n_sel=n_sel),
        grid=(b,),
        in_specs=[per_b((nq, LANES)), per_b((ncp, LANES)), per_b((ncp, LANES)), per_b((wb, KV_ROW)),
                  per_b((1, KV_ROW)), pl.BlockSpec(mimp.shape, lambda i: (0, 0))],
        out_specs=[per_b((nq, LANES)), per_b((nq, LANES)), per_b((8, LANES))],
        out_shape=[jax.ShapeDtypeStruct((b, nq, LANES), F32), jax.ShapeDtypeStruct((b, nq, LANES), F32),
                   jax.ShapeDtypeStruct((b, 8, LANES), jnp.int32)],
        compiler_params=_cparams("arbitrary"),
        name="nsa_step_compressed_window",
    )(q16, kc, vc, state_win, new_win, mimp)


def _nsa_step2_kernel(idx_sm, pt_sm, q_ref, *refs, n_cache_blocks, k_eff):
    blk_refs = refs[:N_KV * k_eff]
    new_ref, oc_ref, ow_ref, br_ref, o_ref = refs[N_KV * k_eff:]
    b = pl.program_id(0)
    q = q_ref[0]
    nq = q.shape[0]
    nk = k_eff * SEL_BLOCK
    k_new = new_ref[0, :, 0:LANES]
    v_new = new_ref[0, :, LANES:2 * LANES]
    s_new_all = jnp.sum(q.astype(F32) * k_new, axis=-1, keepdims=True)
    slot = lax.broadcasted_iota(jnp.int32, (nq, nk), 1) // SEL_BLOCK
    outs = []
    for g in range(N_KV):
        blocks = [blk_refs[g * k_eff + k][0] for k in range(k_eff)]
        keys = jnp.concatenate([x[:, 0:LANES] for x in blocks], axis=0).astype(BF16)
        vals = jnp.concatenate([x[:, LANES:2 * LANES] for x in blocks], axis=0).astype(BF16)
        from_cache = jnp.zeros((nq, nk), jnp.int32)
        new_selected = jnp.int32(0)
        for k in range(k_eff):
            j = idx_sm[b, g * k_eff + k]
            from_cache = jnp.where(slot == k, (j < n_cache_blocks).astype(jnp.int32), from_cache)
            new_selected = new_selected | (j == n_cache_blocks).astype(jnp.int32)
        s = jnp.where(from_cache > 0, _dot_nt(q, keys), NEG)
        s_new = jnp.where(new_selected > 0, s_new_all, NEG)
        m = jnp.maximum(jnp.max(s, axis=-1, keepdims=True), s_new)
        e = jnp.exp(s - m)
        e_new = jnp.exp(s_new - m)
        outs.append((_dot(e.astype(BF16), vals) + e_new * v_new)
                    * (1.0 / (jnp.sum(e, axis=-1, keepdims=True) + e_new)))
    row = lax.broadcasted_iota(jnp.int32, (nq, LANES), 0)
    lane = lax.broadcasted_iota(jnp.int32, (nq, LANES), 1)
    o_slc = jnp.where(row < HPG, outs[0], outs[1])
    br = br_ref[0]
    gates = [jnp.sum(jnp.where(lane == 3 * row + j, br, 0.0), axis=-1, keepdims=True) for j in range(3)]
    o_ref[0] = gates[0] * oc_ref[0] + gates[1] * o_slc + gates[2] * ow_ref[0]


def _nsa_step2(idx, page_table, q16, cache_blocks, new_slc, o_cmp, o_win, br, *, n_cache_blocks, k_eff):
    b, nq, _ = q16.shape
    bpp = PAGE_SIZE // SEL_BLOCK

    def blk_map(i, idx_sm, pt_sm, c):
        j = jnp.minimum(idx_sm[i, c], n_cache_blocks - 1)
        return (pt_sm[i, j // bpp] * bpp + j % bpp, 0, 0)

    def per_b(shape):
        return pl.BlockSpec((1,) + shape, lambda i, idx_sm, pt_sm: (i,) + (0,) * len(shape))

    n_blk = N_KV * k_eff
    in_specs = [per_b((nq, LANES))]
    in_specs += [pl.BlockSpec((1, SEL_BLOCK, KV_ROW), functools.partial(blk_map, c=c)) for c in range(n_blk)]
    in_specs += [per_b((1, KV_ROW)), per_b((nq, LANES)), per_b((nq, LANES)), per_b((1, LANES))]
    return pl.pallas_call(
        functools.partial(_nsa_step2_kernel, n_cache_blocks=n_cache_blocks, k_eff=k_eff),
        grid_spec=pltpu.PrefetchScalarGridSpec(
            num_scalar_prefetch=2,
            grid=(b,),
            in_specs=in_specs,
            out_specs=per_b((nq, LANES)),
        ),
        out_shape=jax.ShapeDtypeStruct((b, nq, LANES), F32),
        compiler_params=_cparams("arbitrary"),
        name="nsa_step_selected",
    )(idx, page_table, q16, *([cache_blocks] * n_blk), new_slc, o_cmp, o_win, br)


def _out_kernel(*refs, mul_ga):
    if mul_ga:
        x_ref, gate_ref, ya_ref, ga_ref, oret_ref, gr_ref, gret_ref, w_ref, y_ref = refs
        ya = (ya_ref[...] * ga_ref[...]).astype(BF16)
    else:
        x_ref, gate_ref, ya_ref, oret_ref, gr_ref, gret_ref, w_ref, y_ref = refs
        ya = ya_ref[...]
    o = oret_ref[...]
    parts = []
    for h in range(N_RET_HEADS):
        t = o[:, h * DV_RET:(h + 1) * DV_RET]
        ms = jnp.mean(t * t, axis=-1, keepdims=True)
        parts.append(t * lax.rsqrt(ms + EPS) * gret_ref[...])
    yr = (jnp.concatenate(parts, axis=-1) * gr_ref[...]).astype(BF16)
    y_ref[...] = x_ref[...] + gate_ref[0] * _dot(jnp.concatenate([ya, yr], axis=-1), w_ref[...])


def _output(x2, gate, ya, ga, o_ret, gr, g_ret, w_out, *, n_batch, per_row_mod):
    m = x2.shape[0]
    seq = m // n_batch
    tm = min(seq, 512)
    nt = seq // tm
    if per_row_mod:
        gate_spec = pl.BlockSpec((1, tm, D_MODEL), lambda i: (0, i, 0))
    else:
        gate_spec = pl.BlockSpec((1, 1, D_MODEL), lambda i: (i // nt, 0, 0))

    def rows_spec(width):
        return pl.BlockSpec((tm, width), lambda i: (i, 0))

    mul_ga = ga is not None
    args = [x2, gate, ya] + ([ga] if mul_ga else []) + [o_ret, gr, g_ret.reshape(1, DV_RET), w_out.astype(BF16)]
    in_specs = [rows_spec(D_MODEL), gate_spec, rows_spec(D_NSA)] + ([rows_spec(D_NSA)] if mul_ga else [])
    in_specs += [rows_spec(D_RET), rows_spec(D_RET), pl.BlockSpec((1, DV_RET), lambda i: (0, 0)),
                 pl.BlockSpec((D_MODEL, D_MODEL), lambda i: (0, 0))]
    return pl.pallas_call(
        functools.partial(_out_kernel, mul_ga=mul_ga),
        grid=(m // tm,),
        in_specs=in_specs,
        out_specs=rows_spec(D_MODEL),
        out_shape=jax.ShapeDtypeStruct((m, D_MODEL), F32),
        compiler_params=_cparams("arbitrary"),
        name="output_projection",
    )(*args)


def kernel(x_prompt, x_sample, c_prompt, c_sample, cache_cmp, cache_slc, state_win, state_ret, page_table, g_norm, w_ada, b_ada, w_in, g_q, g_kc, g_ks, g_kw, pe_ck, w_ck1, w_ck2, pe_cv, w_cv1, w_cv2, g_ret, w_out):
    assert g_norm.shape[0] == 1, "one layer"
    bp, seq, _ = x_prompt.shape
    bs, dec_seq, _ = x_sample.shape
    assert dec_seq == 1
    n_pages = page_table.shape[1]
    past = n_pages * PAGE_SIZE
    assert state_win.shape[2] == WINDOW and past >= WINDOW

    w_proj, gq_slots, gks2, gkw2 = _prep_proj_weights(w_in[0], g_q[0], g_ks[0], g_kw[0])
    w1x, w2x, pe8, gkc2 = _prep_compress_weights(pe_ck[0], w_ck1[0], w_ck2[0], pe_cv[0], w_cv1[0], w_cv2[0], g_kc[0])

    pad = (-(bp + bs)) % 8
    c_all = jnp.concatenate([c_prompt, c_sample, jnp.zeros((pad, D_MODEL), F32)], axis=0)
    mod = _modulation(c_all, w_ada[0], b_ada[0])
    shift_p, scale_p, gate_p = (mod[:bp, k * D_MODEL:(k + 1) * D_MODEL].reshape(bp, 1, D_MODEL) for k in range(3))
    shift_s, scale_s, gate_s = (mod[bp:bp + bs, k * D_MODEL:(k + 1) * D_MODEL].reshape(1, bs, D_MODEL)
                                for k in range(3))

    xp2 = x_prompt.reshape(bp * seq, D_MODEL)
    cos_p, sin_p = _rope_tables(seq, 0, 1)
    (q_p, cmp_p, slc_p, slcb_p, win_p, winb_p, br_p, ga_p, rq_p, rk_p, rv_p, gr_p) = _projection(
        xp2, scale_p, shift_p, g_norm[0], w_proj, gq_slots, gks2, gkw2, cos_p, sin_p, n_batch=bp, per_row_mod=False)

    a_p = _chunk_matmul_dense(cmp_p.reshape(bp, seq // CMP_STRIDE, CHUNK_LANES), w1x)
    kc_p, vc_p = _compress_combine(a_p, jnp.zeros((bp, 1, KV_ROW), F32), pe8, w1x, w2x, gkc2)
    ya_p = _nsa_prompt(q_p, kc_p, vc_p, slcb_p, winb_p, br_p, ga_p, n_batch=bp, seq=seq)
    oret_p, sret_p = _retention_prompt(rq_p, rk_p, rv_p, n_batch=bp, seq=seq)
    y_p = _output(xp2, gate_p, ya_p, None, oret_p.reshape(bp * seq, D_RET), gr_p, g_ret[0], w_out[0],
                  n_batch=bp, per_row_mod=False)

    xs2 = x_sample.reshape(bs, D_MODEL)
    cos_s, sin_s = _rope_tables(bs, past, 0)
    (q_s, cmp_s, slc_s, _, win_s, _, br_s, ga_s, rq_s, rk_s, rv_s, gr_s) = _projection(
        xs2, scale_s, shift_s, g_norm[0], w_proj, gq_slots, gks2, gkw2, cos_s, sin_s, n_batch=1, per_row_mod=True)

    n_chunk_cache = past // CMP_STRIDE
    n_cmp_s = n_chunk_cache
    n_sel_s = past // SEL_BLOCK + 1
    nsp = ((n_sel_s + LANES - 1) // LANES) * LANES
    pages_per_step = math.gcd(n_pages, 32)
    cache_chunks = cache_cmp[0].reshape(cache_cmp.shape[1], PAGE_SIZE // CMP_STRIDE, CHUNK_LANES)
    a_s = _chunk_matmul_paged(cache_chunks, page_table, w1x, pages_per_step)
    kc_s, vc_s = _compress_combine(a_s, cmp_s.reshape(bs, 1, KV_ROW), pe8, w1x, w2x, gkc2)

    q16 = jnp.pad(q_s[0].transpose(1, 0, 2), ((0, 0), (0, 16 - N_HEADS), (0, 0)))
    mimp_s = _importance_matrix(n_cmp_s, n_cmp_s, n_sel_s, nsp)
    o_cmp_s, o_win_s, idx_s = _nsa_step1(q16, kc_s, vc_s, state_win[0].reshape(bs, WINDOW, KV_ROW),
                                         win_s.reshape(bs, 1, KV_ROW), mimp_s, t=past, n_cmp=n_cmp_s,
                                         n_sel=n_sel_s)
    k_eff = min(TOP_N, n_sel_s)
    idx_flat = idx_s[:, :N_KV, :k_eff].reshape(bs, N_KV * k_eff)
    cache_blocks = cache_slc[0].reshape(cache_slc.shape[1] * (PAGE_SIZE // SEL_BLOCK), SEL_BLOCK, KV_ROW)
    o8 = _nsa_step2(idx_flat, page_table, q16, cache_blocks, slc_s.reshape(bs, 1, KV_ROW), o_cmp_s, o_win_s,
                    br_s.reshape(bs, 1, LANES), n_cache_blocks=past // SEL_BLOCK, k_eff=k_eff)
    o_nsa_s = jnp.concatenate([o8[:, :HPG, :HEAD_DIM], o8[:, HPG:N_HEADS, HEAD_DIM:]], axis=1).reshape(bs, D_NSA)

    oret_s, sret_s = _retention_step(rq_s, rk_s, rv_s, state_ret[0])
    y_s = _output(xs2, gate_s, o_nsa_s, ga_s, oret_s.reshape(bs, D_RET), gr_s, g_ret[0], w_out[0],
                  n_batch=1, per_row_mod=True)

    def kv_rows(a, b, l):
        return a.reshape(1, b, l, 2, N_KV, HEAD_DIM)

    keep_p = min(WINDOW, seq)
    s_win = jnp.concatenate([state_win[0][:, 1:].reshape(bs, WINDOW - 1, KV_ROW), win_s.reshape(bs, 1, KV_ROW)],
                            axis=1)
    return (y_p.reshape(bp, seq, D_MODEL), y_s.reshape(bs, 1, D_MODEL),
            kv_rows(cmp_p, bp, seq), kv_rows(slc_p, bp, seq),
            kv_rows(win_p.reshape(bp, seq, KV_ROW)[:, seq - keep_p:], bp, keep_p), sret_p[None],
            kv_rows(cmp_s, bs, 1), kv_rows(slc_s, bs, 1), kv_rows(s_win, bs, WINDOW), sret_s[None])
```

```python
import functools
import math

import jax
import jax.numpy as jnp
import numpy as np
from jax import lax
from jax.experimental import pallas as pl
from jax.experimental.pallas import tpu as pltpu

D_MODEL = 1024
HEAD_DIM = 64
D_NSA = 512
N_HEADS = 8
N_KV = 2
HPG = 4
CMP_BLOCK = 32
CMP_STRIDE = 16
SEL_BLOCK = 64
TOP_N = 16
WINDOW = 512
D_RET = 512
N_RET_HEADS = 4
DV_RET = 128
DK_RET = 64
RET_CHUNK = 128
PAGE_SIZE = 128
ROPE_BASE = 10000.0
EPS = 1e-6
NEG = -1e30
FORCE = 1e4
BELOW_NEG = -3e38

KV_ROW = 2 * N_KV * HEAD_DIM
CHUNK_LANES = CMP_STRIDE * KV_ROW
LANES = 128
VMEM_LIMIT_BYTES = 56 * 1024 * 1024

F32 = jnp.float32
BF16 = jnp.bfloat16

O_Q = 0
O_CMP = O_Q + N_HEADS * LANES
O_SLC = O_CMP + KV_ROW
O_WIN = O_SLC + KV_ROW
O_BR = O_WIN + KV_ROW
O_GA = O_BR + LANES
O_RQ = O_GA + D_NSA
O_RK = O_RQ + 2 * N_RET_HEADS * DK_RET
O_RV = O_RK + 2 * N_RET_HEADS * DK_RET
O_GR = O_RV + D_RET
D_IN_PAD = O_GR + D_RET


def _cparams(*sem):
    return pltpu.CompilerParams(dimension_semantics=sem, vmem_limit_bytes=VMEM_LIMIT_BYTES)


def _dot(a, b):
    return jnp.dot(a, b, preferred_element_type=F32)


def _dot_nt(a, b):
    return lax.dot_general(a, b, (((1,), (1,)), ((), ())), preferred_element_type=F32)


def _dot_tn(a, b):
    return lax.dot_general(a, b, (((0,), (0,)), ((), ())), preferred_element_type=F32)


def _silu(x):
    return x * jax.nn.sigmoid(x)


def _split3_dot(p, m):
    p1 = p.astype(BF16)
    r1 = p - p1.astype(F32)
    p2 = r1.astype(BF16)
    p3 = (r1 - p2.astype(F32)).astype(BF16)
    return _dot(p1, m) + _dot(p2, m) + _dot(p3, m)


def _pair_rmsnorm(k, gain2):
    lo = lax.broadcasted_iota(jnp.int32, k.shape, 1) < HEAD_DIM
    sq = k * k
    s0 = jnp.sum(jnp.where(lo, sq, 0.0), axis=-1, keepdims=True)
    s1 = jnp.sum(jnp.where(lo, 0.0, sq), axis=-1, keepdims=True)
    ms = jnp.where(lo, s0, s1) * (1.0 / HEAD_DIM)
    return k * lax.rsqrt(ms + EPS) * gain2


def _mod_kernel(c_ref, w_ref, b_ref, o_ref):
    o_ref[...] = _dot(_silu(c_ref[...]).astype(BF16), w_ref[...]) + b_ref[...]


def _modulation(c_all, w_ada, b_ada):
    rows = c_all.shape[0]
    n = w_ada.shape[1]
    tn = 1024
    return pl.pallas_call(
        _mod_kernel,
        grid=(n // tn,),
        in_specs=[pl.BlockSpec((rows, D_MODEL), lambda j: (0, 0)),
                  pl.BlockSpec((D_MODEL, tn), lambda j: (0, j)),
                  pl.BlockSpec((1, tn), lambda j: (0, j))],
        out_specs=pl.BlockSpec((rows, tn), lambda j: (0, j)),
        out_shape=jax.ShapeDtypeStruct((rows, n), F32),
        compiler_params=_cparams("arbitrary"),
        name="adaln_modulation",
    )(c_all, w_ada.astype(BF16), b_ada.reshape(1, n))


def _rope_kernel(freq_ref, cos_ref, sin_ref, *, base, stride, rows):
    shape = (rows, LANES)
    r = lax.broadcasted_iota(jnp.int32, shape, 0)
    pos = (base + stride * (pl.program_id(0) * rows + r)).astype(F32)
    ang = pos * freq_ref[...]
    lane = lax.broadcasted_iota(jnp.int32, shape, 1)
    sign = jnp.where((lane & (DK_RET - 1)) < DK_RET // 2, -1.0, 1.0)
    cos_ref[...] = jnp.cos(ang)
    sin_ref[...] = jnp.sin(ang) * sign


def _rope_tables(n_rows, base, stride):
    half = DK_RET // 2
    freqs = ROPE_BASE ** (-jnp.arange(half, dtype=F32) / half)
    freq_row = jnp.tile(freqs, LANES // half).reshape(1, LANES)
    rows = min(n_rows, 512)
    return pl.pallas_call(
        functools.partial(_rope_kernel, base=base, stride=stride, rows=rows),
        grid=(n_rows // rows,),
        in_specs=[pl.BlockSpec((1, LANES), lambda i: (0, 0))],
        out_specs=[pl.BlockSpec((rows, LANES), lambda i: (i, 0))] * 2,
        out_shape=[jax.ShapeDtypeStruct((n_rows, LANES), F32)] * 2,
        compiler_params=_cparams("arbitrary"),
        name="rope_tables",
    )(freq_row)


def _proj_kernel(x_ref, scale_ref, shift_ref, gn_ref, w_ref, gq_ref, gks_ref, gkw_ref, cos_ref, sin_ref,
                 q_ref, cmp_ref, slc_ref, slcb_ref, win_ref, winb_ref, br_ref, ga_ref, rq_ref, rk_ref,
                 rv_ref, gr_ref):
    x = x_ref[...]
    ms = jnp.mean(x * x, axis=-1, keepdims=True)
    xn = x * lax.rsqrt(ms + EPS) * gn_ref[...]
    hb = (xn * (1.0 + scale_ref[0]) + shift_ref[0]).astype(BF16)

    def seg(start, width):
        return _dot(hb, w_ref[:, start:start + width])

    yq = seg(O_Q, N_HEADS * LANES)
    for s in range(N_HEADS):
        t = yq[:, s * LANES:(s + 1) * LANES]
        ms = jnp.sum(t * t, axis=-1, keepdims=True) * (1.0 / HEAD_DIM)
        q_ref[0, s] = (t * lax.rsqrt(ms + EPS) * gq_ref[s]).astype(BF16)

    cmp_ref[...] = seg(O_CMP, KV_ROW)

    y = seg(O_SLC, KV_ROW)
    y = jnp.concatenate([_pair_rmsnorm(y[:, :LANES], gks_ref[...]), y[:, LANES:]], axis=-1)
    slc_ref[...] = y
    slcb_ref[...] = y.astype(BF16)

    y = seg(O_WIN, KV_ROW)
    y = jnp.concatenate([_pair_rmsnorm(y[:, :LANES], gkw_ref[...]), y[:, LANES:]], axis=-1)
    win_ref[...] = y
    winb_ref[...] = y.astype(BF16)

    br_ref[...] = jax.nn.sigmoid(seg(O_BR, LANES))
    ga_ref[...] = _silu(seg(O_GA, D_NSA))

    cos2 = jnp.concatenate([cos_ref[...]] * 2, axis=-1)
    sin2 = jnp.concatenate([sin_ref[...]] * 2, axis=-1)
    nrot = N_RET_HEADS * DK_RET
    y = seg(O_RQ, 2 * nrot)
    rq_ref[...] = (y[:, :nrot] * cos2 + y[:, nrot:] * sin2).astype(BF16)
    y = seg(O_RK, 2 * nrot)
    rk_ref[...] = ((y[:, :nrot] * cos2 + y[:, nrot:] * sin2) * DK_RET ** -0.5).astype(BF16)
    rv_ref[...] = seg(O_RV, D_RET).astype(BF16)
    gr_ref[...] = _silu(seg(O_GR, D_RET))


def _swap_halves(w):
    k = w.shape[0]
    w4 = w.reshape(k, N_RET_HEADS, 2, DK_RET // 2)
    return w4[:, :, ::-1, :].reshape(k, N_RET_HEADS * DK_RET)


def _prep_proj_weights(w_in, g_q, g_ks, g_kw):
    k = w_in.shape[0]
    o = 0
    wq = w_in[:, o:o + D_NSA].reshape(k, N_KV, HPG, HEAD_DIM)
    o += D_NSA
    z = jnp.zeros_like(wq)
    wq = jnp.stack([jnp.concatenate([wq[:, 0], z[:, 0]], axis=-1),
                    jnp.concatenate([z[:, 1], wq[:, 1]], axis=-1)], axis=1).reshape(k, N_HEADS * LANES)
    w_kv = w_in[:, o:o + 3 * KV_ROW]
    o += 3 * KV_ROW
    n_br = 3 * N_HEADS
    w_br = jnp.pad(w_in[:, o:o + n_br], ((0, 0), (0, LANES - n_br)))
    o += n_br
    w_ga = w_in[:, o:o + D_NSA]
    o += D_NSA
    nrot = N_RET_HEADS * DK_RET
    w_rq = w_in[:, o:o + nrot]
    o += nrot
    w_rk = w_in[:, o:o + nrot]
    o += nrot
    w_rv = w_in[:, o:o + D_RET]
    o += D_RET
    w_gr = w_in[:, o:o + D_RET]
    w = jnp.concatenate([wq, w_kv, w_br, w_ga, w_rq, _swap_halves(w_rq), w_rk, _swap_halves(w_rk), w_rv, w_gr],
                        axis=1).astype(BF16)
    assert w.shape[1] == D_IN_PAD
    gz = jnp.zeros((HEAD_DIM,), F32)
    gq = g_q * HEAD_DIM ** -0.5
    gq_slots = jnp.stack([jnp.concatenate([gq, gz]) if s < HPG else jnp.concatenate([gz, gq])
                          for s in range(N_HEADS)]).reshape(N_HEADS, 1, LANES)
    return w, gq_slots, jnp.tile(g_ks, 2).reshape(1, LANES), jnp.tile(g_kw, 2).reshape(1, LANES)


def _projection(x2, scale, shift, g_norm, w, gq_slots, gks2, gkw2, cos_t, sin_t, *, n_batch, per_row_mod):
    m = x2.shape[0]
    seq = m // n_batch
    tm = min(seq, 256)
    nt = seq // tm
    if per_row_mod:
        mod_spec = pl.BlockSpec((1, tm, D_MODEL), lambda i: (0, i, 0))
    else:
        mod_spec = pl.BlockSpec((1, 1, D_MODEL), lambda i: (i // nt, 0, 0))

    def rows_spec(width):
        return pl.BlockSpec((tm, width), lambda i: (i, 0))

    def const_spec(shape):
        return pl.BlockSpec(shape, lambda i: (0,) * len(shape))

    widths = [(KV_ROW, F32), (KV_ROW, F32), (KV_ROW, BF16), (KV_ROW, F32), (KV_ROW, BF16), (LANES, F32),
              (D_NSA, F32), (N_RET_HEADS * DK_RET, BF16), (N_RET_HEADS * DK_RET, BF16), (D_RET, BF16),
              (D_RET, F32)]
    out_shape = [jax.ShapeDtypeStruct((n_batch, N_HEADS, seq, LANES), BF16)]
    out_specs = [pl.BlockSpec((1, N_HEADS, tm, LANES), lambda i: (i // nt, 0, i % nt, 0))]
    for width, dt in widths:
        out_shape.append(jax.ShapeDtypeStruct((m, width), dt))
        out_specs.append(rows_spec(width))
    return pl.pallas_call(
        _proj_kernel,
        grid=(m // tm,),
        in_specs=[rows_spec(D_MODEL), mod_spec, mod_spec, const_spec((1, D_MODEL)),
                  const_spec((D_MODEL, D_IN_PAD)), const_spec((N_HEADS, 1, LANES)), const_spec((1, LANES)),
                  const_spec((1, LANES)), pl.BlockSpec((tm, LANES), lambda i: (i % nt, 0)),
                  pl.BlockSpec((tm, LANES), lambda i: (i % nt, 0))],
        out_specs=out_specs,
        out_shape=out_shape,
        compiler_params=_cparams("arbitrary"),
        name="input_projection",
    )(x2, scale, shift, g_norm.reshape(1, D_MODEL), w, gq_slots, gks2, gkw2, cos_t, sin_t)


def _chunk_matmul_kernel(x_ref, w_ref, o_ref):
    o_ref[0] = _dot(x_ref[0].astype(BF16), w_ref[...])


def _chunk_matmul_dense(chunks, w1x):
    b, n, _ = chunks.shape
    return pl.pallas_call(
        _chunk_matmul_kernel,
        grid=(b,),
        in_specs=[pl.BlockSpec((1, n, CHUNK_LANES), lambda i: (i, 0, 0)),
                  pl.BlockSpec(w1x.shape, lambda i: (0, 0))],
        out_specs=pl.BlockSpec((1, n, w1x.shape[1]), lambda i: (i, 0, 0)),
        out_shape=jax.ShapeDtypeStruct((b, n, w1x.shape[1]), F32),
        compiler_params=_cparams("arbitrary"),
        name="compress_chunks_prompt",
    )(chunks, w1x)


def _chunk_matmul_paged_kernel(pt_ref, *refs, n_in):
    x_refs = refs[:n_in]
    w_ref, o_ref = refs[n_in:n_in + 2]
    rows_sc = refs[n_in + 2:]
    for k in range(n_in):
        for kv in range(2):
            rows_sc[kv][k * PAGE_SIZE:(k + 1) * PAGE_SIZE, :] = x_refs[k][0, kv].reshape(LANES, PAGE_SIZE).T
    n_rows = n_in * PAGE_SIZE // CMP_STRIDE
    for kv in range(2):
        x = jnp.concatenate([rows_sc[kv][pl.ds(l, n_rows, stride=CMP_STRIDE), :] for l in range(CMP_STRIDE)],
                            axis=-1).astype(BF16)
        o_ref[0, :, kv * KV_ROW:(kv + 1) * KV_ROW] = _dot(x, w_ref[kv])


def _chunk_matmul_paged(cache_t, page_table, w1kv, pages_per_step):
    b, n_pages = page_table.shape
    cpp = PAGE_SIZE // CMP_STRIDE
    p = pages_per_step
    page_shape = (1,) + cache_t.shape[1:]
    in_specs = [pl.BlockSpec(page_shape, lambda i, j, pt, k=k: (pt[i, j * p + k], 0, 0, 0, 0)) for k in range(p)]
    in_specs.append(pl.BlockSpec(w1kv.shape, lambda i, j, pt: (0, 0, 0)))
    return pl.pallas_call(
        functools.partial(_chunk_matmul_paged_kernel, n_in=p),
        grid_spec=pltpu.PrefetchScalarGridSpec(
            num_scalar_prefetch=1,
            grid=(b, n_pages // p),
            in_specs=in_specs,
            out_specs=pl.BlockSpec((1, p * cpp, 2 * KV_ROW), lambda i, j, pt: (i, j, 0)),
            scratch_shapes=[pltpu.VMEM((p * PAGE_SIZE, LANES), F32)] * 2,
        ),
        out_shape=jax.ShapeDtypeStruct((b, n_pages * cpp, 2 * KV_ROW), F32),
        compiler_params=_cparams("arbitrary", "arbitrary"),
        name="compress_chunks_paged",
    )(page_table, *([cache_t] * p), w1kv)


def _halves(a):
    lo = jnp.concatenate([a[:, 0:LANES], a[:, 2 * LANES:3 * LANES]], axis=-1)
    hi = jnp.concatenate([a[:, LANES:2 * LANES], a[:, 3 * LANES:4 * LANES]], axis=-1)
    return lo, hi


def _compress_combine_kernel(a_ref, new_ref, pe_ref, w1_ref, w2_ref, gkc_ref, kc_ref, vc_ref, *, n):
    lo, hi = _halves(a_ref[0])
    pe_lo, pe_hi = _halves(_dot(pe_ref[...], w1_ref[...]))
    bias = pe_lo[0:1] + pe_hi[1:2]
    new8 = jnp.broadcast_to(new_ref[0], (8, KV_ROW)).astype(BF16)
    a_new = _halves(_dot(new8, w1_ref[0:KV_ROW, :]))[1][0:1]
    hi = pltpu.roll(hi, n - 1, axis=0)
    row = lax.broadcasted_iota(jnp.int32, (n, KV_ROW), 0)
    hi = jnp.where(row == n - 1, a_new, hi)
    out = _dot(_silu(lo + hi + bias).astype(BF16), w2_ref[...])
    kc_ref[0] = _pair_rmsnorm(out[:, :LANES], gkc_ref[...]).astype(BF16)
    vc_ref[0] = out[:, LANES:].astype(BF16)


def _compress_combine(a, new_rows, pe8, w1x, w2x, gkc2):
    b, n, _ = a.shape
    return pl.pallas_call(
        functools.partial(_compress_combine_kernel, n=n),
        grid=(b,),
        in_specs=[pl.BlockSpec((1, n, 2 * KV_ROW), lambda i: (i, 0, 0)),
                  pl.BlockSpec((1, 1, KV_ROW), lambda i: (i, 0, 0)),
                  pl.BlockSpec(pe8.shape, lambda i: (0, 0)),
                  pl.BlockSpec(w1x.shape, lambda i: (0, 0)),
                  pl.BlockSpec(w2x.shape, lambda i: (0, 0)),
                  pl.BlockSpec((1, LANES), lambda i: (0, 0))],
        out_specs=[pl.BlockSpec((1, n, LANES), lambda i: (i, 0, 0))] * 2,
        out_shape=[jax.ShapeDtypeStruct((b, n, LANES), BF16)] * 2,
        compiler_params=_cparams("arbitrary"),
        name="compress_combine",
    )(a, new_rows, pe8, w1x, w2x, gkc2)


def _prep_compress_weights(pe_ck, w_ck1, w_ck2, pe_cv, w_cv1, w_cv2, g_kc):
    eye = jnp.eye(N_KV, dtype=F32)
    w1 = jnp.stack([w_ck1, w_cv1]).reshape(2, 2, CMP_STRIDE, HEAD_DIM, HEAD_DIM)
    w1x = jnp.einsum('khldf,kq,gp->lkgdqhpf', w1, eye, eye).reshape(CHUNK_LANES, 2 * KV_ROW).astype(BF16)
    w1kv = jnp.einsum('khldf,gp->klgdhpf', w1, eye).reshape(2, CMP_STRIDE * LANES, KV_ROW).astype(BF16)
    w2 = jnp.stack([w_ck2, w_cv2])
    w2x = jnp.einsum('kfd,kq,gp->kgfqpd', w2, eye, eye).reshape(KV_ROW, KV_ROW).astype(BF16)
    pe = jnp.stack([pe_ck, pe_cv]).reshape(2, 2, CMP_STRIDE, HEAD_DIM)
    pe_rows = jnp.broadcast_to(pe.transpose(1, 2, 0, 3)[:, :, :, None, :],
                               (2, CMP_STRIDE, 2, N_KV, HEAD_DIM)).reshape(2, CHUNK_LANES)
    pe8 = jnp.zeros((8, CHUNK_LANES), F32).at[:2].set(pe_rows).astype(BF16)
    return w1x, w1kv, w2x, pe8, jnp.tile(g_kc, 2).reshape(1, LANES)


def _importance_matrix(n_cmp_pad, n_cmp, n_sel, n_sel_pad):
    ratio = SEL_BLOCK // CMP_STRIDE
    span = CMP_BLOCK // CMP_STRIDE
    m = np.zeros((n_cmp_pad, n_sel_pad), np.float32)
    for j in range(n_sel):
        for r in range(ratio):
            for s in range(span):
                n = ratio * j + r - s
                if 0 <= n < n_cmp:
                    m[n, j] += 1.0
    return jnp.asarray(m, BF16)


def _nsa_prompt_kernel(q_ref, kc_ref, vc_ref, slc_ref, win_ref, br_ref, ga_ref, mimp_ref, exp_ref, o_ref,
                       m_sc, l_sc, acc_sc, *, tq, tk, n_cmp, n_sel):
    qs = pl.program_id(1) * tq
    rows = HPG * tq
    ncp = kc_ref.shape[1]
    wlen = WINDOW + tq
    br = br_ref[...]
    lane_lo = lax.broadcasted_iota(jnp.int32, (tq, LANES), 1) < HEAD_DIM

    def row_pos(shape):
        return qs + (lax.broadcasted_iota(jnp.int32, shape, 0) & (tq - 1))

    for g in range(N_KV):
        q = q_ref[0, HPG * g:HPG * (g + 1)].reshape(rows, LANES)

        s = _dot_nt(q, kc_ref[0])
        col = lax.broadcasted_iota(jnp.int32, (rows, ncp), 1)
        mask = (col * CMP_STRIDE + (CMP_BLOCK - 1) <= row_pos((rows, ncp))) & (col < n_cmp)
        s = jnp.where(mask, s, NEG)
        e = jnp.exp(s - jnp.max(s, axis=-1, keepdims=True))
        p = jnp.where(mask, e, 0.0) * (1.0 / jnp.sum(e, axis=-1, keepdims=True))
        o_cmp = _dot(p.astype(BF16), vc_ref[0])
        p_sum = p[0:tq] + p[tq:2 * tq] + p[2 * tq:3 * tq] + p[3 * tq:4 * tq]
        imp = _split3_dot(p_sum, mimp_ref[...])

        blk = lax.broadcasted_iota(jnp.int32, (tq, LANES), 1)
        t = qs + lax.broadcasted_iota(jnp.int32, (tq, LANES), 0)
        jt = t // SEL_BLOCK
        forced = (blk == 0) | (blk == jt) | (blk == jt - 1)
        score = jnp.where(blk * SEL_BLOCK <= t, jnp.where(forced, FORCE, imp), NEG)
        score = jnp.where(blk < n_sel, score, BELOW_NEG)
        n_rank = ((n_sel + 7) // 8) * 8
        s_t = score.T[:n_rank]
        j_row = lax.broadcasted_iota(jnp.int32, (n_rank, tq), 0)
        beaten = jnp.zeros((n_rank, tq), F32)
        for i in range(n_sel):
            r = s_t[i:i + 1, :]
            beaten = beaten + jnp.where((r > s_t) | ((r == s_t) & (i < j_row)), 1.0, 0.0)
        sel_t = jnp.where(beaten < min(TOP_N, n_sel), 1.0, 0.0)
        if n_rank < LANES:
            sel_t = jnp.concatenate([sel_t, jnp.zeros((LANES - n_rank, tq), F32)], axis=0)
        sel = sel_t.T.astype(BF16)

        m_sc[...] = jnp.full((rows, LANES), NEG, F32)
        l_sc[...] = jnp.zeros((rows, LANES), F32)
        acc_sc[...] = jnp.zeros((rows, LANES), F32)

        def key_tile(kt, carry):
            k0 = pl.multiple_of(kt * tk, tk)
            k = slc_ref[0, pl.ds(k0, tk), 0:LANES]
            v = slc_ref[0, pl.ds(k0, tk), LANES:2 * LANES]
            s = _dot_nt(q, k)
            sel_keys = _dot(sel, exp_ref[kt])
            kpos = k0 + lax.broadcasted_iota(jnp.int32, (tq, tk), 1)
            tt = qs + lax.broadcasted_iota(jnp.int32, (tq, tk), 0)
            allowed = (sel_keys > 0.5) & (kpos <= tt)
            s = jnp.concatenate([jnp.where(allowed, s[h * tq:(h + 1) * tq], NEG) for h in range(HPG)], axis=0)
            m_prev = m_sc[...]
            m_next = jnp.maximum(m_prev, jnp.max(s, axis=-1, keepdims=True))
            alpha = jnp.exp(m_prev - m_next)
            p = jnp.exp(s - m_next[:, 0:1])
            l_sc[...] = alpha * l_sc[...] + jnp.sum(p, axis=-1, keepdims=True)
            acc_sc[...] = alpha * acc_sc[...] + _dot(p.astype(BF16), v)
            m_sc[...] = m_next
            return carry

        lax.fori_loop(0, qs // tk + 1, key_tile, 0)
        o_slc = acc_sc[...] * (1.0 / l_sc[...])

        w0 = pl.multiple_of(jnp.maximum(qs - WINDOW, 0), tq)
        k = win_ref[0, pl.ds(w0, wlen), 0:LANES]
        v = win_ref[0, pl.ds(w0, wlen), LANES:2 * LANES]
        s = _dot_nt(q, k)
        kpos = w0 + lax.broadcasted_iota(jnp.int32, (rows, wlen), 1)
        tt = row_pos((rows, wlen))
        s = jnp.where((kpos <= tt) & (kpos > tt - WINDOW), s, NEG)
        e = jnp.exp(s - jnp.max(s, axis=-1, keepdims=True))
        o_win = _dot(e.astype(BF16), v) * (1.0 / jnp.sum(e, axis=-1, keepdims=True))

        heads = []
        for h in range(HPG):
            rs = slice(h * tq, (h + 1) * tq)
            c = 3 * (HPG * g + h)
            heads.append(br[:, c:c + 1] * o_cmp[rs] + br[:, c + 1:c + 2] * o_slc[rs] + br[:, c + 2:c + 3] * o_win[rs])
        for hh in range(HPG // 2):
            even, odd = heads[2 * hh], heads[2 * hh + 1]
            if g == 0:
                odd = pltpu.roll(odd, HEAD_DIM, axis=1)
            else:
                even = pltpu.roll(even, HEAD_DIM, axis=1)
            c0 = (2 * g + hh) * LANES
            o_ref[:, c0:c0 + LANES] = (jnp.where(lane_lo, even, odd) * ga_ref[:, c0:c0 + LANES]).astype(BF16)


def _nsa_prompt(q_slots, kc, vc, slc_b, win_b, br, ga, *, n_batch, seq):
    tq = 128
    tk = min(512, seq)
    n_cmp = seq // CMP_STRIDE - (CMP_BLOCK // CMP_STRIDE) + 1
    n_sel = seq // SEL_BLOCK
    assert n_sel <= LANES and seq % tk == 0 and tk % tq == 0 and seq >= WINDOW + tq
    ncp = kc.shape[1]
    mimp = _importance_matrix(ncp, n_cmp, n_sel, LANES)
    key_blk = (jnp.arange(seq, dtype=jnp.int32) // SEL_BLOCK).reshape(seq // tk, 1, tk)
    expand = (key_blk == jnp.arange(LANES, dtype=jnp.int32).reshape(1, LANES, 1)).astype(BF16)
    nq = seq // tq
    rows = HPG * tq
    return pl.pallas_call(
        functools.partial(_nsa_prompt_kernel, tq=tq, tk=tk, n_cmp=n_cmp, n_sel=n_sel),
        grid=(n_batch, nq),
        in_specs=[pl.BlockSpec((1, N_HEADS, tq, LANES), lambda b, i: (b, 0, i, 0)),
                  pl.BlockSpec((1, ncp, LANES), lambda b, i: (b, 0, 0)),
                  pl.BlockSpec((1, ncp, LANES), lambda b, i: (b, 0, 0)),
                  pl.BlockSpec((1, seq, KV_ROW), lambda b, i: (b, 0, 0)),
                  pl.BlockSpec((1, seq, KV_ROW), lambda b, i: (b, 0, 0)),
                  pl.BlockSpec((tq, LANES), lambda b, i: (b * nq + i, 0)),
                  pl.BlockSpec((tq, D_NSA), lambda b, i: (b * nq + i, 0)),
                  pl.BlockSpec(mimp.shape, lambda b, i: (0, 0)),
                  pl.BlockSpec(expand.shape, lambda b, i: (0, 0, 0))],
        out_specs=pl.BlockSpec((tq, D_NSA), lambda b, i: (b * nq + i, 0)),
        out_shape=jax.ShapeDtypeStruct((n_batch * seq, D_NSA), BF16),
        scratch_shapes=[pltpu.VMEM((rows, LANES), F32)] * 3,
        compiler_params=_cparams("arbitrary", "arbitrary"),
        name="nsa_prompt",
    )(q_slots, kc, vc, slc_b.reshape(n_batch, seq, KV_ROW), win_b.reshape(n_batch, seq, KV_ROW), br, ga, mimp,
      expand)


def _retention_prompt_kernel(rq_ref, rk_ref, rv_ref, o_ref, s_ref, *, cl):
    @pl.when(pl.program_id(1) == 0)
    def _():
        s_ref[...] = jnp.zeros(s_ref.shape, F32)

    i_pos = lax.broadcasted_iota(jnp.int32, (cl, cl), 0)
    j_pos = lax.broadcasted_iota(jnp.int32, (cl, cl), 1)
    diff = (i_pos - j_pos).astype(F32)
    i_col = lax.broadcasted_iota(jnp.int32, (cl, 1), 0).astype(F32)
    lane_lo = lax.broadcasted_iota(jnp.int32, (cl, LANES), 1) < DK_RET
    for pair in range(N_RET_HEADS // 2):
        q2 = rq_ref[0, :, pair * LANES:(pair + 1) * LANES]
        k2 = rk_ref[0, :, pair * LANES:(pair + 1) * LANES]
        s_pair = jnp.concatenate([s_ref[0, 2 * pair], s_ref[0, 2 * pair + 1]], axis=0)
        for odd in range(2):
            h = 2 * pair + odd
            log_g = math.log(1.0 - 2.0 ** (-5.0 - h))
            keep = lane_lo if odd == 0 else jnp.logical_not(lane_lo)
            qh = jnp.where(keep, q2, jnp.zeros_like(q2))
            v = rv_ref[0, :, h * DV_RET:(h + 1) * DV_RET]
            decay = jnp.where(diff >= 0, jnp.exp(log_g * jnp.maximum(diff, 0.0)), 0.0)
            a = _dot_nt(qh, k2) * decay
            xi = jnp.exp(log_g * (i_col + 1.0))
            zeta = jnp.exp(log_g * (cl - 1.0 - i_col))
            o_ref[0, :, h * DV_RET:(h + 1) * DV_RET] = (_dot(a.astype(BF16), v)
                                                         + _dot(qh, s_pair.astype(BF16)) * xi)
            kz = (k2.astype(F32) * zeta).astype(BF16)
            u = _dot_tn(kz, v)[odd * DK_RET:(odd + 1) * DK_RET]
            s_ref[0, h] = s_ref[0, h] * math.exp(log_g * cl) + u


def _retention_prompt(rq, rk, rv, *, n_batch, seq):
    cl = math.gcd(seq, RET_CHUNK)
    nrot = N_RET_HEADS * DK_RET
    return pl.pallas_call(
        functools.partial(_retention_prompt_kernel, cl=cl),
        grid=(n_batch, seq // cl),
        in_specs=[pl.BlockSpec((1, cl, nrot), lambda b, i: (b, i, 0)),
                  pl.BlockSpec((1, cl, nrot), lambda b, i: (b, i, 0)),
                  pl.BlockSpec((1, cl, D_RET), lambda b, i: (b, i, 0))],
        out_specs=[pl.BlockSpec((1, cl, D_RET), lambda b, i: (b, i, 0)),
                   pl.BlockSpec((1, N_RET_HEADS, DK_RET, DV_RET), lambda b, i: (b, 0, 0, 0))],
        out_shape=[jax.ShapeDtypeStruct((n_batch, seq, D_RET), F32),
                   jax.ShapeDtypeStruct((n_batch, N_RET_HEADS, DK_RET, DV_RET), F32)],
        compiler_params=_cparams("arbitrary", "arbitrary"),
        name="retention_prompt",
    )(rq.reshape(n_batch, seq, nrot), rk.reshape(n_batch, seq, nrot), rv.reshape(n_batch, seq, D_RET))


def _retention_step_kernel(rq_ref, rk_ref, rv_ref, s0_ref, o_ref, s_ref):
    lane_lo = lax.broadcasted_iota(jnp.int32, (8, LANES), 1) < DK_RET
    diag = (lax.broadcasted_iota(jnp.int32, (LANES, LANES), 0)
            == lax.broadcasted_iota(jnp.int32, (LANES, LANES), 1))
    for pair in range(N_RET_HEADS // 2):
        q2 = jnp.broadcast_to(rq_ref[0, :, pair * LANES:(pair + 1) * LANES], (8, LANES))
        k_row = rk_ref[0, :, pair * LANES:(pair + 1) * LANES]
        k2 = jnp.broadcast_to(k_row, (8, LANES))
        k_col = jnp.sum(jnp.where(diag, jnp.broadcast_to(k_row, (LANES, LANES)), 0.0), axis=-1, keepdims=True)
        s_pair = jnp.concatenate([s0_ref[0, 2 * pair], s0_ref[0, 2 * pair + 1]], axis=0)
        for odd in range(2):
            h = 2 * pair + odd
            g = math.exp(math.log(1.0 - 2.0 ** (-5.0 - h)))
            keep = lane_lo if odd == 0 else jnp.logical_not(lane_lo)
            qh = jnp.where(keep, q2, 0.0)
            v_row = rv_ref[0, :, h * DV_RET:(h + 1) * DV_RET]
            qk = jnp.sum(qh * k2, axis=-1, keepdims=True)
            o = qk * v_row + _dot(qh, s_pair) * g
            o_ref[0, :, h * DV_RET:(h + 1) * DV_RET] = o[0:1]
            u = (k_col * v_row)[odd * DK_RET:(odd + 1) * DK_RET]
            s_ref[0, h] = s0_ref[0, h] * g + u


def _retention_step(rq, rk, rv, s0):
    b = rq.shape[0]
    nrot = N_RET_HEADS * DK_RET
    state_spec = pl.BlockSpec((1, N_RET_HEADS, DK_RET, DV_RET), lambda i: (i, 0, 0, 0))
    return pl.pallas_call(
        _retention_step_kernel,
        grid=(b,),
        in_specs=[pl.BlockSpec((1, 1, nrot), lambda i: (i, 0, 0)),
                  pl.BlockSpec((1, 1, nrot), lambda i: (i, 0, 0)),
                  pl.BlockSpec((1, 1, D_RET), lambda i: (i, 0, 0)),
                  state_spec],
        out_specs=[pl.BlockSpec((1, 1, D_RET), lambda i: (i, 0, 0)), state_spec],
        out_shape=[jax.ShapeDtypeStruct((b, 1, D_RET), F32),
                   jax.ShapeDtypeStruct((b, N_RET_HEADS, DK_RET, DV_RET), F32)],
        compiler_params=_cparams("arbitrary"),
        name="retention_step",
    )(rq.astype(F32).reshape(b, 1, nrot), rk.astype(F32).reshape(b, 1, nrot), rv.astype(F32).reshape(b, 1, D_RET),
      s0)


def _nsa_step1_kernel(q_ref, kc_ref, vc_ref, win_ref, new_ref, mimp_ref, oc_ref, ow_ref, idx_ref, *,
                      t, n_cmp, n_sel):
    q = q_ref[0]
    nq = q.shape[0]
    ncp = kc_ref.shape[1]
    nsp = mimp_ref.shape[1]

    s = _dot_nt(q, kc_ref[0])
    col = lax.broadcasted_iota(jnp.int32, (nq, ncp), 1)
    mask = (col * CMP_STRIDE + (CMP_BLOCK - 1) <= t) & (col < n_cmp)
    s = jnp.where(mask, s, NEG)
    e = jnp.exp(s - jnp.max(s, axis=-1, keepdims=True))
    p = jnp.where(mask, e, 0.0) * (1.0 / jnp.sum(e, axis=-1, keepdims=True))
    oc_ref[0] = _dot(p.astype(BF16), vc_ref[0])
    imp_rows = _split3_dot(p, mimp_ref[...])

    blk = lax.broadcasted_iota(jnp.int32, (1, nsp), 1)
    jt = t // SEL_BLOCK
    forced = (blk == 0) | (blk == jt) | (blk == jt - 1)
    scores = []
    for g in range(N_KV):
        imp = jnp.sum(imp_rows[HPG * g:HPG * (g + 1)], axis=0, keepdims=True)
        sc = jnp.where(blk * SEL_BLOCK <= t, jnp.where(forced, FORCE, imp), NEG)
        scores.append(jnp.where(blk < n_sel, sc, BELOW_NEG))
    s_rows = jnp.concatenate(scores + [jnp.full((LANES - N_KV, nsp), BELOW_NEG, F32)], axis=0)
    s_cols = s_rows.T
    a_idx = lax.broadcasted_iota(jnp.int32, (nsp, nsp), 0)
    c_idx = lax.broadcasted_iota(jnp.int32, (nsp, nsp), 1)
    rank_lane = lax.broadcasted_iota(jnp.int32, (nsp, LANES), 1).astype(F32)
    a_val = lax.broadcasted_iota(jnp.int32, (nsp, LANES), 0).astype(F32)
    rows_out = []
    for g in range(N_KV):
        s_col = s_cols[:, g:g + 1]
        s_row = s_rows[g:g + 1, :]
        beats = (s_row > s_col) | ((s_row == s_col) & (c_idx < a_idx))
        rank = jnp.sum(jnp.where(beats, 1.0, 0.0), axis=-1, keepdims=True)
        rows_out.append(jnp.sum(jnp.where(rank == rank_lane, a_val, 0.0), axis=0, keepdims=True))
    idx = jnp.concatenate(rows_out + [jnp.zeros((8 - N_KV, LANES), F32)], axis=0)
    idx_ref[0] = idx.astype(jnp.int32)

    wb = win_ref.shape[-1]
    kw_t = win_ref[0, 0].reshape(LANES, wb).astype(BF16)
    vw_t = win_ref[0, 1].reshape(LANES, wb).astype(BF16)
    k_new = new_ref[0, :, 0:LANES]
    v_new = new_ref[0, :, LANES:2 * LANES]
    s = _dot(q, kw_t)
    kpos = t - wb + lax.broadcasted_iota(jnp.int32, (nq, wb), 1)
    s = jnp.where((kpos > t - WINDOW) & (kpos >= 0), s, NEG)
    s_new = jnp.sum(q.astype(F32) * k_new, axis=-1, keepdims=True)
    m = jnp.maximum(jnp.max(s, axis=-1, keepdims=True), s_new)
    e = jnp.exp(s - m)
    e_new = jnp.exp(s_new - m)
    ow_ref[0] = ((_dot_nt(e.astype(BF16), vw_t) + e_new * v_new)
                 * (1.0 / (jnp.sum(e, axis=-1, keepdims=True) + e_new)))


def _nsa_step1(q16, kc, vc, state_win_t, new_win, mimp, *, t, n_cmp, n_sel):
    b, nq, _ = q16.shape
    ncp = kc.shape[1]

    def per_b(shape):
        return pl.BlockSpec((1,) + shape, lambda i: (i,) + (0,) * len(shape))

    return pl.pallas_call(
        functools.partial(_nsa_step1_kernel, t=t, n_cmp=n_cmp, n_sel=n_sel),
        grid=(b,),
        in_specs=[per_b((nq, LANES)), per_b((ncp, LANES)), per_b((ncp, LANES)), per_b(state_win_t.shape[1:]),
                  per_b((1, KV_ROW)), pl.BlockSpec(mimp.shape, lambda i: (0, 0))],
        out_specs=[per_b((nq, LANES)), per_b((nq, LANES)), per_b((8, LANES))],
        out_shape=[jax.ShapeDtypeStruct((b, nq, LANES), F32), jax.ShapeDtypeStruct((b, nq, LANES), F32),
                   jax.ShapeDtypeStruct((b, 8, LANES), jnp.int32)],
        compiler_params=_cparams("arbitrary"),
        name="nsa_step_compressed_window",
    )(q16, kc, vc, state_win_t, new_win, mimp)


def _nsa_step2_kernel(idx_sm, pt_sm, q_ref, *refs, n_cache_blocks, k_eff):
    page_refs = refs[:N_KV * k_eff]
    new_ref, oc_ref, ow_ref, br_ref, o_ref = refs[N_KV * k_eff:]
    b = pl.program_id(0)
    q = q_ref[0]
    nq = q.shape[0]
    bpp = PAGE_SIZE // SEL_BLOCK
    nk = k_eff * PAGE_SIZE
    k_new = new_ref[0, :, 0:LANES]
    v_new = new_ref[0, :, LANES:2 * LANES]
    s_new_all = jnp.sum(q.astype(F32) * k_new, axis=-1, keepdims=True)
    lane_k = lax.broadcasted_iota(jnp.int32, (nq, nk), 1)
    slot = lane_k // PAGE_SIZE
    half = (lane_k % PAGE_SIZE) // SEL_BLOCK
    outs = []
    for g in range(N_KV):
        pages = [page_refs[g * k_eff + k][0] for k in range(k_eff)]
        keys_t = jnp.concatenate([p[0].reshape(LANES, PAGE_SIZE) for p in pages], axis=-1).astype(BF16)
        vals_t = jnp.concatenate([p[1].reshape(LANES, PAGE_SIZE) for p in pages], axis=-1).astype(BF16)
        allowed = jnp.zeros((nq, nk), jnp.int32)
        new_selected = jnp.int32(0)
        for k in range(k_eff):
            j = idx_sm[b, g * k_eff + k]
            in_block = (slot == k) & (half == j % bpp)
            allowed = jnp.where(in_block, (j < n_cache_blocks).astype(jnp.int32), allowed)
            new_selected = new_selected | (j == n_cache_blocks).astype(jnp.int32)
        s = jnp.where(allowed > 0, _dot(q, keys_t), NEG)
        s_new = jnp.where(new_selected > 0, s_new_all, NEG)
        m = jnp.maximum(jnp.max(s, axis=-1, keepdims=True), s_new)
        e = jnp.exp(s - m)
        e_new = jnp.exp(s_new - m)
        outs.append((_dot_nt(e.astype(BF16), vals_t) + e_new * v_new)
                    * (1.0 / (jnp.sum(e, axis=-1, keepdims=True) + e_new)))
    row = lax.broadcasted_iota(jnp.int32, (nq, LANES), 0)
    lane = lax.broadcasted_iota(jnp.int32, (nq, LANES), 1)
    o_slc = jnp.where(row < HPG, outs[0], outs[1])
    br = br_ref[0]
    gates = [jnp.sum(jnp.where(lane == 3 * row + j, br, 0.0), axis=-1, keepdims=True) for j in range(3)]
    o = gates[0] * oc_ref[0] + gates[1] * o_slc + gates[2] * ow_ref[0]
    lane_lo = lax.broadcasted_iota(jnp.int32, (1, LANES), 1) < HEAD_DIM
    for pair in range(N_HEADS // 2):
        even, odd = o[2 * pair:2 * pair + 1], o[2 * pair + 1:2 * pair + 2]
        if 2 * pair < HPG:
            odd = pltpu.roll(odd, HEAD_DIM, axis=1)
        else:
            even = pltpu.roll(even, HEAD_DIM, axis=1)
        o_ref[0, :, pair * LANES:(pair + 1) * LANES] = jnp.where(lane_lo, even, odd)


def _nsa_step2(idx, page_table, q16, cache_t, new_slc, o_cmp, o_win, br, *, n_cache_blocks, k_eff):
    b, nq, _ = q16.shape
    bpp = PAGE_SIZE // SEL_BLOCK

    def page_map(i, idx_sm, pt_sm, c):
        j = jnp.minimum(idx_sm[i, c], n_cache_blocks - 1)
        return (pt_sm[i, j // bpp], 0, 0, 0, 0)

    def per_b(shape):
        return pl.BlockSpec((1,) + shape, lambda i, idx_sm, pt_sm: (i,) + (0,) * len(shape))

    n_blk = N_KV * k_eff
    in_specs = [per_b((nq, LANES))]
    in_specs += [pl.BlockSpec((1,) + cache_t.shape[1:], functools.partial(page_map, c=c)) for c in range(n_blk)]
    in_specs += [per_b((1, KV_ROW)), per_b((nq, LANES)), per_b((nq, LANES)), per_b((1, LANES))]
    return pl.pallas_call(
        functools.partial(_nsa_step2_kernel, n_cache_blocks=n_cache_blocks, k_eff=k_eff),
        grid_spec=pltpu.PrefetchScalarGridSpec(
            num_scalar_prefetch=2,
            grid=(b,),
            in_specs=in_specs,
            out_specs=per_b((1, D_NSA)),
        ),
        out_shape=jax.ShapeDtypeStruct((b, 1, D_NSA), F32),
        compiler_params=_cparams("arbitrary"),
        name="nsa_step_selected",
    )(idx, page_table, q16, *([cache_t] * n_blk), new_slc, o_cmp, o_win, br)


def _out_kernel(*refs, mul_ga):
    if mul_ga:
        x_ref, gate_ref, ya_ref, ga_ref, oret_ref, gr_ref, gret_ref, w_ref, y_ref = refs
        ya = (ya_ref[...] * ga_ref[...]).astype(BF16)
    else:
        x_ref, gate_ref, ya_ref, oret_ref, gr_ref, gret_ref, w_ref, y_ref = refs
        ya = ya_ref[...]
    o = oret_ref[...]
    parts = []
    for h in range(N_RET_HEADS):
        t = o[:, h * DV_RET:(h + 1) * DV_RET]
        ms = jnp.mean(t * t, axis=-1, keepdims=True)
        parts.append(t * lax.rsqrt(ms + EPS) * gret_ref[...])
    yr = (jnp.concatenate(parts, axis=-1) * gr_ref[...]).astype(BF16)
    y_ref[...] = x_ref[...] + gate_ref[0] * _dot(jnp.concatenate([ya, yr], axis=-1), w_ref[...])


def _output(x2, gate, ya, ga, o_ret, gr, g_ret, w_out, *, n_batch, per_row_mod):
    m = x2.shape[0]
    seq = m // n_batch
    tm = min(seq, 512)
    nt = seq // tm
    if per_row_mod:
        gate_spec = pl.BlockSpec((1, tm, D_MODEL), lambda i: (0, i, 0))
    else:
        gate_spec = pl.BlockSpec((1, 1, D_MODEL), lambda i: (i // nt, 0, 0))

    def rows_spec(width):
        return pl.BlockSpec((tm, width), lambda i: (i, 0))

    mul_ga = ga is not None
    args = [x2, gate, ya] + ([ga] if mul_ga else []) + [o_ret, gr, g_ret.reshape(1, DV_RET), w_out.astype(BF16)]
    in_specs = [rows_spec(D_MODEL), gate_spec, rows_spec(D_NSA)] + ([rows_spec(D_NSA)] if mul_ga else [])
    in_specs += [rows_spec(D_RET), rows_spec(D_RET), pl.BlockSpec((1, DV_RET), lambda i: (0, 0)),
                 pl.BlockSpec((D_MODEL, D_MODEL), lambda i: (0, 0))]
    return pl.pallas_call(
        functools.partial(_out_kernel, mul_ga=mul_ga),
        grid=(m // tm,),
        in_specs=in_specs,
        out_specs=rows_spec(D_MODEL),
        out_shape=jax.ShapeDtypeStruct((m, D_MODEL), F32),
        compiler_params=_cparams("arbitrary"),
        name="output_projection",
    )(*args)


def kernel(x_prompt, x_sample, c_prompt, c_sample, cache_cmp, cache_slc, state_win, state_ret, page_table, g_norm, w_ada, b_ada, w_in, g_q, g_kc, g_ks, g_kw, pe_ck, w_ck1, w_ck2, pe_cv, w_cv1, w_cv2, g_ret, w_out):
    assert g_norm.shape[0] == 1, "one layer"
    bp, seq, _ = x_prompt.shape
    bs, dec_seq, _ = x_sample.shape
    assert dec_seq == 1
    n_pages = page_table.shape[1]
    past = n_pages * PAGE_SIZE
    assert state_win.shape[2] == WINDOW and past >= WINDOW

    w_proj, gq_slots, gks2, gkw2 = _prep_proj_weights(w_in[0], g_q[0], g_ks[0], g_kw[0])
    w1x, w1kv, w2x, pe8, gkc2 = _prep_compress_weights(pe_ck[0], w_ck1[0], w_ck2[0], pe_cv[0], w_cv1[0], w_cv2[0], g_kc[0])

    pad = (-(bp + bs)) % 8
    c_all = jnp.concatenate([c_prompt, c_sample, jnp.zeros((pad, D_MODEL), F32)], axis=0)
    mod = _modulation(c_all, w_ada[0], b_ada[0])
    shift_p, scale_p, gate_p = (mod[:bp, k * D_MODEL:(k + 1) * D_MODEL].reshape(bp, 1, D_MODEL) for k in range(3))
    shift_s, scale_s, gate_s = (mod[bp:bp + bs, k * D_MODEL:(k + 1) * D_MODEL].reshape(1, bs, D_MODEL)
                                for k in range(3))

    xp2 = x_prompt.reshape(bp * seq, D_MODEL)
    cos_p, sin_p = _rope_tables(seq, 0, 1)
    (q_p, cmp_p, slc_p, slcb_p, win_p, winb_p, br_p, ga_p, rq_p, rk_p, rv_p, gr_p) = _projection(
        xp2, scale_p, shift_p, g_norm[0], w_proj, gq_slots, gks2, gkw2, cos_p, sin_p, n_batch=bp, per_row_mod=False)

    a_p = _chunk_matmul_dense(cmp_p.reshape(bp, seq // CMP_STRIDE, CHUNK_LANES), w1x)
    kc_p, vc_p = _compress_combine(a_p, jnp.zeros((bp, 1, KV_ROW), F32), pe8, w1x, w2x, gkc2)
    ya_p = _nsa_prompt(q_p, kc_p, vc_p, slcb_p, winb_p, br_p, ga_p, n_batch=bp, seq=seq)
    oret_p, sret_p = _retention_prompt(rq_p, rk_p, rv_p, n_batch=bp, seq=seq)
    y_p = _output(xp2, gate_p, ya_p, None, oret_p.reshape(bp * seq, D_RET), gr_p, g_ret[0], w_out[0],
                  n_batch=bp, per_row_mod=False)

    xs2 = x_sample.reshape(bs, D_MODEL)
    cos_s, sin_s = _rope_tables(bs, past, 0)
    (q_s, cmp_s, slc_s, _, win_s, _, br_s, ga_s, rq_s, rk_s, rv_s, gr_s) = _projection(
        xs2, scale_s, shift_s, g_norm[0], w_proj, gq_slots, gks2, gkw2, cos_s, sin_s, n_batch=1, per_row_mod=True)

    n_chunk_cache = past // CMP_STRIDE
    n_cmp_s = n_chunk_cache
    n_sel_s = past // SEL_BLOCK + 1
    nsp = ((n_sel_s + LANES - 1) // LANES) * LANES
    pages_per_step = math.gcd(n_pages, 32)
    to_native = (0, 2, 3, 4, 1)
    a_s = _chunk_matmul_paged(jnp.transpose(cache_cmp[0], to_native), page_table, w1kv, pages_per_step)
    kc_s, vc_s = _compress_combine(a_s, cmp_s.reshape(bs, 1, KV_ROW), pe8, w1x, w2x, gkc2)

    q16 = jnp.pad(q_s[0].transpose(1, 0, 2), ((0, 0), (0, 16 - N_HEADS), (0, 0)))
    mimp_s = _importance_matrix(n_cmp_s, n_cmp_s, n_sel_s, nsp)
    o_cmp_s, o_win_s, idx_s = _nsa_step1(q16, kc_s, vc_s, jnp.transpose(state_win[0], to_native),
                                         win_s.reshape(bs, 1, KV_ROW), mimp_s, t=past, n_cmp=n_cmp_s,
                                         n_sel=n_sel_s)
    k_eff = min(TOP_N, n_sel_s)
    idx_flat = idx_s[:, :N_KV, :k_eff].reshape(bs, N_KV * k_eff)
    o_nsa_s = _nsa_step2(idx_flat, page_table, q16, jnp.transpose(cache_slc[0], to_native),
                         slc_s.reshape(bs, 1, KV_ROW), o_cmp_s, o_win_s, br_s.reshape(bs, 1, LANES),
                         n_cache_blocks=past // SEL_BLOCK, k_eff=k_eff)

    oret_s, sret_s = _retention_step(rq_s, rk_s, rv_s, state_ret[0])
    y_s = _output(xs2, gate_s, o_nsa_s.reshape(bs, D_NSA), ga_s, oret_s.reshape(bs, D_RET), gr_s, g_ret[0], w_out[0],
                  n_batch=1, per_row_mod=True)

    def kv_rows(a, b, l):
        return a.reshape(1, b, l, 2, N_KV, HEAD_DIM)

    keep_p = min(WINDOW, seq)
    s_win = jnp.concatenate([state_win[0][:, 1:].reshape(bs, WINDOW - 1, KV_ROW), win_s.reshape(bs, 1, KV_ROW)],
                            axis=1)
    return (y_p.reshape(bp, seq, D_MODEL), y_s.reshape(bs, 1, D_MODEL),
            kv_rows(cmp_p, bp, seq), kv_rows(slc_p, bp, seq),
            kv_rows(win_p.reshape(bp, seq, KV_ROW)[:, seq - keep_p:], bp, keep_p), sret_p[None],
            kv_rows(cmp_s, bs, 1), kv_rows(slc_s, bs, 1), kv_rows(s_win, bs, WINDOW), sret_s[None])
```

```python
import functools
import math

import jax
import jax.numpy as jnp
import numpy as np
from jax import lax
from jax.experimental import pallas as pl
from jax.experimental.pallas import tpu as pltpu

D_MODEL = 1024
HEAD_DIM = 64
D_NSA = 512
N_HEADS = 8
N_KV = 2
HPG = 4
CMP_BLOCK = 32
CMP_STRIDE = 16
SEL_BLOCK = 64
TOP_N = 16
WINDOW = 512
D_RET = 512
N_RET_HEADS = 4
DV_RET = 128
DK_RET = 64
RET_CHUNK = 128
PAGE_SIZE = 128
ROPE_BASE = 10000.0
EPS = 1e-6
NEG = -1e30
FORCE = 1e4
BELOW_NEG = -2e38
TAKEN = -3e38

KV_ROW = 2 * N_KV * HEAD_DIM
CHUNK_LANES = CMP_STRIDE * KV_ROW
LANES = 128
VMEM_LIMIT_BYTES = 56 * 1024 * 1024

F32 = jnp.float32
BF16 = jnp.bfloat16

O_Q = 0
O_CMP = O_Q + N_HEADS * LANES
O_SLC = O_CMP + KV_ROW
O_WIN = O_SLC + KV_ROW
O_BR = O_WIN + KV_ROW
O_GA = O_BR + LANES
O_RQ = O_GA + D_NSA
O_RK = O_RQ + 2 * N_RET_HEADS * DK_RET
O_RV = O_RK + 2 * N_RET_HEADS * DK_RET
O_GR = O_RV + D_RET
D_IN_PAD = O_GR + D_RET


def _cparams(*sem):
    return pltpu.CompilerParams(dimension_semantics=sem, vmem_limit_bytes=VMEM_LIMIT_BYTES)


def _dot(a, b):
    return jnp.dot(a, b, preferred_element_type=F32)


def _dot_nt(a, b):
    return lax.dot_general(a, b, (((1,), (1,)), ((), ())), preferred_element_type=F32)


def _dot_tn(a, b):
    return lax.dot_general(a, b, (((0,), (0,)), ((), ())), preferred_element_type=F32)


def _silu(x):
    return x * jax.nn.sigmoid(x)


def _split3_dot(p, m):
    p1 = p.astype(BF16)
    r1 = p - p1.astype(F32)
    p2 = r1.astype(BF16)
    p3 = (r1 - p2.astype(F32)).astype(BF16)
    return _dot(p1, m) + _dot(p2, m) + _dot(p3, m)


def _pair_rmsnorm(k, gain2):
    lo = lax.broadcasted_iota(jnp.int32, k.shape, 1) < HEAD_DIM
    sq = k * k
    s0 = jnp.sum(jnp.where(lo, sq, 0.0), axis=-1, keepdims=True)
    s1 = jnp.sum(jnp.where(lo, 0.0, sq), axis=-1, keepdims=True)
    ms = jnp.where(lo, s0, s1) * (1.0 / HEAD_DIM)
    return k * lax.rsqrt(ms + EPS) * gain2


def _mod_kernel(c_ref, w_ref, b_ref, o_ref):
    o_ref[...] = _dot(_silu(c_ref[...]).astype(BF16), w_ref[...]) + b_ref[...]


def _modulation(c_all, w_ada, b_ada):
    rows = c_all.shape[0]
    n = w_ada.shape[1]
    tn = 1024
    return pl.pallas_call(
        _mod_kernel,
        grid=(n // tn,),
        in_specs=[pl.BlockSpec((rows, D_MODEL), lambda j: (0, 0)),
                  pl.BlockSpec((D_MODEL, tn), lambda j: (0, j)),
                  pl.BlockSpec((1, tn), lambda j: (0, j))],
        out_specs=pl.BlockSpec((rows, tn), lambda j: (0, j)),
        out_shape=jax.ShapeDtypeStruct((rows, n), F32),
        compiler_params=_cparams("arbitrary"),
        name="adaln_modulation",
    )(c_all, w_ada.astype(BF16), b_ada.reshape(1, n))


def _rope_kernel(freq_ref, cos_ref, sin_ref, *, base, stride, rows):
    shape = (rows, LANES)
    r = lax.broadcasted_iota(jnp.int32, shape, 0)
    pos = (base + stride * (pl.program_id(0) * rows + r)).astype(F32)
    ang = pos * freq_ref[...]
    lane = lax.broadcasted_iota(jnp.int32, shape, 1)
    sign = jnp.where((lane & (DK_RET - 1)) < DK_RET // 2, -1.0, 1.0)
    cos_ref[...] = jnp.cos(ang)
    sin_ref[...] = jnp.sin(ang) * sign


def _rope_tables(n_rows, base, stride):
    half = DK_RET // 2
    freqs = ROPE_BASE ** (-jnp.arange(half, dtype=F32) / half)
    freq_row = jnp.tile(freqs, LANES // half).reshape(1, LANES)
    rows = min(n_rows, 512)
    return pl.pallas_call(
        functools.partial(_rope_kernel, base=base, stride=stride, rows=rows),
        grid=(n_rows // rows,),
        in_specs=[pl.BlockSpec((1, LANES), lambda i: (0, 0))],
        out_specs=[pl.BlockSpec((rows, LANES), lambda i: (i, 0))] * 2,
        out_shape=[jax.ShapeDtypeStruct((n_rows, LANES), F32)] * 2,
        compiler_params=_cparams("arbitrary"),
        name="rope_tables",
    )(freq_row)


def _proj_kernel(x_ref, scale_ref, shift_ref, gn_ref, w_ref, gq_ref, gks_ref, gkw_ref, cos_ref, sin_ref,
                 q_ref, cmp_ref, slc_ref, slcb_ref, win_ref, winb_ref, br_ref, ga_ref, rq_ref, rk_ref,
                 rv_ref, gr_ref):
    x = x_ref[...]
    ms = jnp.mean(x * x, axis=-1, keepdims=True)
    xn = x * lax.rsqrt(ms + EPS) * gn_ref[...]
    hb = (xn * (1.0 + scale_ref[0]) + shift_ref[0]).astype(BF16)

    def seg(start, width):
        return _dot(hb, w_ref[:, start:start + width])

    yq = seg(O_Q, N_HEADS * LANES)
    for s in range(N_HEADS):
        t = yq[:, s * LANES:(s + 1) * LANES]
        ms = jnp.sum(t * t, axis=-1, keepdims=True) * (1.0 / HEAD_DIM)
        q_ref[0, s] = (t * lax.rsqrt(ms + EPS) * gq_ref[s]).astype(BF16)

    cmp_ref[...] = seg(O_CMP, KV_ROW)

    y = seg(O_SLC, KV_ROW)
    y = jnp.concatenate([_pair_rmsnorm(y[:, :LANES], gks_ref[...]), y[:, LANES:]], axis=-1)
    slc_ref[...] = y
    slcb_ref[...] = y.astype(BF16)

    y = seg(O_WIN, KV_ROW)
    y = jnp.concatenate([_pair_rmsnorm(y[:, :LANES], gkw_ref[...]), y[:, LANES:]], axis=-1)
    win_ref[...] = y
    winb_ref[...] = y.astype(BF16)

    br_ref[...] = jax.nn.sigmoid(seg(O_BR, LANES))
    ga_ref[...] = _silu(seg(O_GA, D_NSA))

    cos2 = jnp.concatenate([cos_ref[...]] * 2, axis=-1)
    sin2 = jnp.concatenate([sin_ref[...]] * 2, axis=-1)
    nrot = N_RET_HEADS * DK_RET
    y = seg(O_RQ, 2 * nrot)
    rq_ref[...] = (y[:, :nrot] * cos2 + y[:, nrot:] * sin2).astype(BF16)
    y = seg(O_RK, 2 * nrot)
    rk_ref[...] = ((y[:, :nrot] * cos2 + y[:, nrot:] * sin2) * DK_RET ** -0.5).astype(BF16)
    rv_ref[...] = seg(O_RV, D_RET).astype(BF16)
    gr_ref[...] = _silu(seg(O_GR, D_RET))


def _swap_halves(w):
    k = w.shape[0]
    w4 = w.reshape(k, N_RET_HEADS, 2, DK_RET // 2)
    return w4[:, :, ::-1, :].reshape(k, N_RET_HEADS * DK_RET)


def _prep_proj_weights(w_in, g_q, g_ks, g_kw):
    k = w_in.shape[0]
    o = 0
    wq = w_in[:, o:o + D_NSA].reshape(k, N_KV, HPG, HEAD_DIM)
    o += D_NSA
    z = jnp.zeros_like(wq)
    wq = jnp.stack([jnp.concatenate([wq[:, 0], z[:, 0]], axis=-1),
                    jnp.concatenate([z[:, 1], wq[:, 1]], axis=-1)], axis=1).reshape(k, N_HEADS * LANES)
    w_kv = w_in[:, o:o + 3 * KV_ROW]
    o += 3 * KV_ROW
    n_br = 3 * N_HEADS
    w_br = jnp.pad(w_in[:, o:o + n_br], ((0, 0), (0, LANES - n_br)))
    o += n_br
    w_ga = w_in[:, o:o + D_NSA]
    o += D_NSA
    nrot = N_RET_HEADS * DK_RET
    w_rq = w_in[:, o:o + nrot]
    o += nrot
    w_rk = w_in[:, o:o + nrot]
    o += nrot
    w_rv = w_in[:, o:o + D_RET]
    o += D_RET
    w_gr = w_in[:, o:o + D_RET]
    w = jnp.concatenate([wq, w_kv, w_br, w_ga, w_rq, _swap_halves(w_rq), w_rk, _swap_halves(w_rk), w_rv, w_gr],
                        axis=1).astype(BF16)
    assert w.shape[1] == D_IN_PAD
    gz = jnp.zeros((HEAD_DIM,), F32)
    gq = g_q * HEAD_DIM ** -0.5
    gq_slots = jnp.stack([jnp.concatenate([gq, gz]) if s < HPG else jnp.concatenate([gz, gq])
                          for s in range(N_HEADS)]).reshape(N_HEADS, 1, LANES)
    return w, gq_slots, jnp.tile(g_ks, 2).reshape(1, LANES), jnp.tile(g_kw, 2).reshape(1, LANES)


def _projection(x2, scale, shift, g_norm, w, gq_slots, gks2, gkw2, cos_t, sin_t, *, n_batch, per_row_mod):
    m = x2.shape[0]
    seq = m // n_batch
    tm = min(seq, 256)
    nt = seq // tm
    if per_row_mod:
        mod_spec = pl.BlockSpec((1, tm, D_MODEL), lambda i: (0, i, 0))
    else:
        mod_spec = pl.BlockSpec((1, 1, D_MODEL), lambda i: (i // nt, 0, 0))

    def rows_spec(width):
        return pl.BlockSpec((tm, width), lambda i: (i, 0))

    def const_spec(shape):
        return pl.BlockSpec(shape, lambda i: (0,) * len(shape))

    widths = [(KV_ROW, F32), (KV_ROW, F32), (KV_ROW, BF16), (KV_ROW, F32), (KV_ROW, BF16), (LANES, F32),
              (D_NSA, F32), (N_RET_HEADS * DK_RET, BF16), (N_RET_HEADS * DK_RET, BF16), (D_RET, BF16),
              (D_RET, F32)]
    out_shape = [jax.ShapeDtypeStruct((n_batch, N_HEADS, seq, LANES), BF16)]
    out_specs = [pl.BlockSpec((1, N_HEADS, tm, LANES), lambda i: (i // nt, 0, i % nt, 0))]
    for width, dt in widths:
        out_shape.append(jax.ShapeDtypeStruct((m, width), dt))
        out_specs.append(rows_spec(width))
    return pl.pallas_call(
        _proj_kernel,
        grid=(m // tm,),
        in_specs=[rows_spec(D_MODEL), mod_spec, mod_spec, const_spec((1, D_MODEL)),
                  const_spec((D_MODEL, D_IN_PAD)), const_spec((N_HEADS, 1, LANES)), const_spec((1, LANES)),
                  const_spec((1, LANES)), pl.BlockSpec((tm, LANES), lambda i: (i % nt, 0)),
                  pl.BlockSpec((tm, LANES), lambda i: (i % nt, 0))],
        out_specs=out_specs,
        out_shape=out_shape,
        compiler_params=_cparams("arbitrary"),
        name="input_projection",
    )(x2, scale, shift, g_norm.reshape(1, D_MODEL), w, gq_slots, gks2, gkw2, cos_t, sin_t)


def _chunk_matmul_kernel(x_ref, w_ref, o_ref):
    o_ref[0] = _dot(x_ref[0].astype(BF16), w_ref[...])


def _chunk_matmul_dense(chunks, w1x):
    b, n, _ = chunks.shape
    return pl.pallas_call(
        _chunk_matmul_kernel,
        grid=(b,),
        in_specs=[pl.BlockSpec((1, n, CHUNK_LANES), lambda i: (i, 0, 0)),
                  pl.BlockSpec(w1x.shape, lambda i: (0, 0))],
        out_specs=pl.BlockSpec((1, n, w1x.shape[1]), lambda i: (i, 0, 0)),
        out_shape=jax.ShapeDtypeStruct((b, n, w1x.shape[1]), F32),
        compiler_params=_cparams("arbitrary"),
        name="compress_chunks_prompt",
    )(chunks, w1x)


def _chunk_matmul_paged_kernel(pt_ref, *refs, n_in):
    x_refs = refs[:n_in]
    w_ref, o_ref = refs[n_in:n_in + 2]
    rows_sc = refs[n_in + 2:]
    for k in range(n_in):
        for kv in range(2):
            rows_sc[kv][k * PAGE_SIZE:(k + 1) * PAGE_SIZE, :] = x_refs[k][0, kv].reshape(LANES, PAGE_SIZE).T
    n_rows = n_in * PAGE_SIZE // CMP_STRIDE
    for kv in range(2):
        x = jnp.concatenate([rows_sc[kv][pl.ds(l, n_rows, stride=CMP_STRIDE), :] for l in range(CMP_STRIDE)],
                            axis=-1).astype(BF16)
        o_ref[0, :, kv * KV_ROW:(kv + 1) * KV_ROW] = _dot(x, w_ref[kv])


def _chunk_matmul_paged(cache_t, page_table, w1kv, pages_per_step):
    b, n_pages = page_table.shape
    cpp = PAGE_SIZE // CMP_STRIDE
    p = pages_per_step
    page_shape = (1,) + cache_t.shape[1:]
    in_specs = [pl.BlockSpec(page_shape, lambda i, j, pt, k=k: (pt[i, j * p + k], 0, 0, 0, 0)) for k in range(p)]
    in_specs.append(pl.BlockSpec(w1kv.shape, lambda i, j, pt: (0, 0, 0)))
    return pl.pallas_call(
        functools.partial(_chunk_matmul_paged_kernel, n_in=p),
        grid_spec=pltpu.PrefetchScalarGridSpec(
            num_scalar_prefetch=1,
            grid=(b, n_pages // p),
            in_specs=in_specs,
            out_specs=pl.BlockSpec((1, p * cpp, 2 * KV_ROW), lambda i, j, pt: (i, j, 0)),
            scratch_shapes=[pltpu.VMEM((p * PAGE_SIZE, LANES), F32)] * 2,
        ),
        out_shape=jax.ShapeDtypeStruct((b, n_pages * cpp, 2 * KV_ROW), F32),
        compiler_params=_cparams("arbitrary", "arbitrary"),
        name="compress_chunks_paged",
    )(page_table, *([cache_t] * p), w1kv)


def _halves(a):
    lo = jnp.concatenate([a[:, 0:LANES], a[:, 2 * LANES:3 * LANES]], axis=-1)
    hi = jnp.concatenate([a[:, LANES:2 * LANES], a[:, 3 * LANES:4 * LANES]], axis=-1)
    return lo, hi


def _compress_combine_kernel(a_ref, new_ref, pe_ref, w1_ref, w2_ref, gkc_ref, kc_ref, vc_ref, *, n):
    lo, hi = _halves(a_ref[0])
    pe_lo, pe_hi = _halves(_dot(pe_ref[...], w1_ref[...]))
    bias = pe_lo[0:1] + pe_hi[1:2]
    new8 = jnp.broadcast_to(new_ref[0], (8, KV_ROW)).astype(BF16)
    a_new = _halves(_dot(new8, w1_ref[0:KV_ROW, :]))[1][0:1]
    hi = pltpu.roll(hi, n - 1, axis=0)
    row = lax.broadcasted_iota(jnp.int32, (n, KV_ROW), 0)
    hi = jnp.where(row == n - 1, a_new, hi)
    out = _dot(_silu(lo + hi + bias).astype(BF16), w2_ref[...])
    kc_ref[0] = _pair_rmsnorm(out[:, :LANES], gkc_ref[...]).astype(BF16)
    vc_ref[0] = out[:, LANES:].astype(BF16)


def _compress_combine(a, new_rows, pe8, w1x, w2x, gkc2):
    b, n, _ = a.shape
    return pl.pallas_call(
        functools.partial(_compress_combine_kernel, n=n),
        grid=(b,),
        in_specs=[pl.BlockSpec((1, n, 2 * KV_ROW), lambda i: (i, 0, 0)),
                  pl.BlockSpec((1, 1, KV_ROW), lambda i: (i, 0, 0)),
                  pl.BlockSpec(pe8.shape, lambda i: (0, 0)),
                  pl.BlockSpec(w1x.shape, lambda i: (0, 0)),
                  pl.BlockSpec(w2x.shape, lambda i: (0, 0)),
                  pl.BlockSpec((1, LANES), lambda i: (0, 0))],
        out_specs=[pl.BlockSpec((1, n, LANES), lambda i: (i, 0, 0))] * 2,
        out_shape=[jax.ShapeDtypeStruct((b, n, LANES), BF16)] * 2,
        compiler_params=_cparams("arbitrary"),
        name="compress_combine",
    )(a, new_rows, pe8, w1x, w2x, gkc2)


def _prep_compress_weights(pe_ck, w_ck1, w_ck2, pe_cv, w_cv1, w_cv2, g_kc):
    eye = jnp.eye(N_KV, dtype=F32)
    w1 = jnp.stack([w_ck1, w_cv1]).reshape(2, 2, CMP_STRIDE, HEAD_DIM, HEAD_DIM)
    w1x = jnp.einsum('khldf,kq,gp->lkgdqhpf', w1, eye, eye).reshape(CHUNK_LANES, 2 * KV_ROW).astype(BF16)
    w1kv = jnp.einsum('khldf,gp->klgdhpf', w1, eye).reshape(2, CMP_STRIDE * LANES, KV_ROW).astype(BF16)
    w2 = jnp.stack([w_ck2, w_cv2])
    w2x = jnp.einsum('kfd,kq,gp->kgfqpd', w2, eye, eye).reshape(KV_ROW, KV_ROW).astype(BF16)
    pe = jnp.stack([pe_ck, pe_cv]).reshape(2, 2, CMP_STRIDE, HEAD_DIM)
    pe_rows = jnp.broadcast_to(pe.transpose(1, 2, 0, 3)[:, :, :, None, :],
                               (2, CMP_STRIDE, 2, N_KV, HEAD_DIM)).reshape(2, CHUNK_LANES)
    pe8 = jnp.zeros((8, CHUNK_LANES), F32).at[:2].set(pe_rows).astype(BF16)
    return w1x, w1kv, w2x, pe8, jnp.tile(g_kc, 2).reshape(1, LANES)


def _importance_matrix(n_cmp_pad, n_cmp, n_sel, n_sel_pad):
    ratio = SEL_BLOCK // CMP_STRIDE
    span = CMP_BLOCK // CMP_STRIDE
    m = np.zeros((n_cmp_pad, n_sel_pad), np.float32)
    for j in range(n_sel):
        for r in range(ratio):
            for s in range(span):
                n = ratio * j + r - s
                if 0 <= n < n_cmp:
                    m[n, j] += 1.0
    return jnp.asarray(m, BF16)


def _nsa_prompt_kernel(q_ref, kc_ref, vc_ref, slc_ref, win_ref, br_ref, ga_ref, mimp_ref, expt_ref, o_ref,
                       m_sc, acc_sc, *, tq, tk, n_cmp, n_sel):
    qs = pl.program_id(1) * tq
    rows = HPG * tq
    ncp = kc_ref.shape[1]
    wlen = WINDOW + tq
    br = br_ref[...]
    lane_lo = lax.broadcasted_iota(jnp.int32, (tq, LANES), 1) < HEAD_DIM

    def q_pos(width):
        return qs + lax.broadcasted_iota(jnp.int32, (tq, width), 0)

    def per_head(x):
        return jnp.concatenate([x] * HPG, axis=0)

    col = lax.broadcasted_iota(jnp.int32, (tq, ncp), 1)
    cmp_mask = per_head(jnp.where((col * CMP_STRIDE + (CMP_BLOCK - 1) <= q_pos(ncp)) & (col < n_cmp), 0.0, NEG))
    cmp_any = per_head(jnp.where(q_pos(1) >= CMP_BLOCK - 1, 1.0, 0.0))
    w0 = pl.multiple_of(jnp.maximum(qs - WINDOW, 0), tq)
    kpos = w0 + lax.broadcasted_iota(jnp.int32, (tq, wlen), 1)
    win_mask = per_head(jnp.where((kpos <= q_pos(wlen)) & (kpos > q_pos(wlen) - WINDOW), 0.0, NEG))
    kt_diag = qs // tk
    kpos = kt_diag * tk + lax.broadcasted_iota(jnp.int32, (tq, tk), 1)
    diag_mask = per_head(jnp.where(kpos <= q_pos(tk), 0.0, NEG))

    blk = lax.broadcasted_iota(jnp.int32, (tq, LANES), 1)
    t = q_pos(LANES)
    jt = t // SEL_BLOCK
    forced = (blk == 0) | (blk == jt) | (blk == jt - 1)
    valid = blk * SEL_BLOCK <= t
    n_rank = ((n_sel + 7) // 8) * 8
    j_row = lax.broadcasted_iota(jnp.int32, (n_rank, tq), 0)

    def with_ones(v, g):
        own = (lax.broadcasted_iota(jnp.int32, v.shape, 1) // HEAD_DIM) == g
        return jnp.where(own, v, jnp.ones_like(v))

    def normalised(acc):
        return acc * (1.0 / pltpu.roll(acc, HEAD_DIM, axis=1))

    groups = range(N_KV)
    q = [q_ref[0, HPG * g:HPG * (g + 1)].reshape(rows, LANES) for g in groups]

    o_cmp, q_ext = [], []
    for g in groups:
        s = _dot_nt(q[g], kc_ref[0]) + cmp_mask
        e = jnp.exp(s - jnp.max(s, axis=-1, keepdims=True))
        p = e * (cmp_any / jnp.sum(e, axis=-1, keepdims=True))
        o_cmp.append(_dot(p.astype(BF16), vc_ref[0]))
        p_sum = p[0:tq] + p[tq:2 * tq] + p[2 * tq:3 * tq] + p[3 * tq:4 * tq]
        imp = _split3_dot(p_sum, mimp_ref[...])

        score = jnp.where(valid, jnp.where(forced, FORCE, imp), NEG)
        score = jnp.where(blk < n_sel, score, BELOW_NEG)
        left = score.T[:n_rank]
        sel_t = jnp.zeros((n_rank, tq), F32)
        for _ in range(min(TOP_N, n_sel)):
            top = jnp.max(left, axis=0, keepdims=True)
            first = jnp.min(jnp.where(left == top, j_row, n_rank), axis=0, keepdims=True)
            taken = j_row == first
            sel_t = jnp.where(taken, 1.0, sel_t)
            left = jnp.where(taken, TAKEN, left)
        if n_rank < LANES:
            sel_t = jnp.concatenate([sel_t, jnp.zeros((LANES - n_rank, tq), F32)], axis=0)
        sel = sel_t.T
        block_bias = jnp.where(sel > 0.5, 0.0, NEG).astype(BF16)
        q_ext.append(jnp.concatenate([q[g], per_head(block_bias)], axis=1))
    q_ext = jnp.concatenate(q_ext, axis=0)

    m_sc[...] = jnp.full((N_KV * rows, LANES), NEG, F32)
    acc_sc[...] = jnp.zeros((N_KV * rows, LANES), F32)

    def key_tile(kt, causal):
        k0 = pl.multiple_of(kt * tk, tk)
        k_ext = jnp.concatenate([slc_ref[0, pl.ds(k0, tk), 0:LANES], expt_ref[kt]], axis=1)
        v = slc_ref[0, pl.ds(k0, tk), LANES:2 * LANES]
        s_all = _dot_nt(q_ext, k_ext)
        for g in groups:
            rs = slice(g * rows, (g + 1) * rows)
            s = s_all[rs]
            if causal:
                s = s + diag_mask
            m_prev = m_sc[rs]
            m_next = jnp.maximum(m_prev, jnp.max(s, axis=-1, keepdims=True))
            p = jnp.exp(s - jnp.concatenate([m_next] * (tk // LANES), axis=1))
            acc_sc[rs] = jnp.exp(m_prev - m_next) * acc_sc[rs] + _dot(p.astype(BF16), with_ones(v, g))
            m_sc[rs] = m_next

    def full_tile(kt, carry):
        key_tile(kt, False)
        return carry

    lax.fori_loop(0, kt_diag, full_tile, 0)
    key_tile(kt_diag, True)

    k = win_ref[0, pl.ds(w0, wlen), 0:LANES]
    v = win_ref[0, pl.ds(w0, wlen), LANES:2 * LANES]
    s_all = _dot_nt(jnp.concatenate(q, axis=0), k)
    for g in groups:
        rs = slice(g * rows, (g + 1) * rows)
        s = s_all[rs] + win_mask
        e = jnp.exp(s - jnp.max(s, axis=-1, keepdims=True))
        o_win = normalised(_dot(e.astype(BF16), with_ones(v, g)))
        o_slc = normalised(acc_sc[rs])

        heads = []
        for h in range(HPG):
            hs = slice(h * tq, (h + 1) * tq)
            c = 3 * (HPG * g + h)
            heads.append(br[:, c:c + 1] * o_cmp[g][hs] + br[:, c + 1:c + 2] * o_slc[hs]
                         + br[:, c + 2:c + 3] * o_win[hs])
        for hh in range(HPG // 2):
            even, odd = heads[2 * hh], heads[2 * hh + 1]
            if g == 0:
                odd = pltpu.roll(odd, HEAD_DIM, axis=1)
            else:
                even = pltpu.roll(even, HEAD_DIM, axis=1)
            c0 = (2 * g + hh) * LANES
            o_ref[:, c0:c0 + LANES] = (jnp.where(lane_lo, even, odd) * ga_ref[:, c0:c0 + LANES]).astype(BF16)


def _nsa_prompt(q_slots, kc, vc, slc_b, win_b, br, ga, *, n_batch, seq):
    tq = 128
    tk = min(512, seq)
    n_cmp = seq // CMP_STRIDE - (CMP_BLOCK // CMP_STRIDE) + 1
    n_sel = seq // SEL_BLOCK
    assert n_sel <= LANES and seq % tk == 0 and tk % tq == 0 and seq >= WINDOW + tq
    ncp = kc.shape[1]
    mimp = _importance_matrix(ncp, n_cmp, n_sel, LANES)
    key_blk = (jnp.arange(seq, dtype=jnp.int32) // SEL_BLOCK).reshape(seq // tk, tk, 1)
    expand = (key_blk == jnp.arange(LANES, dtype=jnp.int32).reshape(1, 1, LANES)).astype(BF16)
    nq = seq // tq
    rows = HPG * tq
    return pl.pallas_call(
        functools.partial(_nsa_prompt_kernel, tq=tq, tk=tk, n_cmp=n_cmp, n_sel=n_sel),
        grid=(n_batch, nq),
        in_specs=[pl.BlockSpec((1, N_HEADS, tq, LANES), lambda b, i: (b, 0, i, 0)),
                  pl.BlockSpec((1, ncp, LANES), lambda b, i: (b, 0, 0)),
                  pl.BlockSpec((1, ncp, LANES), lambda b, i: (b, 0, 0)),
                  pl.BlockSpec((1, seq, KV_ROW), lambda b, i: (b, 0, 0)),
                  pl.BlockSpec((1, seq, KV_ROW), lambda b, i: (b, 0, 0)),
                  pl.BlockSpec((tq, LANES), lambda b, i: (b * nq + i, 0)),
                  pl.BlockSpec((tq, D_NSA), lambda b, i: (b * nq + i, 0)),
                  pl.BlockSpec(mimp.shape, lambda b, i: (0, 0)),
                  pl.BlockSpec(expand.shape, lambda b, i: (0, 0, 0))],
        out_specs=pl.BlockSpec((tq, D_NSA), lambda b, i: (b * nq + i, 0)),
        out_shape=jax.ShapeDtypeStruct((n_batch * seq, D_NSA), BF16),
        scratch_shapes=[pltpu.VMEM((N_KV * rows, LANES), F32)] * 2,
        compiler_params=_cparams("arbitrary", "arbitrary"),
        name="nsa_prompt",
    )(q_slots, kc, vc, slc_b.reshape(n_batch, seq, KV_ROW), win_b.reshape(n_batch, seq, KV_ROW), br, ga, mimp,
      expand)


def _retention_prompt_kernel(rq_ref, rk_ref, rv_ref, o_ref, s_ref, *, cl):
    @pl.when(pl.program_id(1) == 0)
    def _():
        s_ref[...] = jnp.zeros(s_ref.shape, F32)

    i_pos = lax.broadcasted_iota(jnp.int32, (cl, cl), 0)
    j_pos = lax.broadcasted_iota(jnp.int32, (cl, cl), 1)
    diff = (i_pos - j_pos).astype(F32)
    i_col = lax.broadcasted_iota(jnp.int32, (cl, 1), 0).astype(F32)
    lane_lo = lax.broadcasted_iota(jnp.int32, (cl, LANES), 1) < DK_RET
    for pair in range(N_RET_HEADS // 2):
        q2 = rq_ref[0, :, pair * LANES:(pair + 1) * LANES]
        k2 = rk_ref[0, :, pair * LANES:(pair + 1) * LANES]
        s_pair = jnp.concatenate([s_ref[0, 2 * pair], s_ref[0, 2 * pair + 1]], axis=0)
        for odd in range(2):
            h = 2 * pair + odd
            log_g = math.log(1.0 - 2.0 ** (-5.0 - h))
            keep = lane_lo if odd == 0 else jnp.logical_not(lane_lo)
            qh = jnp.where(keep, q2, jnp.zeros_like(q2))
            v = rv_ref[0, :, h * DV_RET:(h + 1) * DV_RET]
            decay = jnp.where(diff >= 0, jnp.exp(log_g * jnp.maximum(diff, 0.0)), 0.0)
            a = _dot_nt(qh, k2) * decay
            xi = jnp.exp(log_g * (i_col + 1.0))
            zeta = jnp.exp(log_g * (cl - 1.0 - i_col))
            o_ref[0, :, h * DV_RET:(h + 1) * DV_RET] = (_dot(a.astype(BF16), v)
                                                         + _dot(qh, s_pair.astype(BF16)) * xi)
            kz = (k2.astype(F32) * zeta).astype(BF16)
            u = _dot_tn(kz, v)[odd * DK_RET:(odd + 1) * DK_RET]
            s_ref[0, h] = s_ref[0, h] * math.exp(log_g * cl) + u


def _retention_prompt(rq, rk, rv, *, n_batch, seq):
    cl = math.gcd(seq, RET_CHUNK)
    nrot = N_RET_HEADS * DK_RET
    return pl.pallas_call(
        functools.partial(_retention_prompt_kernel, cl=cl),
        grid=(n_batch, seq // cl),
        in_specs=[pl.BlockSpec((1, cl, nrot), lambda b, i: (b, i, 0)),
                  pl.BlockSpec((1, cl, nrot), lambda b, i: (b, i, 0)),
                  pl.BlockSpec((1, cl, D_RET), lambda b, i: (b, i, 0))],
        out_specs=[pl.BlockSpec((1, cl, D_RET), lambda b, i: (b, i, 0)),
                   pl.BlockSpec((1, N_RET_HEADS, DK_RET, DV_RET), lambda b, i: (b, 0, 0, 0))],
        out_shape=[jax.ShapeDtypeStruct((n_batch, seq, D_RET), F32),
                   jax.ShapeDtypeStruct((n_batch, N_RET_HEADS, DK_RET, DV_RET), F32)],
        compiler_params=_cparams("arbitrary", "arbitrary"),
        name="retention_prompt",
    )(rq.reshape(n_batch, seq, nrot), rk.reshape(n_batch, seq, nrot), rv.reshape(n_batch, seq, D_RET))


def _retention_step_kernel(rq_ref, rk_ref, rv_ref, s0_ref, o_ref, s_ref):
    lane_lo = lax.broadcasted_iota(jnp.int32, (8, LANES), 1) < DK_RET
    diag = (lax.broadcasted_iota(jnp.int32, (LANES, LANES), 0)
            == lax.broadcasted_iota(jnp.int32, (LANES, LANES), 1))
    for pair in range(N_RET_HEADS // 2):
        q2 = jnp.broadcast_to(rq_ref[0, :, pair * LANES:(pair + 1) * LANES], (8, LANES))
        k_row = rk_ref[0, :, pair * LANES:(pair + 1) * LANES]
        k2 = jnp.broadcast_to(k_row, (8, LANES))
        k_col = jnp.sum(jnp.where(diag, jnp.broadcast_to(k_row, (LANES, LANES)), 0.0), axis=-1, keepdims=True)
        s_pair = jnp.concatenate([s0_ref[0, 2 * pair], s0_ref[0, 2 * pair + 1]], axis=0)
        for odd in range(2):
            h = 2 * pair + odd
            g = math.exp(math.log(1.0 - 2.0 ** (-5.0 - h)))
            keep = lane_lo if odd == 0 else jnp.logical_not(lane_lo)
            qh = jnp.where(keep, q2, 0.0)
            v_row = rv_ref[0, :, h * DV_RET:(h + 1) * DV_RET]
            qk = jnp.sum(qh * k2, axis=-1, keepdims=True)
            o = qk * v_row + _dot(qh, s_pair) * g
            o_ref[0, :, h * DV_RET:(h + 1) * DV_RET] = o[0:1]
            u = (k_col * v_row)[odd * DK_RET:(odd + 1) * DK_RET]
            s_ref[0, h] = s0_ref[0, h] * g + u


def _retention_step(rq, rk, rv, s0):
    b = rq.shape[0]
    nrot = N_RET_HEADS * DK_RET
    state_spec = pl.BlockSpec((1, N_RET_HEADS, DK_RET, DV_RET), lambda i: (i, 0, 0, 0))
    return pl.pallas_call(
        _retention_step_kernel,
        grid=(b,),
        in_specs=[pl.BlockSpec((1, 1, nrot), lambda i: (i, 0, 0)),
                  pl.BlockSpec((1, 1, nrot), lambda i: (i, 0, 0)),
                  pl.BlockSpec((1, 1, D_RET), lambda i: (i, 0, 0)),
                  state_spec],
        out_specs=[pl.BlockSpec((1, 1, D_RET), lambda i: (i, 0, 0)), state_spec],
        out_shape=[jax.ShapeDtypeStruct((b, 1, D_RET), F32),
                   jax.ShapeDtypeStruct((b, N_RET_HEADS, DK_RET, DV_RET), F32)],
        compiler_params=_cparams("arbitrary"),
        name="retention_step",
    )(rq.astype(F32).reshape(b, 1, nrot), rk.astype(F32).reshape(b, 1, nrot), rv.astype(F32).reshape(b, 1, D_RET),
      s0)


def _nsa_step1_kernel(q_ref, kc_ref, vc_ref, win_ref, new_ref, mimp_ref, oc_ref, ow_ref, idx_ref, *,
                      t, n_cmp, n_sel):
    q = q_ref[0]
    nq = q.shape[0]
    ncp = kc_ref.shape[1]
    nsp = mimp_ref.shape[1]

    s = _dot_nt(q, kc_ref[0])
    col = lax.broadcasted_iota(jnp.int32, (nq, ncp), 1)
    mask = (col * CMP_STRIDE + (CMP_BLOCK - 1) <= t) & (col < n_cmp)
    s = jnp.where(mask, s, NEG)
    e = jnp.exp(s - jnp.max(s, axis=-1, keepdims=True))
    p = jnp.where(mask, e, 0.0) * (1.0 / jnp.sum(e, axis=-1, keepdims=True))
    oc_ref[0] = _dot(p.astype(BF16), vc_ref[0])
    imp_rows = _split3_dot(p, mimp_ref[...])

    blk = lax.broadcasted_iota(jnp.int32, (1, nsp), 1)
    jt = t // SEL_BLOCK
    forced = (blk == 0) | (blk == jt) | (blk == jt - 1)
    scores = []
    for g in range(N_KV):
        imp = jnp.sum(imp_rows[HPG * g:HPG * (g + 1)], axis=0, keepdims=True)
        sc = jnp.where(blk * SEL_BLOCK <= t, jnp.where(forced, FORCE, imp), NEG)
        scores.append(jnp.where(blk < n_sel, sc, BELOW_NEG))
    s_rows = jnp.concatenate(scores + [jnp.full((LANES - N_KV, nsp), BELOW_NEG, F32)], axis=0)
    s_cols = s_rows.T
    a_idx = lax.broadcasted_iota(jnp.int32, (nsp, nsp), 0)
    c_idx = lax.broadcasted_iota(jnp.int32, (nsp, nsp), 1)
    rank_lane = lax.broadcasted_iota(jnp.int32, (nsp, LANES), 1).astype(F32)
    a_val = lax.broadcasted_iota(jnp.int32, (nsp, LANES), 0).astype(F32)
    rows_out = []
    for g in range(N_KV):
        s_col = s_cols[:, g:g + 1]
        s_row = s_rows[g:g + 1, :]
        beats = (s_row > s_col) | ((s_row == s_col) & (c_idx < a_idx))
        rank = jnp.sum(jnp.where(beats, 1.0, 0.0), axis=-1, keepdims=True)
        rows_out.append(jnp.sum(jnp.where(rank == rank_lane, a_val, 0.0), axis=0, keepdims=True))
    idx = jnp.concatenate(rows_out + [jnp.zeros((8 - N_KV, LANES), F32)], axis=0)
    idx_ref[0] = idx.astype(jnp.int32)

    wb = win_ref.shape[-1]
    kw_t = win_ref[0, 0].reshape(LANES, wb).astype(BF16)
    vw_t = win_ref[0, 1].reshape(LANES, wb).astype(BF16)
    k_new = new_ref[0, :, 0:LANES]
    v_new = new_ref[0, :, LANES:2 * LANES]
    s = _dot(q, kw_t)
    kpos = t - wb + lax.broadcasted_iota(jnp.int32, (nq, wb), 1)
    s = jnp.where((kpos > t - WINDOW) & (kpos >= 0), s, NEG)
    s_new = jnp.sum(q.astype(F32) * k_new, axis=-1, keepdims=True)
    m = jnp.maximum(jnp.max(s, axis=-1, keepdims=True), s_new)
    e = jnp.exp(s - m)
    e_new = jnp.exp(s_new - m)
    ow_ref[0] = ((_dot_nt(e.astype(BF16), vw_t) + e_new * v_new)
                 * (1.0 / (jnp.sum(e, axis=-1, keepdims=True) + e_new)))


def _nsa_step1(q16, kc, vc, state_win_t, new_win, mimp, *, t, n_cmp, n_sel):
    b, nq, _ = q16.shape
    ncp = kc.shape[1]

    def per_b(shape):
        return pl.BlockSpec((1,) + shape, lambda i: (i,) + (0,) * len(shape))

    return pl.pallas_call(
        functools.partial(_nsa_step1_kernel, t=t, n_cmp=n_cmp, n_sel=n_sel),
        grid=(b,),
        in_specs=[per_b((nq, LANES)), per_b((ncp, LANES)), per_b((ncp, LANES)), per_b(state_win_t.shape[1:]),
                  per_b((1, KV_ROW)), pl.BlockSpec(mimp.shape, lambda i: (0, 0))],
        out_specs=[per_b((nq, LANES)), per_b((nq, LANES)), per_b((8, LANES))],
        out_shape=[jax.ShapeDtypeStruct((b, nq, LANES), F32), jax.ShapeDtypeStruct((b, nq, LANES), F32),
                   jax.ShapeDtypeStruct((b, 8, LANES), jnp.int32)],
        compiler_params=_cparams("arbitrary"),
        name="nsa_step_compressed_window",
    )(q16, kc, vc, state_win_t, new_win, mimp)


def _nsa_step2_kernel(idx_sm, pt_sm, q_ref, *refs, n_cache_blocks, k_eff):
    page_refs = refs[:N_KV * k_eff]
    new_ref, oc_ref, ow_ref, br_ref, o_ref = refs[N_KV * k_eff:]
    b = pl.program_id(0)
    q = q_ref[0]
    nq = q.shape[0]
    bpp = PAGE_SIZE // SEL_BLOCK
    nk = k_eff * PAGE_SIZE
    k_new = new_ref[0, :, 0:LANES]
    v_new = new_ref[0, :, LANES:2 * LANES]
    s_new_all = jnp.sum(q.astype(F32) * k_new, axis=-1, keepdims=True)
    lane_k = lax.broadcasted_iota(jnp.int32, (nq, nk), 1)
    slot = lane_k // PAGE_SIZE
    half = (lane_k % PAGE_SIZE) // SEL_BLOCK
    outs = []
    for g in range(N_KV):
        pages = [page_refs[g * k_eff + k][0] for k in range(k_eff)]
        keys_t = jnp.concatenate([p[0].reshape(LANES, PAGE_SIZE) for p in pages], axis=-1).astype(BF16)
        vals_t = jnp.concatenate([p[1].reshape(LANES, PAGE_SIZE) for p in pages], axis=-1).astype(BF16)
        allowed = jnp.zeros((nq, nk), jnp.int32)
        new_selected = jnp.int32(0)
        for k in range(k_eff):
            j = idx_sm[b, g * k_eff + k]
            in_block = (slot == k) & (half == j % bpp)
            allowed = jnp.where(in_block, (j < n_cache_blocks).astype(jnp.int32), allowed)
            new_selected = new_selected | (j == n_cache_blocks).astype(jnp.int32)
        s = jnp.where(allowed > 0, _dot(q, keys_t), NEG)
        s_new = jnp.where(new_selected > 0, s_new_all, NEG)
        m = jnp.maximum(jnp.max(s, axis=-1, keepdims=True), s_new)
        e = jnp.exp(s - m)
        e_new = jnp.exp(s_new - m)
        outs.append((_dot_nt(e.astype(BF16), vals_t) + e_new * v_new)
                    * (1.0 / (jnp.sum(e, axis=-1, keepdims=True) + e_new)))
    row = lax.broadcasted_iota(jnp.int32, (nq, LANES), 0)
    lane = lax.broadcasted_iota(jnp.int32, (nq, LANES), 1)
    o_slc = jnp.where(row < HPG, outs[0], outs[1])
    br = br_ref[0]
    gates = [jnp.sum(jnp.where(lane == 3 * row + j, br, 0.0), axis=-1, keepdims=True) for j in range(3)]
    o = gates[0] * oc_ref[0] + gates[1] * o_slc + gates[2] * ow_ref[0]
    lane_lo = lax.broadcasted_iota(jnp.int32, (1, LANES), 1) < HEAD_DIM
    for pair in range(N_HEADS // 2):
        even, odd = o[2 * pair:2 * pair + 1], o[2 * pair + 1:2 * pair + 2]
        if 2 * pair < HPG:
            odd = pltpu.roll(odd, HEAD_DIM, axis=1)
        else:
            even = pltpu.roll(even, HEAD_DIM, axis=1)
        o_ref[0, :, pair * LANES:(pair + 1) * LANES] = jnp.where(lane_lo, even, odd)


def _nsa_step2(idx, page_table, q16, cache_t, new_slc, o_cmp, o_win, br, *, n_cache_blocks, k_eff):
    b, nq, _ = q16.shape
    bpp = PAGE_SIZE // SEL_BLOCK

    def page_map(i, idx_sm, pt_sm, c):
        j = jnp.minimum(idx_sm[i, c], n_cache_blocks - 1)
        return (pt_sm[i, j // bpp], 0, 0, 0, 0)

    def per_b(shape):
        return pl.BlockSpec((1,) + shape, lambda i, idx_sm, pt_sm: (i,) + (0,) * len(shape))

    n_blk = N_KV * k_eff
    in_specs = [per_b((nq, LANES))]
    in_specs += [pl.BlockSpec((1,) + cache_t.shape[1:], functools.partial(page_map, c=c)) for c in range(n_blk)]
    in_specs += [per_b((1, KV_ROW)), per_b((nq, LANES)), per_b((nq, LANES)), per_b((1, LANES))]
    return pl.pallas_call(
        functools.partial(_nsa_step2_kernel, n_cache_blocks=n_cache_blocks, k_eff=k_eff),
        grid_spec=pltpu.PrefetchScalarGridSpec(
            num_scalar_prefetch=2,
            grid=(b,),
            in_specs=in_specs,
            out_specs=per_b((1, D_NSA)),
        ),
        out_shape=jax.ShapeDtypeStruct((b, 1, D_NSA), F32),
        compiler_params=_cparams("arbitrary"),
        name="nsa_step_selected",
    )(idx, page_table, q16, *([cache_t] * n_blk), new_slc, o_cmp, o_win, br)


def _out_kernel(*refs, mul_ga):
    if mul_ga:
        x_ref, gate_ref, ya_ref, ga_ref, oret_ref, gr_ref, gret_ref, w_ref, y_ref = refs
        ya = (ya_ref[...] * ga_ref[...]).astype(BF16)
    else:
        x_ref, gate_ref, ya_ref, oret_ref, gr_ref, gret_ref, w_ref, y_ref = refs
        ya = ya_ref[...]
    o = oret_ref[...]
    parts = []
    for h in range(N_RET_HEADS):
        t = o[:, h * DV_RET:(h + 1) * DV_RET]
        ms = jnp.mean(t * t, axis=-1, keepdims=True)
        parts.append(t * lax.rsqrt(ms + EPS) * gret_ref[...])
    yr = (jnp.concatenate(parts, axis=-1) * gr_ref[...]).astype(BF16)
    y_ref[...] = x_ref[...] + gate_ref[0] * _dot(jnp.concatenate([ya, yr], axis=-1), w_ref[...])


def _output(x2, gate, ya, ga, o_ret, gr, g_ret, w_out, *, n_batch, per_row_mod):
    m = x2.shape[0]
    seq = m // n_batch
    tm = min(seq, 512)
    nt = seq // tm
    if per_row_mod:
        gate_spec = pl.BlockSpec((1, tm, D_MODEL), lambda i: (0, i, 0))
    else:
        gate_spec = pl.BlockSpec((1, 1, D_MODEL), lambda i: (i // nt, 0, 0))

    def rows_spec(width):
        return pl.BlockSpec((tm, width), lambda i: (i, 0))

    mul_ga = ga is not None
    args = [x2, gate, ya] + ([ga] if mul_ga else []) + [o_ret, gr, g_ret.reshape(1, DV_RET), w_out.astype(BF16)]
    in_specs = [rows_spec(D_MODEL), gate_spec, rows_spec(D_NSA)] + ([rows_spec(D_NSA)] if mul_ga else [])
    in_specs += [rows_spec(D_RET), rows_spec(D_RET), pl.BlockSpec((1, DV_RET), lambda i: (0, 0)),
                 pl.BlockSpec((D_MODEL, D_MODEL), lambda i: (0, 0))]
    return pl.pallas_call(
        functools.partial(_out_kernel, mul_ga=mul_ga),
        grid=(m // tm,),
        in_specs=in_specs,
        out_specs=rows_spec(D_MODEL),
        out_shape=jax.ShapeDtypeStruct((m, D_MODEL), F32),
        compiler_params=_cparams("arbitrary"),
        name="output_projection",
    )(*args)


def kernel(x_prompt, x_sample, c_prompt, c_sample, cache_cmp, cache_slc, state_win, state_ret, page_table, g_norm, w_ada, b_ada, w_in, g_q, g_kc, g_ks, g_kw, pe_ck, w_ck1, w_ck2, pe_cv, w_cv1, w_cv2, g_ret, w_out):
    assert g_norm.shape[0] == 1, "one layer"
    bp, seq, _ = x_prompt.shape
    bs, dec_seq, _ = x_sample.shape
    assert dec_seq == 1
    n_pages = page_table.shape[1]
    past = n_pages * PAGE_SIZE
    assert state_win.shape[2] == WINDOW and past >= WINDOW

    w_proj, gq_slots, gks2, gkw2 = _prep_proj_weights(w_in[0], g_q[0], g_ks[0], g_kw[0])
    w1x, w1kv, w2x, pe8, gkc2 = _prep_compress_weights(pe_ck[0], w_ck1[0], w_ck2[0], pe_cv[0], w_cv1[0], w_cv2[0], g_kc[0])

    pad = (-(bp + bs)) % 8
    c_all = jnp.concatenate([c_prompt, c_sample, jnp.zeros((pad, D_MODEL), F32)], axis=0)
    mod = _modulation(c_all, w_ada[0], b_ada[0])
    shift_p, scale_p, gate_p = (mod[:bp, k * D_MODEL:(k + 1) * D_MODEL].reshape(bp, 1, D_MODEL) for k in range(3))
    shift_s, scale_s, gate_s = (mod[bp:bp + bs, k * D_MODEL:(k + 1) * D_MODEL].reshape(1, bs, D_MODEL)
                                for k in range(3))

    xp2 = x_prompt.reshape(bp * seq, D_MODEL)
    cos_p, sin_p = _rope_tables(seq, 0, 1)
    (q_p, cmp_p, slc_p, slcb_p, win_p, winb_p, br_p, ga_p, rq_p, rk_p, rv_p, gr_p) = _projection(
        xp2, scale_p, shift_p, g_norm[0], w_proj, gq_slots, gks2, gkw2, cos_p, sin_p, n_batch=bp, per_row_mod=False)

    a_p = _chunk_matmul_dense(cmp_p.reshape(bp, seq // CMP_STRIDE, CHUNK_LANES), w1x)
    kc_p, vc_p = _compress_combine(a_p, jnp.zeros((bp, 1, KV_ROW), F32), pe8, w1x, w2x, gkc2)
    ya_p = _nsa_prompt(q_p, kc_p, vc_p, slcb_p, winb_p, br_p, ga_p, n_batch=bp, seq=seq)
    oret_p, sret_p = _retention_prompt(rq_p, rk_p, rv_p, n_batch=bp, seq=seq)
    y_p = _output(xp2, gate_p, ya_p, None, oret_p.reshape(bp * seq, D_RET), gr_p, g_ret[0], w_out[0],
                  n_batch=bp, per_row_mod=False)

    xs2 = x_sample.reshape(bs, D_MODEL)
    cos_s, sin_s = _rope_tables(bs, past, 0)
    (q_s, cmp_s, slc_s, _, win_s, _, br_s, ga_s, rq_s, rk_s, rv_s, gr_s) = _projection(
        xs2, scale_s, shift_s, g_norm[0], w_proj, gq_slots, gks2, gkw2, cos_s, sin_s, n_batch=1, per_row_mod=True)

    n_chunk_cache = past // CMP_STRIDE
    n_cmp_s = n_chunk_cache
    n_sel_s = past // SEL_BLOCK + 1
    nsp = ((n_sel_s + LANES - 1) // LANES) * LANES
    pages_per_step = math.gcd(n_pages, 32)
    to_native = (0, 2, 3, 4, 1)
    a_s = _chunk_matmul_paged(jnp.transpose(cache_cmp[0], to_native), page_table, w1kv, pages_per_step)
    kc_s, vc_s = _compress_combine(a_s, cmp_s.reshape(bs, 1, KV_ROW), pe8, w1x, w2x, gkc2)

    q16 = jnp.pad(q_s[0].transpose(1, 0, 2), ((0, 0), (0, 16 - N_HEADS), (0, 0)))
    mimp_s = _importance_matrix(n_cmp_s, n_cmp_s, n_sel_s, nsp)
    o_cmp_s, o_win_s, idx_s = _nsa_step1(q16, kc_s, vc_s, jnp.transpose(state_win[0], to_native),
                                         win_s.reshape(bs, 1, KV_ROW), mimp_s, t=past, n_cmp=n_cmp_s,
                                         n_sel=n_sel_s)
    k_eff = min(TOP_N, n_sel_s)
    idx_flat = idx_s[:, :N_KV, :k_eff].reshape(bs, N_KV * k_eff)
    o_nsa_s = _nsa_step2(idx_flat, page_table, q16, jnp.transpose(cache_slc[0], to_native),
                         slc_s.reshape(bs, 1, KV_ROW), o_cmp_s, o_win_s, br_s.reshape(bs, 1, LANES),
                         n_cache_blocks=past // SEL_BLOCK, k_eff=k_eff)

    oret_s, sret_s = _retention_step(rq_s, rk_s, rv_s, state_ret[0])
    y_s = _output(xs2, gate_s, o_nsa_s.reshape(bs, D_NSA), ga_s, oret_s.reshape(bs, D_RET), gr_s, g_ret[0], w_out[0],
                  n_batch=1, per_row_mod=True)

    def kv_rows(a, b, l):
        return a.reshape(1, b, l, 2, N_KV, HEAD_DIM)

    keep_p = min(WINDOW, seq)
    s_win = jnp.concatenate([state_win[0][:, 1:].reshape(bs, WINDOW - 1, KV_ROW), win_s.reshape(bs, 1, KV_ROW)],
                            axis=1)
    return (y_p.reshape(bp, seq, D_MODEL), y_s.reshape(bs, 1, D_MODEL),
            kv_rows(cmp_p, bp, seq), kv_rows(slc_p, bp, seq),
            kv_rows(win_p.reshape(bp, seq, KV_ROW)[:, seq - keep_p:], bp, keep_p), sret_p[None],
            kv_rows(cmp_s, bs, 1), kv_rows(slc_s, bs, 1), kv_rows(s_win, bs, WINDOW), sret_s[None])
```

```python
import functools
import math

import jax
import jax.numpy as jnp
import numpy as np
from jax import lax
from jax.experimental import pallas as pl
from jax.experimental.pallas import tpu as pltpu

D_MODEL = 1024
HEAD_DIM = 64
D_NSA = 512
N_HEADS = 8
N_KV = 2
HPG = 4
CMP_BLOCK = 32
CMP_STRIDE = 16
SEL_BLOCK = 64
TOP_N = 16
WINDOW = 512
D_RET = 512
N_RET_HEADS = 4
DV_RET = 128
DK_RET = 64
RET_CHUNK = 128
PAGE_SIZE = 128
ROPE_BASE = 10000.0
EPS = 1e-6
NEG = -1e30
FORCE = 1e4
BELOW_NEG = -2e38
TAKEN = -3e38

KV_ROW = 2 * N_KV * HEAD_DIM
CHUNK_LANES = CMP_STRIDE * KV_ROW
LANES = 128
VMEM_LIMIT_BYTES = 56 * 1024 * 1024

F32 = jnp.float32
BF16 = jnp.bfloat16

O_Q = 0
O_CMP = O_Q + N_HEADS * LANES
O_SLC = O_CMP + KV_ROW
O_WIN = O_SLC + KV_ROW
O_BR = O_WIN + KV_ROW
O_GA = O_BR + LANES
O_RQ = O_GA + D_NSA
O_RK = O_RQ + 2 * N_RET_HEADS * DK_RET
O_RV = O_RK + 2 * N_RET_HEADS * DK_RET
O_GR = O_RV + D_RET
D_IN_PAD = O_GR + D_RET


def _cparams(*sem):
    return pltpu.CompilerParams(dimension_semantics=sem, vmem_limit_bytes=VMEM_LIMIT_BYTES)


def _dot(a, b):
    return jnp.dot(a, b, preferred_element_type=F32)


def _dot_nt(a, b):
    return lax.dot_general(a, b, (((1,), (1,)), ((), ())), preferred_element_type=F32)


def _dot_tn(a, b):
    return lax.dot_general(a, b, (((0,), (0,)), ((), ())), preferred_element_type=F32)


def _silu(x):
    return x * jax.nn.sigmoid(x)


def _split3_dot(p, m):
    p1 = p.astype(BF16)
    r1 = p - p1.astype(F32)
    p2 = r1.astype(BF16)
    p3 = (r1 - p2.astype(F32)).astype(BF16)
    return _dot(p1, m) + _dot(p2, m) + _dot(p3, m)


def _pair_rmsnorm(k, gain2):
    lo = lax.broadcasted_iota(jnp.int32, k.shape, 1) < HEAD_DIM
    sq = k * k
    s0 = jnp.sum(jnp.where(lo, sq, 0.0), axis=-1, keepdims=True)
    s1 = jnp.sum(jnp.where(lo, 0.0, sq), axis=-1, keepdims=True)
    ms = jnp.where(lo, s0, s1) * (1.0 / HEAD_DIM)
    return k * lax.rsqrt(ms + EPS) * gain2


def _mod_kernel(c_ref, w_ref, b_ref, o_ref):
    o_ref[...] = _dot(_silu(c_ref[...]).astype(BF16), w_ref[...]) + b_ref[...]


def _modulation(c_all, w_ada, b_ada):
    rows = c_all.shape[0]
    n = w_ada.shape[1]
    tn = 1024
    return pl.pallas_call(
        _mod_kernel,
        grid=(n // tn,),
        in_specs=[pl.BlockSpec((rows, D_MODEL), lambda j: (0, 0)),
                  pl.BlockSpec((D_MODEL, tn), lambda j: (0, j)),
                  pl.BlockSpec((1, tn), lambda j: (0, j))],
        out_specs=pl.BlockSpec((rows, tn), lambda j: (0, j)),
        out_shape=jax.ShapeDtypeStruct((rows, n), F32),
        compiler_params=_cparams("arbitrary"),
        name="adaln_modulation",
    )(c_all, w_ada.astype(BF16), b_ada.reshape(1, n))


def _rope_kernel(freq_ref, cos_ref, sin_ref, *, base, stride, rows):
    shape = (rows, LANES)
    r = lax.broadcasted_iota(jnp.int32, shape, 0)
    pos = (base + stride * (pl.program_id(0) * rows + r)).astype(F32)
    ang = pos * freq_ref[...]
    lane = lax.broadcasted_iota(jnp.int32, shape, 1)
    sign = jnp.where((lane & (DK_RET - 1)) < DK_RET // 2, -1.0, 1.0)
    cos_ref[...] = jnp.cos(ang)
    sin_ref[...] = jnp.sin(ang) * sign


def _rope_tables(n_rows, base, stride):
    half = DK_RET // 2
    freqs = ROPE_BASE ** (-jnp.arange(half, dtype=F32) / half)
    freq_row = jnp.tile(freqs, LANES // half).reshape(1, LANES)
    rows = min(n_rows, 512)
    return pl.pallas_call(
        functools.partial(_rope_kernel, base=base, stride=stride, rows=rows),
        grid=(n_rows // rows,),
        in_specs=[pl.BlockSpec((1, LANES), lambda i: (0, 0))],
        out_specs=[pl.BlockSpec((rows, LANES), lambda i: (i, 0))] * 2,
        out_shape=[jax.ShapeDtypeStruct((n_rows, LANES), F32)] * 2,
        compiler_params=_cparams("arbitrary"),
        name="rope_tables",
    )(freq_row)


def _proj_kernel(x_ref, scale_ref, shift_ref, gn_ref, w_ref, gq_ref, gks_ref, gkw_ref, cos_ref, sin_ref,
                 q_ref, cmp_ref, slc_ref, slcb_ref, win_ref, winb_ref, br_ref, ga_ref, rq_ref, rk_ref,
                 rv_ref, gr_ref, *, pos_minor):
    x = x_ref[...]
    ms = jnp.mean(x * x, axis=-1, keepdims=True)
    xn = x * lax.rsqrt(ms + EPS) * gn_ref[...]
    hb = (xn * (1.0 + scale_ref[0]) + shift_ref[0]).astype(BF16)

    def seg(start, width):
        return _dot(hb, w_ref[:, start:start + width])

    yq = seg(O_Q, N_HEADS * LANES)
    for s in range(N_HEADS):
        t = yq[:, s * LANES:(s + 1) * LANES]
        ms = jnp.sum(t * t, axis=-1, keepdims=True) * (1.0 / HEAD_DIM)
        q_ref[0, s] = (t * lax.rsqrt(ms + EPS) * gq_ref[s]).astype(BF16)

    y_cmp = seg(O_CMP, KV_ROW)
    y = seg(O_SLC, KV_ROW)
    y_slc = jnp.concatenate([_pair_rmsnorm(y[:, :LANES], gks_ref[...]), y[:, LANES:]], axis=-1)
    slcb_ref[...] = y_slc.astype(BF16)
    y = seg(O_WIN, KV_ROW)
    y_win = jnp.concatenate([_pair_rmsnorm(y[:, :LANES], gkw_ref[...]), y[:, LANES:]], axis=-1)
    winb_ref[...] = y_win.astype(BF16)
    for ref, rows in ((cmp_ref, y_cmp), (slc_ref, y_slc), (win_ref, y_win)):
        if pos_minor:
            ref[0] = rows.T
        else:
            ref[...] = rows

    br_ref[...] = jax.nn.sigmoid(seg(O_BR, LANES))
    ga_ref[...] = _silu(seg(O_GA, D_NSA))

    cos2 = jnp.concatenate([cos_ref[...]] * 2, axis=-1)
    sin2 = jnp.concatenate([sin_ref[...]] * 2, axis=-1)
    nrot = N_RET_HEADS * DK_RET
    y = seg(O_RQ, 2 * nrot)
    rq_ref[...] = (y[:, :nrot] * cos2 + y[:, nrot:] * sin2).astype(BF16)
    y = seg(O_RK, 2 * nrot)
    rk_ref[...] = ((y[:, :nrot] * cos2 + y[:, nrot:] * sin2) * DK_RET ** -0.5).astype(BF16)
    rv_ref[...] = seg(O_RV, D_RET).astype(BF16)
    gr_ref[...] = _silu(seg(O_GR, D_RET))


def _swap_halves(w):
    k = w.shape[0]
    w4 = w.reshape(k, N_RET_HEADS, 2, DK_RET // 2)
    return w4[:, :, ::-1, :].reshape(k, N_RET_HEADS * DK_RET)


def _prep_proj_weights(w_in, g_q, g_ks, g_kw):
    k = w_in.shape[0]
    o = 0
    wq = w_in[:, o:o + D_NSA].reshape(k, N_KV, HPG, HEAD_DIM)
    o += D_NSA
    z = jnp.zeros_like(wq)
    wq = jnp.stack([jnp.concatenate([wq[:, 0], z[:, 0]], axis=-1),
                    jnp.concatenate([z[:, 1], wq[:, 1]], axis=-1)], axis=1).reshape(k, N_HEADS * LANES)
    w_kv = w_in[:, o:o + 3 * KV_ROW]
    o += 3 * KV_ROW
    n_br = 3 * N_HEADS
    w_br = jnp.pad(w_in[:, o:o + n_br], ((0, 0), (0, LANES - n_br)))
    o += n_br
    w_ga = w_in[:, o:o + D_NSA]
    o += D_NSA
    nrot = N_RET_HEADS * DK_RET
    w_rq = w_in[:, o:o + nrot]
    o += nrot
    w_rk = w_in[:, o:o + nrot]
    o += nrot
    w_rv = w_in[:, o:o + D_RET]
    o += D_RET
    w_gr = w_in[:, o:o + D_RET]
    w = jnp.concatenate([wq, w_kv, w_br, w_ga, w_rq, _swap_halves(w_rq), w_rk, _swap_halves(w_rk), w_rv, w_gr],
                        axis=1).astype(BF16)
    assert w.shape[1] == D_IN_PAD
    gz = jnp.zeros((HEAD_DIM,), F32)
    gq = g_q * HEAD_DIM ** -0.5
    gq_slots = jnp.stack([jnp.concatenate([gq, gz]) if s < HPG else jnp.concatenate([gz, gq])
                          for s in range(N_HEADS)]).reshape(N_HEADS, 1, LANES)
    return w, gq_slots, jnp.tile(g_ks, 2).reshape(1, LANES), jnp.tile(g_kw, 2).reshape(1, LANES)


def _projection(x2, scale, shift, g_norm, w, gq_slots, gks2, gkw2, cos_t, sin_t, *, n_batch, per_row_mod,
                pos_minor):
    m = x2.shape[0]
    seq = m // n_batch
    tm = min(seq, 256)
    nt = seq // tm
    if per_row_mod:
        mod_spec = pl.BlockSpec((1, tm, D_MODEL), lambda i: (0, i, 0))
    else:
        mod_spec = pl.BlockSpec((1, 1, D_MODEL), lambda i: (i // nt, 0, 0))

    def rows_spec(width):
        return pl.BlockSpec((tm, width), lambda i: (i, 0))

    def const_spec(shape):
        return pl.BlockSpec(shape, lambda i: (0,) * len(shape))

    kv_out = "kv rows"
    widths = [(KV_ROW, kv_out), (KV_ROW, kv_out), (KV_ROW, BF16), (KV_ROW, kv_out), (KV_ROW, BF16), (LANES, F32),
              (D_NSA, F32), (N_RET_HEADS * DK_RET, BF16), (N_RET_HEADS * DK_RET, BF16), (D_RET, BF16),
              (D_RET, F32)]
    out_shape = [jax.ShapeDtypeStruct((n_batch, N_HEADS, seq, LANES), BF16)]
    out_specs = [pl.BlockSpec((1, N_HEADS, tm, LANES), lambda i: (i // nt, 0, i % nt, 0))]
    for width, dt in widths:
        if pos_minor and dt is kv_out:
            out_shape.append(jax.ShapeDtypeStruct((n_batch, KV_ROW, seq), F32))
            out_specs.append(pl.BlockSpec((1, KV_ROW, tm), lambda i: (i // nt, 0, i % nt)))
        else:
            out_shape.append(jax.ShapeDtypeStruct((m, width), F32 if dt is kv_out else dt))
            out_specs.append(rows_spec(width))
    return pl.pallas_call(
        functools.partial(_proj_kernel, pos_minor=pos_minor),
        grid=(m // tm,),
        in_specs=[rows_spec(D_MODEL), mod_spec, mod_spec, const_spec((1, D_MODEL)),
                  const_spec((D_MODEL, D_IN_PAD)), const_spec((N_HEADS, 1, LANES)), const_spec((1, LANES)),
                  const_spec((1, LANES)), pl.BlockSpec((tm, LANES), lambda i: (i % nt, 0)),
                  pl.BlockSpec((tm, LANES), lambda i: (i % nt, 0))],
        out_specs=out_specs,
        out_shape=out_shape,
        compiler_params=_cparams("arbitrary"),
        name="input_projection",
    )(x2, scale, shift, g_norm.reshape(1, D_MODEL), w, gq_slots, gks2, gkw2, cos_t, sin_t)


def _chunk_matmul_paged_kernel(pt_ref, *refs, n_in):
    x_refs = refs[:n_in]
    perm_ref, w_ref, o_ref = refs[n_in:]
    cpp = PAGE_SIZE // CMP_STRIDE
    perm = perm_ref[...]
    moved = [_dot_nt(perm, x_refs[k][0].reshape(KV_ROW, PAGE_SIZE).astype(BF16)) for k in range(n_in)]
    for kv in range(2):
        x = jnp.concatenate(
            [jnp.concatenate([r[cpp * l:cpp * (l + 1), kv * LANES:(kv + 1) * LANES] for l in range(CMP_STRIDE)],
                             axis=1) for r in moved], axis=0)
        o_ref[0, :, kv * KV_ROW:(kv + 1) * KV_ROW] = _dot(x.astype(BF16), w_ref[kv])


def _chunk_matmul_paged(rows_t, page_table, w1kv, *, n_batch, n_pages):
    b = n_batch
    cpp = PAGE_SIZE // CMP_STRIDE
    p = math.gcd(n_pages, 32)
    row = jnp.arange(PAGE_SIZE, dtype=jnp.int32)
    perm = ((row % cpp) * CMP_STRIDE + row // cpp).reshape(PAGE_SIZE, 1) == row.reshape(1, PAGE_SIZE)
    page_shape = (1,) + rows_t.shape[1:4] + (PAGE_SIZE,)
    if page_table is None:
        page_table = jnp.zeros((1, 1), jnp.int32)
        in_specs = [pl.BlockSpec(page_shape, lambda i, j, pt, k=k: (i, 0, 0, 0, j * p + k)) for k in range(p)]
    else:
        in_specs = [pl.BlockSpec(page_shape, lambda i, j, pt, k=k: (pt[i, j * p + k], 0, 0, 0, 0))
                    for k in range(p)]
    in_specs.append(pl.BlockSpec((PAGE_SIZE, PAGE_SIZE), lambda i, j, pt: (0, 0)))
    in_specs.append(pl.BlockSpec(w1kv.shape, lambda i, j, pt: (0, 0, 0)))
    return pl.pallas_call(
        functools.partial(_chunk_matmul_paged_kernel, n_in=p),
        grid_spec=pltpu.PrefetchScalarGridSpec(
            num_scalar_prefetch=1,
            grid=(b, n_pages // p),
            in_specs=in_specs,
            out_specs=pl.BlockSpec((1, p * cpp, 2 * KV_ROW), lambda i, j, pt: (i, j, 0)),
        ),
        out_shape=jax.ShapeDtypeStruct((b, n_pages * cpp, 2 * KV_ROW), F32),
        compiler_params=_cparams("arbitrary", "arbitrary"),
        name="compress_chunks_paged",
    )(page_table, *([rows_t] * p), perm.astype(BF16), w1kv)


def _halves(a):
    lo = jnp.concatenate([a[:, 0:LANES], a[:, 2 * LANES:3 * LANES]], axis=-1)
    hi = jnp.concatenate([a[:, LANES:2 * LANES], a[:, 3 * LANES:4 * LANES]], axis=-1)
    return lo, hi


def _compress_combine_kernel(a_ref, new_ref, pe_ref, w1_ref, w2_ref, gkc_ref, kc_ref, vc_ref, *, n):
    lo, hi = _halves(a_ref[0])
    pe_lo, pe_hi = _halves(_dot(pe_ref[...], w1_ref[...]))
    bias = pe_lo[0:1] + pe_hi[1:2]
    new8 = jnp.broadcast_to(new_ref[0], (8, KV_ROW)).astype(BF16)
    a_new = _halves(_dot(new8, w1_ref[0:KV_ROW, :]))[1][0:1]
    hi = pltpu.roll(hi, n - 1, axis=0)
    row = lax.broadcasted_iota(jnp.int32, (n, KV_ROW), 0)
    hi = jnp.where(row == n - 1, a_new, hi)
    out = _dot(_silu(lo + hi + bias).astype(BF16), w2_ref[...])
    kc_ref[0] = _pair_rmsnorm(out[:, :LANES], gkc_ref[...]).astype(BF16)
    vc_ref[0] = out[:, LANES:].astype(BF16)


def _compress_combine(a, new_rows, pe8, w1x, w2x, gkc2):
    b, n, _ = a.shape
    return pl.pallas_call(
        functools.partial(_compress_combine_kernel, n=n),
        grid=(b,),
        in_specs=[pl.BlockSpec((1, n, 2 * KV_ROW), lambda i: (i, 0, 0)),
                  pl.BlockSpec((1, 1, KV_ROW), lambda i: (i, 0, 0)),
                  pl.BlockSpec(pe8.shape, lambda i: (0, 0)),
                  pl.BlockSpec(w1x.shape, lambda i: (0, 0)),
                  pl.BlockSpec(w2x.shape, lambda i: (0, 0)),
                  pl.BlockSpec((1, LANES), lambda i: (0, 0))],
        out_specs=[pl.BlockSpec((1, n, LANES), lambda i: (i, 0, 0))] * 2,
        out_shape=[jax.ShapeDtypeStruct((b, n, LANES), BF16)] * 2,
        compiler_params=_cparams("arbitrary"),
        name="compress_combine",
    )(a, new_rows, pe8, w1x, w2x, gkc2)


def _prep_compress_weights(pe_ck, w_ck1, w_ck2, pe_cv, w_cv1, w_cv2, g_kc):
    eye = jnp.eye(N_KV, dtype=F32)
    w1 = jnp.stack([w_ck1, w_cv1]).reshape(2, 2, CMP_STRIDE, HEAD_DIM, HEAD_DIM)
    w1x = jnp.einsum('khldf,kq,gp->lkgdqhpf', w1, eye, eye).reshape(CHUNK_LANES, 2 * KV_ROW).astype(BF16)
    w1kv = jnp.einsum('khldf,gp->klgdhpf', w1, eye).reshape(2, CMP_STRIDE * LANES, KV_ROW).astype(BF16)
    w2 = jnp.stack([w_ck2, w_cv2])
    w2x = jnp.einsum('kfd,kq,gp->kgfqpd', w2, eye, eye).reshape(KV_ROW, KV_ROW).astype(BF16)
    pe = jnp.stack([pe_ck, pe_cv]).reshape(2, 2, CMP_STRIDE, HEAD_DIM)
    pe_rows = jnp.broadcast_to(pe.transpose(1, 2, 0, 3)[:, :, :, None, :],
                               (2, CMP_STRIDE, 2, N_KV, HEAD_DIM)).reshape(2, CHUNK_LANES)
    pe8 = jnp.zeros((8, CHUNK_LANES), F32).at[:2].set(pe_rows).astype(BF16)
    return w1x, w1kv, w2x, pe8, jnp.tile(g_kc, 2).reshape(1, LANES)


def _importance_matrix(n_cmp_pad, n_cmp, n_sel, n_sel_pad):
    ratio = SEL_BLOCK // CMP_STRIDE
    span = CMP_BLOCK // CMP_STRIDE
    m = np.zeros((n_cmp_pad, n_sel_pad), np.float32)
    for j in range(n_sel):
        for r in range(ratio):
            for s in range(span):
                n = ratio * j + r - s
                if 0 <= n < n_cmp:
                    m[n, j] += 1.0
    return jnp.asarray(m, BF16)


def _nsa_prompt_kernel(q_ref, kc_ref, vc_ref, slc_ref, win_ref, br_ref, ga_ref, mimp_ref, expt_ref, o_ref,
                       m_sc, acc_sc, *, tq, tk, n_cmp, n_sel):
    qs = pl.program_id(1) * tq
    rows = HPG * tq
    ncp = kc_ref.shape[1]
    wlen = WINDOW + tq
    br = br_ref[...]
    lane_lo = lax.broadcasted_iota(jnp.int32, (tq, LANES), 1) < HEAD_DIM

    def q_pos(width):
        return qs + lax.broadcasted_iota(jnp.int32, (tq, width), 0)

    def per_head(x):
        return jnp.concatenate([x] * HPG, axis=0)

    col = lax.broadcasted_iota(jnp.int32, (tq, ncp), 1)
    cmp_mask = per_head(jnp.where((col * CMP_STRIDE + (CMP_BLOCK - 1) <= q_pos(ncp)) & (col < n_cmp), 0.0, NEG))
    cmp_any = per_head(jnp.where(q_pos(1) >= CMP_BLOCK - 1, 1.0, 0.0))
    w0 = pl.multiple_of(jnp.maximum(qs - WINDOW, 0), tq)
    kpos = w0 + lax.broadcasted_iota(jnp.int32, (tq, wlen), 1)
    win_mask = per_head(jnp.where((kpos <= q_pos(wlen)) & (kpos > q_pos(wlen) - WINDOW), 0.0, NEG))
    kt_diag = qs // tk
    kpos = kt_diag * tk + lax.broadcasted_iota(jnp.int32, (tq, tk), 1)
    diag_mask = per_head(jnp.where(kpos <= q_pos(tk), 0.0, NEG))

    blk = lax.broadcasted_iota(jnp.int32, (tq, LANES), 1)
    t = q_pos(LANES)
    jt = t // SEL_BLOCK
    forced = (blk == 0) | (blk == jt) | (blk == jt - 1)
    valid = blk * SEL_BLOCK <= t
    n_rank = ((n_sel + 7) // 8) * 8
    j_row = lax.broadcasted_iota(jnp.int32, (n_rank, tq), 0)

    def with_ones(v, g):
        own = (lax.broadcasted_iota(jnp.int32, v.shape, 1) // HEAD_DIM) == g
        return jnp.where(own, v, jnp.ones_like(v))

    def normalised(acc):
        return acc * (1.0 / pltpu.roll(acc, HEAD_DIM, axis=1))

    groups = range(N_KV)
    q = [q_ref[0, HPG * g:HPG * (g + 1)].reshape(rows, LANES) for g in groups]

    o_cmp, q_ext = [], []
    for g in groups:
        s = _dot_nt(q[g], kc_ref[0]) + cmp_mask
        e = jnp.exp(s - jnp.max(s, axis=-1, keepdims=True))
        p = e * (cmp_any / jnp.sum(e, axis=-1, keepdims=True))
        o_cmp.append(_dot(p.astype(BF16), vc_ref[0]))
        p_sum = p[0:tq] + p[tq:2 * tq] + p[2 * tq:3 * tq] + p[3 * tq:4 * tq]
        imp = _split3_dot(p_sum, mimp_ref[...])

        score = jnp.where(valid, jnp.where(forced, FORCE, imp), NEG)
        score = jnp.where(blk < n_sel, score, BELOW_NEG)
        left = score.T[:n_rank]
        sel_t = jnp.zeros((n_rank, tq), F32)
        for _ in range(min(TOP_N, n_sel)):
            top = jnp.max(left, axis=0, keepdims=True)
            first = jnp.min(jnp.where(left == top, j_row, n_rank), axis=0, keepdims=True)
            taken = j_row == first
            sel_t = jnp.where(taken, 1.0, sel_t)
            left = jnp.where(taken, TAKEN, left)
        if n_rank < LANES:
            sel_t = jnp.concatenate([sel_t, jnp.zeros((LANES - n_rank, tq), F32)], axis=0)
        sel = sel_t.T
        block_bias = jnp.where(sel > 0.5, 0.0, NEG).astype(BF16)
        q_ext.append(jnp.concatenate([q[g], per_head(block_bias)], axis=1))
    q_ext = jnp.concatenate(q_ext, axis=0)

    m_sc[...] = jnp.full((N_KV * rows, LANES), NEG, F32)
    acc_sc[...] = jnp.zeros((N_KV * rows, LANES), F32)

    def key_tile(kt, causal):
        k0 = pl.multiple_of(kt * tk, tk)
        k_ext = jnp.concatenate([slc_ref[0, pl.ds(k0, tk), 0:LANES], expt_ref[kt]], axis=1)
        v = slc_ref[0, pl.ds(k0, tk), LANES:2 * LANES]
        s_all = _dot_nt(q_ext, k_ext)
        for g in groups:
            rs = slice(g * rows, (g + 1) * rows)
            s = s_all[rs]
            if causal:
                s = s + diag_mask
            m_prev = m_sc[rs]
            m_next = jnp.maximum(m_prev, jnp.max(s, axis=-1, keepdims=True))
            p = jnp.exp(s - jnp.concatenate([m_next] * (tk // LANES), axis=1))
            acc_sc[rs] = jnp.exp(m_prev - m_next) * acc_sc[rs] + _dot(p.astype(BF16), with_ones(v, g))
            m_sc[rs] = m_next

    def full_tile(kt, carry):
        key_tile(kt, False)
        return carry

    lax.fori_loop(0, kt_diag, full_tile, 0)
    key_tile(kt_diag, True)

    k = win_ref[0, pl.ds(w0, wlen), 0:LANES]
    v = win_ref[0, pl.ds(w0, wlen), LANES:2 * LANES]
    s_all = _dot_nt(jnp.concatenate(q, axis=0), k)
    for g in groups:
        rs = slice(g * rows, (g + 1) * rows)
        s = s_all[rs] + win_mask
        e = jnp.exp(s - jnp.max(s, axis=-1, keepdims=True))
        o_win = normalised(_dot(e.astype(BF16), with_ones(v, g)))
        o_slc = normalised(acc_sc[rs])

        heads = []
        for h in range(HPG):
            hs = slice(h * tq, (h + 1) * tq)
            c = 3 * (HPG * g + h)
            heads.append(br[:, c:c + 1] * o_cmp[g][hs] + br[:, c + 1:c + 2] * o_slc[hs]
                         + br[:, c + 2:c + 3] * o_win[hs])
        for hh in range(HPG // 2):
            even, odd = heads[2 * hh], heads[2 * hh + 1]
            if g == 0:
                odd = pltpu.roll(odd, HEAD_DIM, axis=1)
            else:
                even = pltpu.roll(even, HEAD_DIM, axis=1)
            c0 = (2 * g + hh) * LANES
            o_ref[:, c0:c0 + LANES] = (jnp.where(lane_lo, even, odd) * ga_ref[:, c0:c0 + LANES]).astype(BF16)


def _nsa_prompt(q_slots, kc, vc, slc_b, win_b, br, ga, *, n_batch, seq):
    tq = 128
    tk = min(512, seq)
    n_cmp = seq // CMP_STRIDE - (CMP_BLOCK // CMP_STRIDE) + 1
    n_sel = seq // SEL_BLOCK
    assert n_sel <= LANES and seq % tk == 0 and tk % tq == 0 and seq >= WINDOW + tq
    ncp = kc.shape[1]
    mimp = _importance_matrix(ncp, n_cmp, n_sel, LANES)
    key_blk = (jnp.arange(seq, dtype=jnp.int32) // SEL_BLOCK).reshape(seq // tk, tk, 1)
    expand = (key_blk == jnp.arange(LANES, dtype=jnp.int32).reshape(1, 1, LANES)).astype(BF16)
    nq = seq // tq
    rows = HPG * tq
    return pl.pallas_call(
        functools.partial(_nsa_prompt_kernel, tq=tq, tk=tk, n_cmp=n_cmp, n_sel=n_sel),
        grid=(n_batch, nq),
        in_specs=[pl.BlockSpec((1, N_HEADS, tq, LANES), lambda b, i: (b, 0, i, 0)),
                  pl.BlockSpec((1, ncp, LANES), lambda b, i: (b, 0, 0)),
                  pl.BlockSpec((1, ncp, LANES), lambda b, i: (b, 0, 0)),
                  pl.BlockSpec((1, seq, KV_ROW), lambda b, i: (b, 0, 0)),
                  pl.BlockSpec((1, seq, KV_ROW), lambda b, i: (b, 0, 0)),
                  pl.BlockSpec((tq, LANES), lambda b, i: (b * nq + i, 0)),
                  pl.BlockSpec((tq, D_NSA), lambda b, i: (b * nq + i, 0)),
                  pl.BlockSpec(mimp.shape, lambda b, i: (0, 0)),
                  pl.BlockSpec(expand.shape, lambda b, i: (0, 0, 0))],
        out_specs=pl.BlockSpec((tq, D_NSA), lambda b, i: (b * nq + i, 0)),
        out_shape=jax.ShapeDtypeStruct((n_batch * seq, D_NSA), BF16),
        scratch_shapes=[pltpu.VMEM((N_KV * rows, LANES), F32)] * 2,
        compiler_params=_cparams("arbitrary", "arbitrary"),
        name="nsa_prompt",
    )(q_slots, kc, vc, slc_b.reshape(n_batch, seq, KV_ROW), win_b.reshape(n_batch, seq, KV_ROW), br, ga, mimp,
      expand)


def _retention_prompt_kernel(rq_ref, rk_ref, rv_ref, o_ref, s_ref, *, cl):
    @pl.when(pl.program_id(0) == 0)
    def _():
        s_ref[...] = jnp.zeros(s_ref.shape, F32)

    i_pos = lax.broadcasted_iota(jnp.int32, (cl, cl), 0)
    j_pos = lax.broadcasted_iota(jnp.int32, (cl, cl), 1)
    diff = (i_pos - j_pos).astype(F32)
    i_col = lax.broadcasted_iota(jnp.int32, (cl, 1), 0).astype(F32)
    lane_lo = lax.broadcasted_iota(jnp.int32, (cl, LANES), 1) < DK_RET
    for h in range(N_RET_HEADS):
        pair, odd = divmod(h, 2)
        log_g = math.log(1.0 - 2.0 ** (-5.0 - h))
        keep = lane_lo if odd == 0 else jnp.logical_not(lane_lo)
        decay = jnp.where(diff >= 0, jnp.exp(log_g * jnp.maximum(diff, 0.0)), 0.0)
        xi = jnp.exp(log_g * (i_col + 1.0))
        zeta = jnp.exp(log_g * (cl - 1.0 - i_col))
        for b in range(rq_ref.shape[0]):
            q2 = rq_ref[b, :, pair * LANES:(pair + 1) * LANES]
            k2 = rk_ref[b, :, pair * LANES:(pair + 1) * LANES]
            s_pair = jnp.concatenate([s_ref[b, 2 * pair], s_ref[b, 2 * pair + 1]], axis=0)
            qh = jnp.where(keep, q2, jnp.zeros_like(q2))
            v = rv_ref[b, :, h * DV_RET:(h + 1) * DV_RET]
            a = _dot_nt(qh, k2) * decay
            o_ref[b, :, h * DV_RET:(h + 1) * DV_RET] = (_dot(a.astype(BF16), v)
                                                         + _dot(qh, s_pair.astype(BF16)) * xi)
            kz = (k2.astype(F32) * zeta).astype(BF16)
            u = _dot_tn(kz, v)[odd * DK_RET:(odd + 1) * DK_RET]
            s_ref[b, h] = s_ref[b, h] * math.exp(log_g * cl) + u


def _retention_prompt(rq, rk, rv, *, n_batch, seq):
    cl = math.gcd(seq, RET_CHUNK)
    nrot = N_RET_HEADS * DK_RET
    return pl.pallas_call(
        functools.partial(_retention_prompt_kernel, cl=cl),
        grid=(seq // cl,),
        in_specs=[pl.BlockSpec((n_batch, cl, nrot), lambda i: (0, i, 0)),
                  pl.BlockSpec((n_batch, cl, nrot), lambda i: (0, i, 0)),
                  pl.BlockSpec((n_batch, cl, D_RET), lambda i: (0, i, 0))],
        out_specs=[pl.BlockSpec((n_batch, cl, D_RET), lambda i: (0, i, 0)),
                   pl.BlockSpec((n_batch, N_RET_HEADS, DK_RET, DV_RET), lambda i: (0, 0, 0, 0))],
        out_shape=[jax.ShapeDtypeStruct((n_batch, seq, D_RET), F32),
                   jax.ShapeDtypeStruct((n_batch, N_RET_HEADS, DK_RET, DV_RET), F32)],
        compiler_params=_cparams("arbitrary"),
        name="retention_prompt",
    )(rq.reshape(n_batch, seq, nrot), rk.reshape(n_batch, seq, nrot), rv.reshape(n_batch, seq, D_RET))


def _retention_step_kernel(rq_ref, rk_ref, rv_ref, s0_ref, o_ref, s_ref):
    lane_lo = lax.broadcasted_iota(jnp.int32, (8, LANES), 1) < DK_RET
    diag = (lax.broadcasted_iota(jnp.int32, (LANES, LANES), 0)
            == lax.broadcasted_iota(jnp.int32, (LANES, LANES), 1))
    for pair in range(N_RET_HEADS // 2):
        q2 = jnp.broadcast_to(rq_ref[0, :, pair * LANES:(pair + 1) * LANES], (8, LANES))
        k_row = rk_ref[0, :, pair * LANES:(pair + 1) * LANES]
        k2 = jnp.broadcast_to(k_row, (8, LANES))
        k_col = jnp.sum(jnp.where(diag, jnp.broadcast_to(k_row, (LANES, LANES)), 0.0), axis=-1, keepdims=True)
        s_pair = jnp.concatenate([s0_ref[0, 2 * pair], s0_ref[0, 2 * pair + 1]], axis=0)
        for odd in range(2):
            h = 2 * pair + odd
            g = math.exp(math.log(1.0 - 2.0 ** (-5.0 - h)))
            keep = lane_lo if odd == 0 else jnp.logical_not(lane_lo)
            qh = jnp.where(keep, q2, 0.0)
            v_row = rv_ref[0, :, h * DV_RET:(h + 1) * DV_RET]
            qk = jnp.sum(qh * k2, axis=-1, keepdims=True)
            o = qk * v_row + _dot(qh, s_pair) * g
            o_ref[0, :, h * DV_RET:(h + 1) * DV_RET] = o[0:1]
            u = (k_col * v_row)[odd * DK_RET:(odd + 1) * DK_RET]
            s_ref[0, h] = s0_ref[0, h] * g + u


def _retention_step(rq, rk, rv, s0):
    b = rq.shape[0]
    nrot = N_RET_HEADS * DK_RET
    state_spec = pl.BlockSpec((1, N_RET_HEADS, DK_RET, DV_RET), lambda i: (i, 0, 0, 0))
    return pl.pallas_call(
        _retention_step_kernel,
        grid=(b,),
        in_specs=[pl.BlockSpec((1, 1, nrot), lambda i: (i, 0, 0)),
                  pl.BlockSpec((1, 1, nrot), lambda i: (i, 0, 0)),
                  pl.BlockSpec((1, 1, D_RET), lambda i: (i, 0, 0)),
                  state_spec],
        out_specs=[pl.BlockSpec((1, 1, D_RET), lambda i: (i, 0, 0)), state_spec],
        out_shape=[jax.ShapeDtypeStruct((b, 1, D_RET), F32),
                   jax.ShapeDtypeStruct((b, N_RET_HEADS, DK_RET, DV_RET), F32)],
        compiler_params=_cparams("arbitrary"),
        name="retention_step",
    )(rq.astype(F32).reshape(b, 1, nrot), rk.astype(F32).reshape(b, 1, nrot), rv.astype(F32).reshape(b, 1, D_RET),
      s0)


def _nsa_step1_kernel(q_ref, kc_ref, vc_ref, win_ref, new_ref, mimp_ref, oc_ref, ow_ref, idx_ref, *,
                      t, n_cmp, n_sel):
    q = q_ref[0]
    nq = q.shape[0]
    ncp = kc_ref.shape[1]
    nsp = mimp_ref.shape[1]

    s = _dot_nt(q, kc_ref[0])
    col = lax.broadcasted_iota(jnp.int32, (nq, ncp), 1)
    mask = (col * CMP_STRIDE + (CMP_BLOCK - 1) <= t) & (col < n_cmp)
    s = jnp.where(mask, s, NEG)
    e = jnp.exp(s - jnp.max(s, axis=-1, keepdims=True))
    p = jnp.where(mask, e, 0.0) * (1.0 / jnp.sum(e, axis=-1, keepdims=True))
    oc_ref[0] = _dot(p.astype(BF16), vc_ref[0])
    imp_rows = _split3_dot(p, mimp_ref[...])

    blk = lax.broadcasted_iota(jnp.int32, (1, nsp), 1)
    jt = t // SEL_BLOCK
    forced = (blk == 0) | (blk == jt) | (blk == jt - 1)
    scores = []
    for g in range(N_KV):
        imp = jnp.sum(imp_rows[HPG * g:HPG * (g + 1)], axis=0, keepdims=True)
        sc = jnp.where(blk * SEL_BLOCK <= t, jnp.where(forced, FORCE, imp), NEG)
        scores.append(jnp.where(blk < n_sel, sc, BELOW_NEG))
    s_rows = jnp.concatenate(scores + [jnp.full((LANES - N_KV, nsp), BELOW_NEG, F32)], axis=0)
    s_cols = s_rows.T
    a_idx = lax.broadcasted_iota(jnp.int32, (nsp, nsp), 0)
    c_idx = lax.broadcasted_iota(jnp.int32, (nsp, nsp), 1)
    rank_lane = lax.broadcasted_iota(jnp.int32, (nsp, LANES), 1).astype(F32)
    a_val = lax.broadcasted_iota(jnp.int32, (nsp, LANES), 0).astype(F32)
    rows_out = []
    for g in range(N_KV):
        s_col = s_cols[:, g:g + 1]
        s_row = s_rows[g:g + 1, :]
        beats = (s_row > s_col) | ((s_row == s_col) & (c_idx < a_idx))
        rank = jnp.sum(jnp.where(beats, 1.0, 0.0), axis=-1, keepdims=True)
        rows_out.append(jnp.sum(jnp.where(rank == rank_lane, a_val, 0.0), axis=0, keepdims=True))
    idx = jnp.concatenate(rows_out + [jnp.zeros((8 - N_KV, LANES), F32)], axis=0)
    idx_ref[0] = idx.astype(jnp.int32)

    wb = win_ref.shape[-1]
    kw_t = win_ref[0, 0].reshape(LANES, wb).astype(BF16)
    vw_t = win_ref[0, 1].reshape(LANES, wb).astype(BF16)
    k_new = new_ref[0, :, 0:LANES]
    v_new = new_ref[0, :, LANES:2 * LANES]
    s = _dot(q, kw_t)
    kpos = t - wb + lax.broadcasted_iota(jnp.int32, (nq, wb), 1)
    s = jnp.where((kpos > t - WINDOW) & (kpos >= 0), s, NEG)
    s_new = jnp.sum(q.astype(F32) * k_new, axis=-1, keepdims=True)
    m = jnp.maximum(jnp.max(s, axis=-1, keepdims=True), s_new)
    e = jnp.exp(s - m)
    e_new = jnp.exp(s_new - m)
    ow_ref[0] = ((_dot_nt(e.astype(BF16), vw_t) + e_new * v_new)
                 * (1.0 / (jnp.sum(e, axis=-1, keepdims=True) + e_new)))


def _nsa_step1(q16, kc, vc, state_win_t, new_win, mimp, *, t, n_cmp, n_sel):
    b, nq, _ = q16.shape
    ncp = kc.shape[1]

    def per_b(shape):
        return pl.BlockSpec((1,) + shape, lambda i: (i,) + (0,) * len(shape))

    return pl.pallas_call(
        functools.partial(_nsa_step1_kernel, t=t, n_cmp=n_cmp, n_sel=n_sel),
        grid=(b,),
        in_specs=[per_b((nq, LANES)), per_b((ncp, LANES)), per_b((ncp, LANES)), per_b(state_win_t.shape[1:]),
                  per_b((1, KV_ROW)), pl.BlockSpec(mimp.shape, lambda i: (0, 0))],
        out_specs=[per_b((nq, LANES)), per_b((nq, LANES)), per_b((8, LANES))],
        out_shape=[jax.ShapeDtypeStruct((b, nq, LANES), F32), jax.ShapeDtypeStruct((b, nq, LANES), F32),
                   jax.ShapeDtypeStruct((b, 8, LANES), jnp.int32)],
        compiler_params=_cparams("arbitrary"),
        name="nsa_step_compressed_window",
    )(q16, kc, vc, state_win_t, new_win, mimp)


def _nsa_step2_kernel(idx_sm, pt_sm, q_ref, *refs, n_cache_blocks, k_eff):
    page_refs = refs[:N_KV * k_eff]
    new_ref, oc_ref, ow_ref, br_ref, o_ref = refs[N_KV * k_eff:]
    b = pl.program_id(0)
    q = q_ref[0]
    nq = q.shape[0]
    bpp = PAGE_SIZE // SEL_BLOCK
    nk = k_eff * PAGE_SIZE
    k_new = new_ref[0, :, 0:LANES]
    v_new = new_ref[0, :, LANES:2 * LANES]
    s_new_all = jnp.sum(q.astype(F32) * k_new, axis=-1, keepdims=True)
    lane_k = lax.broadcasted_iota(jnp.int32, (nq, nk), 1)
    slot = lane_k // PAGE_SIZE
    half = (lane_k % PAGE_SIZE) // SEL_BLOCK
    outs = []
    for g in range(N_KV):
        pages = [page_refs[g * k_eff + k][0] for k in range(k_eff)]
        keys_t = jnp.concatenate([p[0].reshape(LANES, PAGE_SIZE) for p in pages], axis=-1).astype(BF16)
        vals_t = jnp.concatenate([p[1].reshape(LANES, PAGE_SIZE) for p in pages], axis=-1).astype(BF16)
        allowed = jnp.zeros((nq, nk), jnp.int32)
        new_selected = jnp.int32(0)
        for k in range(k_eff):
            j = idx_sm[b, g * k_eff + k]
            in_block = (slot == k) & (half == j % bpp)
            allowed = jnp.where(in_block, (j < n_cache_blocks).astype(jnp.int32), allowed)
            new_selected = new_selected | (j == n_cache_blocks).astype(jnp.int32)
        s = jnp.where(allowed > 0, _dot(q, keys_t), NEG)
        s_new = jnp.where(new_selected > 0, s_new_all, NEG)
        m = jnp.maximum(jnp.max(s, axis=-1, keepdims=True), s_new)
        e = jnp.exp(s - m)
        e_new = jnp.exp(s_new - m)
        outs.append((_dot_nt(e.astype(BF16), vals_t) + e_new * v_new)
                    * (1.0 / (jnp.sum(e, axis=-1, keepdims=True) + e_new)))
    row = lax.broadcasted_iota(jnp.int32, (nq, LANES), 0)
    lane = lax.broadcasted_iota(jnp.int32, (nq, LANES), 1)
    o_slc = jnp.where(row < HPG, outs[0], outs[1])
    br = br_ref[0]
    gates = [jnp.sum(jnp.where(lane == 3 * row + j, br, 0.0), axis=-1, keepdims=True) for j in range(3)]
    o = gates[0] * oc_ref[0] + gates[1] * o_slc + gates[2] * ow_ref[0]
    lane_lo = lax.broadcasted_iota(jnp.int32, (1, LANES), 1) < HEAD_DIM
    for pair in range(N_HEADS // 2):
        even, odd = o[2 * pair:2 * pair + 1], o[2 * pair + 1:2 * pair + 2]
        if 2 * pair < HPG:
            odd = pltpu.roll(odd, HEAD_DIM, axis=1)
        else:
            even = pltpu.roll(even, HEAD_DIM, axis=1)
        o_ref[0, :, pair * LANES:(pair + 1) * LANES] = jnp.where(lane_lo, even, odd)


def _nsa_step2(idx, page_table, q16, cache_t, new_slc, o_cmp, o_win, br, *, n_cache_blocks, k_eff):
    b, nq, _ = q16.shape
    bpp = PAGE_SIZE // SEL_BLOCK

    def page_map(i, idx_sm, pt_sm, c):
        j = jnp.minimum(idx_sm[i, c], n_cache_blocks - 1)
        return (pt_sm[i, j // bpp], 0, 0, 0, 0)

    def per_b(shape):
        return pl.BlockSpec((1,) + shape, lambda i, idx_sm, pt_sm: (i,) + (0,) * len(shape))

    n_blk = N_KV * k_eff
    in_specs = [per_b((nq, LANES))]
    in_specs += [pl.BlockSpec((1,) + cache_t.shape[1:], functools.partial(page_map, c=c)) for c in range(n_blk)]
    in_specs += [per_b((1, KV_ROW)), per_b((nq, LANES)), per_b((nq, LANES)), per_b((1, LANES))]
    return pl.pallas_call(
        functools.partial(_nsa_step2_kernel, n_cache_blocks=n_cache_blocks, k_eff=k_eff),
        grid_spec=pltpu.PrefetchScalarGridSpec(
            num_scalar_prefetch=2,
            grid=(b,),
            in_specs=in_specs,
            out_specs=per_b((1, D_NSA)),
        ),
        out_shape=jax.ShapeDtypeStruct((b, 1, D_NSA), F32),
        compiler_params=_cparams("arbitrary"),
        name="nsa_step_selected",
    )(idx, page_table, q16, *([cache_t] * n_blk), new_slc, o_cmp, o_win, br)


def _out_kernel(*refs, mul_ga):
    if mul_ga:
        x_ref, gate_ref, ya_ref, ga_ref, oret_ref, gr_ref, gret_ref, w_ref, y_ref = refs
        ya = (ya_ref[...] * ga_ref[...]).astype(BF16)
    else:
        x_ref, gate_ref, ya_ref, oret_ref, gr_ref, gret_ref, w_ref, y_ref = refs
        ya = ya_ref[...]
    o = oret_ref[...]
    parts = []
    for h in range(N_RET_HEADS):
        t = o[:, h * DV_RET:(h + 1) * DV_RET]
        ms = jnp.mean(t * t, axis=-1, keepdims=True)
        parts.append(t * lax.rsqrt(ms + EPS) * gret_ref[...])
    yr = (jnp.concatenate(parts, axis=-1) * gr_ref[...]).astype(BF16)
    y_ref[...] = x_ref[...] + gate_ref[0] * _dot(jnp.concatenate([ya, yr], axis=-1), w_ref[...])


def _output(x2, gate, ya, ga, o_ret, gr, g_ret, w_out, *, n_batch, per_row_mod):
    m = x2.shape[0]
    seq = m // n_batch
    tm = min(seq, 512)
    nt = seq // tm
    if per_row_mod:
        gate_spec = pl.BlockSpec((1, tm, D_MODEL), lambda i: (0, i, 0))
    else:
        gate_spec = pl.BlockSpec((1, 1, D_MODEL), lambda i: (i // nt, 0, 0))

    def rows_spec(width):
        return pl.BlockSpec((tm, width), lambda i: (i, 0))

    mul_ga = ga is not None
    args = [x2, gate, ya] + ([ga] if mul_ga else []) + [o_ret, gr, g_ret.reshape(1, DV_RET), w_out.astype(BF16)]
    in_specs = [rows_spec(D_MODEL), gate_spec, rows_spec(D_NSA)] + ([rows_spec(D_NSA)] if mul_ga else [])
    in_specs += [rows_spec(D_RET), rows_spec(D_RET), pl.BlockSpec((1, DV_RET), lambda i: (0, 0)),
                 pl.BlockSpec((D_MODEL, D_MODEL), lambda i: (0, 0))]
    return pl.pallas_call(
        functools.partial(_out_kernel, mul_ga=mul_ga),
        grid=(m // tm,),
        in_specs=in_specs,
        out_specs=rows_spec(D_MODEL),
        out_shape=jax.ShapeDtypeStruct((m, D_MODEL), F32),
        compiler_params=_cparams("arbitrary"),
        name="output_projection",
    )(*args)


def kernel(x_prompt, x_sample, c_prompt, c_sample, cache_cmp, cache_slc, state_win, state_ret, page_table, g_norm, w_ada, b_ada, w_in, g_q, g_kc, g_ks, g_kw, pe_ck, w_ck1, w_ck2, pe_cv, w_cv1, w_cv2, g_ret, w_out):
    assert g_norm.shape[0] == 1, "one layer"
    bp, seq, _ = x_prompt.shape
    bs, dec_seq, _ = x_sample.shape
    assert dec_seq == 1
    n_pages = page_table.shape[1]
    past = n_pages * PAGE_SIZE
    assert state_win.shape[2] == WINDOW and past >= WINDOW

    w_proj, gq_slots, gks2, gkw2 = _prep_proj_weights(w_in[0], g_q[0], g_ks[0], g_kw[0])
    w1x, w1kv, w2x, pe8, gkc2 = _prep_compress_weights(pe_ck[0], w_ck1[0], w_ck2[0], pe_cv[0], w_cv1[0], w_cv2[0], g_kc[0])

    pad = (-(bp + bs)) % 8
    c_all = jnp.concatenate([c_prompt, c_sample, jnp.zeros((pad, D_MODEL), F32)], axis=0)
    mod = _modulation(c_all, w_ada[0], b_ada[0])
    shift_p, scale_p, gate_p = (mod[:bp, k * D_MODEL:(k + 1) * D_MODEL].reshape(bp, 1, D_MODEL) for k in range(3))
    shift_s, scale_s, gate_s = (mod[bp:bp + bs, k * D_MODEL:(k + 1) * D_MODEL].reshape(1, bs, D_MODEL)
                                for k in range(3))

    xp2 = x_prompt.reshape(bp * seq, D_MODEL)
    cos_p, sin_p = _rope_tables(seq, 0, 1)
    (q_p, cmp_t, slc_t, slcb_p, win_t, winb_p, br_p, ga_p, rq_p, rk_p, rv_p, gr_p) = _projection(
        xp2, scale_p, shift_p, g_norm[0], w_proj, gq_slots, gks2, gkw2, cos_p, sin_p, n_batch=bp, per_row_mod=False,
        pos_minor=True)

    a_p = _chunk_matmul_paged(cmp_t.reshape(bp, 2, N_KV, HEAD_DIM, seq), None, w1kv, n_batch=bp,
                              n_pages=seq // PAGE_SIZE)
    kc_p, vc_p = _compress_combine(a_p, jnp.zeros((bp, 1, KV_ROW), F32), pe8, w1x, w2x, gkc2)
    ya_p = _nsa_prompt(q_p, kc_p, vc_p, slcb_p, winb_p, br_p, ga_p, n_batch=bp, seq=seq)
    oret_p, sret_p = _retention_prompt(rq_p, rk_p, rv_p, n_batch=bp, seq=seq)
    y_p = _output(xp2, gate_p, ya_p, None, oret_p.reshape(bp * seq, D_RET), gr_p, g_ret[0], w_out[0],
                  n_batch=bp, per_row_mod=False)

    xs2 = x_sample.reshape(bs, D_MODEL)
    cos_s, sin_s = _rope_tables(bs, past, 0)
    (q_s, cmp_s, slc_s, _, win_s, _, br_s, ga_s, rq_s, rk_s, rv_s, gr_s) = _projection(
        xs2, scale_s, shift_s, g_norm[0], w_proj, gq_slots, gks2, gkw2, cos_s, sin_s, n_batch=1, per_row_mod=True,
        pos_minor=False)

    n_chunk_cache = past // CMP_STRIDE
    n_cmp_s = n_chunk_cache
    n_sel_s = past // SEL_BLOCK + 1
    nsp = ((n_sel_s + LANES - 1) // LANES) * LANES
    to_native = (0, 2, 3, 4, 1)
    a_s = _chunk_matmul_paged(jnp.transpose(cache_cmp[0], to_native), page_table, w1kv, n_batch=bs,
                              n_pages=n_pages)
    kc_s, vc_s = _compress_combine(a_s, cmp_s.reshape(bs, 1, KV_ROW), pe8, w1x, w2x, gkc2)

    q16 = jnp.pad(q_s[0].transpose(1, 0, 2), ((0, 0), (0, 16 - N_HEADS), (0, 0)))
    mimp_s = _importance_matrix(n_cmp_s, n_cmp_s, n_sel_s, nsp)
    o_cmp_s, o_win_s, idx_s = _nsa_step1(q16, kc_s, vc_s, jnp.transpose(state_win[0], to_native),
                                         win_s.reshape(bs, 1, KV_ROW), mimp_s, t=past, n_cmp=n_cmp_s,
                                         n_sel=n_sel_s)
    k_eff = min(TOP_N, n_sel_s)
    idx_flat = idx_s[:, :N_KV, :k_eff].reshape(bs, N_KV * k_eff)
    o_nsa_s = _nsa_step2(idx_flat, page_table, q16, jnp.transpose(cache_slc[0], to_native),
                         slc_s.reshape(bs, 1, KV_ROW), o_cmp_s, o_win_s, br_s.reshape(bs, 1, LANES),
                         n_cache_blocks=past // SEL_BLOCK, k_eff=k_eff)

    oret_s, sret_s = _retention_step(rq_s, rk_s, rv_s, state_ret[0])
    y_s = _output(xs2, gate_s, o_nsa_s.reshape(bs, D_NSA), ga_s, oret_s.reshape(bs, D_RET), gr_s, g_ret[0], w_out[0],
                  n_batch=1, per_row_mod=True)

    def kv_rows(a, b, l):
        return a.reshape(1, b, l, 2, N_KV, HEAD_DIM)

    def kv_rows_t(a, b, l):
        return jnp.transpose(a.reshape(b, 2, N_KV, HEAD_DIM, l), (0, 4, 1, 2, 3))[None]

    keep_p = min(WINDOW, seq)
    s_win = jnp.concatenate([state_win[0][:, 1:].reshape(bs, WINDOW - 1, KV_ROW), win_s.reshape(bs, 1, KV_ROW)],
                            axis=1)
    return (y_p.reshape(bp, seq, D_MODEL), y_s.reshape(bs, 1, D_MODEL),
            kv_rows_t(cmp_t, bp, seq), kv_rows_t(slc_t, bp, seq),
            kv_rows_t(win_t[:, :, seq - keep_p:], bp, keep_p), sret_p[None],
            kv_rows(cmp_s, bs, 1), kv_rows(slc_s, bs, 1), kv_rows(s_win, bs, WINDOW), sret_s[None])
```

```python
import functools
import math

import jax
import jax.numpy as jnp
import numpy as np
from jax import lax
from jax.experimental import pallas as pl
from jax.experimental.pallas import tpu as pltpu

D_MODEL = 1024
HEAD_DIM = 64
D_NSA = 512
N_HEADS = 8
N_KV = 2
HPG = 4
CMP_BLOCK = 32
CMP_STRIDE = 16
SEL_BLOCK = 64
TOP_N = 16
WINDOW = 512
D_RET = 512
N_RET_HEADS = 4
DV_RET = 128
DK_RET = 64
RET_CHUNK = 128
PAGE_SIZE = 128
ROPE_BASE = 10000.0
EPS = 1e-6
NEG = -1e30
FORCE = 1e4
BELOW_NEG = -2e38
TAKEN = -3e38

KV_ROW = 2 * N_KV * HEAD_DIM
CHUNK_LANES = CMP_STRIDE * KV_ROW
LANES = 128
VMEM_LIMIT_BYTES = 56 * 1024 * 1024

F32 = jnp.float32
BF16 = jnp.bfloat16

O_Q = 0
O_CMP = O_Q + D_NSA
O_SLC = O_CMP + KV_ROW
O_WIN = O_SLC + KV_ROW
O_BR = O_WIN + KV_ROW
O_GA = O_BR + LANES
O_RQ = O_GA + D_NSA
O_RK = O_RQ + N_RET_HEADS * DK_RET
O_RV = O_RK + N_RET_HEADS * DK_RET
O_GR = O_RV + D_RET
D_IN_PAD = O_GR + D_RET


def _cparams(*sem):
    return pltpu.CompilerParams(dimension_semantics=sem, vmem_limit_bytes=VMEM_LIMIT_BYTES)


def _dot(a, b):
    return jnp.dot(a, b, preferred_element_type=F32)


def _dot_nt(a, b):
    return lax.dot_general(a, b, (((1,), (1,)), ((), ())), preferred_element_type=F32)


def _dot_tn(a, b):
    return lax.dot_general(a, b, (((0,), (0,)), ((), ())), preferred_element_type=F32)


def _silu(x):
    return x * jax.nn.sigmoid(x)


def _split3_dot(p, m):
    p1 = p.astype(BF16)
    r1 = p - p1.astype(F32)
    p2 = r1.astype(BF16)
    p3 = (r1 - p2.astype(F32)).astype(BF16)
    return _dot(p1, m) + _dot(p2, m) + _dot(p3, m)


def _pair_rmsnorm(k, gain2):
    lo = lax.broadcasted_iota(jnp.int32, k.shape, 1) < HEAD_DIM
    sq = k * k
    s0 = jnp.sum(jnp.where(lo, sq, 0.0), axis=-1, keepdims=True)
    s1 = jnp.sum(jnp.where(lo, 0.0, sq), axis=-1, keepdims=True)
    ms = jnp.where(lo, s0, s1) * (1.0 / HEAD_DIM)
    return k * lax.rsqrt(ms + EPS) * gain2


def _mod_kernel(c_ref, w_ref, b_ref, o_ref):
    o_ref[...] = _dot(_silu(c_ref[...]).astype(BF16), w_ref[...]) + b_ref[...]


def _modulation(c_all, w_ada, b_ada):
    rows = c_all.shape[0]
    n = w_ada.shape[1]
    tn = 1024
    return pl.pallas_call(
        _mod_kernel,
        grid=(n // tn,),
        in_specs=[pl.BlockSpec((rows, D_MODEL), lambda j: (0, 0)),
                  pl.BlockSpec((D_MODEL, tn), lambda j: (0, j)),
                  pl.BlockSpec((1, tn), lambda j: (0, j))],
        out_specs=pl.BlockSpec((rows, tn), lambda j: (0, j)),
        out_shape=jax.ShapeDtypeStruct((rows, n), F32),
        compiler_params=_cparams("arbitrary"),
        name="adaln_modulation",
    )(c_all, w_ada.astype(BF16), b_ada.reshape(1, n))


def _rope_kernel(freq_ref, cos_ref, sin_ref, *, base, stride, rows):
    shape = (rows, LANES)
    r = lax.broadcasted_iota(jnp.int32, shape, 0)
    pos = (base + stride * (pl.program_id(0) * rows + r)).astype(F32)
    ang = pos * freq_ref[...]
    lane = lax.broadcasted_iota(jnp.int32, shape, 1)
    sign = jnp.where((lane & (DK_RET - 1)) < DK_RET // 2, -1.0, 1.0)
    cos_ref[...] = jnp.cos(ang)
    sin_ref[...] = jnp.sin(ang) * sign


def _rope_tables(n_rows, base, stride):
    half = DK_RET // 2
    freqs = ROPE_BASE ** (-jnp.arange(half, dtype=F32) / half)
    freq_row = jnp.tile(freqs, LANES // half).reshape(1, LANES)
    rows = min(n_rows, 512)
    return pl.pallas_call(
        functools.partial(_rope_kernel, base=base, stride=stride, rows=rows),
        grid=(n_rows // rows,),
        in_specs=[pl.BlockSpec((1, LANES), lambda i: (0, 0))],
        out_specs=[pl.BlockSpec((rows, LANES), lambda i: (i, 0))] * 2,
        out_shape=[jax.ShapeDtypeStruct((n_rows, LANES), F32)] * 2,
        compiler_params=_cparams("arbitrary"),
        name="rope_tables",
    )(freq_row)


def _proj_kernel(x_ref, scale_ref, shift_ref, gn_ref, w_ref, gq_ref, gks_ref, gkw_ref, cos_ref, sin_ref,
                 q_ref, cmp_ref, slc_ref, slcb_ref, win_ref, winb_ref, br_ref, ga_ref, rq_ref, rk_ref,
                 rv_ref, gr_ref, *, pos_minor):
    x = x_ref[...]
    ms = jnp.mean(x * x, axis=-1, keepdims=True)
    xn = x * lax.rsqrt(ms + EPS) * gn_ref[...]
    hb = (xn * (1.0 + scale_ref[0]) + shift_ref[0]).astype(BF16)

    def seg(start, width):
        return _dot(hb, w_ref[:, start:start + width])

    yq = seg(O_Q, D_NSA)
    lane_lo = lax.broadcasted_iota(jnp.int32, (yq.shape[0], LANES), 1) < HEAD_DIM
    for s in range(N_HEADS):
        t = yq[:, (s // 2) * LANES:(s // 2 + 1) * LANES]
        if s % 2 != s // HPG:
            t = pltpu.roll(t, HEAD_DIM, axis=1)
        t = jnp.where(lane_lo == (s < HPG), t, 0.0)
        ms = jnp.sum(t * t, axis=-1, keepdims=True) * (1.0 / HEAD_DIM)
        q_ref[0, s] = (t * lax.rsqrt(ms + EPS) * gq_ref[s]).astype(BF16)

    y_cmp = seg(O_CMP, KV_ROW)
    y = seg(O_SLC, KV_ROW)
    y_slc = jnp.concatenate([_pair_rmsnorm(y[:, :LANES], gks_ref[...]), y[:, LANES:]], axis=-1)
    slcb_ref[...] = y_slc.astype(BF16)
    y = seg(O_WIN, KV_ROW)
    y_win = jnp.concatenate([_pair_rmsnorm(y[:, :LANES], gkw_ref[...]), y[:, LANES:]], axis=-1)
    winb_ref[...] = y_win.astype(BF16)
    for ref, rows in ((cmp_ref, y_cmp), (slc_ref, y_slc), (win_ref, y_win)):
        if pos_minor:
            ref[0] = rows.T
        else:
            ref[...] = rows

    br_ref[...] = jax.nn.sigmoid(seg(O_BR, LANES))
    ga_ref[...] = _silu(seg(O_GA, D_NSA))

    first_half = (lax.broadcasted_iota(jnp.int32, (yq.shape[0], LANES), 1) & (DK_RET - 1)) < DK_RET // 2

    def rotary(y):
        tiles = []
        for k in range(y.shape[1] // LANES):
            t = y[:, k * LANES:(k + 1) * LANES]
            swapped = jnp.where(first_half, pltpu.roll(t, LANES - DK_RET // 2, axis=1),
                                pltpu.roll(t, DK_RET // 2, axis=1))
            tiles.append(t * cos_ref[...] + swapped * sin_ref[...])
        return jnp.concatenate(tiles, axis=-1)

    nrot = N_RET_HEADS * DK_RET
    rq_ref[...] = rotary(seg(O_RQ, nrot)).astype(BF16)
    rk_ref[...] = (rotary(seg(O_RK, nrot)) * DK_RET ** -0.5).astype(BF16)
    rv_ref[...] = seg(O_RV, D_RET).astype(BF16)
    gr_ref[...] = _silu(seg(O_GR, D_RET))


def _prep_proj_weights(w_in, g_q, g_ks, g_kw):
    n_br = 3 * N_HEADS
    w = jnp.concatenate([w_in[:, :O_BR], jnp.pad(w_in[:, O_BR:O_BR + n_br], ((0, 0), (0, LANES - n_br))),
                         w_in[:, O_BR + n_br:]], axis=1).astype(BF16)
    assert w.shape[1] == D_IN_PAD
    gz = jnp.zeros((HEAD_DIM,), F32)
    gq = g_q * (HEAD_DIM ** -0.5 * math.log2(math.e))
    gq_slots = jnp.stack([jnp.concatenate([gq, gz]) if s < HPG else jnp.concatenate([gz, gq])
                          for s in range(N_HEADS)]).reshape(N_HEADS, 1, LANES)
    return w, gq_slots, jnp.tile(g_ks, 2).reshape(1, LANES), jnp.tile(g_kw, 2).reshape(1, LANES)


def _projection(x2, scale, shift, g_norm, w, gq_slots, gks2, gkw2, cos_t, sin_t, *, n_batch, per_row_mod,
                pos_minor):
    m = x2.shape[0]
    seq = m // n_batch
    tm = min(seq, 256)
    nt = seq // tm
    if per_row_mod:
        mod_spec = pl.BlockSpec((1, tm, D_MODEL), lambda i: (0, i, 0))
    else:
        mod_spec = pl.BlockSpec((1, 1, D_MODEL), lambda i: (i // nt, 0, 0))

    def rows_spec(width):
        return pl.BlockSpec((tm, width), lambda i: (i, 0))

    def const_spec(shape):
        return pl.BlockSpec(shape, lambda i: (0,) * len(shape))

    kv_out = "kv rows"
    widths = [(KV_ROW, kv_out), (KV_ROW, kv_out), (KV_ROW, BF16), (KV_ROW, kv_out), (KV_ROW, BF16), (LANES, F32),
              (D_NSA, F32), (N_RET_HEADS * DK_RET, BF16), (N_RET_HEADS * DK_RET, BF16), (D_RET, BF16),
              (D_RET, F32)]
    out_shape = [jax.ShapeDtypeStruct((n_batch, N_HEADS, seq, LANES), BF16)]
    out_specs = [pl.BlockSpec((1, N_HEADS, tm, LANES), lambda i: (i // nt, 0, i % nt, 0))]
    for width, dt in widths:
        if pos_minor and dt is kv_out:
            out_shape.append(jax.ShapeDtypeStruct((n_batch, KV_ROW, seq), F32))
            out_specs.append(pl.BlockSpec((1, KV_ROW, tm), lambda i: (i // nt, 0, i % nt)))
        else:
            out_shape.append(jax.ShapeDtypeStruct((m, width), F32 if dt is kv_out else dt))
            out_specs.append(rows_spec(width))
    return pl.pallas_call(
        functools.partial(_proj_kernel, pos_minor=pos_minor),
        grid=(m // tm,),
        in_specs=[rows_spec(D_MODEL), mod_spec, mod_spec, const_spec((1, D_MODEL)),
                  const_spec((D_MODEL, D_IN_PAD)), const_spec((N_HEADS, 1, LANES)), const_spec((1, LANES)),
                  const_spec((1, LANES)), pl.BlockSpec((tm, LANES), lambda i: (i % nt, 0)),
                  pl.BlockSpec((tm, LANES), lambda i: (i % nt, 0))],
        out_specs=out_specs,
        out_shape=out_shape,
        compiler_params=_cparams("arbitrary"),
        name="input_projection",
    )(x2, scale, shift, g_norm.reshape(1, D_MODEL), w, gq_slots, gks2, gkw2, cos_t, sin_t)


def _chunk_matmul_paged_kernel(pt_ref, *refs, n_in):
    x_refs = refs[:n_in]
    perm_ref, w_ref, o_ref = refs[n_in:]
    cpp = PAGE_SIZE // CMP_STRIDE
    perm = perm_ref[...]
    moved = [_dot_nt(perm, x_refs[k][0].reshape(KV_ROW, PAGE_SIZE).astype(BF16)) for k in range(n_in)]
    for kv in range(2):
        x = jnp.concatenate(
            [jnp.concatenate([r[cpp * l:cpp * (l + 1), kv * LANES:(kv + 1) * LANES] for l in range(CMP_STRIDE)],
                             axis=1) for r in moved], axis=0)
        o_ref[0, :, kv * KV_ROW:(kv + 1) * KV_ROW] = _dot(x.astype(BF16), w_ref[kv])


def _chunk_matmul_paged(rows_t, page_table, w1kv, *, n_batch, n_pages):
    b = n_batch
    cpp = PAGE_SIZE // CMP_STRIDE
    p = math.gcd(n_pages, 32)
    row = jnp.arange(PAGE_SIZE, dtype=jnp.int32)
    perm = ((row % cpp) * CMP_STRIDE + row // cpp).reshape(PAGE_SIZE, 1) == row.reshape(1, PAGE_SIZE)
    page_shape = (1,) + rows_t.shape[1:4] + (PAGE_SIZE,)
    if page_table is None:
        page_table = jnp.zeros((1, 1), jnp.int32)
        in_specs = [pl.BlockSpec(page_shape, lambda i, j, pt, k=k: (i, 0, 0, 0, j * p + k)) for k in range(p)]
    else:
        in_specs = [pl.BlockSpec(page_shape, lambda i, j, pt, k=k: (pt[i, j * p + k], 0, 0, 0, 0))
                    for k in range(p)]
    in_specs.append(pl.BlockSpec((PAGE_SIZE, PAGE_SIZE), lambda i, j, pt: (0, 0)))
    in_specs.append(pl.BlockSpec(w1kv.shape, lambda i, j, pt: (0, 0, 0)))
    return pl.pallas_call(
        functools.partial(_chunk_matmul_paged_kernel, n_in=p),
        grid_spec=pltpu.PrefetchScalarGridSpec(
            num_scalar_prefetch=1,
            grid=(b, n_pages // p),
            in_specs=in_specs,
            out_specs=pl.BlockSpec((1, p * cpp, 2 * KV_ROW), lambda i, j, pt: (i, j, 0)),
        ),
        out_shape=jax.ShapeDtypeStruct((b, n_pages * cpp, 2 * KV_ROW), F32),
        compiler_params=_cparams("arbitrary", "arbitrary"),
        name="compress_chunks_paged",
    )(page_table, *([rows_t] * p), perm.astype(BF16), w1kv)


def _halves(a):
    lo = jnp.concatenate([a[:, 0:LANES], a[:, 2 * LANES:3 * LANES]], axis=-1)
    hi = jnp.concatenate([a[:, LANES:2 * LANES], a[:, 3 * LANES:4 * LANES]], axis=-1)
    return lo, hi


def _compress_combine_kernel(a_ref, new_ref, pe_ref, w1_ref, w2_ref, gkc_ref, kc_ref, vc_ref, *, n):
    lo, hi = _halves(a_ref[0])
    pe_lo, pe_hi = _halves(_dot(pe_ref[...], w1_ref[...]))
    bias = pe_lo[0:1] + pe_hi[1:2]
    new8 = jnp.broadcast_to(new_ref[0], (8, KV_ROW)).astype(BF16)
    a_new = _halves(_dot(new8, w1_ref[0:KV_ROW, :]))[1][0:1]
    hi = pltpu.roll(hi, n - 1, axis=0)
    row = lax.broadcasted_iota(jnp.int32, (n, KV_ROW), 0)
    hi = jnp.where(row == n - 1, a_new, hi)
    out = _dot(_silu(lo + hi + bias).astype(BF16), w2_ref[...])
    kc_ref[0] = _pair_rmsnorm(out[:, :LANES], gkc_ref[...]).astype(BF16)
    vc_ref[0] = out[:, LANES:].astype(BF16)


def _compress_combine(a, new_rows, pe8, w1x, w2x, gkc2):
    b, n, _ = a.shape
    return pl.pallas_call(
        functools.partial(_compress_combine_kernel, n=n),
        grid=(b,),
        in_specs=[pl.BlockSpec((1, n, 2 * KV_ROW), lambda i: (i, 0, 0)),
                  pl.BlockSpec((1, 1, KV_ROW), lambda i: (i, 0, 0)),
                  pl.BlockSpec(pe8.shape, lambda i: (0, 0)),
                  pl.BlockSpec(w1x.shape, lambda i: (0, 0)),
                  pl.BlockSpec(w2x.shape, lambda i: (0, 0)),
                  pl.BlockSpec((1, LANES), lambda i: (0, 0))],
        out_specs=[pl.BlockSpec((1, n, LANES), lambda i: (i, 0, 0))] * 2,
        out_shape=[jax.ShapeDtypeStruct((b, n, LANES), BF16)] * 2,
        compiler_params=_cparams("arbitrary"),
        name="compress_combine",
    )(a, new_rows, pe8, w1x, w2x, gkc2)


def _prep_compress_weights(pe_ck, w_ck1, w_ck2, pe_cv, w_cv1, w_cv2, g_kc):
    eye = jnp.eye(N_KV, dtype=F32)
    w1 = jnp.stack([w_ck1, w_cv1]).reshape(2, 2, CMP_STRIDE, HEAD_DIM, HEAD_DIM)
    w1x = jnp.einsum('khldf,kq,gp->lkgdqhpf', w1, eye, eye).reshape(CHUNK_LANES, 2 * KV_ROW).astype(BF16)
    w1kv = jnp.einsum('khldf,gp->klgdhpf', w1, eye).reshape(2, CMP_STRIDE * LANES, KV_ROW).astype(BF16)
    w2 = jnp.stack([w_ck2, w_cv2])
    w2x = jnp.einsum('kfd,kq,gp->kgfqpd', w2, eye, eye).reshape(KV_ROW, KV_ROW).astype(BF16)
    pe = jnp.stack([pe_ck, pe_cv]).reshape(2, 2, CMP_STRIDE, HEAD_DIM)
    pe_rows = jnp.broadcast_to(pe.transpose(1, 2, 0, 3)[:, :, :, None, :],
                               (2, CMP_STRIDE, 2, N_KV, HEAD_DIM)).reshape(2, CHUNK_LANES)
    pe8 = jnp.zeros((8, CHUNK_LANES), F32).at[:2].set(pe_rows).astype(BF16)
    return w1x, w1kv, w2x, pe8, jnp.tile(g_kc, 2).reshape(1, LANES)


def _importance_matrix(n_cmp_pad, n_cmp, n_sel, n_sel_pad):
    ratio = SEL_BLOCK // CMP_STRIDE
    span = CMP_BLOCK // CMP_STRIDE
    m = np.zeros((n_cmp_pad, n_sel_pad), np.float32)
    for j in range(n_sel):
        for r in range(ratio):
            for s in range(span):
                n = ratio * j + r - s
                if 0 <= n < n_cmp:
                    m[n, j] += 1.0
    return jnp.asarray(m, BF16)


def _nsa_prompt_kernel(q_ref, kc_ref, vc_ref, slc_ref, win_ref, br_ref, ga_ref, mimp_ref, expt_ref, o_ref,
                       m_sc, acc_sc, *, tq, tk, n_cmp, n_sel):
    qs = pl.program_id(1) * tq
    rows = HPG * tq
    ncp = kc_ref.shape[1]
    wlen = WINDOW + tq
    br = br_ref[...]
    lane_lo = lax.broadcasted_iota(jnp.int32, (tq, LANES), 1) < HEAD_DIM

    def q_pos(width):
        return qs + lax.broadcasted_iota(jnp.int32, (tq, width), 0)

    def per_head(x):
        return jnp.concatenate([x] * HPG, axis=0)

    col = lax.broadcasted_iota(jnp.int32, (tq, ncp), 1)
    cmp_mask = per_head(jnp.where((col * CMP_STRIDE + (CMP_BLOCK - 1) <= q_pos(ncp)) & (col < n_cmp), 0.0, NEG))
    cmp_any = per_head(jnp.where(q_pos(1) >= CMP_BLOCK - 1, 1.0, 0.0))
    w0 = pl.multiple_of(jnp.maximum(qs - WINDOW, 0), tq)
    kpos = w0 + lax.broadcasted_iota(jnp.int32, (tq, wlen), 1)
    win_mask = per_head(jnp.where((kpos <= q_pos(wlen)) & (kpos > q_pos(wlen) - WINDOW), 0.0, NEG))
    kt_diag = qs // tk
    kpos = kt_diag * tk + lax.broadcasted_iota(jnp.int32, (tq, tk), 1)
    diag_mask = per_head(jnp.where(kpos <= q_pos(tk), 0.0, NEG))

    blk = lax.broadcasted_iota(jnp.int32, (tq, LANES), 1)
    t = q_pos(LANES)
    jt = t // SEL_BLOCK
    forced = (blk == 0) | (blk == jt) | (blk == jt - 1)
    valid = blk * SEL_BLOCK <= t
    n_rank = ((n_sel + 7) // 8) * 8
    j_row = lax.broadcasted_iota(jnp.int32, (n_rank, tq), 0)

    def with_ones(v, g):
        own = (lax.broadcasted_iota(jnp.int32, v.shape, 1) // HEAD_DIM) == g
        return jnp.where(own, v, jnp.ones_like(v))

    def normalised(acc):
        return acc * (1.0 / pltpu.roll(acc, HEAD_DIM, axis=1))

    groups = range(N_KV)
    q = [q_ref[0, HPG * g:HPG * (g + 1)].reshape(rows, LANES) for g in groups]

    o_cmp, q_ext = [], []
    for g in groups:
        s = _dot_nt(q[g], kc_ref[0]) + cmp_mask
        e = jnp.exp2(s - jnp.max(s, axis=-1, keepdims=True))
        p = e * (cmp_any / jnp.sum(e, axis=-1, keepdims=True))
        o_cmp.append(_dot(p.astype(BF16), vc_ref[0]))
        p_sum = p[0:tq] + p[tq:2 * tq] + p[2 * tq:3 * tq] + p[3 * tq:4 * tq]
        imp = _split3_dot(p_sum, mimp_ref[...])

        score = jnp.where(valid, jnp.where(forced, FORCE, imp), NEG)
        score = jnp.where(blk < n_sel, score, BELOW_NEG)
        left = score.T[:n_rank]
        sel_t = jnp.zeros((n_rank, tq), F32)
        for _ in range(min(TOP_N, n_sel)):
            top = jnp.max(left, axis=0, keepdims=True)
            first = jnp.min(jnp.where(left == top, j_row, n_rank), axis=0, keepdims=True)
            taken = j_row == first
            sel_t = jnp.where(taken, 1.0, sel_t)
            left = jnp.where(taken, TAKEN, left)
        if n_rank < LANES:
            sel_t = jnp.concatenate([sel_t, jnp.zeros((LANES - n_rank, tq), F32)], axis=0)
        sel = sel_t.T
        block_bias = jnp.where(sel > 0.5, 0.0, NEG).astype(BF16)
        q_ext.append(jnp.concatenate([q[g], per_head(block_bias)], axis=1))
    q_ext = jnp.concatenate(q_ext, axis=0)

    m_sc[...] = jnp.full((N_KV * rows, LANES), NEG, F32)
    acc_sc[...] = jnp.zeros((N_KV * rows, LANES), F32)

    def key_tile(kt, causal):
        k0 = pl.multiple_of(kt * tk, tk)
        k_ext = jnp.concatenate([slc_ref[0, pl.ds(k0, tk), 0:LANES], expt_ref[kt]], axis=1)
        v = slc_ref[0, pl.ds(k0, tk), LANES:2 * LANES]
        s_all = _dot_nt(q_ext, k_ext)
        for g in groups:
            rs = slice(g * rows, (g + 1) * rows)
            s = s_all[rs]
            if causal:
                s = s + diag_mask
            m_prev = m_sc[rs]
            m_next = jnp.maximum(m_prev, jnp.max(s, axis=-1, keepdims=True))
            p = jnp.exp2(s - jnp.concatenate([m_next] * (tk // LANES), axis=1))
            acc_sc[rs] = jnp.exp2(m_prev - m_next) * acc_sc[rs] + _dot(p.astype(BF16), with_ones(v, g))
            m_sc[rs] = m_next

    def full_tile(kt, carry):
        key_tile(kt, False)
        return carry

    lax.fori_loop(0, kt_diag, full_tile, 0)
    key_tile(kt_diag, True)

    k = win_ref[0, pl.ds(w0, wlen), 0:LANES]
    v = win_ref[0, pl.ds(w0, wlen), LANES:2 * LANES]
    s_all = _dot_nt(jnp.concatenate(q, axis=0), k)
    for g in groups:
        rs = slice(g * rows, (g + 1) * rows)
        s = s_all[rs] + win_mask
        e = jnp.exp2(s - jnp.max(s, axis=-1, keepdims=True))
        o_win = normalised(_dot(e.astype(BF16), with_ones(v, g)))
        o_slc = normalised(acc_sc[rs])

        heads = []
        for h in range(HPG):
            hs = slice(h * tq, (h + 1) * tq)
            c = 3 * (HPG * g + h)
            heads.append(br[:, c:c + 1] * o_cmp[g][hs] + br[:, c + 1:c + 2] * o_slc[hs]
                         + br[:, c + 2:c + 3] * o_win[hs])
        for hh in range(HPG // 2):
            even, odd = heads[2 * hh], heads[2 * hh + 1]
            if g == 0:
                odd = pltpu.roll(odd, HEAD_DIM, axis=1)
            else:
                even = pltpu.roll(even, HEAD_DIM, axis=1)
            c0 = (2 * g + hh) * LANES
            o_ref[:, c0:c0 + LANES] = (jnp.where(lane_lo, even, odd) * ga_ref[:, c0:c0 + LANES]).astype(BF16)


def _nsa_prompt(q_slots, kc, vc, slc_b, win_b, br, ga, *, n_batch, seq):
    tq = min(256, seq)
    tk = min(512, seq)
    n_cmp = seq // CMP_STRIDE - (CMP_BLOCK // CMP_STRIDE) + 1
    n_sel = seq // SEL_BLOCK
    assert n_sel <= LANES and seq % tk == 0 and tk % tq == 0 and seq >= WINDOW + tq
    ncp = kc.shape[1]
    mimp = _importance_matrix(ncp, n_cmp, n_sel, LANES)
    key_blk = (jnp.arange(seq, dtype=jnp.int32) // SEL_BLOCK).reshape(seq // tk, tk, 1)
    expand = (key_blk == jnp.arange(LANES, dtype=jnp.int32).reshape(1, 1, LANES)).astype(BF16)
    nq = seq // tq
    rows = HPG * tq
    return pl.pallas_call(
        functools.partial(_nsa_prompt_kernel, tq=tq, tk=tk, n_cmp=n_cmp, n_sel=n_sel),
        grid=(n_batch, nq),
        in_specs=[pl.BlockSpec((1, N_HEADS, tq, LANES), lambda b, i: (b, 0, i, 0)),
                  pl.BlockSpec((1, ncp, LANES), lambda b, i: (b, 0, 0)),
                  pl.BlockSpec((1, ncp, LANES), lambda b, i: (b, 0, 0)),
                  pl.BlockSpec((1, seq, KV_ROW), lambda b, i: (b, 0, 0)),
                  pl.BlockSpec((1, seq, KV_ROW), lambda b, i: (b, 0, 0)),
                  pl.BlockSpec((tq, LANES), lambda b, i: (b * nq + i, 0)),
                  pl.BlockSpec((tq, D_NSA), lambda b, i: (b * nq + i, 0)),
                  pl.BlockSpec(mimp.shape, lambda b, i: (0, 0)),
                  pl.BlockSpec(expand.shape, lambda b, i: (0, 0, 0))],
        out_specs=pl.BlockSpec((tq, D_NSA), lambda b, i: (b * nq + i, 0)),
        out_shape=jax.ShapeDtypeStruct((n_batch * seq, D_NSA), BF16),
        scratch_shapes=[pltpu.VMEM((N_KV * rows, LANES), F32)] * 2,
        compiler_params=_cparams("arbitrary", "arbitrary"),
        name="nsa_prompt",
    )(q_slots, kc, vc, slc_b.reshape(n_batch, seq, KV_ROW), win_b.reshape(n_batch, seq, KV_ROW), br, ga, mimp,
      expand)


def _retention_prompt_kernel(rq_ref, rk_ref, rv_ref, o_ref, s_ref, *, cl):
    @pl.when(pl.program_id(0) == 0)
    def _():
        s_ref[...] = jnp.zeros(s_ref.shape, F32)

    i_pos = lax.broadcasted_iota(jnp.int32, (cl, cl), 0)
    j_pos = lax.broadcasted_iota(jnp.int32, (cl, cl), 1)
    diff = (i_pos - j_pos).astype(F32)
    i_col = lax.broadcasted_iota(jnp.int32, (cl, 1), 0).astype(F32)
    lane_lo = lax.broadcasted_iota(jnp.int32, (cl, LANES), 1) < DK_RET
    for h in range(N_RET_HEADS):
        pair, odd = divmod(h, 2)
        log_g = math.log(1.0 - 2.0 ** (-5.0 - h))
        keep = lane_lo if odd == 0 else jnp.logical_not(lane_lo)
        decay = jnp.where(diff >= 0, jnp.exp(log_g * jnp.maximum(diff, 0.0)), 0.0)
        xi = jnp.exp(log_g * (i_col + 1.0))
        zeta = jnp.exp(log_g * (cl - 1.0 - i_col))
        for b in range(rq_ref.shape[0]):
            q2 = rq_ref[b, :, pair * LANES:(pair + 1) * LANES]
            k2 = rk_ref[b, :, pair * LANES:(pair + 1) * LANES]
            s_pair = jnp.concatenate([s_ref[b, 2 * pair], s_ref[b, 2 * pair + 1]], axis=0)
            qh = jnp.where(keep, q2, jnp.zeros_like(q2))
            v = rv_ref[b, :, h * DV_RET:(h + 1) * DV_RET]
            a = _dot_nt(qh, k2) * decay
            o_ref[b, :, h * DV_RET:(h + 1) * DV_RET] = (_dot(a.astype(BF16), v)
                                                         + _dot(qh, s_pair.astype(BF16)) * xi)
            kz = (k2.astype(F32) * zeta).astype(BF16)
            u = _dot_tn(kz, v)[odd * DK_RET:(odd + 1) * DK_RET]
            s_ref[b, h] = s_ref[b, h] * math.exp(log_g * cl) + u


def _retention_prompt(rq, rk, rv, *, n_batch, seq):
    cl = math.gcd(seq, RET_CHUNK)
    nrot = N_RET_HEADS * DK_RET
    return pl.pallas_call(
        functools.partial(_retention_prompt_kernel, cl=cl),
        grid=(seq // cl,),
        in_specs=[pl.BlockSpec((n_batch, cl, nrot), lambda i: (0, i, 0)),
                  pl.BlockSpec((n_batch, cl, nrot), lambda i: (0, i, 0)),
                  pl.BlockSpec((n_batch, cl, D_RET), lambda i: (0, i, 0))],
        out_specs=[pl.BlockSpec((n_batch, cl, D_RET), lambda i: (0, i, 0)),
                   pl.BlockSpec((n_batch, N_RET_HEADS, DK_RET, DV_RET), lambda i: (0, 0, 0, 0))],
        out_shape=[jax.ShapeDtypeStruct((n_batch, seq, D_RET), F32),
                   jax.ShapeDtypeStruct((n_batch, N_RET_HEADS, DK_RET, DV_RET), F32)],
        compiler_params=_cparams("arbitrary"),
        name="retention_prompt",
    )(rq.reshape(n_batch, seq, nrot), rk.reshape(n_batch, seq, nrot), rv.reshape(n_batch, seq, D_RET))


def _retention_step_kernel(rq_ref, rk_ref, rv_ref, s0_ref, o_ref, s_ref):
    lane_lo = lax.broadcasted_iota(jnp.int32, (8, LANES), 1) < DK_RET
    diag = (lax.broadcasted_iota(jnp.int32, (LANES, LANES), 0)
            == lax.broadcasted_iota(jnp.int32, (LANES, LANES), 1))
    for pair in range(N_RET_HEADS // 2):
        q2 = jnp.broadcast_to(rq_ref[0, :, pair * LANES:(pair + 1) * LANES], (8, LANES))
        k_row = rk_ref[0, :, pair * LANES:(pair + 1) * LANES]
        k2 = jnp.broadcast_to(k_row, (8, LANES))
        k_col = jnp.sum(jnp.where(diag, jnp.broadcast_to(k_row, (LANES, LANES)), 0.0), axis=-1, keepdims=True)
        s_pair = jnp.concatenate([s0_ref[0, 2 * pair], s0_ref[0, 2 * pair + 1]], axis=0)
        for odd in range(2):
            h = 2 * pair + odd
            g = math.exp(math.log(1.0 - 2.0 ** (-5.0 - h)))
            keep = lane_lo if odd == 0 else jnp.logical_not(lane_lo)
            qh = jnp.where(keep, q2, 0.0)
            v_row = rv_ref[0, :, h * DV_RET:(h + 1) * DV_RET]
            qk = jnp.sum(qh * k2, axis=-1, keepdims=True)
            o = qk * v_row + _dot(qh, s_pair) * g
            o_ref[0, :, h * DV_RET:(h + 1) * DV_RET] = o[0:1]
            u = (k_col * v_row)[odd * DK_RET:(odd + 1) * DK_RET]
            s_ref[0, h] = s0_ref[0, h] * g + u


def _retention_step(rq, rk, rv, s0):
    b = rq.shape[0]
    nrot = N_RET_HEADS * DK_RET
    state_spec = pl.BlockSpec((1, N_RET_HEADS, DK_RET, DV_RET), lambda i: (i, 0, 0, 0))
    return pl.pallas_call(
        _retention_step_kernel,
        grid=(b,),
        in_specs=[pl.BlockSpec((1, 1, nrot), lambda i: (i, 0, 0)),
                  pl.BlockSpec((1, 1, nrot), lambda i: (i, 0, 0)),
                  pl.BlockSpec((1, 1, D_RET), lambda i: (i, 0, 0)),
                  state_spec],
        out_specs=[pl.BlockSpec((1, 1, D_RET), lambda i: (i, 0, 0)), state_spec],
        out_shape=[jax.ShapeDtypeStruct((b, 1, D_RET), F32),
                   jax.ShapeDtypeStruct((b, N_RET_HEADS, DK_RET, DV_RET), F32)],
        compiler_params=_cparams("arbitrary"),
        name="retention_step",
    )(rq.astype(F32).reshape(b, 1, nrot), rk.astype(F32).reshape(b, 1, nrot), rv.astype(F32).reshape(b, 1, D_RET),
      s0)


def _nsa_step1_kernel(q_ref, kc_ref, vc_ref, win_ref, new_ref, mimp_ref, oc_ref, ow_ref, idx_ref, *,
                      t, n_cmp, n_sel):
    q = q_ref[0]
    nq = q.shape[0]
    ncp = kc_ref.shape[1]
    nsp = mimp_ref.shape[1]

    s = _dot_nt(q, kc_ref[0])
    col = lax.broadcasted_iota(jnp.int32, (nq, ncp), 1)
    mask = (col * CMP_STRIDE + (CMP_BLOCK - 1) <= t) & (col < n_cmp)
    s = jnp.where(mask, s, NEG)
    e = jnp.exp2(s - jnp.max(s, axis=-1, keepdims=True))
    p = jnp.where(mask, e, 0.0) * (1.0 / jnp.sum(e, axis=-1, keepdims=True))
    oc_ref[0] = _dot(p.astype(BF16), vc_ref[0])
    imp_rows = _split3_dot(p, mimp_ref[...])

    blk = lax.broadcasted_iota(jnp.int32, (1, nsp), 1)
    jt = t // SEL_BLOCK
    forced = (blk == 0) | (blk == jt) | (blk == jt - 1)
    scores = []
    for g in range(N_KV):
        imp = jnp.sum(imp_rows[HPG * g:HPG * (g + 1)], axis=0, keepdims=True)
        sc = jnp.where(blk * SEL_BLOCK <= t, jnp.where(forced, FORCE, imp), NEG)
        scores.append(jnp.where(blk < n_sel, sc, BELOW_NEG))
    s_rows = jnp.concatenate(scores + [jnp.full((LANES - N_KV, nsp), BELOW_NEG, F32)], axis=0)
    s_cols = s_rows.T
    a_idx = lax.broadcasted_iota(jnp.int32, (nsp, nsp), 0)
    c_idx = lax.broadcasted_iota(jnp.int32, (nsp, nsp), 1)
    rank_lane = lax.broadcasted_iota(jnp.int32, (nsp, LANES), 1).astype(F32)
    a_val = lax.broadcasted_iota(jnp.int32, (nsp, LANES), 0).astype(F32)
    rows_out = []
    for g in range(N_KV):
        s_col = s_cols[:, g:g + 1]
        s_row = s_rows[g:g + 1, :]
        beats = (s_row > s_col) | ((s_row == s_col) & (c_idx < a_idx))
        rank = jnp.sum(jnp.where(beats, 1.0, 0.0), axis=-1, keepdims=True)
        rows_out.append(jnp.sum(jnp.where(rank == rank_lane, a_val, 0.0), axis=0, keepdims=True))
    idx = jnp.concatenate(rows_out + [jnp.zeros((8 - N_KV, LANES), F32)], axis=0)
    idx_ref[0] = idx.astype(jnp.int32)

    wb = win_ref.shape[-1]
    kw_t = win_ref[0, 0].reshape(LANES, wb).astype(BF16)
    vw_t = win_ref[0, 1].reshape(LANES, wb).astype(BF16)
    k_new = new_ref[0, :, 0:LANES]
    v_new = new_ref[0, :, LANES:2 * LANES]
    s = _dot(q, kw_t)
    kpos = t - wb + lax.broadcasted_iota(jnp.int32, (nq, wb), 1)
    s = jnp.where((kpos > t - WINDOW) & (kpos >= 0), s, NEG)
    s_new = jnp.sum(q.astype(F32) * k_new, axis=-1, keepdims=True)
    m = jnp.maximum(jnp.max(s, axis=-1, keepdims=True), s_new)
    e = jnp.exp2(s - m)
    e_new = jnp.exp2(s_new - m)
    ow_ref[0] = ((_dot_nt(e.astype(BF16), vw_t) + e_new * v_new)
                 * (1.0 / (jnp.sum(e, axis=-1, keepdims=True) + e_new)))


def _nsa_step1(q16, kc, vc, state_win_t, new_win, mimp, *, t, n_cmp, n_sel):
    b, nq, _ = q16.shape
    ncp = kc.shape[1]

    def per_b(shape):
        return pl.BlockSpec((1,) + shape, lambda i: (i,) + (0,) * len(shape))

    return pl.pallas_call(
        functools.partial(_nsa_step1_kernel, t=t, n_cmp=n_cmp, n_sel=n_sel),
        grid=(b,),
        in_specs=[per_b((nq, LANES)), per_b((ncp, LANES)), per_b((ncp, LANES)), per_b(state_win_t.shape[1:]),
                  per_b((1, KV_ROW)), pl.BlockSpec(mimp.shape, lambda i: (0, 0))],
        out_specs=[per_b((nq, LANES)), per_b((nq, LANES)), per_b((8, LANES))],
        out_shape=[jax.ShapeDtypeStruct((b, nq, LANES), F32), jax.ShapeDtypeStruct((b, nq, LANES), F32),
                   jax.ShapeDtypeStruct((b, 8, LANES), jnp.int32)],
        compiler_params=_cparams("arbitrary"),
        name="nsa_step_compressed_window",
    )(q16, kc, vc, state_win_t, new_win, mimp)


def _nsa_step2_kernel(idx_sm, pt_sm, q_ref, *refs, n_cache_blocks, k_eff):
    page_refs = refs[:N_KV * k_eff]
    new_ref, oc_ref, ow_ref, br_ref, o_ref = refs[N_KV * k_eff:]
    b = pl.program_id(0)
    q = q_ref[0]
    nq = q.shape[0]
    bpp = PAGE_SIZE // SEL_BLOCK
    nk = k_eff * PAGE_SIZE
    k_new = new_ref[0, :, 0:LANES]
    v_new = new_ref[0, :, LANES:2 * LANES]
    s_new_all = jnp.sum(q.astype(F32) * k_new, axis=-1, keepdims=True)
    lane_k = lax.broadcasted_iota(jnp.int32, (nq, nk), 1)
    slot = lane_k // PAGE_SIZE
    half = (lane_k % PAGE_SIZE) // SEL_BLOCK
    outs = []
    for g in range(N_KV):
        pages = [page_refs[g * k_eff + k][0] for k in range(k_eff)]
        keys_t = jnp.concatenate([p[0].reshape(LANES, PAGE_SIZE) for p in pages], axis=-1).astype(BF16)
        vals_t = jnp.concatenate([p[1].reshape(LANES, PAGE_SIZE) for p in pages], axis=-1).astype(BF16)
        allowed = jnp.zeros((nq, nk), jnp.int32)
        new_selected = jnp.int32(0)
        for k in range(k_eff):
            j = idx_sm[b, g * k_eff + k]
            in_block = (slot == k) & (half == j % bpp)
            allowed = jnp.where(in_block, (j < n_cache_blocks).astype(jnp.int32), allowed)
            new_selected = new_selected | (j == n_cache_blocks).astype(jnp.int32)
        s = jnp.where(allowed > 0, _dot(q, keys_t), NEG)
        s_new = jnp.where(new_selected > 0, s_new_all, NEG)
        m = jnp.maximum(jnp.max(s, axis=-1, keepdims=True), s_new)
        e = jnp.exp2(s - m)
        e_new = jnp.exp2(s_new - m)
        outs.append((_dot_nt(e.astype(BF16), vals_t) + e_new * v_new)
                    * (1.0 / (jnp.sum(e, axis=-1, keepdims=True) + e_new)))
    row = lax.broadcasted_iota(jnp.int32, (nq, LANES), 0)
    lane = lax.broadcasted_iota(jnp.int32, (nq, LANES), 1)
    o_slc = jnp.where(row < HPG, outs[0], outs[1])
    br = br_ref[0]
    gates = [jnp.sum(jnp.where(lane == 3 * row + j, br, 0.0), axis=-1, keepdims=True) for j in range(3)]
    o = gates[0] * oc_ref[0] + gates[1] * o_slc + gates[2] * ow_ref[0]
    lane_lo = lax.broadcasted_iota(jnp.int32, (1, LANES), 1) < HEAD_DIM
    for pair in range(N_HEADS // 2):
        even, odd = o[2 * pair:2 * pair + 1], o[2 * pair + 1:2 * pair + 2]
        if 2 * pair < HPG:
            odd = pltpu.roll(odd, HEAD_DIM, axis=1)
        else:
            even = pltpu.roll(even, HEAD_DIM, axis=1)
        o_ref[0, :, pair * LANES:(pair + 1) * LANES] = jnp.where(lane_lo, even, odd)


def _nsa_step2(idx, page_table, q16, cache_t, new_slc, o_cmp, o_win, br, *, n_cache_blocks, k_eff):
    b, nq, _ = q16.shape
    bpp = PAGE_SIZE // SEL_BLOCK

    def page_map(i, idx_sm, pt_sm, c):
        j = jnp.minimum(idx_sm[i, c], n_cache_blocks - 1)
        return (pt_sm[i, j // bpp], 0, 0, 0, 0)

    def per_b(shape):
        return pl.BlockSpec((1,) + shape, lambda i, idx_sm, pt_sm: (i,) + (0,) * len(shape))

    n_blk = N_KV * k_eff
    in_specs = [per_b((nq, LANES))]
    in_specs += [pl.BlockSpec((1,) + cache_t.shape[1:], functools.partial(page_map, c=c)) for c in range(n_blk)]
    in_specs += [per_b((1, KV_ROW)), per_b((nq, LANES)), per_b((nq, LANES)), per_b((1, LANES))]
    return pl.pallas_call(
        functools.partial(_nsa_step2_kernel, n_cache_blocks=n_cache_blocks, k_eff=k_eff),
        grid_spec=pltpu.PrefetchScalarGridSpec(
            num_scalar_prefetch=2,
            grid=(b,),
            in_specs=in_specs,
            out_specs=per_b((1, D_NSA)),
        ),
        out_shape=jax.ShapeDtypeStruct((b, 1, D_NSA), F32),
        compiler_params=_cparams("arbitrary"),
        name="nsa_step_selected",
    )(idx, page_table, q16, *([cache_t] * n_blk), new_slc, o_cmp, o_win, br)


def _out_kernel(*refs, mul_ga):
    if mul_ga:
        x_ref, gate_ref, ya_ref, ga_ref, oret_ref, gr_ref, gret_ref, w_ref, y_ref = refs
        ya = (ya_ref[...] * ga_ref[...]).astype(BF16)
    else:
        x_ref, gate_ref, ya_ref, oret_ref, gr_ref, gret_ref, w_ref, y_ref = refs
        ya = ya_ref[...]
    o = oret_ref[...]
    parts = []
    for h in range(N_RET_HEADS):
        t = o[:, h * DV_RET:(h + 1) * DV_RET]
        ms = jnp.mean(t * t, axis=-1, keepdims=True)
        parts.append(t * lax.rsqrt(ms + EPS) * gret_ref[...])
    yr = (jnp.concatenate(parts, axis=-1) * gr_ref[...]).astype(BF16)
    y_ref[...] = x_ref[...] + gate_ref[0] * _dot(jnp.concatenate([ya, yr], axis=-1), w_ref[...])


def _output(x2, gate, ya, ga, o_ret, gr, g_ret, w_out, *, n_batch, per_row_mod):
    m = x2.shape[0]
    seq = m // n_batch
    tm = min(seq, 512)
    nt = seq // tm
    if per_row_mod:
        gate_spec = pl.BlockSpec((1, tm, D_MODEL), lambda i: (0, i, 0))
    else:
        gate_spec = pl.BlockSpec((1, 1, D_MODEL), lambda i: (i // nt, 0, 0))

    def rows_spec(width):
        return pl.BlockSpec((tm, width), lambda i: (i, 0))

    mul_ga = ga is not None
    args = [x2, gate, ya] + ([ga] if mul_ga else []) + [o_ret, gr, g_ret.reshape(1, DV_RET), w_out.astype(BF16)]
    in_specs = [rows_spec(D_MODEL), gate_spec, rows_spec(D_NSA)] + ([rows_spec(D_NSA)] if mul_ga else [])
    in_specs += [rows_spec(D_RET), rows_spec(D_RET), pl.BlockSpec((1, DV_RET), lambda i: (0, 0)),
                 pl.BlockSpec((D_MODEL, D_MODEL), lambda i: (0, 0))]
    return pl.pallas_call(
        functools.partial(_out_kernel, mul_ga=mul_ga),
        grid=(m // tm,),
        in_specs=in_specs,
        out_specs=rows_spec(D_MODEL),
        out_shape=jax.ShapeDtypeStruct((m, D_MODEL), F32),
        compiler_params=_cparams("arbitrary"),
        name="output_projection",
    )(*args)


def kernel(x_prompt, x_sample, c_prompt, c_sample, cache_cmp, cache_slc, state_win, state_ret, page_table, g_norm, w_ada, b_ada, w_in, g_q, g_kc, g_ks, g_kw, pe_ck, w_ck1, w_ck2, pe_cv, w_cv1, w_cv2, g_ret, w_out):
    assert g_norm.shape[0] == 1, "one layer"
    bp, seq, _ = x_prompt.shape
    bs, dec_seq, _ = x_sample.shape
    assert dec_seq == 1
    n_pages = page_table.shape[1]
    past = n_pages * PAGE_SIZE
    assert state_win.shape[2] == WINDOW and past >= WINDOW

    w_proj, gq_slots, gks2, gkw2 = _prep_proj_weights(w_in[0], g_q[0], g_ks[0], g_kw[0])
    w1x, w1kv, w2x, pe8, gkc2 = _prep_compress_weights(pe_ck[0], w_ck1[0], w_ck2[0], pe_cv[0], w_cv1[0], w_cv2[0], g_kc[0])

    pad = (-(bp + bs)) % 8
    c_all = jnp.concatenate([c_prompt, c_sample, jnp.zeros((pad, D_MODEL), F32)], axis=0)
    mod = _modulation(c_all, w_ada[0], b_ada[0])
    shift_p, scale_p, gate_p = (mod[:bp, k * D_MODEL:(k + 1) * D_MODEL].reshape(bp, 1, D_MODEL) for k in range(3))
    shift_s, scale_s, gate_s = (mod[bp:bp + bs, k * D_MODEL:(k + 1) * D_MODEL].reshape(1, bs, D_MODEL)
                                for k in range(3))

    xp2 = x_prompt.reshape(bp * seq, D_MODEL)
    cos_p, sin_p = _rope_tables(seq, 0, 1)
    (q_p, cmp_t, slc_t, slcb_p, win_t, winb_p, br_p, ga_p, rq_p, rk_p, rv_p, gr_p) = _projection(
        xp2, scale_p, shift_p, g_norm[0], w_proj, gq_slots, gks2, gkw2, cos_p, sin_p, n_batch=bp, per_row_mod=False,
        pos_minor=True)

    a_p = _chunk_matmul_paged(cmp_t.reshape(bp, 2, N_KV, HEAD_DIM, seq), None, w1kv, n_batch=bp,
                              n_pages=seq // PAGE_SIZE)
    kc_p, vc_p = _compress_combine(a_p, jnp.zeros((bp, 1, KV_ROW), F32), pe8, w1x, w2x, gkc2)
    ya_p = _nsa_prompt(q_p, kc_p, vc_p, slcb_p, winb_p, br_p, ga_p, n_batch=bp, seq=seq)
    oret_p, sret_p = _retention_prompt(rq_p, rk_p, rv_p, n_batch=bp, seq=seq)
    y_p = _output(xp2, gate_p, ya_p, None, oret_p.reshape(bp * seq, D_RET), gr_p, g_ret[0], w_out[0],
                  n_batch=bp, per_row_mod=False)

    xs2 = x_sample.reshape(bs, D_MODEL)
    cos_s, sin_s = _rope_tables(bs, past, 0)
    (q_s, cmp_s, slc_s, _, win_s, _, br_s, ga_s, rq_s, rk_s, rv_s, gr_s) = _projection(
        xs2, scale_s, shift_s, g_norm[0], w_proj, gq_slots, gks2, gkw2, cos_s, sin_s, n_batch=1, per_row_mod=True,
        pos_minor=False)

    n_chunk_cache = past // CMP_STRIDE
    n_cmp_s = n_chunk_cache
    n_sel_s = past // SEL_BLOCK + 1
    nsp = ((n_sel_s + LANES - 1) // LANES) * LANES
    to_native = (0, 2, 3, 4, 1)
    a_s = _chunk_matmul_paged(jnp.transpose(cache_cmp[0], to_native), page_table, w1kv, n_batch=bs,
                              n_pages=n_pages)
    kc_s, vc_s = _compress_combine(a_s, cmp_s.reshape(bs, 1, KV_ROW), pe8, w1x, w2x, gkc2)

    q16 = jnp.pad(q_s[0].transpose(1, 0, 2), ((0, 0), (0, 16 - N_HEADS), (0, 0)))
    mimp_s = _importance_matrix(n_cmp_s, n_cmp_s, n_sel_s, nsp)
    o_cmp_s, o_win_s, idx_s = _nsa_step1(q16, kc_s, vc_s, jnp.transpose(state_win[0], to_native),
                                         win_s.reshape(bs, 1, KV_ROW), mimp_s, t=past, n_cmp=n_cmp_s,
                                         n_sel=n_sel_s)
    k_eff = min(TOP_N, n_sel_s)
    idx_flat = idx_s[:, :N_KV, :k_eff].reshape(bs, N_KV * k_eff)
    o_nsa_s = _nsa_step2(idx_flat, page_table, q16, jnp.transpose(cache_slc[0], to_native),
                         slc_s.reshape(bs, 1, KV_ROW), o_cmp_s, o_win_s, br_s.reshape(bs, 1, LANES),
                         n_cache_blocks=past // SEL_BLOCK, k_eff=k_eff)

    oret_s, sret_s = _retention_step(rq_s, rk_s, rv_s, state_ret[0])
    y_s = _output(xs2, gate_s, o_nsa_s.reshape(bs, D_NSA), ga_s, oret_s.reshape(bs, D_RET), gr_s, g_ret[0], w_out[0],
                  n_batch=1, per_row_mod=True)

    def kv_rows(a, b, l):
        return a.reshape(1, b, l, 2, N_KV, HEAD_DIM)

    def kv_rows_t(a, b, l):
        return jnp.transpose(a.reshape(b, 2, N_KV, HEAD_DIM, l), (0, 4, 1, 2, 3))[None]

    keep_p = min(WINDOW, seq)
    s_win = jnp.concatenate([state_win[0][:, 1:].reshape(bs, WINDOW - 1, KV_ROW), win_s.reshape(bs, 1, KV_ROW)],
                            axis=1)
    return (y_p.reshape(bp, seq, D_MODEL), y_s.reshape(bs, 1, D_MODEL),
            kv_rows_t(cmp_t, bp, seq), kv_rows_t(slc_t, bp, seq),
            kv_rows_t(win_t[:, :, seq - keep_p:], bp, keep_p), sret_p[None],
            kv_rows(cmp_s, bs, 1), kv_rows(slc_s, bs, 1), kv_rows(s_win, bs, WINDOW), sret_s[None])
```

```python
import functools
import math

import jax
import jax.numpy as jnp
import numpy as np
from jax import lax
from jax.experimental import pallas as pl
from jax.experimental.pallas import tpu as pltpu

D_MODEL = 1024
HEAD_DIM = 64
D_NSA = 512
N_HEADS = 8
N_KV = 2
HPG = 4
CMP_BLOCK = 32
CMP_STRIDE = 16
SEL_BLOCK = 64
TOP_N = 16
WINDOW = 512
D_RET = 512
N_RET_HEADS = 4
DV_RET = 128
DK_RET = 64
RET_CHUNK = 128
PAGE_SIZE = 128
ROPE_BASE = 10000.0
EPS = 1e-6
NEG = -1e30
FORCE = 1e4
BELOW_NEG = -2e38
TAKEN = -3e38

KV_ROW = 2 * N_KV * HEAD_DIM
LANES = 128
VMEM_LIMIT_BYTES = 56 * 1024 * 1024

F32 = jnp.float32
BF16 = jnp.bfloat16

O_Q = 0
O_CMP = O_Q + D_NSA
O_SLC = O_CMP + KV_ROW
O_WIN = O_SLC + KV_ROW
O_BR = O_WIN + KV_ROW
O_GA = O_BR + LANES
O_RQ = O_GA + D_NSA
O_RK = O_RQ + N_RET_HEADS * DK_RET
O_RV = O_RK + N_RET_HEADS * DK_RET
O_GR = O_RV + D_RET
D_IN_PAD = O_GR + D_RET


def _cparams(*sem):
    return pltpu.CompilerParams(dimension_semantics=sem, vmem_limit_bytes=VMEM_LIMIT_BYTES)


def _dot(a, b):
    return jnp.dot(a, b, preferred_element_type=F32)


def _dot_nt(a, b):
    return lax.dot_general(a, b, (((1,), (1,)), ((), ())), preferred_element_type=F32)


def _dot_tn(a, b):
    return lax.dot_general(a, b, (((0,), (0,)), ((), ())), preferred_element_type=F32)


def _silu(x):
    return x * jax.nn.sigmoid(x)


def _split3_dot(p, m):
    p1 = p.astype(BF16)
    r1 = p - p1.astype(F32)
    p2 = r1.astype(BF16)
    p3 = (r1 - p2.astype(F32)).astype(BF16)
    return _dot(p1, m) + _dot(p2, m) + _dot(p3, m)


def _pair_rmsnorm(k, gain2):
    lo = lax.broadcasted_iota(jnp.int32, k.shape, 1) < HEAD_DIM
    sq = k * k
    s0 = jnp.sum(jnp.where(lo, sq, 0.0), axis=-1, keepdims=True)
    s1 = jnp.sum(jnp.where(lo, 0.0, sq), axis=-1, keepdims=True)
    ms = jnp.where(lo, s0, s1) * (1.0 / HEAD_DIM)
    return k * lax.rsqrt(ms + EPS) * gain2


def _mod_kernel(c_ref, w_ref, b_ref, o_ref):
    o_ref[...] = _dot(_silu(c_ref[...]).astype(BF16), w_ref[...]) + b_ref[...]


def _modulation(c_all, w_ada, b_ada):
    rows = c_all.shape[0]
    n = w_ada.shape[1]
    tn = 1024
    return pl.pallas_call(
        _mod_kernel,
        grid=(n // tn,),
        in_specs=[pl.BlockSpec((rows, D_MODEL), lambda j: (0, 0)),
                  pl.BlockSpec((D_MODEL, tn), lambda j: (0, j)),
                  pl.BlockSpec((1, tn), lambda j: (0, j))],
        out_specs=pl.BlockSpec((rows, tn), lambda j: (0, j)),
        out_shape=jax.ShapeDtypeStruct((rows, n), F32),
        compiler_params=_cparams("arbitrary"),
        name="adaln_modulation",
    )(c_all, w_ada.astype(BF16), b_ada.reshape(1, n))


def _rope_kernel(freq_ref, cos_ref, sin_ref, *, base, stride, rows):
    shape = (rows, LANES)
    r = lax.broadcasted_iota(jnp.int32, shape, 0)
    pos = (base + stride * (pl.program_id(0) * rows + r)).astype(F32)
    ang = pos * freq_ref[...]
    lane = lax.broadcasted_iota(jnp.int32, shape, 1)
    sign = jnp.where((lane & (DK_RET - 1)) < DK_RET // 2, -1.0, 1.0)
    cos_ref[...] = jnp.cos(ang)
    sin_ref[...] = jnp.sin(ang) * sign


def _rope_tables(n_rows, base, stride):
    half = DK_RET // 2
    freqs = ROPE_BASE ** (-jnp.arange(half, dtype=F32) / half)
    freq_row = jnp.tile(freqs, LANES // half).reshape(1, LANES)
    rows = min(n_rows, 512)
    return pl.pallas_call(
        functools.partial(_rope_kernel, base=base, stride=stride, rows=rows),
        grid=(n_rows // rows,),
        in_specs=[pl.BlockSpec((1, LANES), lambda i: (0, 0))],
        out_specs=[pl.BlockSpec((rows, LANES), lambda i: (i, 0))] * 2,
        out_shape=[jax.ShapeDtypeStruct((n_rows, LANES), F32)] * 2,
        compiler_params=_cparams("arbitrary"),
        name="rope_tables",
    )(freq_row)


def _proj_kernel(x_ref, scale_ref, shift_ref, gn_ref, w_ref, gq_ref, gks_ref, gkw_ref, cos_ref, sin_ref,
                 q_ref, cmp_ref, slc_ref, slcb_ref, win_ref, winb_ref, br_ref, ga_ref, rq_ref, rk_ref,
                 rv_ref, gr_ref, *, pos_minor):
    x = x_ref[...]
    ms = jnp.mean(x * x, axis=-1, keepdims=True)
    xn = x * lax.rsqrt(ms + EPS) * gn_ref[...]
    hb = (xn * (1.0 + scale_ref[0]) + shift_ref[0]).astype(BF16)

    def seg(start, width):
        return _dot(hb, w_ref[:, start:start + width])

    yq = seg(O_Q, D_NSA)
    lane_lo = lax.broadcasted_iota(jnp.int32, (yq.shape[0], LANES), 1) < HEAD_DIM
    for s in range(N_HEADS):
        t = yq[:, (s // 2) * LANES:(s // 2 + 1) * LANES]
        if s % 2 != s // HPG:
            t = pltpu.roll(t, HEAD_DIM, axis=1)
        t = jnp.where(lane_lo == (s < HPG), t, 0.0)
        ms = jnp.sum(t * t, axis=-1, keepdims=True) * (1.0 / HEAD_DIM)
        q_ref[0, s] = (t * lax.rsqrt(ms + EPS) * gq_ref[s]).astype(BF16)

    y_cmp = seg(O_CMP, KV_ROW)
    y = seg(O_SLC, KV_ROW)
    y_slc = jnp.concatenate([_pair_rmsnorm(y[:, :LANES], gks_ref[...]), y[:, LANES:]], axis=-1)
    slcb_ref[...] = y_slc.astype(BF16)
    y = seg(O_WIN, KV_ROW)
    y_win = jnp.concatenate([_pair_rmsnorm(y[:, :LANES], gkw_ref[...]), y[:, LANES:]], axis=-1)
    winb_ref[...] = y_win.astype(BF16)
    for ref, rows in ((cmp_ref, y_cmp), (slc_ref, y_slc), (win_ref, y_win)):
        if pos_minor:
            ref[0] = rows.T
        else:
            ref[...] = rows

    br_ref[...] = jax.nn.sigmoid(seg(O_BR, LANES))
    ga_ref[...] = _silu(seg(O_GA, D_NSA))

    first_half = (lax.broadcasted_iota(jnp.int32, (yq.shape[0], LANES), 1) & (DK_RET - 1)) < DK_RET // 2

    def rotary(y):
        tiles = []
        for k in range(y.shape[1] // LANES):
            t = y[:, k * LANES:(k + 1) * LANES]
            swapped = jnp.where(first_half, pltpu.roll(t, LANES - DK_RET // 2, axis=1),
                                pltpu.roll(t, DK_RET // 2, axis=1))
            tiles.append(t * cos_ref[...] + swapped * sin_ref[...])
        return jnp.concatenate(tiles, axis=-1)

    nrot = N_RET_HEADS * DK_RET
    rq_ref[...] = rotary(seg(O_RQ, nrot)).astype(BF16)
    rk_ref[...] = (rotary(seg(O_RK, nrot)) * DK_RET ** -0.5).astype(BF16)
    rv_ref[...] = seg(O_RV, D_RET).astype(BF16)
    gr_ref[...] = _silu(seg(O_GR, D_RET))


def _prep_proj_weights(w_in, g_q, g_ks, g_kw):
    n_br = 3 * N_HEADS
    w = jnp.concatenate([w_in[:, :O_BR], jnp.pad(w_in[:, O_BR:O_BR + n_br], ((0, 0), (0, LANES - n_br))),
                         w_in[:, O_BR + n_br:]], axis=1).astype(BF16)
    assert w.shape[1] == D_IN_PAD
    gz = jnp.zeros((HEAD_DIM,), F32)
    gq = g_q * (HEAD_DIM ** -0.5 * math.log2(math.e))
    gq_slots = jnp.stack([jnp.concatenate([gq, gz]) if s < HPG else jnp.concatenate([gz, gq])
                          for s in range(N_HEADS)]).reshape(N_HEADS, 1, LANES)
    return w, gq_slots, jnp.tile(g_ks, 2).reshape(1, LANES), jnp.tile(g_kw, 2).reshape(1, LANES)


def _projection(x2, scale, shift, g_norm, w, gq_slots, gks2, gkw2, cos_t, sin_t, *, n_batch, per_row_mod,
                pos_minor):
    m = x2.shape[0]
    seq = m // n_batch
    tm = min(seq, 256)
    nt = seq // tm
    if per_row_mod:
        mod_spec = pl.BlockSpec((1, tm, D_MODEL), lambda i: (0, i, 0))
    else:
        mod_spec = pl.BlockSpec((1, 1, D_MODEL), lambda i: (i // nt, 0, 0))

    def rows_spec(width):
        return pl.BlockSpec((tm, width), lambda i: (i, 0))

    def const_spec(shape):
        return pl.BlockSpec(shape, lambda i: (0,) * len(shape))

    kv_out = "kv rows"
    widths = [(KV_ROW, kv_out), (KV_ROW, kv_out), (KV_ROW, BF16), (KV_ROW, kv_out), (KV_ROW, BF16), (LANES, F32),
              (D_NSA, F32), (N_RET_HEADS * DK_RET, BF16), (N_RET_HEADS * DK_RET, BF16), (D_RET, BF16),
              (D_RET, F32)]
    out_shape = [jax.ShapeDtypeStruct((n_batch, N_HEADS, seq, LANES), BF16)]
    out_specs = [pl.BlockSpec((1, N_HEADS, tm, LANES), lambda i: (i // nt, 0, i % nt, 0))]
    for width, dt in widths:
        if pos_minor and dt is kv_out:
            out_shape.append(jax.ShapeDtypeStruct((n_batch, KV_ROW, seq), F32))
            out_specs.append(pl.BlockSpec((1, KV_ROW, tm), lambda i: (i // nt, 0, i % nt)))
        else:
            out_shape.append(jax.ShapeDtypeStruct((m, width), F32 if dt is kv_out else dt))
            out_specs.append(rows_spec(width))
    return pl.pallas_call(
        functools.partial(_proj_kernel, pos_minor=pos_minor),
        grid=(m // tm,),
        in_specs=[rows_spec(D_MODEL), mod_spec, mod_spec, const_spec((1, D_MODEL)),
                  const_spec((D_MODEL, D_IN_PAD)), const_spec((N_HEADS, 1, LANES)), const_spec((1, LANES)),
                  const_spec((1, LANES)), pl.BlockSpec((tm, LANES), lambda i: (i % nt, 0)),
                  pl.BlockSpec((tm, LANES), lambda i: (i % nt, 0))],
        out_specs=out_specs,
        out_shape=out_shape,
        compiler_params=_cparams("arbitrary"),
        name="input_projection",
    )(x2, scale, shift, g_norm.reshape(1, D_MODEL), w, gq_slots, gks2, gkw2, cos_t, sin_t)


def _compress_kernel(pt_ref, *refs, n_in):
    x_refs, next_ref = refs[:n_in], refs[n_in]
    new_ref, perm_ref, w1_ref, pe_ref, w2_ref, gkc_ref, kc_ref, vc_ref = refs[n_in + 1:]
    cpp = PAGE_SIZE // CMP_STRIDE
    n = n_in * cpp
    last_step = pl.program_id(1) == pl.num_programs(1) - 1
    perm = perm_ref[...]

    def chunk_rows(page_ref):
        return _dot_nt(perm, page_ref[0].reshape(KV_ROW, PAGE_SIZE).astype(BF16))

    def flat_chunks(moved, kv):
        return jnp.concatenate(
            [jnp.concatenate([r[cpp * l:cpp * (l + 1), kv * LANES:(kv + 1) * LANES] for l in range(CMP_STRIDE)],
                             axis=1) for r in moved], axis=0).astype(BF16)

    moved = [chunk_rows(r) for r in x_refs]
    moved_next = [chunk_rows(next_ref)]
    new8 = jnp.broadcast_to(new_ref[0], (8, KV_ROW)).astype(BF16)
    zeros = jnp.zeros((8, (CMP_STRIDE - 1) * LANES), BF16)
    row = lax.broadcasted_iota(jnp.int32, (n, LANES), 0)
    pre = []
    for kv in range(2):
        new_chunk = jnp.concatenate([new8[:, kv * LANES:(kv + 1) * LANES], zeros], axis=1)
        a_all = _dot(jnp.concatenate([flat_chunks(moved, kv), flat_chunks(moved_next, kv), pe_ref[kv], new_chunk],
                                     axis=0), w1_ref[kv])
        a = a_all[:n]
        a_next = a_all[n:n + 1, LANES:]
        bias = a_all[n + cpp:n + cpp + 1, :LANES] + a_all[n + cpp + 1:n + cpp + 2, LANES:]
        a_new = a_all[n + cpp + 8:n + cpp + 9, LANES:]
        hi = pltpu.roll(a[:, LANES:], n - 1, axis=0)
        hi = jnp.where(row == n - 1, jnp.where(last_step, a_new, a_next), hi)
        pre.append(a[:, :LANES] + hi + bias)
    out = _dot(_silu(jnp.concatenate(pre, axis=1)).astype(BF16), w2_ref[...])
    kc_ref[0] = _pair_rmsnorm(out[:, :LANES], gkc_ref[...]).astype(BF16)
    vc_ref[0] = out[:, LANES:].astype(BF16)


def _compress(rows_t, page_table, new_rows, weights, *, n_batch, n_pages):
    w1kv, w2x, pe8kv, gkc2 = weights
    b = n_batch
    cpp = PAGE_SIZE // CMP_STRIDE
    p = math.gcd(n_pages, 32)
    row = jnp.arange(PAGE_SIZE, dtype=jnp.int32)
    perm = ((row % cpp) * CMP_STRIDE + row // cpp).reshape(PAGE_SIZE, 1) == row.reshape(1, PAGE_SIZE)
    page_shape = (1,) + rows_t.shape[1:4] + (PAGE_SIZE,)

    def page_spec(page_of_step):
        if page_table is None:
            return pl.BlockSpec(page_shape, lambda i, j, pt: (i, 0, 0, 0, page_of_step(j)))
        return pl.BlockSpec(page_shape, lambda i, j, pt: (pt[i, page_of_step(j)], 0, 0, 0, 0))

    in_specs = [page_spec(lambda j, k=k: j * p + k) for k in range(p)]
    in_specs.append(page_spec(lambda j: jnp.minimum((j + 1) * p, n_pages - 1)))

    def const_spec(shape):
        return pl.BlockSpec(shape, lambda i, j, pt: (0,) * len(shape))

    in_specs += [pl.BlockSpec((1, 1, KV_ROW), lambda i, j, pt: (i, 0, 0)), const_spec((PAGE_SIZE, PAGE_SIZE)),
                 const_spec(w1kv.shape), const_spec(pe8kv.shape), const_spec(w2x.shape), const_spec((1, LANES))]
    if page_table is None:
        page_table = jnp.zeros((1, 1), jnp.int32)
    return pl.pallas_call(
        functools.partial(_compress_kernel, n_in=p),
        grid_spec=pltpu.PrefetchScalarGridSpec(
            num_scalar_prefetch=1,
            grid=(b, n_pages // p),
            in_specs=in_specs,
            out_specs=[pl.BlockSpec((1, p * cpp, LANES), lambda i, j, pt: (i, j, 0))] * 2,
        ),
        out_shape=[jax.ShapeDtypeStruct((b, n_pages * cpp, LANES), BF16)] * 2,
        compiler_params=_cparams("arbitrary", "arbitrary"),
        name="compress",
    )(page_table, *([rows_t] * (p + 1)), new_rows, perm.astype(BF16), w1kv, pe8kv, w2x, gkc2)


def _prep_compress_weights(pe_ck, w_ck1, w_ck2, pe_cv, w_cv1, w_cv2, g_kc):
    eye = jnp.eye(N_KV, dtype=F32)
    w1 = jnp.stack([w_ck1, w_cv1]).reshape(2, 2, CMP_STRIDE, HEAD_DIM, HEAD_DIM)
    w1kv = jnp.einsum('khldf,gp->klgdhpf', w1, eye).reshape(2, CMP_STRIDE * LANES, KV_ROW).astype(BF16)
    w2 = jnp.stack([w_ck2, w_cv2])
    w2x = jnp.einsum('kfd,kq,gp->kgfqpd', w2, eye, eye).reshape(KV_ROW, KV_ROW).astype(BF16)
    pe = jnp.stack([pe_ck, pe_cv]).reshape(2, 2, CMP_STRIDE, 1, HEAD_DIM)
    pe_rows = jnp.broadcast_to(pe, (2, 2, CMP_STRIDE, N_KV, HEAD_DIM)).reshape(2, 2, CMP_STRIDE * LANES)
    pe8kv = jnp.zeros((2, 8, CMP_STRIDE * LANES), F32).at[:, :2].set(pe_rows).astype(BF16)
    return w1kv, w2x, pe8kv, jnp.tile(g_kc, 2).reshape(1, LANES)


def _importance_matrix(n_cmp_pad, n_cmp, n_sel, n_sel_pad):
    ratio = SEL_BLOCK // CMP_STRIDE
    span = CMP_BLOCK // CMP_STRIDE
    m = np.zeros((n_cmp_pad, n_sel_pad), np.float32)
    for j in range(n_sel):
        for r in range(ratio):
            for s in range(span):
                n = ratio * j + r - s
                if 0 <= n < n_cmp:
                    m[n, j] += 1.0
    return jnp.asarray(m, BF16)


def _nsa_select_kernel(q_ref, kc_ref, vc_ref, mimp_ref, oc_ref, bias_ref, *, tq, n_cmp, n_sel):
    qs = pl.program_id(1) * tq
    rows = HPG * tq
    ncp = kc_ref.shape[1]

    def q_pos(width):
        return qs + lax.broadcasted_iota(jnp.int32, (tq, width), 0)

    def per_head(x):
        return jnp.concatenate([x] * HPG, axis=0)

    col = lax.broadcasted_iota(jnp.int32, (tq, ncp), 1)
    cmp_mask = per_head(jnp.where((col * CMP_STRIDE + (CMP_BLOCK - 1) <= q_pos(ncp)) & (col < n_cmp), 0.0, NEG))
    cmp_any = per_head(jnp.where(q_pos(1) >= CMP_BLOCK - 1, 1.0, 0.0))
    blk = lax.broadcasted_iota(jnp.int32, (tq, LANES), 1)
    t = q_pos(LANES)
    jt = t // SEL_BLOCK
    forced = (blk == 0) | (blk == jt) | (blk == jt - 1)
    valid = blk * SEL_BLOCK <= t
    n_rank = ((n_sel + 7) // 8) * 8
    j_row = lax.broadcasted_iota(jnp.int32, (n_rank, tq), 0)

    for g in range(N_KV):
        q = q_ref[0, HPG * g:HPG * (g + 1)].reshape(rows, LANES)
        s = _dot_nt(q, kc_ref[0]) + cmp_mask
        e = jnp.exp2(s - jnp.max(s, axis=-1, keepdims=True))
        p = e * (cmp_any / jnp.sum(e, axis=-1, keepdims=True))
        oc_ref[0, HPG * g:HPG * (g + 1)] = _dot(p.astype(BF16), vc_ref[0]).astype(BF16).reshape(HPG, tq, LANES)
        p_sum = p[0:tq] + p[tq:2 * tq] + p[2 * tq:3 * tq] + p[3 * tq:4 * tq]
        imp = _split3_dot(p_sum, mimp_ref[...])

        score = jnp.where(valid, jnp.where(forced, FORCE, imp), NEG)
        score = jnp.where(blk < n_sel, score, BELOW_NEG)
        left = score.T[:n_rank]
        sel_t = jnp.zeros((n_rank, tq), F32)
        for _ in range(min(TOP_N, n_sel)):
            top = jnp.max(left, axis=0, keepdims=True)
            first = jnp.min(jnp.where(left == top, j_row, n_rank), axis=0, keepdims=True)
            taken = j_row == first
            sel_t = jnp.where(taken, 1.0, sel_t)
            left = jnp.where(taken, TAKEN, left)
        if n_rank < LANES:
            sel_t = jnp.concatenate([sel_t, jnp.zeros((LANES - n_rank, tq), F32)], axis=0)
        bias_ref[0, g] = jnp.where(sel_t.T > 0.5, 0.0, NEG).astype(BF16)


def _nsa_select(q_slots, kc, vc, *, n_batch, seq):
    tq = min(512, seq)
    n_cmp = seq // CMP_STRIDE - (CMP_BLOCK // CMP_STRIDE) + 1
    n_sel = seq // SEL_BLOCK
    assert n_sel <= LANES
    ncp = kc.shape[1]
    mimp = _importance_matrix(ncp, n_cmp, n_sel, LANES)
    return pl.pallas_call(
        functools.partial(_nsa_select_kernel, tq=tq, n_cmp=n_cmp, n_sel=n_sel),
        grid=(n_batch, seq // tq),
        in_specs=[pl.BlockSpec((1, N_HEADS, tq, LANES), lambda b, i: (b, 0, i, 0)),
                  pl.BlockSpec((1, ncp, LANES), lambda b, i: (b, 0, 0)),
                  pl.BlockSpec((1, ncp, LANES), lambda b, i: (b, 0, 0)),
                  pl.BlockSpec(mimp.shape, lambda b, i: (0, 0))],
        out_specs=[pl.BlockSpec((1, N_HEADS, tq, LANES), lambda b, i: (b, 0, i, 0)),
                   pl.BlockSpec((1, N_KV, tq, LANES), lambda b, i: (b, 0, i, 0))],
        out_shape=[jax.ShapeDtypeStruct((n_batch, N_HEADS, seq, LANES), BF16),
                   jax.ShapeDtypeStruct((n_batch, N_KV, seq, LANES), BF16)],
        compiler_params=_cparams("arbitrary", "arbitrary"),
        name="nsa_select",
    )(q_slots, kc, vc, mimp)


def _nsa_prompt_kernel(q_ref, oc_ref, bias_ref, slc_ref, win_ref, br_ref, ga_ref, expt_ref, o_ref,
                       m_sc, acc_sc, *, tq, tk):
    qs = pl.program_id(1) * tq
    rows = HPG * tq
    wlen = WINDOW + tq
    br = br_ref[...]
    lane_lo = lax.broadcasted_iota(jnp.int32, (tq, LANES), 1) < HEAD_DIM

    def q_pos(width):
        return qs + lax.broadcasted_iota(jnp.int32, (tq, width), 0)

    def per_head(x):
        return jnp.concatenate([x] * HPG, axis=0)

    w0 = pl.multiple_of(jnp.maximum(qs - WINDOW, 0), tq)
    kpos = w0 + lax.broadcasted_iota(jnp.int32, (tq, wlen), 1)
    win_mask = per_head(jnp.where((kpos <= q_pos(wlen)) & (kpos > q_pos(wlen) - WINDOW), 0.0, NEG))
    kt_diag = qs // tk
    kpos = kt_diag * tk + lax.broadcasted_iota(jnp.int32, (tq, tk), 1)
    diag_mask = per_head(jnp.where(kpos <= q_pos(tk), 0.0, NEG))

    def with_ones(v, g):
        own = (lax.broadcasted_iota(jnp.int32, v.shape, 1) // HEAD_DIM) == g
        return jnp.where(own, v, jnp.ones_like(v))

    def normalised(acc):
        return acc * (1.0 / pltpu.roll(acc, HEAD_DIM, axis=1))

    groups = range(N_KV)
    q = [q_ref[0, HPG * g:HPG * (g + 1)].reshape(rows, LANES) for g in groups]
    q_ext = jnp.concatenate([jnp.concatenate([q[g], per_head(bias_ref[0, g])], axis=1) for g in groups], axis=0)

    m_sc[...] = jnp.full((N_KV * rows, LANES), NEG, F32)
    acc_sc[...] = jnp.zeros((N_KV * rows, LANES), F32)

    def key_tile(kt, causal):
        k0 = pl.multiple_of(kt * tk, tk)
        k_ext = jnp.concatenate([slc_ref[0, pl.ds(k0, tk), 0:LANES], expt_ref[kt]], axis=1)
        v = slc_ref[0, pl.ds(k0, tk), LANES:2 * LANES]
        s_all = _dot_nt(q_ext, k_ext)
        for g in groups:
            rs = slice(g * rows, (g + 1) * rows)
            s = s_all[rs]
            if causal:
                s = s + diag_mask
            m_prev = m_sc[rs]
            m_next = jnp.maximum(m_prev, jnp.max(s, axis=-1, keepdims=True))
            p = jnp.exp2(s - jnp.concatenate([m_next] * (tk // LANES), axis=1))
            acc_sc[rs] = jnp.exp2(m_prev - m_next) * acc_sc[rs] + _dot(p.astype(BF16), with_ones(v, g))
            m_sc[rs] = m_next

    def full_tile(kt, carry):
        key_tile(kt, False)
        return carry

    lax.fori_loop(0, kt_diag, full_tile, 0)
    key_tile(kt_diag, True)

    k = win_ref[0, pl.ds(w0, wlen), 0:LANES]
    v = win_ref[0, pl.ds(w0, wlen), LANES:2 * LANES]
    s_all = _dot_nt(jnp.concatenate(q, axis=0), k)
    for g in groups:
        rs = slice(g * rows, (g + 1) * rows)
        s = s_all[rs] + win_mask
        e = jnp.exp2(s - jnp.max(s, axis=-1, keepdims=True))
        o_win = normalised(_dot(e.astype(BF16), with_ones(v, g)))
        o_slc = normalised(acc_sc[rs])

        heads = []
        for h in range(HPG):
            hs = slice(h * tq, (h + 1) * tq)
            c = 3 * (HPG * g + h)
            heads.append(br[:, c:c + 1] * oc_ref[0, HPG * g + h].astype(F32) + br[:, c + 1:c + 2] * o_slc[hs]
                         + br[:, c + 2:c + 3] * o_win[hs])
        for hh in range(HPG // 2):
            even, odd = heads[2 * hh], heads[2 * hh + 1]
            if g == 0:
                odd = pltpu.roll(odd, HEAD_DIM, axis=1)
            else:
                even = pltpu.roll(even, HEAD_DIM, axis=1)
            c0 = (2 * g + hh) * LANES
            o_ref[:, c0:c0 + LANES] = (jnp.where(lane_lo, even, odd) * ga_ref[:, c0:c0 + LANES]).astype(BF16)


def _nsa_prompt(q_slots, o_cmp, block_bias, slc_b, win_b, br, ga, *, n_batch, seq):
    tq = min(256, seq)
    tk = min(512, seq)
    assert seq % tk == 0 and tk % tq == 0 and seq >= WINDOW + tq
    key_blk = (jnp.arange(seq, dtype=jnp.int32) // SEL_BLOCK).reshape(seq // tk, tk, 1)
    expand = (key_blk == jnp.arange(LANES, dtype=jnp.int32).reshape(1, 1, LANES)).astype(BF16)
    nq = seq // tq
    rows = HPG * tq
    return pl.pallas_call(
        functools.partial(_nsa_prompt_kernel, tq=tq, tk=tk),
        grid=(n_batch, nq),
        in_specs=[pl.BlockSpec((1, N_HEADS, tq, LANES), lambda b, i: (b, 0, i, 0)),
                  pl.BlockSpec((1, N_HEADS, tq, LANES), lambda b, i: (b, 0, i, 0)),
                  pl.BlockSpec((1, N_KV, tq, LANES), lambda b, i: (b, 0, i, 0)),
                  pl.BlockSpec((1, seq, KV_ROW), lambda b, i: (b, 0, 0)),
                  pl.BlockSpec((1, seq, KV_ROW), lambda b, i: (b, 0, 0)),
                  pl.BlockSpec((tq, LANES), lambda b, i: (b * nq + i, 0)),
                  pl.BlockSpec((tq, D_NSA), lambda b, i: (b * nq + i, 0)),
                  pl.BlockSpec(expand.shape, lambda b, i: (0, 0, 0))],
        out_specs=pl.BlockSpec((tq, D_NSA), lambda b, i: (b * nq + i, 0)),
        out_shape=jax.ShapeDtypeStruct((n_batch * seq, D_NSA), BF16),
        scratch_shapes=[pltpu.VMEM((N_KV * rows, LANES), F32)] * 2,
        compiler_params=_cparams("arbitrary", "arbitrary"),
        name="nsa_prompt",
    )(q_slots, o_cmp, block_bias, slc_b.reshape(n_batch, seq, KV_ROW), win_b.reshape(n_batch, seq, KV_ROW), br, ga,
      expand)


def _retention_prompt_kernel(rq_ref, rk_ref, rv_ref, o_ref, s_ref, *, cl):
    @pl.when(pl.program_id(0) == 0)
    def _():
        s_ref[...] = jnp.zeros(s_ref.shape, F32)

    i_pos = lax.broadcasted_iota(jnp.int32, (cl, cl), 0)
    j_pos = lax.broadcasted_iota(jnp.int32, (cl, cl), 1)
    diff = (i_pos - j_pos).astype(F32)
    i_col = lax.broadcasted_iota(jnp.int32, (cl, 1), 0).astype(F32)
    lane_lo = lax.broadcasted_iota(jnp.int32, (cl, LANES), 1) < DK_RET
    for h in range(N_RET_HEADS):
        pair, odd = divmod(h, 2)
        log_g = math.log(1.0 - 2.0 ** (-5.0 - h))
        keep = lane_lo if odd == 0 else jnp.logical_not(lane_lo)
        decay = jnp.where(diff >= 0, jnp.exp(log_g * jnp.maximum(diff, 0.0)), 0.0)
        xi = jnp.exp(log_g * (i_col + 1.0))
        zeta = jnp.exp(log_g * (cl - 1.0 - i_col))
        for b in range(rq_ref.shape[0]):
            q2 = rq_ref[b, :, pair * LANES:(pair + 1) * LANES]
            k2 = rk_ref[b, :, pair * LANES:(pair + 1) * LANES]
            s_pair = jnp.concatenate([s_ref[b, 2 * pair], s_ref[b, 2 * pair + 1]], axis=0)
            qh = jnp.where(keep, q2, jnp.zeros_like(q2))
            v = rv_ref[b, :, h * DV_RET:(h + 1) * DV_RET]
            a = _dot_nt(qh, k2) * decay
            o_ref[b, :, h * DV_RET:(h + 1) * DV_RET] = (_dot(a.astype(BF16), v)
                                                         + _dot(qh, s_pair.astype(BF16)) * xi)
            kz = (k2.astype(F32) * zeta).astype(BF16)
            u = _dot_tn(kz, v)[odd * DK_RET:(odd + 1) * DK_RET]
            s_ref[b, h] = s_ref[b, h] * math.exp(log_g * cl) + u


def _retention_prompt(rq, rk, rv, *, n_batch, seq):
    cl = math.gcd(seq, RET_CHUNK)
    nrot = N_RET_HEADS * DK_RET
    return pl.pallas_call(
        functools.partial(_retention_prompt_kernel, cl=cl),
        grid=(seq // cl,),
        in_specs=[pl.BlockSpec((n_batch, cl, nrot), lambda i: (0, i, 0)),
                  pl.BlockSpec((n_batch, cl, nrot), lambda i: (0, i, 0)),
                  pl.BlockSpec((n_batch, cl, D_RET), lambda i: (0, i, 0))],
        out_specs=[pl.BlockSpec((n_batch, cl, D_RET), lambda i: (0, i, 0)),
                   pl.BlockSpec((n_batch, N_RET_HEADS, DK_RET, DV_RET), lambda i: (0, 0, 0, 0))],
        out_shape=[jax.ShapeDtypeStruct((n_batch, seq, D_RET), F32),
                   jax.ShapeDtypeStruct((n_batch, N_RET_HEADS, DK_RET, DV_RET), F32)],
        compiler_params=_cparams("arbitrary"),
        name="retention_prompt",
    )(rq.reshape(n_batch, seq, nrot), rk.reshape(n_batch, seq, nrot), rv.reshape(n_batch, seq, D_RET))


def _retention_step_kernel(rq_ref, rk_ref, rv_ref, s0_ref, o_ref, s_ref):
    lane_lo = lax.broadcasted_iota(jnp.int32, (8, LANES), 1) < DK_RET
    diag = (lax.broadcasted_iota(jnp.int32, (LANES, LANES), 0)
            == lax.broadcasted_iota(jnp.int32, (LANES, LANES), 1))
    for pair in range(N_RET_HEADS // 2):
        q2 = jnp.broadcast_to(rq_ref[0, :, pair * LANES:(pair + 1) * LANES], (8, LANES))
        k_row = rk_ref[0, :, pair * LANES:(pair + 1) * LANES]
        k2 = jnp.broadcast_to(k_row, (8, LANES))
        k_col = jnp.sum(jnp.where(diag, jnp.broadcast_to(k_row, (LANES, LANES)), 0.0), axis=-1, keepdims=True)
        s_pair = jnp.concatenate([s0_ref[0, 2 * pair], s0_ref[0, 2 * pair + 1]], axis=0)
        for odd in range(2):
            h = 2 * pair + odd
            g = math.exp(math.log(1.0 - 2.0 ** (-5.0 - h)))
            keep = lane_lo if odd == 0 else jnp.logical_not(lane_lo)
            qh = jnp.where(keep, q2, 0.0)
            v_row = rv_ref[0, :, h * DV_RET:(h + 1) * DV_RET]
            qk = jnp.sum(qh * k2, axis=-1, keepdims=True)
            o = qk * v_row + _dot(qh, s_pair) * g
            o_ref[0, :, h * DV_RET:(h + 1) * DV_RET] = o[0:1]
            u = (k_col * v_row)[odd * DK_RET:(odd + 1) * DK_RET]
            s_ref[0, h] = s0_ref[0, h] * g + u


def _retention_step(rq, rk, rv, s0):
    b = rq.shape[0]
    nrot = N_RET_HEADS * DK_RET
    state_spec = pl.BlockSpec((1, N_RET_HEADS, DK_RET, DV_RET), lambda i: (i, 0, 0, 0))
    return pl.pallas_call(
        _retention_step_kernel,
        grid=(b,),
        in_specs=[pl.BlockSpec((1, 1, nrot), lambda i: (i, 0, 0)),
                  pl.BlockSpec((1, 1, nrot), lambda i: (i, 0, 0)),
                  pl.BlockSpec((1, 1, D_RET), lambda i: (i, 0, 0)),
                  state_spec],
        out_specs=[pl.BlockSpec((1, 1, D_RET), lambda i: (i, 0, 0)), state_spec],
        out_shape=[jax.ShapeDtypeStruct((b, 1, D_RET), F32),
                   jax.ShapeDtypeStruct((b, N_RET_HEADS, DK_RET, DV_RET), F32)],
        compiler_params=_cparams("arbitrary"),
        name="retention_step",
    )(rq.astype(F32).reshape(b, 1, nrot), rk.astype(F32).reshape(b, 1, nrot), rv.astype(F32).reshape(b, 1, D_RET),
      s0)


def _nsa_step1_kernel(q_ref, kc_ref, vc_ref, win_ref, new_ref, mimp_ref, oc_ref, ow_ref, idx_ref, *,
                      t, n_cmp, n_sel):
    q = q_ref[0]
    nq = q.shape[0]
    ncp = kc_ref.shape[1]
    nsp = mimp_ref.shape[1]

    s = _dot_nt(q, kc_ref[0])
    col = lax.broadcasted_iota(jnp.int32, (nq, ncp), 1)
    mask = (col * CMP_STRIDE + (CMP_BLOCK - 1) <= t) & (col < n_cmp)
    s = jnp.where(mask, s, NEG)
    e = jnp.exp2(s - jnp.max(s, axis=-1, keepdims=True))
    p = jnp.where(mask, e, 0.0) * (1.0 / jnp.sum(e, axis=-1, keepdims=True))
    oc_ref[0] = _dot(p.astype(BF16), vc_ref[0])
    imp_rows = _split3_dot(p, mimp_ref[...])

    blk = lax.broadcasted_iota(jnp.int32, (1, nsp), 1)
    jt = t // SEL_BLOCK
    forced = (blk == 0) | (blk == jt) | (blk == jt - 1)
    scores = []
    for g in range(N_KV):
        imp = jnp.sum(imp_rows[HPG * g:HPG * (g + 1)], axis=0, keepdims=True)
        sc = jnp.where(blk * SEL_BLOCK <= t, jnp.where(forced, FORCE, imp), NEG)
        scores.append(jnp.where(blk < n_sel, sc, BELOW_NEG))
    s_rows = jnp.concatenate(scores + [jnp.full((LANES - N_KV, nsp), BELOW_NEG, F32)], axis=0)
    s_cols = s_rows.T
    a_idx = lax.broadcasted_iota(jnp.int32, (nsp, nsp), 0)
    c_idx = lax.broadcasted_iota(jnp.int32, (nsp, nsp), 1)
    rank_lane = lax.broadcasted_iota(jnp.int32, (nsp, LANES), 1).astype(F32)
    a_val = lax.broadcasted_iota(jnp.int32, (nsp, LANES), 0).astype(F32)
    rows_out = []
    for g in range(N_KV):
        s_col = s_cols[:, g:g + 1]
        s_row = s_rows[g:g + 1, :]
        beats = (s_row > s_col) | ((s_row == s_col) & (c_idx < a_idx))
        rank = jnp.sum(jnp.where(beats, 1.0, 0.0), axis=-1, keepdims=True)
        rows_out.append(jnp.sum(jnp.where(rank == rank_lane, a_val, 0.0), axis=0, keepdims=True))
    idx = jnp.concatenate(rows_out + [jnp.zeros((8 - N_KV, LANES), F32)], axis=0)
    idx_ref[0] = idx.astype(jnp.int32)

    wb = win_ref.shape[-1]
    kw_t = win_ref[0, 0].reshape(LANES, wb).astype(BF16)
    vw_t = win_ref[0, 1].reshape(LANES, wb).astype(BF16)
    k_new = new_ref[0, :, 0:LANES]
    v_new = new_ref[0, :, LANES:2 * LANES]
    s = _dot(q, kw_t)
    kpos = t - wb + lax.broadcasted_iota(jnp.int32, (nq, wb), 1)
    s = jnp.where((kpos > t - WINDOW) & (kpos >= 0), s, NEG)
    s_new = jnp.sum(q.astype(F32) * k_new, axis=-1, keepdims=True)
    m = jnp.maximum(jnp.max(s, axis=-1, keepdims=True), s_new)
    e = jnp.exp2(s - m)
    e_new = jnp.exp2(s_new - m)
    ow_ref[0] = ((_dot_nt(e.astype(BF16), vw_t) + e_new * v_new)
                 * (1.0 / (jnp.sum(e, axis=-1, keepdims=True) + e_new)))


def _nsa_step1(q16, kc, vc, state_win_t, new_win, mimp, *, t, n_cmp, n_sel):
    b, nq, _ = q16.shape
    ncp = kc.shape[1]

    def per_b(shape):
        return pl.BlockSpec((1,) + shape, lambda i: (i,) + (0,) * len(shape))

    return pl.pallas_call(
        functools.partial(_nsa_step1_kernel, t=t, n_cmp=n_cmp, n_sel=n_sel),
        grid=(b,),
        in_specs=[per_b((nq, LANES)), per_b((ncp, LANES)), per_b((ncp, LANES)), per_b(state_win_t.shape[1:]),
                  per_b((1, KV_ROW)), pl.BlockSpec(mimp.shape, lambda i: (0, 0))],
        out_specs=[per_b((nq, LANES)), per_b((nq, LANES)), per_b((8, LANES))],
        out_shape=[jax.ShapeDtypeStruct((b, nq, LANES), F32), jax.ShapeDtypeStruct((b, nq, LANES), F32),
                   jax.ShapeDtypeStruct((b, 8, LANES), jnp.int32)],
        compiler_params=_cparams("arbitrary"),
        name="nsa_step_compressed_window",
    )(q16, kc, vc, state_win_t, new_win, mimp)


def _nsa_step2_kernel(idx_sm, pt_sm, q_ref, *refs, n_cache_blocks, k_eff):
    page_refs = refs[:N_KV * k_eff]
    new_ref, oc_ref, ow_ref, br_ref, o_ref = refs[N_KV * k_eff:]
    b = pl.program_id(0)
    q = q_ref[0]
    nq = q.shape[0]
    bpp = PAGE_SIZE // SEL_BLOCK
    nk = k_eff * PAGE_SIZE
    k_new = new_ref[0, :, 0:LANES]
    v_new = new_ref[0, :, LANES:2 * LANES]
    s_new_all = jnp.sum(q.astype(F32) * k_new, axis=-1, keepdims=True)
    lane_k = lax.broadcasted_iota(jnp.int32, (nq, nk), 1)
    slot = lane_k // PAGE_SIZE
    half = (lane_k % PAGE_SIZE) // SEL_BLOCK
    outs = []
    for g in range(N_KV):
        pages = [page_refs[g * k_eff + k][0] for k in range(k_eff)]
        keys_t = jnp.concatenate([p[0].reshape(LANES, PAGE_SIZE) for p in pages], axis=-1).astype(BF16)
        vals_t = jnp.concatenate([p[1].reshape(LANES, PAGE_SIZE) for p in pages], axis=-1).astype(BF16)
        allowed = jnp.zeros((nq, nk), jnp.int32)
        new_selected = jnp.int32(0)
        for k in range(k_eff):
            j = idx_sm[b, g * k_eff + k]
            in_block = (slot == k) & (half == j % bpp)
            allowed = jnp.where(in_block, (j < n_cache_blocks).astype(jnp.int32), allowed)
            new_selected = new_selected | (j == n_cache_blocks).astype(jnp.int32)
        s = jnp.where(allowed > 0, _dot(q, keys_t), NEG)
        s_new = jnp.where(new_selected > 0, s_new_all, NEG)
        m = jnp.maximum(jnp.max(s, axis=-1, keepdims=True), s_new)
        e = jnp.exp2(s - m)
        e_new = jnp.exp2(s_new - m)
        outs.append((_dot_nt(e.astype(BF16), vals_t) + e_new * v_new)
                    * (1.0 / (jnp.sum(e, axis=-1, keepdims=True) + e_new)))
    row = lax.broadcasted_iota(jnp.int32, (nq, LANES), 0)
    lane = lax.broadcasted_iota(jnp.int32, (nq, LANES), 1)
    o_slc = jnp.where(row < HPG, outs[0], outs[1])
    br = br_ref[0]
    gates = [jnp.sum(jnp.where(lane == 3 * row + j, br, 0.0), axis=-1, keepdims=True) for j in range(3)]
    o = gates[0] * oc_ref[0] + gates[1] * o_slc + gates[2] * ow_ref[0]
    lane_lo = lax.broadcasted_iota(jnp.int32, (1, LANES), 1) < HEAD_DIM
    for pair in range(N_HEADS // 2):
        even, odd = o[2 * pair:2 * pair + 1], o[2 * pair + 1:2 * pair + 2]
        if 2 * pair < HPG:
            odd = pltpu.roll(odd, HEAD_DIM, axis=1)
        else:
            even = pltpu.roll(even, HEAD_DIM, axis=1)
        o_ref[0, :, pair * LANES:(pair + 1) * LANES] = jnp.where(lane_lo, even, odd)


def _nsa_step2(idx, page_table, q16, cache_t, new_slc, o_cmp, o_win, br, *, n_cache_blocks, k_eff):
    b, nq, _ = q16.shape
    bpp = PAGE_SIZE // SEL_BLOCK

    def page_map(i, idx_sm, pt_sm, c):
        j = jnp.minimum(idx_sm[i, c], n_cache_blocks - 1)
        return (pt_sm[i, j // bpp], 0, 0, 0, 0)

    def per_b(shape):
        return pl.BlockSpec((1,) + shape, lambda i, idx_sm, pt_sm: (i,) + (0,) * len(shape))

    n_blk = N_KV * k_eff
    in_specs = [per_b((nq, LANES))]
    in_specs += [pl.BlockSpec((1,) + cache_t.shape[1:], functools.partial(page_map, c=c)) for c in range(n_blk)]
    in_specs += [per_b((1, KV_ROW)), per_b((nq, LANES)), per_b((nq, LANES)), per_b((1, LANES))]
    return pl.pallas_call(
        functools.partial(_nsa_step2_kernel, n_cache_blocks=n_cache_blocks, k_eff=k_eff),
        grid_spec=pltpu.PrefetchScalarGridSpec(
            num_scalar_prefetch=2,
            grid=(b,),
            in_specs=in_specs,
            out_specs=per_b((1, D_NSA)),
        ),
        out_shape=jax.ShapeDtypeStruct((b, 1, D_NSA), F32),
        compiler_params=_cparams("arbitrary"),
        name="nsa_step_selected",
    )(idx, page_table, q16, *([cache_t] * n_blk), new_slc, o_cmp, o_win, br)


def _out_kernel(*refs, mul_ga):
    if mul_ga:
        x_ref, gate_ref, ya_ref, ga_ref, oret_ref, gr_ref, gret_ref, w_ref, y_ref = refs
        ya = (ya_ref[...] * ga_ref[...]).astype(BF16)
    else:
        x_ref, gate_ref, ya_ref, oret_ref, gr_ref, gret_ref, w_ref, y_ref = refs
        ya = ya_ref[...]
    o = oret_ref[...]
    parts = []
    for h in range(N_RET_HEADS):
        t = o[:, h * DV_RET:(h + 1) * DV_RET]
        ms = jnp.mean(t * t, axis=-1, keepdims=True)
        parts.append(t * lax.rsqrt(ms + EPS) * gret_ref[...])
    yr = (jnp.concatenate(parts, axis=-1) * gr_ref[...]).astype(BF16)
    y_ref[...] = x_ref[...] + gate_ref[0] * _dot(jnp.concatenate([ya, yr], axis=-1), w_ref[...])


def _output(x2, gate, ya, ga, o_ret, gr, g_ret, w_out, *, n_batch, per_row_mod):
    m = x2.shape[0]
    seq = m // n_batch
    tm = min(seq, 512)
    nt = seq // tm
    if per_row_mod:
        gate_spec = pl.BlockSpec((1, tm, D_MODEL), lambda i: (0, i, 0))
    else:
        gate_spec = pl.BlockSpec((1, 1, D_MODEL), lambda i: (i // nt, 0, 0))

    def rows_spec(width):
        return pl.BlockSpec((tm, width), lambda i: (i, 0))

    mul_ga = ga is not None
    args = [x2, gate, ya] + ([ga] if mul_ga else []) + [o_ret, gr, g_ret.reshape(1, DV_RET), w_out.astype(BF16)]
    in_specs = [rows_spec(D_MODEL), gate_spec, rows_spec(D_NSA)] + ([rows_spec(D_NSA)] if mul_ga else [])
    in_specs += [rows_spec(D_RET), rows_spec(D_RET), pl.BlockSpec((1, DV_RET), lambda i: (0, 0)),
                 pl.BlockSpec((D_MODEL, D_MODEL), lambda i: (0, 0))]
    return pl.pallas_call(
        functools.partial(_out_kernel, mul_ga=mul_ga),
        grid=(m // tm,),
        in_specs=in_specs,
        out_specs=rows_spec(D_MODEL),
        out_shape=jax.ShapeDtypeStruct((m, D_MODEL), F32),
        compiler_params=_cparams("arbitrary"),
        name="output_projection",
    )(*args)


def kernel(x_prompt, x_sample, c_prompt, c_sample, cache_cmp, cache_slc, state_win, state_ret, page_table, g_norm, w_ada, b_ada, w_in, g_q, g_kc, g_ks, g_kw, pe_ck, w_ck1, w_ck2, pe_cv, w_cv1, w_cv2, g_ret, w_out):
    assert g_norm.shape[0] == 1, "one layer"
    bp, seq, _ = x_prompt.shape
    bs, dec_seq, _ = x_sample.shape
    assert dec_seq == 1
    n_pages = page_table.shape[1]
    past = n_pages * PAGE_SIZE
    assert state_win.shape[2] == WINDOW and past >= WINDOW

    w_proj, gq_slots, gks2, gkw2 = _prep_proj_weights(w_in[0], g_q[0], g_ks[0], g_kw[0])
    cmp_weights = _prep_compress_weights(pe_ck[0], w_ck1[0], w_ck2[0], pe_cv[0], w_cv1[0], w_cv2[0], g_kc[0])

    pad = (-(bp + bs)) % 8
    c_all = jnp.concatenate([c_prompt, c_sample, jnp.zeros((pad, D_MODEL), F32)], axis=0)
    mod = _modulation(c_all, w_ada[0], b_ada[0])
    shift_p, scale_p, gate_p = (mod[:bp, k * D_MODEL:(k + 1) * D_MODEL].reshape(bp, 1, D_MODEL) for k in range(3))
    shift_s, scale_s, gate_s = (mod[bp:bp + bs, k * D_MODEL:(k + 1) * D_MODEL].reshape(1, bs, D_MODEL)
                                for k in range(3))

    xp2 = x_prompt.reshape(bp * seq, D_MODEL)
    cos_p, sin_p = _rope_tables(seq, 0, 1)
    (q_p, cmp_t, slc_t, slcb_p, win_t, winb_p, br_p, ga_p, rq_p, rk_p, rv_p, gr_p) = _projection(
        xp2, scale_p, shift_p, g_norm[0], w_proj, gq_slots, gks2, gkw2, cos_p, sin_p, n_batch=bp, per_row_mod=False,
        pos_minor=True)

    kc_p, vc_p = _compress(cmp_t.reshape(bp, 2, N_KV, HEAD_DIM, seq), None, jnp.zeros((bp, 1, KV_ROW), F32),
                           cmp_weights, n_batch=bp, n_pages=seq // PAGE_SIZE)
    ocmp_p, bias_p = _nsa_select(q_p, kc_p, vc_p, n_batch=bp, seq=seq)
    ya_p = _nsa_prompt(q_p, ocmp_p, bias_p, slcb_p, winb_p, br_p, ga_p, n_batch=bp, seq=seq)
    oret_p, sret_p = _retention_prompt(rq_p, rk_p, rv_p, n_batch=bp, seq=seq)
    y_p = _output(xp2, gate_p, ya_p, None, oret_p.reshape(bp * seq, D_RET), gr_p, g_ret[0], w_out[0],
                  n_batch=bp, per_row_mod=False)

    xs2 = x_sample.reshape(bs, D_MODEL)
    cos_s, sin_s = _rope_tables(bs, past, 0)
    (q_s, cmp_s, slc_s, _, win_s, _, br_s, ga_s, rq_s, rk_s, rv_s, gr_s) = _projection(
        xs2, scale_s, shift_s, g_norm[0], w_proj, gq_slots, gks2, gkw2, cos_s, sin_s, n_batch=1, per_row_mod=True,
        pos_minor=False)

    n_chunk_cache = past // CMP_STRIDE
    n_cmp_s = n_chunk_cache
    n_sel_s = past // SEL_BLOCK + 1
    nsp = ((n_sel_s + LANES - 1) // LANES) * LANES
    to_native = (0, 2, 3, 4, 1)
    kc_s, vc_s = _compress(jnp.transpose(cache_cmp[0], to_native), page_table, cmp_s.reshape(bs, 1, KV_ROW),
                           cmp_weights, n_batch=bs, n_pages=n_pages)

    q16 = jnp.pad(q_s[0].transpose(1, 0, 2), ((0, 0), (0, 16 - N_HEADS), (0, 0)))
    mimp_s = _importance_matrix(n_cmp_s, n_cmp_s, n_sel_s, nsp)
    o_cmp_s, o_win_s, idx_s = _nsa_step1(q16, kc_s, vc_s, jnp.transpose(state_win[0], to_native),
                                         win_s.reshape(bs, 1, KV_ROW), mimp_s, t=past, n_cmp=n_cmp_s,
                                         n_sel=n_sel_s)
    k_eff = min(TOP_N, n_sel_s)
    idx_flat = idx_s[:, :N_KV, :k_eff].reshape(bs, N_KV * k_eff)
    o_nsa_s = _nsa_step2(idx_flat, page_table, q16, jnp.transpose(cache_slc[0], to_native),
                         slc_s.reshape(bs, 1, KV_ROW), o_cmp_s, o_win_s, br_s.reshape(bs, 1, LANES),
                         n_cache_blocks=past // SEL_BLOCK, k_eff=k_eff)

    oret_s, sret_s = _retention_step(rq_s, rk_s, rv_s, state_ret[0])
    y_s = _output(xs2, gate_s, o_nsa_s.reshape(bs, D_NSA), ga_s, oret_s.reshape(bs, D_RET), gr_s, g_ret[0], w_out[0],
                  n_batch=1, per_row_mod=True)

    def kv_rows(a, b, l):
        return a.reshape(1, b, l, 2, N_KV, HEAD_DIM)

    def kv_rows_t(a, b, l):
        return jnp.transpose(a.reshape(b, 2, N_KV, HEAD_DIM, l), (0, 4, 1, 2, 3))[None]

    keep_p = min(WINDOW, seq)
    s_win = jnp.concatenate([state_win[0][:, 1:].reshape(bs, WINDOW - 1, KV_ROW), win_s.reshape(bs, 1, KV_ROW)],
                            axis=1)
    return (y_p.reshape(bp, seq, D_MODEL), y_s.reshape(bs, 1, D_MODEL),
            kv_rows_t(cmp_t, bp, seq), kv_rows_t(slc_t, bp, seq),
            kv_rows_t(win_t[:, :, seq - keep_p:], bp, keep_p), sret_p[None],
            kv_rows(cmp_s, bs, 1), kv_rows(slc_s, bs, 1), kv_rows(s_win, bs, WINDOW), sret_s[None])
```

```python
import functools
import math

import jax
import jax.numpy as jnp
import numpy as np
from jax import lax
from jax.experimental import pallas as pl
from jax.experimental.pallas import tpu as pltpu

D_MODEL = 1024
HEAD_DIM = 64
D_NSA = 512
N_HEADS = 8
N_KV = 2
HPG = 4
CMP_BLOCK = 32
CMP_STRIDE = 16
SEL_BLOCK = 64
TOP_N = 16
WINDOW = 512
D_RET = 512
N_RET_HEADS = 4
DV_RET = 128
DK_RET = 64
RET_CHUNK = 128
PAGE_SIZE = 128
ROPE_BASE = 10000.0
EPS = 1e-6
NEG = -1e30
FORCE = 1e4
BELOW_NEG = -2e38
TAKEN = -3e38

KV_ROW = 2 * N_KV * HEAD_DIM
LANES = 128
VMEM_LIMIT_BYTES = 56 * 1024 * 1024

F32 = jnp.float32
BF16 = jnp.bfloat16

O_Q = 0
O_CMP = O_Q + D_NSA
O_SLC = O_CMP + KV_ROW
O_WIN = O_SLC + KV_ROW
O_BR = O_WIN + KV_ROW
O_GA = O_BR + LANES
O_RQ = O_GA + D_NSA
O_RK = O_RQ + N_RET_HEADS * DK_RET
O_RV = O_RK + N_RET_HEADS * DK_RET
O_GR = O_RV + D_RET
D_IN_PAD = O_GR + D_RET


def _cparams(*sem):
    return pltpu.CompilerParams(dimension_semantics=sem, vmem_limit_bytes=VMEM_LIMIT_BYTES)


def _dot(a, b):
    return jnp.dot(a, b, preferred_element_type=F32)


def _dot_nt(a, b):
    return lax.dot_general(a, b, (((1,), (1,)), ((), ())), preferred_element_type=F32)


def _dot_tn(a, b):
    return lax.dot_general(a, b, (((0,), (0,)), ((), ())), preferred_element_type=F32)


def _silu(x):
    return x * jax.nn.sigmoid(x)


def _split3_dot(p, m):
    p1 = p.astype(BF16)
    r1 = p - p1.astype(F32)
    p2 = r1.astype(BF16)
    p3 = (r1 - p2.astype(F32)).astype(BF16)
    return _dot(p1, m) + _dot(p2, m) + _dot(p3, m)


def _pair_rmsnorm(k, gain2):
    lo = lax.broadcasted_iota(jnp.int32, k.shape, 1) < HEAD_DIM
    sq = k * k
    s0 = jnp.sum(jnp.where(lo, sq, 0.0), axis=-1, keepdims=True)
    s1 = jnp.sum(jnp.where(lo, 0.0, sq), axis=-1, keepdims=True)
    ms = jnp.where(lo, s0, s1) * (1.0 / HEAD_DIM)
    return k * lax.rsqrt(ms + EPS) * gain2


def _mod_kernel(c_ref, w_ref, b_ref, o_ref):
    o_ref[...] = _dot(_silu(c_ref[...]).astype(BF16), w_ref[...]) + b_ref[...]


def _modulation(c_all, w_ada, b_ada):
    rows = c_all.shape[0]
    n = w_ada.shape[1]
    tn = 1024
    return pl.pallas_call(
        _mod_kernel,
        grid=(n // tn,),
        in_specs=[pl.BlockSpec((rows, D_MODEL), lambda j: (0, 0)),
                  pl.BlockSpec((D_MODEL, tn), lambda j: (0, j)),
                  pl.BlockSpec((1, tn), lambda j: (0, j))],
        out_specs=pl.BlockSpec((rows, tn), lambda j: (0, j)),
        out_shape=jax.ShapeDtypeStruct((rows, n), F32),
        compiler_params=_cparams("arbitrary"),
        name="adaln_modulation",
    )(c_all, w_ada.astype(BF16), b_ada.reshape(1, n))


def _rope_kernel(freq_ref, cos_ref, sin_ref, *, base, stride, rows):
    shape = (rows, LANES)
    r = lax.broadcasted_iota(jnp.int32, shape, 0)
    pos = (base + stride * (pl.program_id(0) * rows + r)).astype(F32)
    ang = pos * freq_ref[...]
    lane = lax.broadcasted_iota(jnp.int32, shape, 1)
    sign = jnp.where((lane & (DK_RET - 1)) < DK_RET // 2, -1.0, 1.0)
    cos_ref[...] = jnp.cos(ang)
    sin_ref[...] = jnp.sin(ang) * sign


def _rope_tables(n_rows, base, stride):
    half = DK_RET // 2
    freqs = ROPE_BASE ** (-jnp.arange(half, dtype=F32) / half)
    freq_row = jnp.tile(freqs, LANES // half).reshape(1, LANES)
    rows = min(n_rows, 512)
    return pl.pallas_call(
        functools.partial(_rope_kernel, base=base, stride=stride, rows=rows),
        grid=(n_rows // rows,),
        in_specs=[pl.BlockSpec((1, LANES), lambda i: (0, 0))],
        out_specs=[pl.BlockSpec((rows, LANES), lambda i: (i, 0))] * 2,
        out_shape=[jax.ShapeDtypeStruct((n_rows, LANES), F32)] * 2,
        compiler_params=_cparams("arbitrary"),
        name="rope_tables",
    )(freq_row)


def _proj_kernel(x_ref, scale_ref, shift_ref, gn_ref, w_ref, gq_ref, gks_ref, gkw_ref, cos_ref, sin_ref,
                 q_ref, cmp_ref, slc_ref, slcb_ref, win_ref, winb_ref, br_ref, ga_ref, rq_ref, rk_ref,
                 rv_ref, gr_ref, *, pos_minor):
    x = x_ref[...]
    ms = jnp.mean(x * x, axis=-1, keepdims=True)
    xn = x * lax.rsqrt(ms + EPS) * gn_ref[...]
    hb = (xn * (1.0 + scale_ref[0]) + shift_ref[0]).astype(BF16)

    def seg(start, width):
        return _dot(hb, w_ref[:, start:start + width])

    yq = seg(O_Q, D_NSA)
    lane_lo = lax.broadcasted_iota(jnp.int32, (yq.shape[0], LANES), 1) < HEAD_DIM
    for s in range(N_HEADS):
        t = yq[:, (s // 2) * LANES:(s // 2 + 1) * LANES]
        if s % 2 != s // HPG:
            t = pltpu.roll(t, HEAD_DIM, axis=1)
        t = jnp.where(lane_lo == (s < HPG), t, 0.0)
        ms = jnp.sum(t * t, axis=-1, keepdims=True) * (1.0 / HEAD_DIM)
        q_ref[0, s] = (t * lax.rsqrt(ms + EPS) * gq_ref[s]).astype(BF16)

    y_cmp = seg(O_CMP, KV_ROW)
    y = seg(O_SLC, KV_ROW)
    y_slc = jnp.concatenate([_pair_rmsnorm(y[:, :LANES], gks_ref[...]), y[:, LANES:]], axis=-1)
    slcb_ref[...] = y_slc.astype(BF16)
    y = seg(O_WIN, KV_ROW)
    y_win = jnp.concatenate([_pair_rmsnorm(y[:, :LANES], gkw_ref[...]), y[:, LANES:]], axis=-1)
    winb_ref[...] = y_win.astype(BF16)
    for ref, rows in ((cmp_ref, y_cmp), (slc_ref, y_slc), (win_ref, y_win)):
        if pos_minor:
            ref[0] = rows.T
        else:
            ref[...] = rows

    br_ref[...] = jax.nn.sigmoid(seg(O_BR, LANES))
    ga_ref[...] = _silu(seg(O_GA, D_NSA))

    first_half = (lax.broadcasted_iota(jnp.int32, (yq.shape[0], LANES), 1) & (DK_RET - 1)) < DK_RET // 2

    def rotary(y):
        tiles = []
        for k in range(y.shape[1] // LANES):
            t = y[:, k * LANES:(k + 1) * LANES]
            swapped = jnp.where(first_half, pltpu.roll(t, LANES - DK_RET // 2, axis=1),
                                pltpu.roll(t, DK_RET // 2, axis=1))
            tiles.append(t * cos_ref[...] + swapped * sin_ref[...])
        return jnp.concatenate(tiles, axis=-1)

    nrot = N_RET_HEADS * DK_RET
    rq_ref[...] = rotary(seg(O_RQ, nrot)).astype(BF16)
    rk_ref[...] = (rotary(seg(O_RK, nrot)) * DK_RET ** -0.5).astype(BF16)
    rv_ref[...] = seg(O_RV, D_RET).astype(BF16)
    gr_ref[...] = _silu(seg(O_GR, D_RET))


def _prep_proj_weights(w_in, g_q, g_ks, g_kw):
    n_br = 3 * N_HEADS
    w = jnp.concatenate([w_in[:, :O_BR], jnp.pad(w_in[:, O_BR:O_BR + n_br], ((0, 0), (0, LANES - n_br))),
                         w_in[:, O_BR + n_br:]], axis=1).astype(BF16)
    assert w.shape[1] == D_IN_PAD
    gz = jnp.zeros((HEAD_DIM,), F32)
    gq = g_q * (HEAD_DIM ** -0.5 * math.log2(math.e))
    gq_slots = jnp.stack([jnp.concatenate([gq, gz]) if s < HPG else jnp.concatenate([gz, gq])
                          for s in range(N_HEADS)]).reshape(N_HEADS, 1, LANES)
    return w, gq_slots, jnp.tile(g_ks, 2).reshape(1, LANES), jnp.tile(g_kw, 2).reshape(1, LANES)


def _projection(x2, scale, shift, g_norm, w, gq_slots, gks2, gkw2, cos_t, sin_t, *, n_batch, per_row_mod,
                pos_minor):
    m = x2.shape[0]
    seq = m // n_batch
    tm = min(seq, 256)
    nt = seq // tm
    if per_row_mod:
        mod_spec = pl.BlockSpec((1, tm, D_MODEL), lambda i: (0, i, 0))
    else:
        mod_spec = pl.BlockSpec((1, 1, D_MODEL), lambda i: (i // nt, 0, 0))

    def rows_spec(width):
        return pl.BlockSpec((tm, width), lambda i: (i, 0))

    def const_spec(shape):
        return pl.BlockSpec(shape, lambda i: (0,) * len(shape))

    kv_out = "kv rows"
    widths = [(KV_ROW, kv_out), (KV_ROW, kv_out), (KV_ROW, BF16), (KV_ROW, kv_out), (KV_ROW, BF16), (LANES, F32),
              (D_NSA, F32), (N_RET_HEADS * DK_RET, BF16), (N_RET_HEADS * DK_RET, BF16), (D_RET, BF16),
              (D_RET, F32)]
    out_shape = [jax.ShapeDtypeStruct((n_batch, N_HEADS, seq, LANES), BF16)]
    out_specs = [pl.BlockSpec((1, N_HEADS, tm, LANES), lambda i: (i // nt, 0, i % nt, 0))]
    for width, dt in widths:
        if pos_minor and dt is kv_out:
            out_shape.append(jax.ShapeDtypeStruct((n_batch, KV_ROW, seq), F32))
            out_specs.append(pl.BlockSpec((1, KV_ROW, tm), lambda i: (i // nt, 0, i % nt)))
        else:
            out_shape.append(jax.ShapeDtypeStruct((m, width), F32 if dt is kv_out else dt))
            out_specs.append(rows_spec(width))
    return pl.pallas_call(
        functools.partial(_proj_kernel, pos_minor=pos_minor),
        grid=(m // tm,),
        in_specs=[rows_spec(D_MODEL), mod_spec, mod_spec, const_spec((1, D_MODEL)),
                  const_spec((D_MODEL, D_IN_PAD)), const_spec((N_HEADS, 1, LANES)), const_spec((1, LANES)),
                  const_spec((1, LANES)), pl.BlockSpec((tm, LANES), lambda i: (i % nt, 0)),
                  pl.BlockSpec((tm, LANES), lambda i: (i % nt, 0))],
        out_specs=out_specs,
        out_shape=out_shape,
        compiler_params=_cparams("arbitrary"),
        name="input_projection",
    )(x2, scale, shift, g_norm.reshape(1, D_MODEL), w, gq_slots, gks2, gkw2, cos_t, sin_t)


def _compress_kernel(pt_ref, *refs, n_in):
    x_refs, next_ref = refs[:n_in], refs[n_in]
    new_ref, perm_ref, w1_ref, pe_ref, w2_ref, gkc_ref, kc_ref, vc_ref = refs[n_in + 1:]
    cpp = PAGE_SIZE // CMP_STRIDE
    n = n_in * cpp
    last_step = pl.program_id(1) == pl.num_programs(1) - 1
    perm = perm_ref[...]

    def chunk_rows(page_ref):
        return _dot_nt(perm, page_ref[0].reshape(KV_ROW, PAGE_SIZE).astype(BF16))

    def flat_chunks(moved, kv):
        return jnp.concatenate(
            [jnp.concatenate([r[cpp * l:cpp * (l + 1), kv * LANES:(kv + 1) * LANES] for l in range(CMP_STRIDE)],
                             axis=1) for r in moved], axis=0).astype(BF16)

    moved = [chunk_rows(r) for r in x_refs]
    moved_next = [chunk_rows(next_ref)]
    new8 = jnp.broadcast_to(new_ref[0], (8, KV_ROW)).astype(BF16)
    zeros = jnp.zeros((8, (CMP_STRIDE - 1) * LANES), BF16)
    row = lax.broadcasted_iota(jnp.int32, (n, LANES), 0)
    pre = []
    for kv in range(2):
        new_chunk = jnp.concatenate([new8[:, kv * LANES:(kv + 1) * LANES], zeros], axis=1)
        a_all = _dot(jnp.concatenate([flat_chunks(moved, kv), flat_chunks(moved_next, kv), pe_ref[kv], new_chunk],
                                     axis=0), w1_ref[kv])
        a = a_all[:n]
        a_next = a_all[n:n + 1, LANES:]
        bias = a_all[n + cpp:n + cpp + 1, :LANES] + a_all[n + cpp + 1:n + cpp + 2, LANES:]
        a_new = a_all[n + cpp + 8:n + cpp + 9, LANES:]
        hi = pltpu.roll(a[:, LANES:], n - 1, axis=0)
        hi = jnp.where(row == n - 1, jnp.where(last_step, a_new, a_next), hi)
        pre.append(a[:, :LANES] + hi + bias)
    out = _dot(_silu(jnp.concatenate(pre, axis=1)).astype(BF16), w2_ref[...])
    kc_ref[0] = _pair_rmsnorm(out[:, :LANES], gkc_ref[...]).astype(BF16)
    vc_ref[0] = out[:, LANES:].astype(BF16)


def _compress(rows_t, page_table, new_rows, weights, *, n_batch, n_pages):
    w1kv, w2x, pe8kv, gkc2 = weights
    b = n_batch
    cpp = PAGE_SIZE // CMP_STRIDE
    p = math.gcd(n_pages, 64)
    row = jnp.arange(PAGE_SIZE, dtype=jnp.int32)
    perm = ((row % cpp) * CMP_STRIDE + row // cpp).reshape(PAGE_SIZE, 1) == row.reshape(1, PAGE_SIZE)
    page_shape = (1,) + rows_t.shape[1:4] + (PAGE_SIZE,)

    def page_spec(page_of_step):
        if page_table is None:
            return pl.BlockSpec(page_shape, lambda i, j, pt: (i, 0, 0, 0, page_of_step(j)))
        return pl.BlockSpec(page_shape, lambda i, j, pt: (pt[i, page_of_step(j)], 0, 0, 0, 0))

    in_specs = [page_spec(lambda j, k=k: j * p + k) for k in range(p)]
    in_specs.append(page_spec(lambda j: jnp.minimum((j + 1) * p, n_pages - 1)))

    def const_spec(shape):
        return pl.BlockSpec(shape, lambda i, j, pt: (0,) * len(shape))

    in_specs += [pl.BlockSpec((1, 1, KV_ROW), lambda i, j, pt: (i, 0, 0)), const_spec((PAGE_SIZE, PAGE_SIZE)),
                 const_spec(w1kv.shape), const_spec(pe8kv.shape), const_spec(w2x.shape), const_spec((1, LANES))]
    if page_table is None:
        page_table = jnp.zeros((1, 1), jnp.int32)
    return pl.pallas_call(
        functools.partial(_compress_kernel, n_in=p),
        grid_spec=pltpu.PrefetchScalarGridSpec(
            num_scalar_prefetch=1,
            grid=(b, n_pages // p),
            in_specs=in_specs,
            out_specs=[pl.BlockSpec((1, p * cpp, LANES), lambda i, j, pt: (i, j, 0))] * 2,
        ),
        out_shape=[jax.ShapeDtypeStruct((b, n_pages * cpp, LANES), BF16)] * 2,
        compiler_params=_cparams("arbitrary", "arbitrary"),
        name="compress",
    )(page_table, *([rows_t] * (p + 1)), new_rows, perm.astype(BF16), w1kv, pe8kv, w2x, gkc2)


def _prep_compress_weights(pe_ck, w_ck1, w_ck2, pe_cv, w_cv1, w_cv2, g_kc):
    eye = jnp.eye(N_KV, dtype=F32)
    w1 = jnp.stack([w_ck1, w_cv1]).reshape(2, 2, CMP_STRIDE, HEAD_DIM, HEAD_DIM)
    w1kv = jnp.einsum('khldf,gp->klgdhpf', w1, eye).reshape(2, CMP_STRIDE * LANES, KV_ROW).astype(BF16)
    w2 = jnp.stack([w_ck2, w_cv2])
    w2x = jnp.einsum('kfd,kq,gp->kgfqpd', w2, eye, eye).reshape(KV_ROW, KV_ROW).astype(BF16)
    pe = jnp.stack([pe_ck, pe_cv]).reshape(2, 2, CMP_STRIDE, 1, HEAD_DIM)
    pe_rows = jnp.broadcast_to(pe, (2, 2, CMP_STRIDE, N_KV, HEAD_DIM)).reshape(2, 2, CMP_STRIDE * LANES)
    pe8kv = jnp.zeros((2, 8, CMP_STRIDE * LANES), F32).at[:, :2].set(pe_rows).astype(BF16)
    return w1kv, w2x, pe8kv, jnp.tile(g_kc, 2).reshape(1, LANES)


def _importance_matrix(n_cmp_pad, n_cmp, n_sel, n_sel_pad):
    ratio = SEL_BLOCK // CMP_STRIDE
    span = CMP_BLOCK // CMP_STRIDE
    m = np.zeros((n_cmp_pad, n_sel_pad), np.float32)
    for j in range(n_sel):
        for r in range(ratio):
            for s in range(span):
                n = ratio * j + r - s
                if 0 <= n < n_cmp:
                    m[n, j] += 1.0
    return jnp.asarray(m, BF16)


def _nsa_select_kernel(q_ref, kc_ref, vc_ref, mimp_ref, oc_ref, bias_ref, *, tq, n_cmp, n_sel):
    qs = pl.program_id(1) * tq
    rows = HPG * tq
    ncp = kc_ref.shape[1]

    def q_pos(width):
        return qs + lax.broadcasted_iota(jnp.int32, (tq, width), 0)

    def per_head(x):
        return jnp.concatenate([x] * HPG, axis=0)

    col = lax.broadcasted_iota(jnp.int32, (tq, ncp), 1)
    cmp_mask = per_head(jnp.where((col * CMP_STRIDE + (CMP_BLOCK - 1) <= q_pos(ncp)) & (col < n_cmp), 0.0, NEG))
    cmp_any = per_head(jnp.where(q_pos(1) >= CMP_BLOCK - 1, 1.0, 0.0))
    blk = lax.broadcasted_iota(jnp.int32, (tq, LANES), 1)
    t = q_pos(LANES)
    jt = t // SEL_BLOCK
    forced = (blk == 0) | (blk == jt) | (blk == jt - 1)
    valid = blk * SEL_BLOCK <= t
    n_rank = ((n_sel + 7) // 8) * 8
    j_row = lax.broadcasted_iota(jnp.int32, (n_rank, tq), 0)

    for g in range(N_KV):
        q = q_ref[0, HPG * g:HPG * (g + 1)].reshape(rows, LANES)
        s = _dot_nt(q, kc_ref[0]) + cmp_mask
        e = jnp.exp2(s - jnp.max(s, axis=-1, keepdims=True))
        p = e * (cmp_any / jnp.sum(e, axis=-1, keepdims=True))
        oc_ref[0, HPG * g:HPG * (g + 1)] = _dot(p.astype(BF16), vc_ref[0]).astype(BF16).reshape(HPG, tq, LANES)
        p_sum = p[0:tq] + p[tq:2 * tq] + p[2 * tq:3 * tq] + p[3 * tq:4 * tq]
        imp = _split3_dot(p_sum, mimp_ref[...])

        score = jnp.where(valid, jnp.where(forced, FORCE, imp), NEG)
        score = jnp.where(blk < n_sel, score, BELOW_NEG)
        left = score.T[:n_rank]
        sel_t = jnp.zeros((n_rank, tq), F32)
        for _ in range(min(TOP_N, n_sel)):
            top = jnp.max(left, axis=0, keepdims=True)
            first = jnp.min(jnp.where(left == top, j_row, n_rank), axis=0, keepdims=True)
            taken = j_row == first
            sel_t = jnp.where(taken, 1.0, sel_t)
            left = jnp.where(taken, TAKEN, left)
        if n_rank < LANES:
            sel_t = jnp.concatenate([sel_t, jnp.zeros((LANES - n_rank, tq), F32)], axis=0)
        bias_ref[0, g] = jnp.where(sel_t.T > 0.5, 0.0, NEG).astype(BF16)


def _nsa_select(q_slots, kc, vc, *, n_batch, seq):
    tq = min(512, seq)
    n_cmp = seq // CMP_STRIDE - (CMP_BLOCK // CMP_STRIDE) + 1
    n_sel = seq // SEL_BLOCK
    assert n_sel <= LANES
    ncp = kc.shape[1]
    mimp = _importance_matrix(ncp, n_cmp, n_sel, LANES)
    return pl.pallas_call(
        functools.partial(_nsa_select_kernel, tq=tq, n_cmp=n_cmp, n_sel=n_sel),
        grid=(n_batch, seq // tq),
        in_specs=[pl.BlockSpec((1, N_HEADS, tq, LANES), lambda b, i: (b, 0, i, 0)),
                  pl.BlockSpec((1, ncp, LANES), lambda b, i: (b, 0, 0)),
                  pl.BlockSpec((1, ncp, LANES), lambda b, i: (b, 0, 0)),
                  pl.BlockSpec(mimp.shape, lambda b, i: (0, 0))],
        out_specs=[pl.BlockSpec((1, N_HEADS, tq, LANES), lambda b, i: (b, 0, i, 0)),
                   pl.BlockSpec((1, N_KV, tq, LANES), lambda b, i: (b, 0, i, 0))],
        out_shape=[jax.ShapeDtypeStruct((n_batch, N_HEADS, seq, LANES), BF16),
                   jax.ShapeDtypeStruct((n_batch, N_KV, seq, LANES), BF16)],
        compiler_params=_cparams("arbitrary", "arbitrary"),
        name="nsa_select",
    )(q_slots, kc, vc, mimp)


def _nsa_prompt_kernel(q_ref, oc_ref, bias_ref, slc_ref, win_ref, br_ref, ga_ref, expt_ref, o_ref,
                       m_sc, acc_sc, *, tq, tk):
    qs = pl.program_id(1) * tq
    rows = HPG * tq
    wlen = WINDOW + tq
    br = br_ref[...]
    lane_lo = lax.broadcasted_iota(jnp.int32, (tq, LANES), 1) < HEAD_DIM

    def q_pos(width):
        return qs + lax.broadcasted_iota(jnp.int32, (tq, width), 0)

    def per_head(x):
        return jnp.concatenate([x] * HPG, axis=0)

    w0 = pl.multiple_of(jnp.maximum(qs - WINDOW, 0), tq)
    kpos = w0 + lax.broadcasted_iota(jnp.int32, (tq, wlen), 1)
    win_mask = per_head(jnp.where((kpos <= q_pos(wlen)) & (kpos > q_pos(wlen) - WINDOW), 0.0, NEG))
    kt_diag = qs // tk
    kpos = kt_diag * tk + lax.broadcasted_iota(jnp.int32, (tq, tk), 1)
    diag_mask = per_head(jnp.where(kpos <= q_pos(tk), 0.0, NEG))

    def with_ones(v, g):
        own = (lax.broadcasted_iota(jnp.int32, v.shape, 1) // HEAD_DIM) == g
        return jnp.where(own, v, jnp.ones_like(v))

    def normalised(acc):
        return acc * (1.0 / pltpu.roll(acc, HEAD_DIM, axis=1))

    groups = range(N_KV)
    q = [q_ref[0, HPG * g:HPG * (g + 1)].reshape(rows, LANES) for g in groups]
    q_ext = jnp.concatenate([jnp.concatenate([q[g], per_head(bias_ref[0, g])], axis=1) for g in groups], axis=0)

    m_sc[...] = jnp.full((N_KV * rows, LANES), NEG, F32)
    acc_sc[...] = jnp.zeros((N_KV * rows, LANES), F32)

    def key_tile(kt, causal):
        k0 = pl.multiple_of(kt * tk, tk)
        k_ext = jnp.concatenate([slc_ref[0, pl.ds(k0, tk), 0:LANES], expt_ref[kt]], axis=1)
        v = slc_ref[0, pl.ds(k0, tk), LANES:2 * LANES]
        s_all = _dot_nt(q_ext, k_ext)
        for g in groups:
            rs = slice(g * rows, (g + 1) * rows)
            s = s_all[rs]
            if causal:
                s = s + diag_mask
            m_prev = m_sc[rs]
            m_next = jnp.maximum(m_prev, jnp.max(s, axis=-1, keepdims=True))
            p = jnp.exp2(s - jnp.concatenate([m_next] * (tk // LANES), axis=1))
            acc_sc[rs] = jnp.exp2(m_prev - m_next) * acc_sc[rs] + _dot(p.astype(BF16), with_ones(v, g))
            m_sc[rs] = m_next

    def full_tile(kt, carry):
        key_tile(kt, False)
        return carry

    lax.fori_loop(0, kt_diag, full_tile, 0)
    key_tile(kt_diag, True)

    k = win_ref[0, pl.ds(w0, wlen), 0:LANES]
    v = win_ref[0, pl.ds(w0, wlen), LANES:2 * LANES]
    s_all = _dot_nt(jnp.concatenate(q, axis=0), k)
    for g in groups:
        rs = slice(g * rows, (g + 1) * rows)
        s = s_all[rs] + win_mask
        e = jnp.exp2(s - jnp.max(s, axis=-1, keepdims=True))
        o_win = normalised(_dot(e.astype(BF16), with_ones(v, g)))
        o_slc = normalised(acc_sc[rs])

        heads = []
        for h in range(HPG):
            hs = slice(h * tq, (h + 1) * tq)
            c = 3 * (HPG * g + h)
            heads.append(br[:, c:c + 1] * oc_ref[0, HPG * g + h].astype(F32) + br[:, c + 1:c + 2] * o_slc[hs]
                         + br[:, c + 2:c + 3] * o_win[hs])
        for hh in range(HPG // 2):
            even, odd = heads[2 * hh], heads[2 * hh + 1]
            if g == 0:
                odd = pltpu.roll(odd, HEAD_DIM, axis=1)
            else:
                even = pltpu.roll(even, HEAD_DIM, axis=1)
            c0 = (2 * g + hh) * LANES
            o_ref[:, c0:c0 + LANES] = (jnp.where(lane_lo, even, odd) * ga_ref[:, c0:c0 + LANES]).astype(BF16)


def _nsa_prompt(q_slots, o_cmp, block_bias, slc_b, win_b, br, ga, *, n_batch, seq):
    tq = min(256, seq)
    tk = min(512, seq)
    assert seq % tk == 0 and tk % tq == 0 and seq >= WINDOW + tq
    key_blk = (jnp.arange(seq, dtype=jnp.int32) // SEL_BLOCK).reshape(seq // tk, tk, 1)
    expand = (key_blk == jnp.arange(LANES, dtype=jnp.int32).reshape(1, 1, LANES)).astype(BF16)
    nq = seq // tq
    rows = HPG * tq
    return pl.pallas_call(
        functools.partial(_nsa_prompt_kernel, tq=tq, tk=tk),
        grid=(n_batch, nq),
        in_specs=[pl.BlockSpec((1, N_HEADS, tq, LANES), lambda b, i: (b, 0, i, 0)),
                  pl.BlockSpec((1, N_HEADS, tq, LANES), lambda b, i: (b, 0, i, 0)),
                  pl.BlockSpec((1, N_KV, tq, LANES), lambda b, i: (b, 0, i, 0)),
                  pl.BlockSpec((1, seq, KV_ROW), lambda b, i: (b, 0, 0)),
                  pl.BlockSpec((1, seq, KV_ROW), lambda b, i: (b, 0, 0)),
                  pl.BlockSpec((tq, LANES), lambda b, i: (b * nq + i, 0)),
                  pl.BlockSpec((tq, D_NSA), lambda b, i: (b * nq + i, 0)),
                  pl.BlockSpec(expand.shape, lambda b, i: (0, 0, 0))],
        out_specs=pl.BlockSpec((tq, D_NSA), lambda b, i: (b * nq + i, 0)),
        out_shape=jax.ShapeDtypeStruct((n_batch * seq, D_NSA), BF16),
        scratch_shapes=[pltpu.VMEM((N_KV * rows, LANES), F32)] * 2,
        compiler_params=_cparams("arbitrary", "arbitrary"),
        name="nsa_prompt",
    )(q_slots, o_cmp, block_bias, slc_b.reshape(n_batch, seq, KV_ROW), win_b.reshape(n_batch, seq, KV_ROW), br, ga,
      expand)


def _retention_prompt_kernel(rq_ref, rk_ref, rv_ref, o_ref, s_ref, *, cl):
    @pl.when(pl.program_id(0) == 0)
    def _():
        s_ref[...] = jnp.zeros(s_ref.shape, F32)

    i_pos = lax.broadcasted_iota(jnp.int32, (cl, cl), 0)
    j_pos = lax.broadcasted_iota(jnp.int32, (cl, cl), 1)
    diff = (i_pos - j_pos).astype(F32)
    i_col = lax.broadcasted_iota(jnp.int32, (cl, 1), 0).astype(F32)
    lane_lo = lax.broadcasted_iota(jnp.int32, (cl, LANES), 1) < DK_RET
    for h in range(N_RET_HEADS):
        pair, odd = divmod(h, 2)
        log_g = math.log(1.0 - 2.0 ** (-5.0 - h))
        keep = lane_lo if odd == 0 else jnp.logical_not(lane_lo)
        decay = jnp.where(diff >= 0, jnp.exp(log_g * jnp.maximum(diff, 0.0)), 0.0)
        xi = jnp.exp(log_g * (i_col + 1.0))
        zeta = jnp.exp(log_g * (cl - 1.0 - i_col))
        for b in range(rq_ref.shape[0]):
            q2 = rq_ref[b, :, pair * LANES:(pair + 1) * LANES]
            k2 = rk_ref[b, :, pair * LANES:(pair + 1) * LANES]
            s_pair = jnp.concatenate([s_ref[b, 2 * pair], s_ref[b, 2 * pair + 1]], axis=0)
            qh = jnp.where(keep, q2, jnp.zeros_like(q2))
            v = rv_ref[b, :, h * DV_RET:(h + 1) * DV_RET]
            a = _dot_nt(qh, k2) * decay
            o_ref[b, :, h * DV_RET:(h + 1) * DV_RET] = (_dot(a.astype(BF16), v)
                                                         + _dot(qh, s_pair.astype(BF16)) * xi)
            kz = (k2.astype(F32) * zeta).astype(BF16)
            u = _dot_tn(kz, v)[odd * DK_RET:(odd + 1) * DK_RET]
            s_ref[b, h] = s_ref[b, h] * math.exp(log_g * cl) + u


def _retention_prompt(rq, rk, rv, *, n_batch, seq):
    cl = math.gcd(seq, 2 * RET_CHUNK)
    nrot = N_RET_HEADS * DK_RET
    return pl.pallas_call(
        functools.partial(_retention_prompt_kernel, cl=cl),
        grid=(seq // cl,),
        in_specs=[pl.BlockSpec((n_batch, cl, nrot), lambda i: (0, i, 0)),
                  pl.BlockSpec((n_batch, cl, nrot), lambda i: (0, i, 0)),
                  pl.BlockSpec((n_batch, cl, D_RET), lambda i: (0, i, 0))],
        out_specs=[pl.BlockSpec((n_batch, cl, D_RET), lambda i: (0, i, 0)),
                   pl.BlockSpec((n_batch, N_RET_HEADS, DK_RET, DV_RET), lambda i: (0, 0, 0, 0))],
        out_shape=[jax.ShapeDtypeStruct((n_batch, seq, D_RET), F32),
                   jax.ShapeDtypeStruct((n_batch, N_RET_HEADS, DK_RET, DV_RET), F32)],
        compiler_params=_cparams("arbitrary"),
        name="retention_prompt",
    )(rq.reshape(n_batch, seq, nrot), rk.reshape(n_batch, seq, nrot), rv.reshape(n_batch, seq, D_RET))


def _retention_step_kernel(rq_ref, rk_ref, rv_ref, s0_ref, o_ref, s_ref):
    lane_lo = lax.broadcasted_iota(jnp.int32, (8, LANES), 1) < DK_RET
    diag = (lax.broadcasted_iota(jnp.int32, (LANES, LANES), 0)
            == lax.broadcasted_iota(jnp.int32, (LANES, LANES), 1))
    for pair in range(N_RET_HEADS // 2):
        q2 = jnp.broadcast_to(rq_ref[0, :, pair * LANES:(pair + 1) * LANES], (8, LANES))
        k_row = rk_ref[0, :, pair * LANES:(pair + 1) * LANES]
        k2 = jnp.broadcast_to(k_row, (8, LANES))
        k_col = jnp.sum(jnp.where(diag, jnp.broadcast_to(k_row, (LANES, LANES)), 0.0), axis=-1, keepdims=True)
        s_pair = jnp.concatenate([s0_ref[0, 2 * pair], s0_ref[0, 2 * pair + 1]], axis=0)
        for odd in range(2):
            h = 2 * pair + odd
            g = math.exp(math.log(1.0 - 2.0 ** (-5.0 - h)))
            keep = lane_lo if odd == 0 else jnp.logical_not(lane_lo)
            qh = jnp.where(keep, q2, 0.0)
            v_row = rv_ref[0, :, h * DV_RET:(h + 1) * DV_RET]
            qk = jnp.sum(qh * k2, axis=-1, keepdims=True)
            o = qk * v_row + _dot(qh, s_pair) * g
            o_ref[0, :, h * DV_RET:(h + 1) * DV_RET] = o[0:1]
            u = (k_col * v_row)[odd * DK_RET:(odd + 1) * DK_RET]
            s_ref[0, h] = s0_ref[0, h] * g + u


def _retention_step(rq, rk, rv, s0):
    b = rq.shape[0]
    nrot = N_RET_HEADS * DK_RET
    state_spec = pl.BlockSpec((1, N_RET_HEADS, DK_RET, DV_RET), lambda i: (i, 0, 0, 0))
    return pl.pallas_call(
        _retention_step_kernel,
        grid=(b,),
        in_specs=[pl.BlockSpec((1, 1, nrot), lambda i: (i, 0, 0)),
                  pl.BlockSpec((1, 1, nrot), lambda i: (i, 0, 0)),
                  pl.BlockSpec((1, 1, D_RET), lambda i: (i, 0, 0)),
                  state_spec],
        out_specs=[pl.BlockSpec((1, 1, D_RET), lambda i: (i, 0, 0)), state_spec],
        out_shape=[jax.ShapeDtypeStruct((b, 1, D_RET), F32),
                   jax.ShapeDtypeStruct((b, N_RET_HEADS, DK_RET, DV_RET), F32)],
        compiler_params=_cparams("arbitrary"),
        name="retention_step",
    )(rq.astype(F32).reshape(b, 1, nrot), rk.astype(F32).reshape(b, 1, nrot), rv.astype(F32).reshape(b, 1, D_RET),
      s0)


def _nsa_step1_kernel(q_ref, kc_ref, vc_ref, win_ref, new_ref, mimp_ref, oc_ref, ow_ref, idx_ref, *,
                      t, n_cmp, n_sel):
    q = q_ref[0]
    nq = q.shape[0]
    ncp = kc_ref.shape[1]
    nsp = mimp_ref.shape[1]

    s = _dot_nt(q, kc_ref[0])
    col = lax.broadcasted_iota(jnp.int32, (nq, ncp), 1)
    mask = (col * CMP_STRIDE + (CMP_BLOCK - 1) <= t) & (col < n_cmp)
    s = jnp.where(mask, s, NEG)
    e = jnp.exp2(s - jnp.max(s, axis=-1, keepdims=True))
    p = jnp.where(mask, e, 0.0) * (1.0 / jnp.sum(e, axis=-1, keepdims=True))
    oc_ref[0] = _dot(p.astype(BF16), vc_ref[0])
    imp_rows = _split3_dot(p, mimp_ref[...])

    blk = lax.broadcasted_iota(jnp.int32, (1, nsp), 1)
    jt = t // SEL_BLOCK
    forced = (blk == 0) | (blk == jt) | (blk == jt - 1)
    scores = []
    for g in range(N_KV):
        imp = jnp.sum(imp_rows[HPG * g:HPG * (g + 1)], axis=0, keepdims=True)
        sc = jnp.where(blk * SEL_BLOCK <= t, jnp.where(forced, FORCE, imp), NEG)
        scores.append(jnp.where(blk < n_sel, sc, BELOW_NEG))
    s_rows = jnp.concatenate(scores + [jnp.full((LANES - N_KV, nsp), BELOW_NEG, F32)], axis=0)
    s_cols = s_rows.T
    a_idx = lax.broadcasted_iota(jnp.int32, (nsp, nsp), 0)
    c_idx = lax.broadcasted_iota(jnp.int32, (nsp, nsp), 1)
    rank_lane = lax.broadcasted_iota(jnp.int32, (nsp, LANES), 1).astype(F32)
    a_val = lax.broadcasted_iota(jnp.int32, (nsp, LANES), 0).astype(F32)
    rows_out = []
    for g in range(N_KV):
        s_col = s_cols[:, g:g + 1]
        s_row = s_rows[g:g + 1, :]
        beats = (s_row > s_col) | ((s_row == s_col) & (c_idx < a_idx))
        rank = jnp.sum(jnp.where(beats, 1.0, 0.0), axis=-1, keepdims=True)
        rows_out.append(jnp.sum(jnp.where(rank == rank_lane, a_val, 0.0), axis=0, keepdims=True))
    idx = jnp.concatenate(rows_out + [jnp.zeros((8 - N_KV, LANES), F32)], axis=0)
    idx_ref[0] = idx.astype(jnp.int32)

    wb = win_ref.shape[-1]
    kw_t = win_ref[0, 0].reshape(LANES, wb).astype(BF16)
    vw_t = win_ref[0, 1].reshape(LANES, wb).astype(BF16)
    k_new = new_ref[0, :, 0:LANES]
    v_new = new_ref[0, :, LANES:2 * LANES]
    s = _dot(q, kw_t)
    kpos = t - wb + lax.broadcasted_iota(jnp.int32, (nq, wb), 1)
    s = jnp.where((kpos > t - WINDOW) & (kpos >= 0), s, NEG)
    s_new = jnp.sum(q.astype(F32) * k_new, axis=-1, keepdims=True)
    m = jnp.maximum(jnp.max(s, axis=-1, keepdims=True), s_new)
    e = jnp.exp2(s - m)
    e_new = jnp.exp2(s_new - m)
    ow_ref[0] = ((_dot_nt(e.astype(BF16), vw_t) + e_new * v_new)
                 * (1.0 / (jnp.sum(e, axis=-1, keepdims=True) + e_new)))


def _nsa_step1(q16, kc, vc, state_win_t, new_win, mimp, *, t, n_cmp, n_sel):
    b, nq, _ = q16.shape
    ncp = kc.shape[1]

    def per_b(shape):
        return pl.BlockSpec((1,) + shape, lambda i: (i,) + (0,) * len(shape))

    return pl.pallas_call(
        functools.partial(_nsa_step1_kernel, t=t, n_cmp=n_cmp, n_sel=n_sel),
        grid=(b,),
        in_specs=[per_b((nq, LANES)), per_b((ncp, LANES)), per_b((ncp, LANES)), per_b(state_win_t.shape[1:]),
                  per_b((1, KV_ROW)), pl.BlockSpec(mimp.shape, lambda i: (0, 0))],
        out_specs=[per_b((nq, LANES)), per_b((nq, LANES)), per_b((8, LANES))],
        out_shape=[jax.ShapeDtypeStruct((b, nq, LANES), F32), jax.ShapeDtypeStruct((b, nq, LANES), F32),
                   jax.ShapeDtypeStruct((b, 8, LANES), jnp.int32)],
        compiler_params=_cparams("arbitrary"),
        name="nsa_step_compressed_window",
    )(q16, kc, vc, state_win_t, new_win, mimp)


def _nsa_step2_kernel(idx_sm, pt_sm, q_ref, *refs, n_cache_blocks, k_eff):
    page_refs = refs[:N_KV * k_eff]
    new_ref, oc_ref, ow_ref, br_ref, o_ref = refs[N_KV * k_eff:]
    b = pl.program_id(0)
    q = q_ref[0]
    nq = q.shape[0]
    bpp = PAGE_SIZE // SEL_BLOCK
    nk = k_eff * PAGE_SIZE
    q32 = q.astype(F32)
    lane_k = lax.broadcasted_iota(jnp.int32, (nq, nk), 1)
    slot = lane_k // PAGE_SIZE
    half = (lane_k % PAGE_SIZE) // SEL_BLOCK
    outs = []
    for g in range(N_KV):
        own = slice(g * HEAD_DIM, (g + 1) * HEAD_DIM)
        q_g = q32[:, own]
        k_new = new_ref[0, :, own]
        v_new = new_ref[0, :, LANES + g * HEAD_DIM:LANES + (g + 1) * HEAD_DIM]
        s_new_all = jnp.sum(q_g * k_new, axis=-1, keepdims=True)
        pages = [page_refs[g * k_eff + k][0] for k in range(k_eff)]
        keys_t = jnp.concatenate([p[0, 0] for p in pages], axis=-1).astype(BF16)
        vals_t = jnp.concatenate([p[1, 0] for p in pages], axis=-1).astype(BF16)
        allowed = jnp.zeros((nq, nk), jnp.int32)
        new_selected = jnp.int32(0)
        for k in range(k_eff):
            j = idx_sm[b, g * k_eff + k]
            in_block = (slot == k) & (half == j % bpp)
            allowed = jnp.where(in_block, (j < n_cache_blocks).astype(jnp.int32), allowed)
            new_selected = new_selected | (j == n_cache_blocks).astype(jnp.int32)
        s = jnp.where(allowed > 0, _dot(q_g.astype(BF16), keys_t), NEG)
        s_new = jnp.where(new_selected > 0, s_new_all, NEG)
        m = jnp.maximum(jnp.max(s, axis=-1, keepdims=True), s_new)
        e = jnp.exp2(s - m)
        e_new = jnp.exp2(s_new - m)
        outs.append((_dot_nt(e.astype(BF16), vals_t) + e_new * v_new)
                    * (1.0 / (jnp.sum(e, axis=-1, keepdims=True) + e_new)))
    row = lax.broadcasted_iota(jnp.int32, (nq, LANES), 0)
    lane = lax.broadcasted_iota(jnp.int32, (nq, LANES), 1)
    o_slc = jnp.concatenate(outs, axis=1)
    br = br_ref[0]
    gates = [jnp.sum(jnp.where(lane == 3 * row + j, br, 0.0), axis=-1, keepdims=True) for j in range(3)]
    o = gates[0] * oc_ref[0] + gates[1] * o_slc + gates[2] * ow_ref[0]
    lane_lo = lax.broadcasted_iota(jnp.int32, (1, LANES), 1) < HEAD_DIM
    for pair in range(N_HEADS // 2):
        even, odd = o[2 * pair:2 * pair + 1], o[2 * pair + 1:2 * pair + 2]
        if 2 * pair < HPG:
            odd = pltpu.roll(odd, HEAD_DIM, axis=1)
        else:
            even = pltpu.roll(even, HEAD_DIM, axis=1)
        o_ref[0, :, pair * LANES:(pair + 1) * LANES] = jnp.where(lane_lo, even, odd)


def _nsa_step2(idx, page_table, q16, cache_t, new_slc, o_cmp, o_win, br, *, n_cache_blocks, k_eff):
    b, nq, _ = q16.shape
    bpp = PAGE_SIZE // SEL_BLOCK

    def page_map(i, idx_sm, pt_sm, c):
        j = jnp.minimum(idx_sm[i, c], n_cache_blocks - 1)
        return (pt_sm[i, j // bpp], 0, c // k_eff, 0, 0)

    def per_b(shape):
        return pl.BlockSpec((1,) + shape, lambda i, idx_sm, pt_sm: (i,) + (0,) * len(shape))

    n_blk = N_KV * k_eff
    in_specs = [per_b((nq, LANES))]
    slab_shape = (1, 2, 1) + cache_t.shape[3:]
    in_specs += [pl.BlockSpec(slab_shape, functools.partial(page_map, c=c)) for c in range(n_blk)]
    in_specs += [per_b((1, KV_ROW)), per_b((nq, LANES)), per_b((nq, LANES)), per_b((1, LANES))]
    return pl.pallas_call(
        functools.partial(_nsa_step2_kernel, n_cache_blocks=n_cache_blocks, k_eff=k_eff),
        grid_spec=pltpu.PrefetchScalarGridSpec(
            num_scalar_prefetch=2,
            grid=(b,),
            in_specs=in_specs,
            out_specs=per_b((1, D_NSA)),
        ),
        out_shape=jax.ShapeDtypeStruct((b, 1, D_NSA), F32),
        compiler_params=_cparams("arbitrary"),
        name="nsa_step_selected",
    )(idx, page_table, q16, *([cache_t] * n_blk), new_slc, o_cmp, o_win, br)


def _out_kernel(*refs, mul_ga):
    if mul_ga:
        x_ref, gate_ref, ya_ref, ga_ref, oret_ref, gr_ref, gret_ref, w_ref, y_ref = refs
        ya = (ya_ref[...] * ga_ref[...]).astype(BF16)
    else:
        x_ref, gate_ref, ya_ref, oret_ref, gr_ref, gret_ref, w_ref, y_ref = refs
        ya = ya_ref[...]
    o = oret_ref[...]
    parts = []
    for h in range(N_RET_HEADS):
        t = o[:, h * DV_RET:(h + 1) * DV_RET]
        ms = jnp.mean(t * t, axis=-1, keepdims=True)
        parts.append(t * lax.rsqrt(ms + EPS) * gret_ref[...])
    yr = (jnp.concatenate(parts, axis=-1) * gr_ref[...]).astype(BF16)
    y_ref[...] = x_ref[...] + gate_ref[0] * _dot(jnp.concatenate([ya, yr], axis=-1), w_ref[...])


def _output(x2, gate, ya, ga, o_ret, gr, g_ret, w_out, *, n_batch, per_row_mod):
    m = x2.shape[0]
    seq = m // n_batch
    tm = min(seq, 512)
    nt = seq // tm
    if per_row_mod:
        gate_spec = pl.BlockSpec((1, tm, D_MODEL), lambda i: (0, i, 0))
    else:
        gate_spec = pl.BlockSpec((1, 1, D_MODEL), lambda i: (i // nt, 0, 0))

    def rows_spec(width):
        return pl.BlockSpec((tm, width), lambda i: (i, 0))

    mul_ga = ga is not None
    args = [x2, gate, ya] + ([ga] if mul_ga else []) + [o_ret, gr, g_ret.reshape(1, DV_RET), w_out.astype(BF16)]
    in_specs = [rows_spec(D_MODEL), gate_spec, rows_spec(D_NSA)] + ([rows_spec(D_NSA)] if mul_ga else [])
    in_specs += [rows_spec(D_RET), rows_spec(D_RET), pl.BlockSpec((1, DV_RET), lambda i: (0, 0)),
                 pl.BlockSpec((D_MODEL, D_MODEL), lambda i: (0, 0))]
    return pl.pallas_call(
        functools.partial(_out_kernel, mul_ga=mul_ga),
        grid=(m // tm,),
        in_specs=in_specs,
        out_specs=rows_spec(D_MODEL),
        out_shape=jax.ShapeDtypeStruct((m, D_MODEL), F32),
        compiler_params=_cparams("arbitrary"),
        name="output_projection",
    )(*args)


def kernel(x_prompt, x_sample, c_prompt, c_sample, cache_cmp, cache_slc, state_win, state_ret, page_table, g_norm, w_ada, b_ada, w_in, g_q, g_kc, g_ks, g_kw, pe_ck, w_ck1, w_ck2, pe_cv, w_cv1, w_cv2, g_ret, w_out):
    assert g_norm.shape[0] == 1, "one layer"
    bp, seq, _ = x_prompt.shape
    bs, dec_seq, _ = x_sample.shape
    assert dec_seq == 1
    n_pages = page_table.shape[1]
    past = n_pages * PAGE_SIZE
    assert state_win.shape[2] == WINDOW and past >= WINDOW

    w_proj, gq_slots, gks2, gkw2 = _prep_proj_weights(w_in[0], g_q[0], g_ks[0], g_kw[0])
    cmp_weights = _prep_compress_weights(pe_ck[0], w_ck1[0], w_ck2[0], pe_cv[0], w_cv1[0], w_cv2[0], g_kc[0])

    pad = (-(bp + bs)) % 8
    c_all = jnp.concatenate([c_prompt, c_sample, jnp.zeros((pad, D_MODEL), F32)], axis=0)
    mod = _modulation(c_all, w_ada[0], b_ada[0])
    shift_p, scale_p, gate_p = (mod[:bp, k * D_MODEL:(k + 1) * D_MODEL].reshape(bp, 1, D_MODEL) for k in range(3))
    shift_s, scale_s, gate_s = (mod[bp:bp + bs, k * D_MODEL:(k + 1) * D_MODEL].reshape(1, bs, D_MODEL)
                                for k in range(3))

    xp2 = x_prompt.reshape(bp * seq, D_MODEL)
    cos_p, sin_p = _rope_tables(seq, 0, 1)
    (q_p, cmp_t, slc_t, slcb_p, win_t, winb_p, br_p, ga_p, rq_p, rk_p, rv_p, gr_p) = _projection(
        xp2, scale_p, shift_p, g_norm[0], w_proj, gq_slots, gks2, gkw2, cos_p, sin_p, n_batch=bp, per_row_mod=False,
        pos_minor=True)

    kc_p, vc_p = _compress(cmp_t.reshape(bp, 2, N_KV, HEAD_DIM, seq), None, jnp.zeros((bp, 1, KV_ROW), F32),
                           cmp_weights, n_batch=bp, n_pages=seq // PAGE_SIZE)
    ocmp_p, bias_p = _nsa_select(q_p, kc_p, vc_p, n_batch=bp, seq=seq)
    ya_p = _nsa_prompt(q_p, ocmp_p, bias_p, slcb_p, winb_p, br_p, ga_p, n_batch=bp, seq=seq)
    oret_p, sret_p = _retention_prompt(rq_p, rk_p, rv_p, n_batch=bp, seq=seq)
    y_p = _output(xp2, gate_p, ya_p, None, oret_p.reshape(bp * seq, D_RET), gr_p, g_ret[0], w_out[0],
                  n_batch=bp, per_row_mod=False)

    xs2 = x_sample.reshape(bs, D_MODEL)
    cos_s, sin_s = _rope_tables(bs, past, 0)
    (q_s, cmp_s, slc_s, _, win_s, _, br_s, ga_s, rq_s, rk_s, rv_s, gr_s) = _projection(
        xs2, scale_s, shift_s, g_norm[0], w_proj, gq_slots, gks2, gkw2, cos_s, sin_s, n_batch=1, per_row_mod=True,
        pos_minor=False)

    n_chunk_cache = past // CMP_STRIDE
    n_cmp_s = n_chunk_cache
    n_sel_s = past // SEL_BLOCK + 1
    nsp = ((n_sel_s + LANES - 1) // LANES) * LANES
    to_native = (0, 2, 3, 4, 1)
    kc_s, vc_s = _compress(jnp.transpose(cache_cmp[0], to_native), page_table, cmp_s.reshape(bs, 1, KV_ROW),
                           cmp_weights, n_batch=bs, n_pages=n_pages)

    q16 = jnp.pad(q_s[0].transpose(1, 0, 2), ((0, 0), (0, 16 - N_HEADS), (0, 0)))
    mimp_s = _importance_matrix(n_cmp_s, n_cmp_s, n_sel_s, nsp)
    o_cmp_s, o_win_s, idx_s = _nsa_step1(q16, kc_s, vc_s, jnp.transpose(state_win[0], to_native),
                                         win_s.reshape(bs, 1, KV_ROW), mimp_s, t=past, n_cmp=n_cmp_s,
                                         n_sel=n_sel_s)
    k_eff = min(TOP_N, n_sel_s)
    idx_flat = idx_s[:, :N_KV, :k_eff].reshape(bs, N_KV * k_eff)
    o_nsa_s = _nsa_step2(idx_flat, page_table, q16, jnp.transpose(cache_slc[0], to_native),
                         slc_s.reshape(bs, 1, KV_ROW), o_cmp_s, o_win_s, br_s.reshape(bs, 1, LANES),
                         n_cache_blocks=past // SEL_BLOCK, k_eff=k_eff)

    oret_s, sret_s = _retention_step(rq_s, rk_s, rv_s, state_ret[0])
    y_s = _output(xs2, gate_s, o_nsa_s.reshape(bs, D_NSA), ga_s, oret_s.reshape(bs, D_RET), gr_s, g_ret[0], w_out[0],
                  n_batch=1, per_row_mod=True)

    def kv_rows(a, b, l):
        return a.reshape(1, b, l, 2, N_KV, HEAD_DIM)

    def kv_rows_t(a, b, l):
        return jnp.transpose(a.reshape(b, 2, N_KV, HEAD_DIM, l), (0, 4, 1, 2, 3))[None]

    keep_p = min(WINDOW, seq)
    s_win = jnp.concatenate([state_win[0][:, 1:].reshape(bs, WINDOW - 1, KV_ROW), win_s.reshape(bs, 1, KV_ROW)],
                            axis=1)
    return (y_p.reshape(bp, seq, D_MODEL), y_s.reshape(bs, 1, D_MODEL),
            kv_rows_t(cmp_t, bp, seq), kv_rows_t(slc_t, bp, seq),
            kv_rows_t(win_t[:, :, seq - keep_p:], bp, keep_p), sret_p[None],
            kv_rows(cmp_s, bs, 1), kv_rows(slc_s, bs, 1), kv_rows(s_win, bs, WINDOW), sret_s[None])
```

```python
import functools
import math

import jax
import jax.numpy as jnp
import numpy as np
from jax import lax
from jax.experimental import pallas as pl
from jax.experimental.pallas import tpu as pltpu

D_MODEL = 1024
HEAD_DIM = 64
D_NSA = 512
N_HEADS = 8
N_KV = 2
HPG = 4
CMP_BLOCK = 32
CMP_STRIDE = 16
SEL_BLOCK = 64
TOP_N = 16
WINDOW = 512
D_RET = 512
N_RET_HEADS = 4
DV_RET = 128
DK_RET = 64
RET_CHUNK = 128
PAGE_SIZE = 128
ROPE_BASE = 10000.0
EPS = 1e-6
NEG = -1e30
FORCE = 1e4
BELOW_NEG = -2e38
TAKEN = -3e38

KV_ROW = 2 * N_KV * HEAD_DIM
LANES = 128
VMEM_LIMIT_BYTES = 56 * 1024 * 1024

F32 = jnp.float32
BF16 = jnp.bfloat16

O_Q = 0
O_CMP = O_Q + D_NSA
O_SLC = O_CMP + KV_ROW
O_WIN = O_SLC + KV_ROW
O_BR = O_WIN + KV_ROW
O_GA = O_BR + LANES
O_RQ = O_GA + D_NSA
O_RK = O_RQ + N_RET_HEADS * DK_RET
O_RV = O_RK + N_RET_HEADS * DK_RET
O_GR = O_RV + D_RET
D_IN_PAD = O_GR + D_RET


def _cparams(*sem):
    return pltpu.CompilerParams(dimension_semantics=sem, vmem_limit_bytes=VMEM_LIMIT_BYTES)


def _dot(a, b):
    return jnp.dot(a, b, preferred_element_type=F32)


def _dot_nt(a, b):
    return lax.dot_general(a, b, (((1,), (1,)), ((), ())), preferred_element_type=F32)


def _dot_tn(a, b):
    return lax.dot_general(a, b, (((0,), (0,)), ((), ())), preferred_element_type=F32)


def _silu(x):
    return x * jax.nn.sigmoid(x)


def _split3_dot(p, m):
    p1 = p.astype(BF16)
    r1 = p - p1.astype(F32)
    p2 = r1.astype(BF16)
    p3 = (r1 - p2.astype(F32)).astype(BF16)
    return _dot(p1, m) + _dot(p2, m) + _dot(p3, m)


def _pair_rmsnorm(k, gain2):
    lo = lax.broadcasted_iota(jnp.int32, k.shape, 1) < HEAD_DIM
    sq = k * k
    s0 = jnp.sum(jnp.where(lo, sq, 0.0), axis=-1, keepdims=True)
    s1 = jnp.sum(jnp.where(lo, 0.0, sq), axis=-1, keepdims=True)
    ms = jnp.where(lo, s0, s1) * (1.0 / HEAD_DIM)
    return k * lax.rsqrt(ms + EPS) * gain2


def _mod_kernel(c_ref, w_ref, b_ref, o_ref):
    o_ref[...] = _dot(_silu(c_ref[...]).astype(BF16), w_ref[...]) + b_ref[...]


def _modulation(c_all, w_ada, b_ada):
    rows = c_all.shape[0]
    n = w_ada.shape[1]
    tn = 1024
    return pl.pallas_call(
        _mod_kernel,
        grid=(n // tn,),
        in_specs=[pl.BlockSpec((rows, D_MODEL), lambda j: (0, 0)),
                  pl.BlockSpec((D_MODEL, tn), lambda j: (0, j)),
                  pl.BlockSpec((1, tn), lambda j: (0, j))],
        out_specs=pl.BlockSpec((rows, tn), lambda j: (0, j)),
        out_shape=jax.ShapeDtypeStruct((rows, n), F32),
        compiler_params=_cparams("arbitrary"),
        name="adaln_modulation",
    )(c_all, w_ada.astype(BF16), b_ada.reshape(1, n))


def _rope_kernel(freq_ref, cos_ref, sin_ref, *, base, stride, rows):
    shape = (rows, LANES)
    r = lax.broadcasted_iota(jnp.int32, shape, 0)
    pos = (base + stride * (pl.program_id(0) * rows + r)).astype(F32)
    ang = pos * freq_ref[...]
    lane = lax.broadcasted_iota(jnp.int32, shape, 1)
    sign = jnp.where((lane & (DK_RET - 1)) < DK_RET // 2, -1.0, 1.0)
    cos_ref[...] = jnp.cos(ang)
    sin_ref[...] = jnp.sin(ang) * sign


def _rope_tables(n_rows, base, stride):
    half = DK_RET // 2
    freqs = ROPE_BASE ** (-jnp.arange(half, dtype=F32) / half)
    freq_row = jnp.tile(freqs, LANES // half).reshape(1, LANES)
    rows = min(n_rows, 512)
    return pl.pallas_call(
        functools.partial(_rope_kernel, base=base, stride=stride, rows=rows),
        grid=(n_rows // rows,),
        in_specs=[pl.BlockSpec((1, LANES), lambda i: (0, 0))],
        out_specs=[pl.BlockSpec((rows, LANES), lambda i: (i, 0))] * 2,
        out_shape=[jax.ShapeDtypeStruct((n_rows, LANES), F32)] * 2,
        compiler_params=_cparams("arbitrary"),
        name="rope_tables",
    )(freq_row)


def _proj_kernel(x_ref, scale_ref, shift_ref, gn_ref, w_ref, gq_ref, gks_ref, gkw_ref, cos_ref, sin_ref,
                 q_ref, cmp_ref, slc_ref, slcb_ref, win_ref, winb_ref, br_ref, ga_ref, rq_ref, rk_ref,
                 rv_ref, gr_ref, *, pos_minor):
    x = x_ref[...]
    ms = jnp.mean(x * x, axis=-1, keepdims=True)
    xn = x * lax.rsqrt(ms + EPS) * gn_ref[...]
    hb = (xn * (1.0 + scale_ref[0]) + shift_ref[0]).astype(BF16)

    def seg(start, width):
        return _dot(hb, w_ref[:, start:start + width])

    yq = seg(O_Q, D_NSA)
    lane_lo = lax.broadcasted_iota(jnp.int32, (yq.shape[0], LANES), 1) < HEAD_DIM
    for s in range(N_HEADS):
        t = yq[:, (s // 2) * LANES:(s // 2 + 1) * LANES]
        if s % 2 != s // HPG:
            t = pltpu.roll(t, HEAD_DIM, axis=1)
        t = jnp.where(lane_lo == (s < HPG), t, 0.0)
        ms = jnp.sum(t * t, axis=-1, keepdims=True) * (1.0 / HEAD_DIM)
        q_ref[0, s] = (t * lax.rsqrt(ms + EPS) * gq_ref[s]).astype(BF16)

    y_cmp = seg(O_CMP, KV_ROW)
    y = seg(O_SLC, KV_ROW)
    y_slc = jnp.concatenate([_pair_rmsnorm(y[:, :LANES], gks_ref[...]), y[:, LANES:]], axis=-1)
    slcb_ref[...] = y_slc.astype(BF16)
    y = seg(O_WIN, KV_ROW)
    y_win = jnp.concatenate([_pair_rmsnorm(y[:, :LANES], gkw_ref[...]), y[:, LANES:]], axis=-1)
    winb_ref[...] = y_win.astype(BF16)
    for ref, rows in ((cmp_ref, y_cmp), (slc_ref, y_slc), (win_ref, y_win)):
        if pos_minor:
            ref[0] = rows.T
        else:
            ref[...] = rows

    br_ref[...] = jax.nn.sigmoid(seg(O_BR, LANES))
    ga_ref[...] = _silu(seg(O_GA, D_NSA))

    first_half = (lax.broadcasted_iota(jnp.int32, (yq.shape[0], LANES), 1) & (DK_RET - 1)) < DK_RET // 2

    def rotary(y):
        tiles = []
        for k in range(y.shape[1] // LANES):
            t = y[:, k * LANES:(k + 1) * LANES]
            swapped = jnp.where(first_half, pltpu.roll(t, LANES - DK_RET // 2, axis=1),
                                pltpu.roll(t, DK_RET // 2, axis=1))
            tiles.append(t * cos_ref[...] + swapped * sin_ref[...])
        return jnp.concatenate(tiles, axis=-1)

    nrot = N_RET_HEADS * DK_RET
    rq_ref[...] = rotary(seg(O_RQ, nrot)).astype(BF16)
    rk_ref[...] = (rotary(seg(O_RK, nrot)) * DK_RET ** -0.5).astype(BF16)
    rv_ref[...] = seg(O_RV, D_RET).astype(BF16)
    gr_ref[...] = _silu(seg(O_GR, D_RET))


def _prep_proj_weights(w_in, g_q, g_ks, g_kw):
    n_br = 3 * N_HEADS
    w = jnp.concatenate([w_in[:, :O_BR], jnp.pad(w_in[:, O_BR:O_BR + n_br], ((0, 0), (0, LANES - n_br))),
                         w_in[:, O_BR + n_br:]], axis=1).astype(BF16)
    assert w.shape[1] == D_IN_PAD
    gz = jnp.zeros((HEAD_DIM,), F32)
    gq = g_q * (HEAD_DIM ** -0.5 * math.log2(math.e))
    gq_slots = jnp.stack([jnp.concatenate([gq, gz]) if s < HPG else jnp.concatenate([gz, gq])
                          for s in range(N_HEADS)]).reshape(N_HEADS, 1, LANES)
    return w, gq_slots, jnp.tile(g_ks, 2).reshape(1, LANES), jnp.tile(g_kw, 2).reshape(1, LANES)


def _projection(x2, scale, shift, g_norm, w, gq_slots, gks2, gkw2, cos_t, sin_t, *, n_batch, per_row_mod,
                pos_minor):
    m = x2.shape[0]
    seq = m // n_batch
    tm = min(seq, 256)
    nt = seq // tm
    if per_row_mod:
        mod_spec = pl.BlockSpec((1, tm, D_MODEL), lambda i: (0, i, 0))
    else:
        mod_spec = pl.BlockSpec((1, 1, D_MODEL), lambda i: (i // nt, 0, 0))

    def rows_spec(width):
        return pl.BlockSpec((tm, width), lambda i: (i, 0))

    def const_spec(shape):
        return pl.BlockSpec(shape, lambda i: (0,) * len(shape))

    kv_out = "kv rows"
    widths = [(KV_ROW, kv_out), (KV_ROW, kv_out), (KV_ROW, BF16), (KV_ROW, kv_out), (KV_ROW, BF16), (LANES, F32),
              (D_NSA, F32), (N_RET_HEADS * DK_RET, BF16), (N_RET_HEADS * DK_RET, BF16), (D_RET, BF16),
              (D_RET, F32)]
    out_shape = [jax.ShapeDtypeStruct((n_batch, N_HEADS, seq, LANES), BF16)]
    out_specs = [pl.BlockSpec((1, N_HEADS, tm, LANES), lambda i: (i // nt, 0, i % nt, 0))]
    for width, dt in widths:
        if pos_minor and dt is kv_out:
            out_shape.append(jax.ShapeDtypeStruct((n_batch, KV_ROW, seq), F32))
            out_specs.append(pl.BlockSpec((1, KV_ROW, tm), lambda i: (i // nt, 0, i % nt)))
        else:
            out_shape.append(jax.ShapeDtypeStruct((m, width), F32 if dt is kv_out else dt))
            out_specs.append(rows_spec(width))
    return pl.pallas_call(
        functools.partial(_proj_kernel, pos_minor=pos_minor),
        grid=(m // tm,),
        in_specs=[rows_spec(D_MODEL), mod_spec, mod_spec, const_spec((1, D_MODEL)),
                  const_spec((D_MODEL, D_IN_PAD)), const_spec((N_HEADS, 1, LANES)), const_spec((1, LANES)),
                  const_spec((1, LANES)), pl.BlockSpec((tm, LANES), lambda i: (i % nt, 0)),
                  pl.BlockSpec((tm, LANES), lambda i: (i % nt, 0))],
        out_specs=out_specs,
        out_shape=out_shape,
        compiler_params=_cparams("arbitrary"),
        name="input_projection",
    )(x2, scale, shift, g_norm.reshape(1, D_MODEL), w, gq_slots, gks2, gkw2, cos_t, sin_t)


def _compress_kernel(pt_ref, *refs, n_in):
    x_refs, next_ref = refs[:n_in], refs[n_in]
    new_ref, perm_ref, w1_ref, pe_ref, w2_ref, gkc_ref, kc_ref, vc_ref = refs[n_in + 1:]
    cpp = PAGE_SIZE // CMP_STRIDE
    n = n_in * cpp
    last_step = pl.program_id(1) == pl.num_programs(1) - 1
    perm = perm_ref[...]

    def chunk_rows(page_ref):
        return _dot_nt(perm, page_ref[0].reshape(KV_ROW, PAGE_SIZE).astype(BF16))

    def flat_chunks(moved, kv):
        return jnp.concatenate(
            [jnp.concatenate([r[cpp * l:cpp * (l + 1), kv * LANES:(kv + 1) * LANES] for l in range(CMP_STRIDE)],
                             axis=1) for r in moved], axis=0).astype(BF16)

    moved = [chunk_rows(r) for r in x_refs]
    moved_next = [chunk_rows(next_ref)]
    new8 = jnp.broadcast_to(new_ref[0], (8, KV_ROW)).astype(BF16)
    zeros = jnp.zeros((8, (CMP_STRIDE - 1) * LANES), BF16)
    row = lax.broadcasted_iota(jnp.int32, (n, LANES), 0)
    pre = []
    for kv in range(2):
        new_chunk = jnp.concatenate([new8[:, kv * LANES:(kv + 1) * LANES], zeros], axis=1)
        a_all = _dot(jnp.concatenate([flat_chunks(moved, kv), flat_chunks(moved_next, kv), pe_ref[kv], new_chunk],
                                     axis=0), w1_ref[kv])
        a = a_all[:n]
        a_next = a_all[n:n + 1, LANES:]
        bias = a_all[n + cpp:n + cpp + 1, :LANES] + a_all[n + cpp + 1:n + cpp + 2, LANES:]
        a_new = a_all[n + cpp + 8:n + cpp + 9, LANES:]
        hi = pltpu.roll(a[:, LANES:], n - 1, axis=0)
        hi = jnp.where(row == n - 1, jnp.where(last_step, a_new, a_next), hi)
        pre.append(a[:, :LANES] + hi + bias)
    out = _dot(_silu(jnp.concatenate(pre, axis=1)).astype(BF16), w2_ref[...])
    kc_ref[0] = _pair_rmsnorm(out[:, :LANES], gkc_ref[...]).astype(BF16)
    vc_ref[0] = out[:, LANES:].astype(BF16)


def _compress(rows_t, page_table, new_rows, weights, *, n_batch, n_pages):
    w1kv, w2x, pe8kv, gkc2 = weights
    b = n_batch
    cpp = PAGE_SIZE // CMP_STRIDE
    p = math.gcd(n_pages, 128)
    row = jnp.arange(PAGE_SIZE, dtype=jnp.int32)
    perm = ((row % cpp) * CMP_STRIDE + row // cpp).reshape(PAGE_SIZE, 1) == row.reshape(1, PAGE_SIZE)
    page_shape = (1,) + rows_t.shape[1:4] + (PAGE_SIZE,)

    def page_spec(page_of_step):
        if page_table is None:
            return pl.BlockSpec(page_shape, lambda i, j, pt: (i, 0, 0, 0, page_of_step(j)))
        return pl.BlockSpec(page_shape, lambda i, j, pt: (pt[i, page_of_step(j)], 0, 0, 0, 0))

    in_specs = [page_spec(lambda j, k=k: j * p + k) for k in range(p)]
    in_specs.append(page_spec(lambda j: jnp.minimum((j + 1) * p, n_pages - 1)))

    def const_spec(shape):
        return pl.BlockSpec(shape, lambda i, j, pt: (0,) * len(shape))

    in_specs += [pl.BlockSpec((1, 1, KV_ROW), lambda i, j, pt: (i, 0, 0)), const_spec((PAGE_SIZE, PAGE_SIZE)),
                 const_spec(w1kv.shape), const_spec(pe8kv.shape), const_spec(w2x.shape), const_spec((1, LANES))]
    if page_table is None:
        page_table = jnp.zeros((1, 1), jnp.int32)
    return pl.pallas_call(
        functools.partial(_compress_kernel, n_in=p),
        grid_spec=pltpu.PrefetchScalarGridSpec(
            num_scalar_prefetch=1,
            grid=(b, n_pages // p),
            in_specs=in_specs,
            out_specs=[pl.BlockSpec((1, p * cpp, LANES), lambda i, j, pt: (i, j, 0))] * 2,
        ),
        out_shape=[jax.ShapeDtypeStruct((b, n_pages * cpp, LANES), BF16)] * 2,
        compiler_params=_cparams("arbitrary", "arbitrary"),
        name="compress",
    )(page_table, *([rows_t] * (p + 1)), new_rows, perm.astype(BF16), w1kv, pe8kv, w2x, gkc2)


def _prep_compress_weights(pe_ck, w_ck1, w_ck2, pe_cv, w_cv1, w_cv2, g_kc):
    eye = jnp.eye(N_KV, dtype=F32)
    w1 = jnp.stack([w_ck1, w_cv1]).reshape(2, 2, CMP_STRIDE, HEAD_DIM, HEAD_DIM)
    w1kv = jnp.einsum('khldf,gp->klgdhpf', w1, eye).reshape(2, CMP_STRIDE * LANES, KV_ROW).astype(BF16)
    w2 = jnp.stack([w_ck2, w_cv2])
    w2x = jnp.einsum('kfd,kq,gp->kgfqpd', w2, eye, eye).reshape(KV_ROW, KV_ROW).astype(BF16)
    pe = jnp.stack([pe_ck, pe_cv]).reshape(2, 2, CMP_STRIDE, 1, HEAD_DIM)
    pe_rows = jnp.broadcast_to(pe, (2, 2, CMP_STRIDE, N_KV, HEAD_DIM)).reshape(2, 2, CMP_STRIDE * LANES)
    pe8kv = jnp.zeros((2, 8, CMP_STRIDE * LANES), F32).at[:, :2].set(pe_rows).astype(BF16)
    return w1kv, w2x, pe8kv, jnp.tile(g_kc, 2).reshape(1, LANES)


def _importance_matrix(n_cmp_pad, n_cmp, n_sel, n_sel_pad):
    ratio = SEL_BLOCK // CMP_STRIDE
    span = CMP_BLOCK // CMP_STRIDE
    m = np.zeros((n_cmp_pad, n_sel_pad), np.float32)
    for j in range(n_sel):
        for r in range(ratio):
            for s in range(span):
                n = ratio * j + r - s
                if 0 <= n < n_cmp:
                    m[n, j] += 1.0
    return jnp.asarray(m, BF16)


def _nsa_select_kernel(q_ref, kc_ref, vc_ref, mimp_ref, oc_ref, bias_ref, *, tq, n_cmp, n_sel):
    qs = pl.program_id(1) * tq
    rows = HPG * tq
    ncp = kc_ref.shape[1]

    def q_pos(width):
        return qs + lax.broadcasted_iota(jnp.int32, (tq, width), 0)

    def per_head(x):
        return jnp.concatenate([x] * HPG, axis=0)

    col = lax.broadcasted_iota(jnp.int32, (tq, ncp), 1)
    cmp_mask = per_head(jnp.where((col * CMP_STRIDE + (CMP_BLOCK - 1) <= q_pos(ncp)) & (col < n_cmp), 0.0, NEG))
    cmp_any = per_head(jnp.where(q_pos(1) >= CMP_BLOCK - 1, 1.0, 0.0))
    blk = lax.broadcasted_iota(jnp.int32, (tq, LANES), 1)
    t = q_pos(LANES)
    jt = t // SEL_BLOCK
    forced = (blk == 0) | (blk == jt) | (blk == jt - 1)
    valid = blk * SEL_BLOCK <= t
    n_rank = ((n_sel + 7) // 8) * 8
    j_row = lax.broadcasted_iota(jnp.int32, (n_rank, tq), 0)

    for g in range(N_KV):
        q = q_ref[0, HPG * g:HPG * (g + 1)].reshape(rows, LANES)
        s = _dot_nt(q, kc_ref[0]) + cmp_mask
        e = jnp.exp2(s - jnp.max(s, axis=-1, keepdims=True))
        p = e * (cmp_any / jnp.sum(e, axis=-1, keepdims=True))
        oc_ref[0, HPG * g:HPG * (g + 1)] = _dot(p.astype(BF16), vc_ref[0]).astype(BF16).reshape(HPG, tq, LANES)
        p_sum = p[0:tq] + p[tq:2 * tq] + p[2 * tq:3 * tq] + p[3 * tq:4 * tq]
        imp = _split3_dot(p_sum, mimp_ref[...])

        score = jnp.where(valid, jnp.where(forced, FORCE, imp), NEG)
        score = jnp.where(blk < n_sel, score, BELOW_NEG)
        left = score.T[:n_rank]
        sel_t = jnp.zeros((n_rank, tq), F32)
        for _ in range(min(TOP_N, n_sel)):
            top = jnp.max(left, axis=0, keepdims=True)
            first = jnp.min(jnp.where(left == top, j_row, n_rank), axis=0, keepdims=True)
            taken = j_row == first
            sel_t = jnp.where(taken, 1.0, sel_t)
            left = jnp.where(taken, TAKEN, left)
        if n_rank < LANES:
            sel_t = jnp.concatenate([sel_t, jnp.zeros((LANES - n_rank, tq), F32)], axis=0)
        bias_ref[0, g] = jnp.where(sel_t.T > 0.5, 0.0, NEG).astype(BF16)


def _nsa_select(q_slots, kc, vc, *, n_batch, seq):
    tq = min(512, seq)
    n_cmp = seq // CMP_STRIDE - (CMP_BLOCK // CMP_STRIDE) + 1
    n_sel = seq // SEL_BLOCK
    assert n_sel <= LANES
    ncp = kc.shape[1]
    mimp = _importance_matrix(ncp, n_cmp, n_sel, LANES)
    return pl.pallas_call(
        functools.partial(_nsa_select_kernel, tq=tq, n_cmp=n_cmp, n_sel=n_sel),
        grid=(n_batch, seq // tq),
        in_specs=[pl.BlockSpec((1, N_HEADS, tq, LANES), lambda b, i: (b, 0, i, 0)),
                  pl.BlockSpec((1, ncp, LANES), lambda b, i: (b, 0, 0)),
                  pl.BlockSpec((1, ncp, LANES), lambda b, i: (b, 0, 0)),
                  pl.BlockSpec(mimp.shape, lambda b, i: (0, 0))],
        out_specs=[pl.BlockSpec((1, N_HEADS, tq, LANES), lambda b, i: (b, 0, i, 0)),
                   pl.BlockSpec((1, N_KV, tq, LANES), lambda b, i: (b, 0, i, 0))],
        out_shape=[jax.ShapeDtypeStruct((n_batch, N_HEADS, seq, LANES), BF16),
                   jax.ShapeDtypeStruct((n_batch, N_KV, seq, LANES), BF16)],
        compiler_params=_cparams("arbitrary", "arbitrary"),
        name="nsa_select",
    )(q_slots, kc, vc, mimp)


def _nsa_prompt_kernel(q_ref, oc_ref, bias_ref, slc_ref, win_ref, br_ref, ga_ref, expt_ref, o_ref,
                       m_sc, acc_sc, *, tq, tk):
    qs = pl.program_id(1) * tq
    rows = HPG * tq
    wlen = WINDOW + tq
    br = br_ref[...]
    lane_lo = lax.broadcasted_iota(jnp.int32, (tq, LANES), 1) < HEAD_DIM

    def q_pos(width):
        return qs + lax.broadcasted_iota(jnp.int32, (tq, width), 0)

    def per_head(x):
        return jnp.concatenate([x] * HPG, axis=0)

    w0 = pl.multiple_of(jnp.maximum(qs - WINDOW, 0), tq)
    kpos = w0 + lax.broadcasted_iota(jnp.int32, (tq, wlen), 1)
    win_mask = per_head(jnp.where((kpos <= q_pos(wlen)) & (kpos > q_pos(wlen) - WINDOW), 0.0, NEG))
    kt_diag = qs // tk
    kpos = kt_diag * tk + lax.broadcasted_iota(jnp.int32, (tq, tk), 1)
    diag_mask = per_head(jnp.where(kpos <= q_pos(tk), 0.0, NEG))

    def with_ones(v, g):
        own = (lax.broadcasted_iota(jnp.int32, v.shape, 1) // HEAD_DIM) == g
        return jnp.where(own, v, jnp.ones_like(v))

    def normalised(acc):
        return acc * (1.0 / pltpu.roll(acc, HEAD_DIM, axis=1))

    groups = range(N_KV)
    q = [q_ref[0, HPG * g:HPG * (g + 1)].reshape(rows, LANES) for g in groups]
    q_ext = jnp.concatenate([jnp.concatenate([q[g], per_head(bias_ref[0, g])], axis=1) for g in groups], axis=0)

    m_sc[...] = jnp.full((N_KV * rows, LANES), NEG, F32)
    acc_sc[...] = jnp.zeros((N_KV * rows, LANES), F32)

    def key_tile(kt, causal):
        k0 = pl.multiple_of(kt * tk, tk)
        k_ext = jnp.concatenate([slc_ref[0, pl.ds(k0, tk), 0:LANES], expt_ref[kt]], axis=1)
        v = slc_ref[0, pl.ds(k0, tk), LANES:2 * LANES]
        s_all = _dot_nt(q_ext, k_ext)
        for g in groups:
            rs = slice(g * rows, (g + 1) * rows)
            s = s_all[rs]
            if causal:
                s = s + diag_mask
            m_prev = m_sc[rs]
            m_next = jnp.maximum(m_prev, jnp.max(s, axis=-1, keepdims=True))
            p = jnp.exp2(s - jnp.concatenate([m_next] * (tk // LANES), axis=1))
            acc_sc[rs] = jnp.exp2(m_prev - m_next) * acc_sc[rs] + _dot(p.astype(BF16), with_ones(v, g))
            m_sc[rs] = m_next

    def full_tile(kt, carry):
        key_tile(kt, False)
        return carry

    lax.fori_loop(0, kt_diag, full_tile, 0)
    key_tile(kt_diag, True)

    k = win_ref[0, pl.ds(w0, wlen), 0:LANES]
    v = win_ref[0, pl.ds(w0, wlen), LANES:2 * LANES]
    s_all = _dot_nt(jnp.concatenate(q, axis=0), k)
    for g in groups:
        rs = slice(g * rows, (g + 1) * rows)
        s = s_all[rs] + win_mask
        e = jnp.exp2(s - jnp.max(s, axis=-1, keepdims=True))
        o_win = normalised(_dot(e.astype(BF16), with_ones(v, g)))
        o_slc = normalised(acc_sc[rs])

        heads = []
        for h in range(HPG):
            hs = slice(h * tq, (h + 1) * tq)
            c = 3 * (HPG * g + h)
            heads.append(br[:, c:c + 1] * oc_ref[0, HPG * g + h].astype(F32) + br[:, c + 1:c + 2] * o_slc[hs]
                         + br[:, c + 2:c + 3] * o_win[hs])
        for hh in range(HPG // 2):
            even, odd = heads[2 * hh], heads[2 * hh + 1]
            if g == 0:
                odd = pltpu.roll(odd, HEAD_DIM, axis=1)
            else:
                even = pltpu.roll(even, HEAD_DIM, axis=1)
            c0 = (2 * g + hh) * LANES
            o_ref[:, c0:c0 + LANES] = (jnp.where(lane_lo, even, odd) * ga_ref[:, c0:c0 + LANES]).astype(BF16)


def _nsa_prompt(q_slots, o_cmp, block_bias, slc_b, win_b, br, ga, *, n_batch, seq):
    tq = min(256, seq)
    tk = min(512, seq)
    assert seq % tk == 0 and tk % tq == 0 and seq >= WINDOW + tq
    key_blk = (jnp.arange(seq, dtype=jnp.int32) // SEL_BLOCK).reshape(seq // tk, tk, 1)
    expand = (key_blk == jnp.arange(LANES, dtype=jnp.int32).reshape(1, 1, LANES)).astype(BF16)
    nq = seq // tq
    rows = HPG * tq
    return pl.pallas_call(
        functools.partial(_nsa_prompt_kernel, tq=tq, tk=tk),
        grid=(n_batch, nq),
        in_specs=[pl.BlockSpec((1, N_HEADS, tq, LANES), lambda b, i: (b, 0, i, 0)),
                  pl.BlockSpec((1, N_HEADS, tq, LANES), lambda b, i: (b, 0, i, 0)),
                  pl.BlockSpec((1, N_KV, tq, LANES), lambda b, i: (b, 0, i, 0)),
                  pl.BlockSpec((1, seq, KV_ROW), lambda b, i: (b, 0, 0)),
                  pl.BlockSpec((1, seq, KV_ROW), lambda b, i: (b, 0, 0)),
                  pl.BlockSpec((tq, LANES), lambda b, i: (b * nq + i, 0)),
                  pl.BlockSpec((tq, D_NSA), lambda b, i: (b * nq + i, 0)),
                  pl.BlockSpec(expand.shape, lambda b, i: (0, 0, 0))],
        out_specs=pl.BlockSpec((tq, D_NSA), lambda b, i: (b * nq + i, 0)),
        out_shape=jax.ShapeDtypeStruct((n_batch * seq, D_NSA), BF16),
        scratch_shapes=[pltpu.VMEM((N_KV * rows, LANES), F32)] * 2,
        compiler_params=_cparams("arbitrary", "arbitrary"),
        name="nsa_prompt",
    )(q_slots, o_cmp, block_bias, slc_b.reshape(n_batch, seq, KV_ROW), win_b.reshape(n_batch, seq, KV_ROW), br, ga,
      expand)


def _retention_prompt_kernel(rq_ref, rk_ref, rv_ref, o_ref, s_ref, *, cl):
    @pl.when(pl.program_id(0) == 0)
    def _():
        s_ref[...] = jnp.zeros(s_ref.shape, F32)

    i_pos = lax.broadcasted_iota(jnp.int32, (cl, cl), 0)
    j_pos = lax.broadcasted_iota(jnp.int32, (cl, cl), 1)
    diff = (i_pos - j_pos).astype(F32)
    i_col = lax.broadcasted_iota(jnp.int32, (cl, 1), 0).astype(F32)
    lane_lo = lax.broadcasted_iota(jnp.int32, (cl, LANES), 1) < DK_RET
    for h in range(N_RET_HEADS):
        pair, odd = divmod(h, 2)
        log_g = math.log(1.0 - 2.0 ** (-5.0 - h))
        keep = lane_lo if odd == 0 else jnp.logical_not(lane_lo)
        decay = jnp.where(diff >= 0, jnp.exp(log_g * jnp.maximum(diff, 0.0)), 0.0)
        xi = jnp.exp(log_g * (i_col + 1.0))
        zeta = jnp.exp(log_g * (cl - 1.0 - i_col))
        for b in range(rq_ref.shape[0]):
            q2 = rq_ref[b, :, pair * LANES:(pair + 1) * LANES]
            k2 = rk_ref[b, :, pair * LANES:(pair + 1) * LANES]
            s_pair = jnp.concatenate([s_ref[b, 2 * pair], s_ref[b, 2 * pair + 1]], axis=0)
            qh = jnp.where(keep, q2, jnp.zeros_like(q2))
            v = rv_ref[b, :, h * DV_RET:(h + 1) * DV_RET]
            a = _dot_nt(qh, k2) * decay
            o_ref[b, :, h * DV_RET:(h + 1) * DV_RET] = (_dot(a.astype(BF16), v)
                                                         + _dot(qh, s_pair.astype(BF16)) * xi)
            kz = (k2.astype(F32) * zeta).astype(BF16)
            u = _dot_tn(kz, v)[odd * DK_RET:(odd + 1) * DK_RET]
            s_ref[b, h] = s_ref[b, h] * math.exp(log_g * cl) + u


def _retention_prompt(rq, rk, rv, *, n_batch, seq):
    cl = math.gcd(seq, 2 * RET_CHUNK)
    nrot = N_RET_HEADS * DK_RET
    return pl.pallas_call(
        functools.partial(_retention_prompt_kernel, cl=cl),
        grid=(seq // cl,),
        in_specs=[pl.BlockSpec((n_batch, cl, nrot), lambda i: (0, i, 0)),
                  pl.BlockSpec((n_batch, cl, nrot), lambda i: (0, i, 0)),
                  pl.BlockSpec((n_batch, cl, D_RET), lambda i: (0, i, 0))],
        out_specs=[pl.BlockSpec((n_batch, cl, D_RET), lambda i: (0, i, 0)),
                   pl.BlockSpec((n_batch, N_RET_HEADS, DK_RET, DV_RET), lambda i: (0, 0, 0, 0))],
        out_shape=[jax.ShapeDtypeStruct((n_batch, seq, D_RET), F32),
                   jax.ShapeDtypeStruct((n_batch, N_RET_HEADS, DK_RET, DV_RET), F32)],
        compiler_params=_cparams("arbitrary"),
        name="retention_prompt",
    )(rq.reshape(n_batch, seq, nrot), rk.reshape(n_batch, seq, nrot), rv.reshape(n_batch, seq, D_RET))


def _retention_step_kernel(rq_ref, rk_ref, rv_ref, s0_ref, o_ref, s_ref):
    lane_lo = lax.broadcasted_iota(jnp.int32, (8, LANES), 1) < DK_RET
    diag = (lax.broadcasted_iota(jnp.int32, (LANES, LANES), 0)
            == lax.broadcasted_iota(jnp.int32, (LANES, LANES), 1))
    for pair in range(N_RET_HEADS // 2):
        q2 = jnp.broadcast_to(rq_ref[0, :, pair * LANES:(pair + 1) * LANES], (8, LANES))
        k_row = rk_ref[0, :, pair * LANES:(pair + 1) * LANES]
        k2 = jnp.broadcast_to(k_row, (8, LANES))
        k_col = jnp.sum(jnp.where(diag, jnp.broadcast_to(k_row, (LANES, LANES)), 0.0), axis=-1, keepdims=True)
        s_pair = jnp.concatenate([s0_ref[0, 2 * pair], s0_ref[0, 2 * pair + 1]], axis=0)
        for odd in range(2):
            h = 2 * pair + odd
            g = math.exp(math.log(1.0 - 2.0 ** (-5.0 - h)))
            keep = lane_lo if odd == 0 else jnp.logical_not(lane_lo)
            qh = jnp.where(keep, q2, 0.0)
            v_row = rv_ref[0, :, h * DV_RET:(h + 1) * DV_RET]
            qk = jnp.sum(qh * k2, axis=-1, keepdims=True)
            o = qk * v_row + _dot(qh, s_pair) * g
            o_ref[0, :, h * DV_RET:(h + 1) * DV_RET] = o[0:1]
            u = (k_col * v_row)[odd * DK_RET:(odd + 1) * DK_RET]
            s_ref[0, h] = s0_ref[0, h] * g + u


def _retention_step(rq, rk, rv, s0):
    b = rq.shape[0]
    nrot = N_RET_HEADS * DK_RET
    state_spec = pl.BlockSpec((1, N_RET_HEADS, DK_RET, DV_RET), lambda i: (i, 0, 0, 0))
    return pl.pallas_call(
        _retention_step_kernel,
        grid=(b,),
        in_specs=[pl.BlockSpec((1, 1, nrot), lambda i: (i, 0, 0)),
                  pl.BlockSpec((1, 1, nrot), lambda i: (i, 0, 0)),
                  pl.BlockSpec((1, 1, D_RET), lambda i: (i, 0, 0)),
                  state_spec],
        out_specs=[pl.BlockSpec((1, 1, D_RET), lambda i: (i, 0, 0)), state_spec],
        out_shape=[jax.ShapeDtypeStruct((b, 1, D_RET), F32),
                   jax.ShapeDtypeStruct((b, N_RET_HEADS, DK_RET, DV_RET), F32)],
        compiler_params=_cparams("arbitrary"),
        name="retention_step",
    )(rq.astype(F32).reshape(b, 1, nrot), rk.astype(F32).reshape(b, 1, nrot), rv.astype(F32).reshape(b, 1, D_RET),
      s0)


def _nsa_step1_kernel(q_ref, kc_ref, vc_ref, win_ref, new_ref, mimp_ref, oc_ref, ow_ref, idx_ref, *,
                      t, n_cmp, n_sel):
    q = q_ref[0]
    nq = q.shape[0]
    ncp = kc_ref.shape[1]
    nsp = mimp_ref.shape[1]

    s = _dot_nt(q, kc_ref[0])
    col = lax.broadcasted_iota(jnp.int32, (nq, ncp), 1)
    mask = (col * CMP_STRIDE + (CMP_BLOCK - 1) <= t) & (col < n_cmp)
    s = jnp.where(mask, s, NEG)
    e = jnp.exp2(s - jnp.max(s, axis=-1, keepdims=True))
    p = jnp.where(mask, e, 0.0) * (1.0 / jnp.sum(e, axis=-1, keepdims=True))
    oc_ref[0] = _dot(p.astype(BF16), vc_ref[0])
    imp_rows = _split3_dot(p, mimp_ref[...])

    blk = lax.broadcasted_iota(jnp.int32, (1, nsp), 1)
    jt = t // SEL_BLOCK
    forced = (blk == 0) | (blk == jt) | (blk == jt - 1)
    scores = []
    for g in range(N_KV):
        imp = jnp.sum(imp_rows[HPG * g:HPG * (g + 1)], axis=0, keepdims=True)
        sc = jnp.where(blk * SEL_BLOCK <= t, jnp.where(forced, FORCE, imp), NEG)
        scores.append(jnp.where(blk < n_sel, sc, BELOW_NEG))
    s_rows = jnp.concatenate(scores + [jnp.full((LANES - N_KV, nsp), BELOW_NEG, F32)], axis=0)
    s_cols = s_rows.T
    a_idx = lax.broadcasted_iota(jnp.int32, (nsp, nsp), 0)
    c_idx = lax.broadcasted_iota(jnp.int32, (nsp, nsp), 1)
    rank_lane = lax.broadcasted_iota(jnp.int32, (nsp, LANES), 1).astype(F32)
    a_val = lax.broadcasted_iota(jnp.int32, (nsp, LANES), 0).astype(F32)
    rows_out = []
    for g in range(N_KV):
        s_col = s_cols[:, g:g + 1]
        s_row = s_rows[g:g + 1, :]
        beats = (s_row > s_col) | ((s_row == s_col) & (c_idx < a_idx))
        rank = jnp.sum(jnp.where(beats, 1.0, 0.0), axis=-1, keepdims=True)
        rows_out.append(jnp.sum(jnp.where(rank == rank_lane, a_val, 0.0), axis=0, keepdims=True))
    idx = jnp.concatenate(rows_out + [jnp.zeros((8 - N_KV, LANES), F32)], axis=0)
    idx_ref[0] = idx.astype(jnp.int32)

    wb = win_ref.shape[-1]
    kw_t = win_ref[0, 0].reshape(LANES, wb).astype(BF16)
    vw_t = win_ref[0, 1].reshape(LANES, wb).astype(BF16)
    k_new = new_ref[0, :, 0:LANES]
    v_new = new_ref[0, :, LANES:2 * LANES]
    s = _dot(q, kw_t)
    kpos = t - wb + lax.broadcasted_iota(jnp.int32, (nq, wb), 1)
    s = jnp.where((kpos > t - WINDOW) & (kpos >= 0), s, NEG)
    s_new = jnp.sum(q.astype(F32) * k_new, axis=-1, keepdims=True)
    m = jnp.maximum(jnp.max(s, axis=-1, keepdims=True), s_new)
    e = jnp.exp2(s - m)
    e_new = jnp.exp2(s_new - m)
    ow_ref[0] = ((_dot_nt(e.astype(BF16), vw_t) + e_new * v_new)
                 * (1.0 / (jnp.sum(e, axis=-1, keepdims=True) + e_new)))


def _nsa_step1(q16, kc, vc, state_win_t, new_win, mimp, *, t, n_cmp, n_sel):
    b, nq, _ = q16.shape
    ncp = kc.shape[1]

    def per_b(shape):
        return pl.BlockSpec((1,) + shape, lambda i: (i,) + (0,) * len(shape))

    return pl.pallas_call(
        functools.partial(_nsa_step1_kernel, t=t, n_cmp=n_cmp, n_sel=n_sel),
        grid=(b,),
        in_specs=[per_b((nq, LANES)), per_b((ncp, LANES)), per_b((ncp, LANES)), per_b(state_win_t.shape[1:]),
                  per_b((1, KV_ROW)), pl.BlockSpec(mimp.shape, lambda i: (0, 0))],
        out_specs=[per_b((nq, LANES)), per_b((nq, LANES)), per_b((8, LANES))],
        out_shape=[jax.ShapeDtypeStruct((b, nq, LANES), F32), jax.ShapeDtypeStruct((b, nq, LANES), F32),
                   jax.ShapeDtypeStruct((b, 8, LANES), jnp.int32)],
        compiler_params=_cparams("arbitrary"),
        name="nsa_step_compressed_window",
    )(q16, kc, vc, state_win_t, new_win, mimp)


def _nsa_step2_kernel(idx_sm, pt_sm, q_ref, *refs, n_cache_blocks, k_eff):
    page_refs = refs[:N_KV * k_eff]
    new_ref, oc_ref, ow_ref, br_ref, o_ref = refs[N_KV * k_eff:]
    b = pl.program_id(0)
    q = q_ref[0]
    nq = q.shape[0]
    bpp = PAGE_SIZE // SEL_BLOCK
    nk = k_eff * PAGE_SIZE
    q32 = q.astype(F32)
    lane_k = lax.broadcasted_iota(jnp.int32, (nq, nk), 1)
    slot = lane_k // PAGE_SIZE
    half = (lane_k % PAGE_SIZE) // SEL_BLOCK
    outs = []
    for g in range(N_KV):
        own = slice(g * HEAD_DIM, (g + 1) * HEAD_DIM)
        q_g = q32[:, own]
        k_new = new_ref[0, :, own]
        v_new = new_ref[0, :, LANES + g * HEAD_DIM:LANES + (g + 1) * HEAD_DIM]
        s_new_all = jnp.sum(q_g * k_new, axis=-1, keepdims=True)
        pages = [page_refs[g * k_eff + k][0] for k in range(k_eff)]
        keys_t = jnp.concatenate([p[0, 0] for p in pages], axis=-1).astype(BF16)
        vals_t = jnp.concatenate([p[1, 0] for p in pages], axis=-1).astype(BF16)
        allowed = jnp.zeros((nq, nk), jnp.int32)
        new_selected = jnp.int32(0)
        for k in range(k_eff):
            j = idx_sm[b, g * k_eff + k]
            in_block = (slot == k) & (half == j % bpp)
            allowed = jnp.where(in_block, (j < n_cache_blocks).astype(jnp.int32), allowed)
            new_selected = new_selected | (j == n_cache_blocks).astype(jnp.int32)
        s = jnp.where(allowed > 0, _dot(q_g.astype(BF16), keys_t), NEG)
        s_new = jnp.where(new_selected > 0, s_new_all, NEG)
        m = jnp.maximum(jnp.max(s, axis=-1, keepdims=True), s_new)
        e = jnp.exp2(s - m)
        e_new = jnp.exp2(s_new - m)
        outs.append((_dot_nt(e.astype(BF16), vals_t) + e_new * v_new)
                    * (1.0 / (jnp.sum(e, axis=-1, keepdims=True) + e_new)))
    row = lax.broadcasted_iota(jnp.int32, (nq, LANES), 0)
    lane = lax.broadcasted_iota(jnp.int32, (nq, LANES), 1)
    o_slc = jnp.concatenate(outs, axis=1)
    br = br_ref[0]
    gates = [jnp.sum(jnp.where(lane == 3 * row + j, br, 0.0), axis=-1, keepdims=True) for j in range(3)]
    o = gates[0] * oc_ref[0] + gates[1] * o_slc + gates[2] * ow_ref[0]
    lane_lo = lax.broadcasted_iota(jnp.int32, (1, LANES), 1) < HEAD_DIM
    for pair in range(N_HEADS // 2):
        even, odd = o[2 * pair:2 * pair + 1], o[2 * pair + 1:2 * pair + 2]
        if 2 * pair < HPG:
            odd = pltpu.roll(odd, HEAD_DIM, axis=1)
        else:
            even = pltpu.roll(even, HEAD_DIM, axis=1)
        o_ref[0, :, pair * LANES:(pair + 1) * LANES] = jnp.where(lane_lo, even, odd)


def _nsa_step2(idx, page_table, q16, cache_t, new_slc, o_cmp, o_win, br, *, n_cache_blocks, k_eff):
    b, nq, _ = q16.shape
    bpp = PAGE_SIZE // SEL_BLOCK

    def page_map(i, idx_sm, pt_sm, c):
        j = jnp.minimum(idx_sm[i, c], n_cache_blocks - 1)
        return (pt_sm[i, j // bpp], 0, c // k_eff, 0, 0)

    def per_b(shape):
        return pl.BlockSpec((1,) + shape, lambda i, idx_sm, pt_sm: (i,) + (0,) * len(shape))

    n_blk = N_KV * k_eff
    in_specs = [per_b((nq, LANES))]
    slab_shape = (1, 2, 1) + cache_t.shape[3:]
    in_specs += [pl.BlockSpec(slab_shape, functools.partial(page_map, c=c)) for c in range(n_blk)]
    in_specs += [per_b((1, KV_ROW)), per_b((nq, LANES)), per_b((nq, LANES)), per_b((1, LANES))]
    return pl.pallas_call(
        functools.partial(_nsa_step2_kernel, n_cache_blocks=n_cache_blocks, k_eff=k_eff),
        grid_spec=pltpu.PrefetchScalarGridSpec(
            num_scalar_prefetch=2,
            grid=(b,),
            in_specs=in_specs,
            out_specs=per_b((1, D_NSA)),
        ),
        out_shape=jax.ShapeDtypeStruct((b, 1, D_NSA), F32),
        compiler_params=_cparams("arbitrary"),
        name="nsa_step_selected",
    )(idx, page_table, q16, *([cache_t] * n_blk), new_slc, o_cmp, o_win, br)


def _out_kernel(*refs, mul_ga):
    if mul_ga:
        x_ref, gate_ref, ya_ref, ga_ref, oret_ref, gr_ref, gret_ref, w_ref, y_ref = refs
        ya = (ya_ref[...] * ga_ref[...]).astype(BF16)
    else:
        x_ref, gate_ref, ya_ref, oret_ref, gr_ref, gret_ref, w_ref, y_ref = refs
        ya = ya_ref[...]
    o = oret_ref[...]
    parts = []
    for h in range(N_RET_HEADS):
        t = o[:, h * DV_RET:(h + 1) * DV_RET]
        ms = jnp.mean(t * t, axis=-1, keepdims=True)
        parts.append(t * lax.rsqrt(ms + EPS) * gret_ref[...])
    yr = (jnp.concatenate(parts, axis=-1) * gr_ref[...]).astype(BF16)
    y_ref[...] = x_ref[...] + gate_ref[0] * _dot(jnp.concatenate([ya, yr], axis=-1), w_ref[...])


def _output(x2, gate, ya, ga, o_ret, gr, g_ret, w_out, *, n_batch, per_row_mod):
    m = x2.shape[0]
    seq = m // n_batch
    tm = min(seq, 512)
    nt = seq // tm
    if per_row_mod:
        gate_spec = pl.BlockSpec((1, tm, D_MODEL), lambda i: (0, i, 0))
    else:
        gate_spec = pl.BlockSpec((1, 1, D_MODEL), lambda i: (i // nt, 0, 0))

    def rows_spec(width):
        return pl.BlockSpec((tm, width), lambda i: (i, 0))

    mul_ga = ga is not None
    args = [x2, gate, ya] + ([ga] if mul_ga else []) + [o_ret, gr, g_ret.reshape(1, DV_RET), w_out.astype(BF16)]
    in_specs = [rows_spec(D_MODEL), gate_spec, rows_spec(D_NSA)] + ([rows_spec(D_NSA)] if mul_ga else [])
    in_specs += [rows_spec(D_RET), rows_spec(D_RET), pl.BlockSpec((1, DV_RET), lambda i: (0, 0)),
                 pl.BlockSpec((D_MODEL, D_MODEL), lambda i: (0, 0))]
    return pl.pallas_call(
        functools.partial(_out_kernel, mul_ga=mul_ga),
        grid=(m // tm,),
        in_specs=in_specs,
        out_specs=rows_spec(D_MODEL),
        out_shape=jax.ShapeDtypeStruct((m, D_MODEL), F32),
        compiler_params=_cparams("arbitrary"),
        name="output_projection",
    )(*args)


def kernel(x_prompt, x_sample, c_prompt, c_sample, cache_cmp, cache_slc, state_win, state_ret, page_table, g_norm, w_ada, b_ada, w_in, g_q, g_kc, g_ks, g_kw, pe_ck, w_ck1, w_ck2, pe_cv, w_cv1, w_cv2, g_ret, w_out):
    assert g_norm.shape[0] == 1, "one layer"
    bp, seq, _ = x_prompt.shape
    bs, dec_seq, _ = x_sample.shape
    assert dec_seq == 1
    n_pages = page_table.shape[1]
    past = n_pages * PAGE_SIZE
    assert state_win.shape[2] == WINDOW and past >= WINDOW

    w_proj, gq_slots, gks2, gkw2 = _prep_proj_weights(w_in[0], g_q[0], g_ks[0], g_kw[0])
    cmp_weights = _prep_compress_weights(pe_ck[0], w_ck1[0], w_ck2[0], pe_cv[0], w_cv1[0], w_cv2[0], g_kc[0])

    pad = (-(bp + bs)) % 8
    c_all = jnp.concatenate([c_prompt, c_sample, jnp.zeros((pad, D_MODEL), F32)], axis=0)
    mod = _modulation(c_all, w_ada[0], b_ada[0])
    shift_p, scale_p, gate_p = (mod[:bp, k * D_MODEL:(k + 1) * D_MODEL].reshape(bp, 1, D_MODEL) for k in range(3))
    shift_s, scale_s, gate_s = (mod[bp:bp + bs, k * D_MODEL:(k + 1) * D_MODEL].reshape(1, bs, D_MODEL)
                                for k in range(3))

    xp2 = x_prompt.reshape(bp * seq, D_MODEL)
    cos_p, sin_p = _rope_tables(seq, 0, 1)
    (q_p, cmp_t, slc_t, slcb_p, win_t, winb_p, br_p, ga_p, rq_p, rk_p, rv_p, gr_p) = _projection(
        xp2, scale_p, shift_p, g_norm[0], w_proj, gq_slots, gks2, gkw2, cos_p, sin_p, n_batch=bp, per_row_mod=False,
        pos_minor=True)

    kc_p, vc_p = _compress(cmp_t.reshape(bp, 2, N_KV, HEAD_DIM, seq), None, jnp.zeros((bp, 1, KV_ROW), F32),
                           cmp_weights, n_batch=bp, n_pages=seq // PAGE_SIZE)
    ocmp_p, bias_p = _nsa_select(q_p, kc_p, vc_p, n_batch=bp, seq=seq)
    ya_p = _nsa_prompt(q_p, ocmp_p, bias_p, slcb_p, winb_p, br_p, ga_p, n_batch=bp, seq=seq)
    oret_p, sret_p = _retention_prompt(rq_p, rk_p, rv_p, n_batch=bp, seq=seq)
    y_p = _output(xp2, gate_p, ya_p, None, oret_p.reshape(bp * seq, D_RET), gr_p, g_ret[0], w_out[0],
                  n_batch=bp, per_row_mod=False)

    xs2 = x_sample.reshape(bs, D_MODEL)
    cos_s, sin_s = _rope_tables(bs, past, 0)
    (q_s, cmp_s, slc_s, _, win_s, _, br_s, ga_s, rq_s, rk_s, rv_s, gr_s) = _projection(
        xs2, scale_s, shift_s, g_norm[0], w_proj, gq_slots, gks2, gkw2, cos_s, sin_s, n_batch=1, per_row_mod=True,
        pos_minor=False)

    n_chunk_cache = past // CMP_STRIDE
    n_cmp_s = n_chunk_cache
    n_sel_s = past // SEL_BLOCK + 1
    nsp = ((n_sel_s + LANES - 1) // LANES) * LANES
    to_native = (0, 2, 3, 4, 1)
    kc_s, vc_s = _compress(jnp.transpose(cache_cmp[0], to_native), page_table, cmp_s.reshape(bs, 1, KV_ROW),
                           cmp_weights, n_batch=bs, n_pages=n_pages)

    q16 = jnp.pad(q_s[0].transpose(1, 0, 2), ((0, 0), (0, 16 - N_HEADS), (0, 0)))
    mimp_s = _importance_matrix(n_cmp_s, n_cmp_s, n_sel_s, nsp)
    o_cmp_s, o_win_s, idx_s = _nsa_step1(q16, kc_s, vc_s, jnp.transpose(state_win[0], to_native),
                                         win_s.reshape(bs, 1, KV_ROW), mimp_s, t=past, n_cmp=n_cmp_s,
                                         n_sel=n_sel_s)
    k_eff = min(TOP_N, n_sel_s)
    idx_flat = idx_s[:, :N_KV, :k_eff].reshape(bs, N_KV * k_eff)
    o_nsa_s = _nsa_step2(idx_flat, page_table, q16, jnp.transpose(cache_slc[0], to_native),
                         slc_s.reshape(bs, 1, KV_ROW), o_cmp_s, o_win_s, br_s.reshape(bs, 1, LANES),
                         n_cache_blocks=past // SEL_BLOCK, k_eff=k_eff)

    oret_s, sret_s = _retention_step(rq_s, rk_s, rv_s, state_ret[0])
    y_s = _output(xs2, gate_s, o_nsa_s.reshape(bs, D_NSA), ga_s, oret_s.reshape(bs, D_RET), gr_s, g_ret[0], w_out[0],
                  n_batch=1, per_row_mod=True)

    def kv_rows(a, b, l):
        return a.reshape(1, b, l, 2, N_KV, HEAD_DIM)

    def kv_rows_t(a, b, l):
        return jnp.transpose(a.reshape(b, 2, N_KV, HEAD_DIM, l), (0, 4, 1, 2, 3))[None]

    keep_p = min(WINDOW, seq)
    s_win = jnp.concatenate([state_win[0][:, 1:].reshape(bs, WINDOW - 1, KV_ROW), win_s.reshape(bs, 1, KV_ROW)],
                            axis=1)
    return (y_p.reshape(bp, seq, D_MODEL), y_s.reshape(bs, 1, D_MODEL),
            kv_rows_t(cmp_t, bp, seq), kv_rows_t(slc_t, bp, seq),
            kv_rows_t(win_t[:, :, seq - keep_p:], bp, keep_p), sret_p[None],
            kv_rows(cmp_s, bs, 1), kv_rows(slc_s, bs, 1), kv_rows(s_win, bs, WINDOW), sret_s[None])
```

```python
import functools
import math

import jax
import jax.numpy as jnp
import numpy as np
from jax import lax
from jax.experimental import pallas as pl
from jax.experimental.pallas import tpu as pltpu

D_MODEL = 1024
HEAD_DIM = 64
D_NSA = 512
N_HEADS = 8
N_KV = 2
HPG = 4
CMP_BLOCK = 32
CMP_STRIDE = 16
SEL_BLOCK = 64
TOP_N = 16
WINDOW = 512
D_RET = 512
N_RET_HEADS = 4
DV_RET = 128
DK_RET = 64
RET_CHUNK = 128
PAGE_SIZE = 128
ROPE_BASE = 10000.0
EPS = 1e-6
NEG = -1e30
FORCE = 1e4
BELOW_NEG = -2e38
TAKEN = -3e38

KV_ROW = 2 * N_KV * HEAD_DIM
LANES = 128
VMEM_LIMIT_BYTES = 56 * 1024 * 1024

F32 = jnp.float32
BF16 = jnp.bfloat16

O_Q = 0
O_CMP = O_Q + D_NSA
O_SLC = O_CMP + KV_ROW
O_WIN = O_SLC + KV_ROW
O_BR = O_WIN + KV_ROW
O_GA = O_BR + LANES
O_RQ = O_GA + D_NSA
O_RK = O_RQ + N_RET_HEADS * DK_RET
O_RV = O_RK + N_RET_HEADS * DK_RET
O_GR = O_RV + D_RET
D_IN_PAD = O_GR + D_RET


def _cparams(*sem):
    return pltpu.CompilerParams(dimension_semantics=sem, vmem_limit_bytes=VMEM_LIMIT_BYTES)


def _dot(a, b):
    return jnp.dot(a, b, preferred_element_type=F32)


def _dot_nt(a, b):
    return lax.dot_general(a, b, (((1,), (1,)), ((), ())), preferred_element_type=F32)


def _dot_tn(a, b):
    return lax.dot_general(a, b, (((0,), (0,)), ((), ())), preferred_element_type=F32)


def _silu(x):
    return x * jax.nn.sigmoid(x)


def _split3_dot(p, m):
    p1 = p.astype(BF16)
    r1 = p - p1.astype(F32)
    p2 = r1.astype(BF16)
    p3 = (r1 - p2.astype(F32)).astype(BF16)
    return _dot(p1, m) + _dot(p2, m) + _dot(p3, m)


def _pair_rmsnorm(k, gain2):
    lo = lax.broadcasted_iota(jnp.int32, k.shape, 1) < HEAD_DIM
    sq = k * k
    s0 = jnp.sum(jnp.where(lo, sq, 0.0), axis=-1, keepdims=True)
    s1 = jnp.sum(jnp.where(lo, 0.0, sq), axis=-1, keepdims=True)
    ms = jnp.where(lo, s0, s1) * (1.0 / HEAD_DIM)
    return k * lax.rsqrt(ms + EPS) * gain2


def _mod_kernel(c_ref, w_ref, b_ref, o_ref):
    o_ref[...] = _dot(_silu(c_ref[...]).astype(BF16), w_ref[...]) + b_ref[...]


def _modulation(c_all, w_ada, b_ada):
    rows = c_all.shape[0]
    n = w_ada.shape[1]
    tn = 1024
    return pl.pallas_call(
        _mod_kernel,
        grid=(n // tn,),
        in_specs=[pl.BlockSpec((rows, D_MODEL), lambda j: (0, 0)),
                  pl.BlockSpec((D_MODEL, tn), lambda j: (0, j)),
                  pl.BlockSpec((1, tn), lambda j: (0, j))],
        out_specs=pl.BlockSpec((rows, tn), lambda j: (0, j)),
        out_shape=jax.ShapeDtypeStruct((rows, n), F32),
        compiler_params=_cparams("arbitrary"),
        name="adaln_modulation",
    )(c_all, w_ada.astype(BF16), b_ada.reshape(1, n))


def _rope_kernel(freq_ref, cos_ref, sin_ref, *, base, stride, rows):
    shape = (rows, LANES)
    r = lax.broadcasted_iota(jnp.int32, shape, 0)
    pos = (base + stride * (pl.program_id(0) * rows + r)).astype(F32)
    ang = pos * freq_ref[...]
    lane = lax.broadcasted_iota(jnp.int32, shape, 1)
    sign = jnp.where((lane & (DK_RET - 1)) < DK_RET // 2, -1.0, 1.0)
    cos_ref[...] = jnp.cos(ang)
    sin_ref[...] = jnp.sin(ang) * sign


def _rope_tables(n_rows, base, stride):
    half = DK_RET // 2
    freqs = ROPE_BASE ** (-jnp.arange(half, dtype=F32) / half)
    freq_row = jnp.tile(freqs, LANES // half).reshape(1, LANES)
    rows = min(n_rows, 512)
    return pl.pallas_call(
        functools.partial(_rope_kernel, base=base, stride=stride, rows=rows),
        grid=(n_rows // rows,),
        in_specs=[pl.BlockSpec((1, LANES), lambda i: (0, 0))],
        out_specs=[pl.BlockSpec((rows, LANES), lambda i: (i, 0))] * 2,
        out_shape=[jax.ShapeDtypeStruct((n_rows, LANES), F32)] * 2,
        compiler_params=_cparams("arbitrary"),
        name="rope_tables",
    )(freq_row)


def _proj_kernel(x_ref, scale_ref, shift_ref, gn_ref, w_ref, gq_ref, gks_ref, gkw_ref, cos_ref, sin_ref,
                 q_ref, cmp_ref, slc_ref, slcb_ref, win_ref, winb_ref, br_ref, ga_ref, rq_ref, rk_ref,
                 rv_ref, gr_ref, *, pos_minor):
    x = x_ref[...]
    ms = jnp.mean(x * x, axis=-1, keepdims=True)
    xn = x * lax.rsqrt(ms + EPS) * gn_ref[...]
    hb = (xn * (1.0 + scale_ref[0]) + shift_ref[0]).astype(BF16)

    def seg(start, width):
        return _dot(hb, w_ref[:, start:start + width])

    yq = seg(O_Q, D_NSA)
    lane_lo = lax.broadcasted_iota(jnp.int32, (yq.shape[0], LANES), 1) < HEAD_DIM
    for s in range(N_HEADS):
        t = yq[:, (s // 2) * LANES:(s // 2 + 1) * LANES]
        if s % 2 != s // HPG:
            t = pltpu.roll(t, HEAD_DIM, axis=1)
        t = jnp.where(lane_lo == (s < HPG), t, 0.0)
        ms = jnp.sum(t * t, axis=-1, keepdims=True) * (1.0 / HEAD_DIM)
        q_ref[0, s] = (t * lax.rsqrt(ms + EPS) * gq_ref[s]).astype(BF16)

    y_cmp = seg(O_CMP, KV_ROW)
    y = seg(O_SLC, KV_ROW)
    y_slc = jnp.concatenate([_pair_rmsnorm(y[:, :LANES], gks_ref[...]), y[:, LANES:]], axis=-1)
    slcb_ref[...] = y_slc.astype(BF16)
    y = seg(O_WIN, KV_ROW)
    y_win = jnp.concatenate([_pair_rmsnorm(y[:, :LANES], gkw_ref[...]), y[:, LANES:]], axis=-1)
    winb_ref[...] = y_win.astype(BF16)
    for ref, rows in ((cmp_ref, y_cmp), (slc_ref, y_slc), (win_ref, y_win)):
        if pos_minor:
            ref[0] = rows.T
        else:
            ref[...] = rows

    br_ref[...] = jax.nn.sigmoid(seg(O_BR, LANES))
    ga_ref[...] = _silu(seg(O_GA, D_NSA))

    first_half = (lax.broadcasted_iota(jnp.int32, (yq.shape[0], LANES), 1) & (DK_RET - 1)) < DK_RET // 2

    def rotary(y):
        tiles = []
        for k in range(y.shape[1] // LANES):
            t = y[:, k * LANES:(k + 1) * LANES]
            swapped = jnp.where(first_half, pltpu.roll(t, LANES - DK_RET // 2, axis=1),
                                pltpu.roll(t, DK_RET // 2, axis=1))
            tiles.append(t * cos_ref[...] + swapped * sin_ref[...])
        return jnp.concatenate(tiles, axis=-1)

    nrot = N_RET_HEADS * DK_RET
    rq_ref[...] = rotary(seg(O_RQ, nrot)).astype(BF16)
    rk_ref[...] = (rotary(seg(O_RK, nrot)) * DK_RET ** -0.5).astype(BF16)
    rv_ref[...] = seg(O_RV, D_RET).astype(BF16)
    gr_ref[...] = _silu(seg(O_GR, D_RET))


def _prep_proj_weights(w_in, g_q, g_ks, g_kw):
    n_br = 3 * N_HEADS
    w = jnp.concatenate([w_in[:, :O_BR], jnp.pad(w_in[:, O_BR:O_BR + n_br], ((0, 0), (0, LANES - n_br))),
                         w_in[:, O_BR + n_br:]], axis=1).astype(BF16)
    assert w.shape[1] == D_IN_PAD
    gz = jnp.zeros((HEAD_DIM,), F32)
    gq = g_q * (HEAD_DIM ** -0.5 * math.log2(math.e))
    gq_slots = jnp.stack([jnp.concatenate([gq, gz]) if s < HPG else jnp.concatenate([gz, gq])
                          for s in range(N_HEADS)]).reshape(N_HEADS, 1, LANES)
    return w, gq_slots, jnp.tile(g_ks, 2).reshape(1, LANES), jnp.tile(g_kw, 2).reshape(1, LANES)


def _projection(x2, scale, shift, g_norm, w, gq_slots, gks2, gkw2, cos_t, sin_t, *, n_batch, per_row_mod,
                pos_minor):
    m = x2.shape[0]
    seq = m // n_batch
    tm = min(seq, 256)
    nt = seq // tm
    if per_row_mod:
        mod_spec = pl.BlockSpec((1, tm, D_MODEL), lambda i: (0, i, 0))
    else:
        mod_spec = pl.BlockSpec((1, 1, D_MODEL), lambda i: (i // nt, 0, 0))

    def rows_spec(width):
        return pl.BlockSpec((tm, width), lambda i: (i, 0))

    def const_spec(shape):
        return pl.BlockSpec(shape, lambda i: (0,) * len(shape))

    kv_out = "kv rows"
    widths = [(KV_ROW, kv_out), (KV_ROW, kv_out), (KV_ROW, BF16), (KV_ROW, kv_out), (KV_ROW, BF16), (LANES, F32),
              (D_NSA, F32), (N_RET_HEADS * DK_RET, BF16), (N_RET_HEADS * DK_RET, BF16), (D_RET, BF16),
              (D_RET, F32)]
    out_shape = [jax.ShapeDtypeStruct((n_batch, N_HEADS, seq, LANES), BF16)]
    out_specs = [pl.BlockSpec((1, N_HEADS, tm, LANES), lambda i: (i // nt, 0, i % nt, 0))]
    for width, dt in widths:
        if pos_minor and dt is kv_out:
            out_shape.append(jax.ShapeDtypeStruct((n_batch, KV_ROW, seq), F32))
            out_specs.append(pl.BlockSpec((1, KV_ROW, tm), lambda i: (i // nt, 0, i % nt)))
        else:
            out_shape.append(jax.ShapeDtypeStruct((m, width), F32 if dt is kv_out else dt))
            out_specs.append(rows_spec(width))
    return pl.pallas_call(
        functools.partial(_proj_kernel, pos_minor=pos_minor),
        grid=(m // tm,),
        in_specs=[rows_spec(D_MODEL), mod_spec, mod_spec, const_spec((1, D_MODEL)),
                  const_spec((D_MODEL, D_IN_PAD)), const_spec((N_HEADS, 1, LANES)), const_spec((1, LANES)),
                  const_spec((1, LANES)), pl.BlockSpec((tm, LANES), lambda i: (i % nt, 0)),
                  pl.BlockSpec((tm, LANES), lambda i: (i % nt, 0))],
        out_specs=out_specs,
        out_shape=out_shape,
        compiler_params=_cparams("arbitrary"),
        name="input_projection",
    )(x2, scale, shift, g_norm.reshape(1, D_MODEL), w, gq_slots, gks2, gkw2, cos_t, sin_t)


def _compress_kernel(pt_ref, *refs, n_in):
    x_refs, next_ref = refs[:n_in], refs[n_in]
    new_ref, perm_ref, w1_ref, pe_ref, w2_ref, gkc_ref, kc_ref, vc_ref = refs[n_in + 1:]
    cpp = PAGE_SIZE // CMP_STRIDE
    n = n_in * cpp
    last_step = pl.program_id(1) == pl.num_programs(1) - 1
    perm = perm_ref[...]

    def chunk_rows(page_ref):
        return _dot_nt(perm, page_ref[0].reshape(KV_ROW, PAGE_SIZE).astype(BF16))

    def flat_chunks(moved, kv):
        return jnp.concatenate(
            [jnp.concatenate([r[cpp * l:cpp * (l + 1), kv * LANES:(kv + 1) * LANES] for l in range(CMP_STRIDE)],
                             axis=1) for r in moved], axis=0).astype(BF16)

    moved = [chunk_rows(r) for r in x_refs]
    moved_next = [chunk_rows(next_ref)]
    new8 = jnp.broadcast_to(new_ref[0], (8, KV_ROW)).astype(BF16)
    zeros = jnp.zeros((8, (CMP_STRIDE - 1) * LANES), BF16)
    row = lax.broadcasted_iota(jnp.int32, (n, LANES), 0)
    pre = []
    for kv in range(2):
        new_chunk = jnp.concatenate([new8[:, kv * LANES:(kv + 1) * LANES], zeros], axis=1)
        a_all = _dot(jnp.concatenate([flat_chunks(moved, kv), flat_chunks(moved_next, kv), pe_ref[kv], new_chunk],
                                     axis=0), w1_ref[kv])
        a = a_all[:n]
        a_next = a_all[n:n + 1, LANES:]
        bias = a_all[n + cpp:n + cpp + 1, :LANES] + a_all[n + cpp + 1:n + cpp + 2, LANES:]
        a_new = a_all[n + cpp + 8:n + cpp + 9, LANES:]
        hi = pltpu.roll(a[:, LANES:], n - 1, axis=0)
        hi = jnp.where(row == n - 1, jnp.where(last_step, a_new, a_next), hi)
        pre.append(a[:, :LANES] + hi + bias)
    out = _dot(_silu(jnp.concatenate(pre, axis=1)).astype(BF16), w2_ref[...])
    kc_ref[0] = _pair_rmsnorm(out[:, :LANES], gkc_ref[...]).astype(BF16)
    vc_ref[0] = out[:, LANES:].astype(BF16)


def _compress(rows_t, page_table, new_rows, weights, *, n_batch, n_pages):
    w1kv, w2x, pe8kv, gkc2 = weights
    b = n_batch
    cpp = PAGE_SIZE // CMP_STRIDE
    p = math.gcd(n_pages, 128)
    row = jnp.arange(PAGE_SIZE, dtype=jnp.int32)
    perm = ((row % cpp) * CMP_STRIDE + row // cpp).reshape(PAGE_SIZE, 1) == row.reshape(1, PAGE_SIZE)
    page_shape = (1,) + rows_t.shape[1:4] + (PAGE_SIZE,)

    def page_spec(page_of_step):
        if page_table is None:
            return pl.BlockSpec(page_shape, lambda i, j, pt: (i, 0, 0, 0, page_of_step(j)))
        return pl.BlockSpec(page_shape, lambda i, j, pt: (pt[i, page_of_step(j)], 0, 0, 0, 0))

    in_specs = [page_spec(lambda j, k=k: j * p + k) for k in range(p)]
    in_specs.append(page_spec(lambda j: jnp.minimum((j + 1) * p, n_pages - 1)))

    def const_spec(shape):
        return pl.BlockSpec(shape, lambda i, j, pt: (0,) * len(shape))

    in_specs += [pl.BlockSpec((1, 1, KV_ROW), lambda i, j, pt: (i, 0, 0)), const_spec((PAGE_SIZE, PAGE_SIZE)),
                 const_spec(w1kv.shape), const_spec(pe8kv.shape), const_spec(w2x.shape), const_spec((1, LANES))]
    if page_table is None:
        page_table = jnp.zeros((1, 1), jnp.int32)
    return pl.pallas_call(
        functools.partial(_compress_kernel, n_in=p),
        grid_spec=pltpu.PrefetchScalarGridSpec(
            num_scalar_prefetch=1,
            grid=(b, n_pages // p),
            in_specs=in_specs,
            out_specs=[pl.BlockSpec((1, p * cpp, LANES), lambda i, j, pt: (i, j, 0))] * 2,
        ),
        out_shape=[jax.ShapeDtypeStruct((b, n_pages * cpp, LANES), BF16)] * 2,
        compiler_params=_cparams("arbitrary", "arbitrary"),
        name="compress",
    )(page_table, *([rows_t] * (p + 1)), new_rows, perm.astype(BF16), w1kv, pe8kv, w2x, gkc2)


def _prep_compress_weights(pe_ck, w_ck1, w_ck2, pe_cv, w_cv1, w_cv2, g_kc):
    eye = jnp.eye(N_KV, dtype=F32)
    w1 = jnp.stack([w_ck1, w_cv1]).reshape(2, 2, CMP_STRIDE, HEAD_DIM, HEAD_DIM)
    w1kv = jnp.einsum('khldf,gp->klgdhpf', w1, eye).reshape(2, CMP_STRIDE * LANES, KV_ROW).astype(BF16)
    w2 = jnp.stack([w_ck2, w_cv2])
    w2x = jnp.einsum('kfd,kq,gp->kgfqpd', w2, eye, eye).reshape(KV_ROW, KV_ROW).astype(BF16)
    pe = jnp.stack([pe_ck, pe_cv]).reshape(2, 2, CMP_STRIDE, 1, HEAD_DIM)
    pe_rows = jnp.broadcast_to(pe, (2, 2, CMP_STRIDE, N_KV, HEAD_DIM)).reshape(2, 2, CMP_STRIDE * LANES)
    pe8kv = jnp.zeros((2, 8, CMP_STRIDE * LANES), F32).at[:, :2].set(pe_rows).astype(BF16)
    return w1kv, w2x, pe8kv, jnp.tile(g_kc, 2).reshape(1, LANES)


def _importance_matrix(n_cmp_pad, n_cmp, n_sel, n_sel_pad):
    ratio = SEL_BLOCK // CMP_STRIDE
    span = CMP_BLOCK // CMP_STRIDE
    m = np.zeros((n_cmp_pad, n_sel_pad), np.float32)
    for j in range(n_sel):
        for r in range(ratio):
            for s in range(span):
                n = ratio * j + r - s
                if 0 <= n < n_cmp:
                    m[n, j] += 1.0
    return jnp.asarray(m, BF16)


def _nsa_select_kernel(q_ref, kc_ref, vc_ref, mimp_ref, oc_ref, bias_ref, *, tq, n_cmp, n_sel):
    qs = pl.program_id(1) * tq
    rows = HPG * tq
    ncp = kc_ref.shape[1]

    def q_pos(width):
        return qs + lax.broadcasted_iota(jnp.int32, (tq, width), 0)

    def per_head(x):
        return jnp.concatenate([x] * HPG, axis=0)

    col = lax.broadcasted_iota(jnp.int32, (tq, ncp), 1)
    cmp_mask = per_head(jnp.where((col * CMP_STRIDE + (CMP_BLOCK - 1) <= q_pos(ncp)) & (col < n_cmp), 0.0, NEG))
    cmp_any = per_head(jnp.where(q_pos(1) >= CMP_BLOCK - 1, 1.0, 0.0))
    blk = lax.broadcasted_iota(jnp.int32, (tq, LANES), 1)
    t = q_pos(LANES)
    jt = t // SEL_BLOCK
    forced = (blk == 0) | (blk == jt) | (blk == jt - 1)
    valid = blk * SEL_BLOCK <= t
    n_rank = ((n_sel + 7) // 8) * 8
    j_row = lax.broadcasted_iota(jnp.int32, (n_rank, tq), 0)

    for g in range(N_KV):
        q = q_ref[0, HPG * g:HPG * (g + 1)].reshape(rows, LANES)
        s = _dot_nt(q, kc_ref[0]) + cmp_mask
        e = jnp.exp2(s - jnp.max(s, axis=-1, keepdims=True))
        p = e * (cmp_any / jnp.sum(e, axis=-1, keepdims=True))
        oc_ref[0, HPG * g:HPG * (g + 1)] = _dot(p.astype(BF16), vc_ref[0]).astype(BF16).reshape(HPG, tq, LANES)
        p_sum = p[0:tq] + p[tq:2 * tq] + p[2 * tq:3 * tq] + p[3 * tq:4 * tq]
        imp = _split3_dot(p_sum, mimp_ref[...])

        score = jnp.where(valid, jnp.where(forced, FORCE, imp), NEG)
        score = jnp.where(blk < n_sel, score, BELOW_NEG)
        left = score.T[:n_rank]
        sel_t = jnp.zeros((n_rank, tq), F32)
        for _ in range(min(TOP_N, n_sel)):
            top = jnp.max(left, axis=0, keepdims=True)
            first = jnp.min(jnp.where(left == top, j_row, n_rank), axis=0, keepdims=True)
            taken = j_row == first
            sel_t = jnp.where(taken, 1.0, sel_t)
            left = jnp.where(taken, TAKEN, left)
        if n_rank < LANES:
            sel_t = jnp.concatenate([sel_t, jnp.zeros((LANES - n_rank, tq), F32)], axis=0)
        bias_ref[0, g] = jnp.where(sel_t.T > 0.5, 0.0, NEG).astype(BF16)


def _nsa_select(q_slots, kc, vc, *, n_batch, seq):
    tq = min(512, seq)
    n_cmp = seq // CMP_STRIDE - (CMP_BLOCK // CMP_STRIDE) + 1
    n_sel = seq // SEL_BLOCK
    assert n_sel <= LANES
    ncp = kc.shape[1]
    mimp = _importance_matrix(ncp, n_cmp, n_sel, LANES)
    return pl.pallas_call(
        functools.partial(_nsa_select_kernel, tq=tq, n_cmp=n_cmp, n_sel=n_sel),
        grid=(n_batch, seq // tq),
        in_specs=[pl.BlockSpec((1, N_HEADS, tq, LANES), lambda b, i: (b, 0, i, 0)),
                  pl.BlockSpec((1, ncp, LANES), lambda b, i: (b, 0, 0)),
                  pl.BlockSpec((1, ncp, LANES), lambda b, i: (b, 0, 0)),
                  pl.BlockSpec(mimp.shape, lambda b, i: (0, 0))],
        out_specs=[pl.BlockSpec((1, N_HEADS, tq, LANES), lambda b, i: (b, 0, i, 0)),
                   pl.BlockSpec((1, N_KV, tq, LANES), lambda b, i: (b, 0, i, 0))],
        out_shape=[jax.ShapeDtypeStruct((n_batch, N_HEADS, seq, LANES), BF16),
                   jax.ShapeDtypeStruct((n_batch, N_KV, seq, LANES), BF16)],
        compiler_params=_cparams("arbitrary", "arbitrary"),
        name="nsa_select",
    )(q_slots, kc, vc, mimp)


def _nsa_prompt_kernel(q_ref, oc_ref, bias_ref, slc_ref, win_ref, br_ref, ga_ref, expt_ref, o_ref,
                       m_sc, acc_sc, *, tq, tk):
    qs = pl.program_id(1) * tq
    rows = HPG * tq
    wlen = WINDOW + tq
    br = br_ref[...]
    lane_lo = lax.broadcasted_iota(jnp.int32, (tq, LANES), 1) < HEAD_DIM

    def q_pos(width):
        return qs + lax.broadcasted_iota(jnp.int32, (tq, width), 0)

    def per_head(x):
        return jnp.concatenate([x] * HPG, axis=0)

    w0 = pl.multiple_of(jnp.maximum(qs - WINDOW, 0), tq)
    kpos = w0 + lax.broadcasted_iota(jnp.int32, (tq, wlen), 1)
    win_mask = per_head(jnp.where((kpos <= q_pos(wlen)) & (kpos > q_pos(wlen) - WINDOW), 0.0, NEG))
    kt_diag = qs // tk
    kpos = kt_diag * tk + lax.broadcasted_iota(jnp.int32, (tq, tk), 1)
    diag_mask = per_head(jnp.where(kpos <= q_pos(tk), 0.0, NEG))

    def with_ones(v, g):
        own = (lax.broadcasted_iota(jnp.int32, v.shape, 1) // HEAD_DIM) == g
        return jnp.where(own, v, jnp.ones_like(v))

    def normalised(acc):
        return acc * (1.0 / pltpu.roll(acc, HEAD_DIM, axis=1))

    groups = range(N_KV)
    q = [q_ref[0, HPG * g:HPG * (g + 1)].reshape(rows, LANES) for g in groups]
    q_ext = jnp.concatenate([jnp.concatenate([q[g], per_head(bias_ref[0, g])], axis=1) for g in groups], axis=0)

    m_sc[...] = jnp.full((N_KV * rows, LANES), NEG, F32)
    acc_sc[...] = jnp.zeros((N_KV * rows, LANES), F32)

    def key_tile(kt, causal):
        k0 = pl.multiple_of(kt * tk, tk)
        k_ext = jnp.concatenate([slc_ref[0, pl.ds(k0, tk), 0:LANES], expt_ref[kt]], axis=1)
        v = slc_ref[0, pl.ds(k0, tk), LANES:2 * LANES]
        s_all = _dot_nt(q_ext, k_ext)
        for g in groups:
            rs = slice(g * rows, (g + 1) * rows)
            s = s_all[rs]
            if causal:
                s = s + diag_mask
            m_prev = m_sc[rs]
            m_next = jnp.maximum(m_prev, jnp.max(s, axis=-1, keepdims=True))
            p = jnp.exp2(s - jnp.concatenate([m_next] * (tk // LANES), axis=1))
            acc_sc[rs] = jnp.exp2(m_prev - m_next) * acc_sc[rs] + _dot(p.astype(BF16), with_ones(v, g))
            m_sc[rs] = m_next

    def full_tile(kt, carry):
        key_tile(kt, False)
        return carry

    lax.fori_loop(0, kt_diag, full_tile, 0)
    key_tile(kt_diag, True)

    k = win_ref[0, pl.ds(w0, wlen), 0:LANES]
    v = win_ref[0, pl.ds(w0, wlen), LANES:2 * LANES]
    s_all = _dot_nt(jnp.concatenate(q, axis=0), k)
    for g in groups:
        rs = slice(g * rows, (g + 1) * rows)
        s = s_all[rs] + win_mask
        e = jnp.exp2(s - jnp.max(s, axis=-1, keepdims=True))
        o_win = normalised(_dot(e.astype(BF16), with_ones(v, g)))
        o_slc = normalised(acc_sc[rs])

        heads = []
        for h in range(HPG):
            hs = slice(h * tq, (h + 1) * tq)
            c = 3 * (HPG * g + h)
            heads.append(br[:, c:c + 1] * oc_ref[0, HPG * g + h].astype(F32) + br[:, c + 1:c + 2] * o_slc[hs]
                         + br[:, c + 2:c + 3] * o_win[hs])
        for hh in range(HPG // 2):
            even, odd = heads[2 * hh], heads[2 * hh + 1]
            if g == 0:
                odd = pltpu.roll(odd, HEAD_DIM, axis=1)
            else:
                even = pltpu.roll(even, HEAD_DIM, axis=1)
            c0 = (2 * g + hh) * LANES
            o_ref[:, c0:c0 + LANES] = (jnp.where(lane_lo, even, odd) * ga_ref[:, c0:c0 + LANES]).astype(BF16)


def _nsa_prompt(q_slots, o_cmp, block_bias, slc_b, win_b, br, ga, *, n_batch, seq):
    tq = min(256, seq)
    tk = min(512, seq)
    assert seq % tk == 0 and tk % tq == 0 and seq >= WINDOW + tq
    key_blk = (jnp.arange(seq, dtype=jnp.int32) // SEL_BLOCK).reshape(seq // tk, tk, 1)
    expand = (key_blk == jnp.arange(LANES, dtype=jnp.int32).reshape(1, 1, LANES)).astype(BF16)
    nq = seq // tq
    rows = HPG * tq
    return pl.pallas_call(
        functools.partial(_nsa_prompt_kernel, tq=tq, tk=tk),
        grid=(n_batch, nq),
        in_specs=[pl.BlockSpec((1, N_HEADS, tq, LANES), lambda b, i: (b, 0, i, 0)),
                  pl.BlockSpec((1, N_HEADS, tq, LANES), lambda b, i: (b, 0, i, 0)),
                  pl.BlockSpec((1, N_KV, tq, LANES), lambda b, i: (b, 0, i, 0)),
                  pl.BlockSpec((1, seq, KV_ROW), lambda b, i: (b, 0, 0)),
                  pl.BlockSpec((1, seq, KV_ROW), lambda b, i: (b, 0, 0)),
                  pl.BlockSpec((tq, LANES), lambda b, i: (b * nq + i, 0)),
                  pl.BlockSpec((tq, D_NSA), lambda b, i: (b * nq + i, 0)),
                  pl.BlockSpec(expand.shape, lambda b, i: (0, 0, 0))],
        out_specs=pl.BlockSpec((tq, D_NSA), lambda b, i: (b * nq + i, 0)),
        out_shape=jax.ShapeDtypeStruct((n_batch * seq, D_NSA), BF16),
        scratch_shapes=[pltpu.VMEM((N_KV * rows, LANES), F32)] * 2,
        compiler_params=_cparams("arbitrary", "arbitrary"),
        name="nsa_prompt",
    )(q_slots, o_cmp, block_bias, slc_b.reshape(n_batch, seq, KV_ROW), win_b.reshape(n_batch, seq, KV_ROW), br, ga,
      expand)


def _retention_prompt_kernel(rq_ref, rk_ref, rv_ref, o_ref, s_ref, *, cl):
    @pl.when(pl.program_id(0) == 0)
    def _():
        s_ref[...] = jnp.zeros(s_ref.shape, F32)

    i_pos = lax.broadcasted_iota(jnp.int32, (cl, cl), 0)
    j_pos = lax.broadcasted_iota(jnp.int32, (cl, cl), 1)
    diff = (i_pos - j_pos).astype(F32)
    i_col = lax.broadcasted_iota(jnp.int32, (cl, 1), 0).astype(F32)
    lane_lo = lax.broadcasted_iota(jnp.int32, (cl, LANES), 1) < DK_RET
    for h in range(N_RET_HEADS):
        pair, odd = divmod(h, 2)
        log_g = math.log(1.0 - 2.0 ** (-5.0 - h))
        keep = lane_lo if odd == 0 else jnp.logical_not(lane_lo)
        decay = jnp.where(diff >= 0, jnp.exp(log_g * jnp.maximum(diff, 0.0)), 0.0)
        xi = jnp.exp(log_g * (i_col + 1.0))
        zeta = jnp.exp(log_g * (cl - 1.0 - i_col))
        for b in range(rq_ref.shape[0]):
            q2 = rq_ref[b, :, pair * LANES:(pair + 1) * LANES]
            k2 = rk_ref[b, :, pair * LANES:(pair + 1) * LANES]
            s_pair = jnp.concatenate([s_ref[b, 2 * pair], s_ref[b, 2 * pair + 1]], axis=0)
            qh = jnp.where(keep, q2, jnp.zeros_like(q2))
            v = rv_ref[b, :, h * DV_RET:(h + 1) * DV_RET]
            a = _dot_nt(qh, k2) * decay
            o_ref[b, :, h * DV_RET:(h + 1) * DV_RET] = (_dot(a.astype(BF16), v)
                                                         + _dot(qh, s_pair.astype(BF16)) * xi)
            kz = (k2.astype(F32) * zeta).astype(BF16)
            u = _dot_tn(kz, v)[odd * DK_RET:(odd + 1) * DK_RET]
            s_ref[b, h] = s_ref[b, h] * math.exp(log_g * cl) + u


def _retention_prompt(rq, rk, rv, *, n_batch, seq):
    cl = math.gcd(seq, 2 * RET_CHUNK)
    nrot = N_RET_HEADS * DK_RET
    return pl.pallas_call(
        functools.partial(_retention_prompt_kernel, cl=cl),
        grid=(seq // cl,),
        in_specs=[pl.BlockSpec((n_batch, cl, nrot), lambda i: (0, i, 0)),
                  pl.BlockSpec((n_batch, cl, nrot), lambda i: (0, i, 0)),
                  pl.BlockSpec((n_batch, cl, D_RET), lambda i: (0, i, 0))],
        out_specs=[pl.BlockSpec((n_batch, cl, D_RET), lambda i: (0, i, 0)),
                   pl.BlockSpec((n_batch, N_RET_HEADS, DK_RET, DV_RET), lambda i: (0, 0, 0, 0))],
        out_shape=[jax.ShapeDtypeStruct((n_batch, seq, D_RET), F32),
                   jax.ShapeDtypeStruct((n_batch, N_RET_HEADS, DK_RET, DV_RET), F32)],
        compiler_params=_cparams("arbitrary"),
        name="retention_prompt",
    )(rq.reshape(n_batch, seq, nrot), rk.reshape(n_batch, seq, nrot), rv.reshape(n_batch, seq, D_RET))


def _retention_step_kernel(rq_ref, rk_ref, rv_ref, s0_ref, o_ref, s_ref):
    lane_lo = lax.broadcasted_iota(jnp.int32, (8, LANES), 1) < DK_RET
    diag = (lax.broadcasted_iota(jnp.int32, (LANES, LANES), 0)
            == lax.broadcasted_iota(jnp.int32, (LANES, LANES), 1))
    for pair in range(N_RET_HEADS // 2):
        q2 = jnp.broadcast_to(rq_ref[0, :, pair * LANES:(pair + 1) * LANES], (8, LANES))
        k_row = rk_ref[0, :, pair * LANES:(pair + 1) * LANES]
        k2 = jnp.broadcast_to(k_row, (8, LANES))
        k_col = jnp.sum(jnp.where(diag, jnp.broadcast_to(k_row, (LANES, LANES)), 0.0), axis=-1, keepdims=True)
        s_pair = jnp.concatenate([s0_ref[0, 2 * pair], s0_ref[0, 2 * pair + 1]], axis=0)
        for odd in range(2):
            h = 2 * pair + odd
            g = math.exp(math.log(1.0 - 2.0 ** (-5.0 - h)))
            keep = lane_lo if odd == 0 else jnp.logical_not(lane_lo)
            qh = jnp.where(keep, q2, 0.0)
            v_row = rv_ref[0, :, h * DV_RET:(h + 1) * DV_RET]
            qk = jnp.sum(qh * k2, axis=-1, keepdims=True)
            o = qk * v_row + _dot(qh, s_pair) * g
            o_ref[0, :, h * DV_RET:(h + 1) * DV_RET] = o[0:1]
            u = (k_col * v_row)[odd * DK_RET:(odd + 1) * DK_RET]
            s_ref[0, h] = s0_ref[0, h] * g + u


def _nsa_step1_kernel(q_ref, kc_ref, vc_ref, win_ref, new_ref, mimp_ref, rq_ref, rk_ref, rv_ref, s0_ref,
                      oc_ref, ow_ref, idx_ref, oret_ref, sret_ref, *, t, n_cmp, n_sel):
    _retention_step_kernel(rq_ref, rk_ref, rv_ref, s0_ref, oret_ref, sret_ref)
    q = q_ref[0]
    nq = q.shape[0]
    ncp = kc_ref.shape[1]
    nsp = mimp_ref.shape[1]

    s = _dot_nt(q, kc_ref[0])
    col = lax.broadcasted_iota(jnp.int32, (nq, ncp), 1)
    mask = (col * CMP_STRIDE + (CMP_BLOCK - 1) <= t) & (col < n_cmp)
    s = jnp.where(mask, s, NEG)
    e = jnp.exp2(s - jnp.max(s, axis=-1, keepdims=True))
    p = jnp.where(mask, e, 0.0) * (1.0 / jnp.sum(e, axis=-1, keepdims=True))
    oc_ref[0] = _dot(p.astype(BF16), vc_ref[0])
    imp_rows = _split3_dot(p, mimp_ref[...])

    blk = lax.broadcasted_iota(jnp.int32, (1, nsp), 1)
    jt = t // SEL_BLOCK
    forced = (blk == 0) | (blk == jt) | (blk == jt - 1)
    scores = []
    for g in range(N_KV):
        imp = jnp.sum(imp_rows[HPG * g:HPG * (g + 1)], axis=0, keepdims=True)
        sc = jnp.where(blk * SEL_BLOCK <= t, jnp.where(forced, FORCE, imp), NEG)
        scores.append(jnp.where(blk < n_sel, sc, BELOW_NEG))
    s_rows = jnp.concatenate(scores + [jnp.full((LANES - N_KV, nsp), BELOW_NEG, F32)], axis=0)
    s_cols = s_rows.T
    a_idx = lax.broadcasted_iota(jnp.int32, (nsp, nsp), 0)
    c_idx = lax.broadcasted_iota(jnp.int32, (nsp, nsp), 1)
    rank_lane = lax.broadcasted_iota(jnp.int32, (nsp, LANES), 1).astype(F32)
    a_val = lax.broadcasted_iota(jnp.int32, (nsp, LANES), 0).astype(F32)
    rows_out = []
    for g in range(N_KV):
        s_col = s_cols[:, g:g + 1]
        s_row = s_rows[g:g + 1, :]
        beats = (s_row > s_col) | ((s_row == s_col) & (c_idx < a_idx))
        rank = jnp.sum(jnp.where(beats, 1.0, 0.0), axis=-1, keepdims=True)
        rows_out.append(jnp.sum(jnp.where(rank == rank_lane, a_val, 0.0), axis=0, keepdims=True))
    idx = jnp.concatenate(rows_out + [jnp.zeros((8 - N_KV, LANES), F32)], axis=0)
    idx_ref[0] = idx.astype(jnp.int32)

    wb = win_ref.shape[-1]
    kw_t = win_ref[0, 0].reshape(LANES, wb).astype(BF16)
    vw_t = win_ref[0, 1].reshape(LANES, wb).astype(BF16)
    k_new = new_ref[0, :, 0:LANES]
    v_new = new_ref[0, :, LANES:2 * LANES]
    s = _dot(q, kw_t)
    kpos = t - wb + lax.broadcasted_iota(jnp.int32, (nq, wb), 1)
    s = jnp.where((kpos > t - WINDOW) & (kpos >= 0), s, NEG)
    s_new = jnp.sum(q.astype(F32) * k_new, axis=-1, keepdims=True)
    m = jnp.maximum(jnp.max(s, axis=-1, keepdims=True), s_new)
    e = jnp.exp2(s - m)
    e_new = jnp.exp2(s_new - m)
    ow_ref[0] = ((_dot_nt(e.astype(BF16), vw_t) + e_new * v_new)
                 * (1.0 / (jnp.sum(e, axis=-1, keepdims=True) + e_new)))


def _nsa_step1(q16, kc, vc, state_win_t, new_win, mimp, rq, rk, rv, s0, *, t, n_cmp, n_sel):
    b, nq, _ = q16.shape
    ncp = kc.shape[1]
    nrot = N_RET_HEADS * DK_RET

    def per_b(shape):
        return pl.BlockSpec((1,) + shape, lambda i: (i,) + (0,) * len(shape))

    return pl.pallas_call(
        functools.partial(_nsa_step1_kernel, t=t, n_cmp=n_cmp, n_sel=n_sel),
        grid=(b,),
        in_specs=[per_b((nq, LANES)), per_b((ncp, LANES)), per_b((ncp, LANES)), per_b(state_win_t.shape[1:]),
                  per_b((1, KV_ROW)), pl.BlockSpec(mimp.shape, lambda i: (0, 0)),
                  per_b((1, nrot)), per_b((1, nrot)), per_b((1, D_RET)), per_b(s0.shape[1:])],
        out_specs=[per_b((nq, LANES)), per_b((nq, LANES)), per_b((8, LANES)), per_b((1, D_RET)),
                   per_b(s0.shape[1:])],
        out_shape=[jax.ShapeDtypeStruct((b, nq, LANES), F32), jax.ShapeDtypeStruct((b, nq, LANES), F32),
                   jax.ShapeDtypeStruct((b, 8, LANES), jnp.int32), jax.ShapeDtypeStruct((b, 1, D_RET), F32),
                   jax.ShapeDtypeStruct(s0.shape, F32)],
        compiler_params=_cparams("arbitrary"),
        name="nsa_step_compressed_window",
    )(q16, kc, vc, state_win_t, new_win, mimp, rq.astype(F32).reshape(b, 1, nrot), rk.astype(F32).reshape(b, 1, nrot),
      rv.astype(F32).reshape(b, 1, D_RET), s0)


def _nsa_step2_kernel(idx_sm, pt_sm, q_ref, *refs, n_cache_blocks, k_eff):
    page_refs = refs[:N_KV * k_eff]
    new_ref, oc_ref, ow_ref, br_ref, o_ref = refs[N_KV * k_eff:]
    b = pl.program_id(0)
    q = q_ref[0]
    nq = q.shape[0]
    bpp = PAGE_SIZE // SEL_BLOCK
    nk = k_eff * PAGE_SIZE
    q32 = q.astype(F32)
    lane_k = lax.broadcasted_iota(jnp.int32, (nq, nk), 1)
    slot = lane_k // PAGE_SIZE
    half = (lane_k % PAGE_SIZE) // SEL_BLOCK
    outs = []
    for g in range(N_KV):
        own = slice(g * HEAD_DIM, (g + 1) * HEAD_DIM)
        q_g = q32[:, own]
        k_new = new_ref[0, :, own]
        v_new = new_ref[0, :, LANES + g * HEAD_DIM:LANES + (g + 1) * HEAD_DIM]
        s_new_all = jnp.sum(q_g * k_new, axis=-1, keepdims=True)
        pages = [page_refs[g * k_eff + k][0] for k in range(k_eff)]
        keys_t = jnp.concatenate([p[0, 0] for p in pages], axis=-1).astype(BF16)
        vals_t = jnp.concatenate([p[1, 0] for p in pages], axis=-1).astype(BF16)
        allowed = jnp.zeros((nq, nk), jnp.int32)
        new_selected = jnp.int32(0)
        for k in range(k_eff):
            j = idx_sm[b, g * k_eff + k]
            in_block = (slot == k) & (half == j % bpp)
            allowed = jnp.where(in_block, (j < n_cache_blocks).astype(jnp.int32), allowed)
            new_selected = new_selected | (j == n_cache_blocks).astype(jnp.int32)
        s = jnp.where(allowed > 0, _dot(q_g.astype(BF16), keys_t), NEG)
        s_new = jnp.where(new_selected > 0, s_new_all, NEG)
        m = jnp.maximum(jnp.max(s, axis=-1, keepdims=True), s_new)
        e = jnp.exp2(s - m)
        e_new = jnp.exp2(s_new - m)
        outs.append((_dot_nt(e.astype(BF16), vals_t) + e_new * v_new)
                    * (1.0 / (jnp.sum(e, axis=-1, keepdims=True) + e_new)))
    row = lax.broadcasted_iota(jnp.int32, (nq, LANES), 0)
    lane = lax.broadcasted_iota(jnp.int32, (nq, LANES), 1)
    o_slc = jnp.concatenate(outs, axis=1)
    br = br_ref[0]
    gates = [jnp.sum(jnp.where(lane == 3 * row + j, br, 0.0), axis=-1, keepdims=True) for j in range(3)]
    o = gates[0] * oc_ref[0] + gates[1] * o_slc + gates[2] * ow_ref[0]
    lane_lo = lax.broadcasted_iota(jnp.int32, (1, LANES), 1) < HEAD_DIM
    for pair in range(N_HEADS // 2):
        even, odd = o[2 * pair:2 * pair + 1], o[2 * pair + 1:2 * pair + 2]
        if 2 * pair < HPG:
            odd = pltpu.roll(odd, HEAD_DIM, axis=1)
        else:
            even = pltpu.roll(even, HEAD_DIM, axis=1)
        o_ref[0, :, pair * LANES:(pair + 1) * LANES] = jnp.where(lane_lo, even, odd)


def _nsa_step2(idx, page_table, q16, cache_t, new_slc, o_cmp, o_win, br, *, n_cache_blocks, k_eff):
    b, nq, _ = q16.shape
    bpp = PAGE_SIZE // SEL_BLOCK

    def page_map(i, idx_sm, pt_sm, c):
        j = jnp.minimum(idx_sm[i, c], n_cache_blocks - 1)
        return (pt_sm[i, j // bpp], 0, c // k_eff, 0, 0)

    def per_b(shape):
        return pl.BlockSpec((1,) + shape, lambda i, idx_sm, pt_sm: (i,) + (0,) * len(shape))

    n_blk = N_KV * k_eff
    in_specs = [per_b((nq, LANES))]
    slab_shape = (1, 2, 1) + cache_t.shape[3:]
    in_specs += [pl.BlockSpec(slab_shape, functools.partial(page_map, c=c)) for c in range(n_blk)]
    in_specs += [per_b((1, KV_ROW)), per_b((nq, LANES)), per_b((nq, LANES)), per_b((1, LANES))]
    return pl.pallas_call(
        functools.partial(_nsa_step2_kernel, n_cache_blocks=n_cache_blocks, k_eff=k_eff),
        grid_spec=pltpu.PrefetchScalarGridSpec(
            num_scalar_prefetch=2,
            grid=(b,),
            in_specs=in_specs,
            out_specs=per_b((1, D_NSA)),
        ),
        out_shape=jax.ShapeDtypeStruct((b, 1, D_NSA), F32),
        compiler_params=_cparams("arbitrary"),
        name="nsa_step_selected",
    )(idx, page_table, q16, *([cache_t] * n_blk), new_slc, o_cmp, o_win, br)


def _out_kernel(*refs, mul_ga):
    if mul_ga:
        x_ref, gate_ref, ya_ref, ga_ref, oret_ref, gr_ref, gret_ref, w_ref, y_ref = refs
        ya = (ya_ref[...] * ga_ref[...]).astype(BF16)
    else:
        x_ref, gate_ref, ya_ref, oret_ref, gr_ref, gret_ref, w_ref, y_ref = refs
        ya = ya_ref[...]
    o = oret_ref[...]
    parts = []
    for h in range(N_RET_HEADS):
        t = o[:, h * DV_RET:(h + 1) * DV_RET]
        ms = jnp.mean(t * t, axis=-1, keepdims=True)
        parts.append(t * lax.rsqrt(ms + EPS) * gret_ref[...])
    yr = (jnp.concatenate(parts, axis=-1) * gr_ref[...]).astype(BF16)
    y_ref[...] = x_ref[...] + gate_ref[0] * _dot(jnp.concatenate([ya, yr], axis=-1), w_ref[...])


def _output(x2, gate, ya, ga, o_ret, gr, g_ret, w_out, *, n_batch, per_row_mod):
    m = x2.shape[0]
    seq = m // n_batch
    tm = min(seq, 512)
    nt = seq // tm
    if per_row_mod:
        gate_spec = pl.BlockSpec((1, tm, D_MODEL), lambda i: (0, i, 0))
    else:
        gate_spec = pl.BlockSpec((1, 1, D_MODEL), lambda i: (i // nt, 0, 0))

    def rows_spec(width):
        return pl.BlockSpec((tm, width), lambda i: (i, 0))

    mul_ga = ga is not None
    args = [x2, gate, ya] + ([ga] if mul_ga else []) + [o_ret, gr, g_ret.reshape(1, DV_RET), w_out.astype(BF16)]
    in_specs = [rows_spec(D_MODEL), gate_spec, rows_spec(D_NSA)] + ([rows_spec(D_NSA)] if mul_ga else [])
    in_specs += [rows_spec(D_RET), rows_spec(D_RET), pl.BlockSpec((1, DV_RET), lambda i: (0, 0)),
                 pl.BlockSpec((D_MODEL, D_MODEL), lambda i: (0, 0))]
    return pl.pallas_call(
        functools.partial(_out_kernel, mul_ga=mul_ga),
        grid=(m // tm,),
        in_specs=in_specs,
        out_specs=rows_spec(D_MODEL),
        out_shape=jax.ShapeDtypeStruct((m, D_MODEL), F32),
        compiler_params=_cparams("arbitrary"),
        name="output_projection",
    )(*args)


def kernel(x_prompt, x_sample, c_prompt, c_sample, cache_cmp, cache_slc, state_win, state_ret, page_table, g_norm, w_ada, b_ada, w_in, g_q, g_kc, g_ks, g_kw, pe_ck, w_ck1, w_ck2, pe_cv, w_cv1, w_cv2, g_ret, w_out):
    assert g_norm.shape[0] == 1, "one layer"
    bp, seq, _ = x_prompt.shape
    bs, dec_seq, _ = x_sample.shape
    assert dec_seq == 1
    n_pages = page_table.shape[1]
    past = n_pages * PAGE_SIZE
    assert state_win.shape[2] == WINDOW and past >= WINDOW

    w_proj, gq_slots, gks2, gkw2 = _prep_proj_weights(w_in[0], g_q[0], g_ks[0], g_kw[0])
    cmp_weights = _prep_compress_weights(pe_ck[0], w_ck1[0], w_ck2[0], pe_cv[0], w_cv1[0], w_cv2[0], g_kc[0])

    pad = (-(bp + bs)) % 8
    c_all = jnp.concatenate([c_prompt, c_sample, jnp.zeros((pad, D_MODEL), F32)], axis=0)
    mod = _modulation(c_all, w_ada[0], b_ada[0])
    shift_p, scale_p, gate_p = (mod[:bp, k * D_MODEL:(k + 1) * D_MODEL].reshape(bp, 1, D_MODEL) for k in range(3))
    shift_s, scale_s, gate_s = (mod[bp:bp + bs, k * D_MODEL:(k + 1) * D_MODEL].reshape(1, bs, D_MODEL)
                                for k in range(3))

    xp2 = x_prompt.reshape(bp * seq, D_MODEL)
    cos_p, sin_p = _rope_tables(seq, 0, 1)
    (q_p, cmp_t, slc_t, slcb_p, win_t, winb_p, br_p, ga_p, rq_p, rk_p, rv_p, gr_p) = _projection(
        xp2, scale_p, shift_p, g_norm[0], w_proj, gq_slots, gks2, gkw2, cos_p, sin_p, n_batch=bp, per_row_mod=False,
        pos_minor=True)

    kc_p, vc_p = _compress(cmp_t.reshape(bp, 2, N_KV, HEAD_DIM, seq), None, jnp.zeros((bp, 1, KV_ROW), F32),
                           cmp_weights, n_batch=bp, n_pages=seq // PAGE_SIZE)
    ocmp_p, bias_p = _nsa_select(q_p, kc_p, vc_p, n_batch=bp, seq=seq)
    ya_p = _nsa_prompt(q_p, ocmp_p, bias_p, slcb_p, winb_p, br_p, ga_p, n_batch=bp, seq=seq)
    oret_p, sret_p = _retention_prompt(rq_p, rk_p, rv_p, n_batch=bp, seq=seq)
    y_p = _output(xp2, gate_p, ya_p, None, oret_p.reshape(bp * seq, D_RET), gr_p, g_ret[0], w_out[0],
                  n_batch=bp, per_row_mod=False)

    xs2 = x_sample.reshape(bs, D_MODEL)
    cos_s, sin_s = _rope_tables(bs, past, 0)
    (q_s, cmp_s, slc_s, _, win_s, _, br_s, ga_s, rq_s, rk_s, rv_s, gr_s) = _projection(
        xs2, scale_s, shift_s, g_norm[0], w_proj, gq_slots, gks2, gkw2, cos_s, sin_s, n_batch=1, per_row_mod=True,
        pos_minor=False)

    n_chunk_cache = past // CMP_STRIDE
    n_cmp_s = n_chunk_cache
    n_sel_s = past // SEL_BLOCK + 1
    nsp = ((n_sel_s + LANES - 1) // LANES) * LANES
    to_native = (0, 2, 3, 4, 1)
    kc_s, vc_s = _compress(jnp.transpose(cache_cmp[0], to_native), page_table, cmp_s.reshape(bs, 1, KV_ROW),
                           cmp_weights, n_batch=bs, n_pages=n_pages)

    q16 = jnp.pad(q_s[0].transpose(1, 0, 2), ((0, 0), (0, 16 - N_HEADS), (0, 0)))
    mimp_s = _importance_matrix(n_cmp_s, n_cmp_s, n_sel_s, nsp)
    o_cmp_s, o_win_s, idx_s, oret_s, sret_s = _nsa_step1(
        q16, kc_s, vc_s, jnp.transpose(state_win[0], to_native), win_s.reshape(bs, 1, KV_ROW), mimp_s,
        rq_s, rk_s, rv_s, state_ret[0], t=past, n_cmp=n_cmp_s, n_sel=n_sel_s)
    k_eff = min(TOP_N, n_sel_s)
    idx_flat = idx_s[:, :N_KV, :k_eff].reshape(bs, N_KV * k_eff)
    o_nsa_s = _nsa_step2(idx_flat, page_table, q16, jnp.transpose(cache_slc[0], to_native),
                         slc_s.reshape(bs, 1, KV_ROW), o_cmp_s, o_win_s, br_s.reshape(bs, 1, LANES),
                         n_cache_blocks=past // SEL_BLOCK, k_eff=k_eff)

    y_s = _output(xs2, gate_s, o_nsa_s.reshape(bs, D_NSA), ga_s, oret_s.reshape(bs, D_RET), gr_s, g_ret[0], w_out[0],
                  n_batch=1, per_row_mod=True)

    def kv_rows(a, b, l):
        return a.reshape(1, b, l, 2, N_KV, HEAD_DIM)

    def kv_rows_t(a, b, l):
        return jnp.transpose(a.reshape(b, 2, N_KV, HEAD_DIM, l), (0, 4, 1, 2, 3))[None]

    keep_p = min(WINDOW, seq)
    s_win = jnp.concatenate([state_win[0][:, 1:].reshape(bs, WINDOW - 1, KV_ROW), win_s.reshape(bs, 1, KV_ROW)],
                            axis=1)
    return (y_p.reshape(bp, seq, D_MODEL), y_s.reshape(bs, 1, D_MODEL),
            kv_rows_t(cmp_t, bp, seq), kv_rows_t(slc_t, bp, seq),
            kv_rows_t(win_t[:, :, seq - keep_p:], bp, keep_p), sret_p[None],
            kv_rows(cmp_s, bs, 1), kv_rows(slc_s, bs, 1), kv_rows(s_win, bs, WINDOW), sret_s[None])
```

```python
import functools
import math

import jax
import jax.numpy as jnp
import numpy as np
from jax import lax
from jax.experimental import pallas as pl
from jax.experimental.pallas import tpu as pltpu

D_MODEL = 1024
HEAD_DIM = 64
D_NSA = 512
N_HEADS = 8
N_KV = 2
HPG = 4
CMP_BLOCK = 32
CMP_STRIDE = 16
SEL_BLOCK = 64
TOP_N = 16
WINDOW = 512
D_RET = 512
N_RET_HEADS = 4
DV_RET = 128
DK_RET = 64
RET_CHUNK = 128
PAGE_SIZE = 128
ROPE_BASE = 10000.0
EPS = 1e-6
NEG = -1e30
FORCE = 1e4
BELOW_NEG = -2e38
TAKEN = -3e38

KV_ROW = 2 * N_KV * HEAD_DIM
LANES = 128
VMEM_LIMIT_BYTES = 56 * 1024 * 1024

F32 = jnp.float32
BF16 = jnp.bfloat16

O_Q = 0
O_CMP = O_Q + D_NSA
O_SLC = O_CMP + KV_ROW
O_WIN = O_SLC + KV_ROW
O_BR = O_WIN + KV_ROW
O_GA = O_BR + LANES
O_RQ = O_GA + D_NSA
O_RK = O_RQ + N_RET_HEADS * DK_RET
O_RV = O_RK + N_RET_HEADS * DK_RET
O_GR = O_RV + D_RET
D_IN_PAD = O_GR + D_RET


def _cparams(*sem):
    return pltpu.CompilerParams(dimension_semantics=sem, vmem_limit_bytes=VMEM_LIMIT_BYTES)


def _dot(a, b):
    return jnp.dot(a, b, preferred_element_type=F32)


def _dot_nt(a, b):
    return lax.dot_general(a, b, (((1,), (1,)), ((), ())), preferred_element_type=F32)


def _dot_tn(a, b):
    return lax.dot_general(a, b, (((0,), (0,)), ((), ())), preferred_element_type=F32)


def _silu(x):
    return x * jax.nn.sigmoid(x)


def _split3_dot(p, m):
    p1 = p.astype(BF16)
    r1 = p - p1.astype(F32)
    p2 = r1.astype(BF16)
    p3 = (r1 - p2.astype(F32)).astype(BF16)
    return _dot(p1, m) + _dot(p2, m) + _dot(p3, m)


def _pair_rmsnorm(k, gain2):
    lo = lax.broadcasted_iota(jnp.int32, k.shape, 1) < HEAD_DIM
    sq = k * k
    s0 = jnp.sum(jnp.where(lo, sq, 0.0), axis=-1, keepdims=True)
    s1 = jnp.sum(jnp.where(lo, 0.0, sq), axis=-1, keepdims=True)
    ms = jnp.where(lo, s0, s1) * (1.0 / HEAD_DIM)
    return k * lax.rsqrt(ms + EPS) * gain2


def _mod_kernel(c_ref, w_ref, b_ref, o_ref):
    o_ref[...] = _dot(_silu(c_ref[...]).astype(BF16), w_ref[...]) + b_ref[...]


def _modulation(c_all, w_ada, b_ada):
    rows = c_all.shape[0]
    n = w_ada.shape[1]
    tn = 1024
    return pl.pallas_call(
        _mod_kernel,
        grid=(n // tn,),
        in_specs=[pl.BlockSpec((rows, D_MODEL), lambda j: (0, 0)),
                  pl.BlockSpec((D_MODEL, tn), lambda j: (0, j)),
                  pl.BlockSpec((1, tn), lambda j: (0, j))],
        out_specs=pl.BlockSpec((rows, tn), lambda j: (0, j)),
        out_shape=jax.ShapeDtypeStruct((rows, n), F32),
        compiler_params=_cparams("arbitrary"),
        name="adaln_modulation",
    )(c_all, w_ada.astype(BF16), b_ada.reshape(1, n))


def _rope_kernel(freq_ref, cos_ref, sin_ref, *, base, stride, rows):
    shape = (rows, LANES)
    r = lax.broadcasted_iota(jnp.int32, shape, 0)
    pos = (base + stride * (pl.program_id(0) * rows + r)).astype(F32)
    ang = pos * freq_ref[...]
    lane = lax.broadcasted_iota(jnp.int32, shape, 1)
    sign = jnp.where((lane & (DK_RET - 1)) < DK_RET // 2, -1.0, 1.0)
    cos_ref[...] = jnp.cos(ang)
    sin_ref[...] = jnp.sin(ang) * sign


def _rope_tables(n_rows, base, stride):
    half = DK_RET // 2
    freqs = ROPE_BASE ** (-jnp.arange(half, dtype=F32) / half)
    freq_row = jnp.tile(freqs, LANES // half).reshape(1, LANES)
    rows = min(n_rows, 512)
    return pl.pallas_call(
        functools.partial(_rope_kernel, base=base, stride=stride, rows=rows),
        grid=(n_rows // rows,),
        in_specs=[pl.BlockSpec((1, LANES), lambda i: (0, 0))],
        out_specs=[pl.BlockSpec((rows, LANES), lambda i: (i, 0))] * 2,
        out_shape=[jax.ShapeDtypeStruct((n_rows, LANES), F32)] * 2,
        compiler_params=_cparams("arbitrary"),
        name="rope_tables",
    )(freq_row)


def _proj_kernel(x_ref, scale_ref, shift_ref, gn_ref, w_ref, gq_ref, gks_ref, gkw_ref, cos_ref, sin_ref,
                 q_ref, cmp_ref, slc_ref, slcb_ref, win_ref, winb_ref, br_ref, ga_ref, rq_ref, rk_ref,
                 rv_ref, gr_ref, *, pos_minor):
    x = x_ref[...]
    ms = jnp.mean(x * x, axis=-1, keepdims=True)
    xn = x * lax.rsqrt(ms + EPS) * gn_ref[...]
    hb = (xn * (1.0 + scale_ref[0]) + shift_ref[0]).astype(BF16)

    def seg(start, width):
        return _dot(hb, w_ref[:, start:start + width])

    yq = seg(O_Q, D_NSA)
    lane_lo = lax.broadcasted_iota(jnp.int32, (yq.shape[0], LANES), 1) < HEAD_DIM
    for s in range(N_HEADS):
        t = yq[:, (s // 2) * LANES:(s // 2 + 1) * LANES]
        if s % 2 != s // HPG:
            t = pltpu.roll(t, HEAD_DIM, axis=1)
        t = jnp.where(lane_lo == (s < HPG), t, 0.0)
        ms = jnp.sum(t * t, axis=-1, keepdims=True) * (1.0 / HEAD_DIM)
        q_ref[0, s] = (t * lax.rsqrt(ms + EPS) * gq_ref[s]).astype(BF16)

    y_cmp = seg(O_CMP, KV_ROW)
    y = seg(O_SLC, KV_ROW)
    y_slc = jnp.concatenate([_pair_rmsnorm(y[:, :LANES], gks_ref[...]), y[:, LANES:]], axis=-1)
    slcb_ref[...] = y_slc.astype(BF16)
    y = seg(O_WIN, KV_ROW)
    y_win = jnp.concatenate([_pair_rmsnorm(y[:, :LANES], gkw_ref[...]), y[:, LANES:]], axis=-1)
    winb_ref[...] = y_win.astype(BF16)
    for ref, rows in ((cmp_ref, y_cmp), (slc_ref, y_slc), (win_ref, y_win)):
        if pos_minor:
            ref[0] = rows.T
        else:
            ref[...] = rows

    br_ref[...] = jax.nn.sigmoid(seg(O_BR, LANES))
    ga_ref[...] = _silu(seg(O_GA, D_NSA))

    first_half = (lax.broadcasted_iota(jnp.int32, (yq.shape[0], LANES), 1) & (DK_RET - 1)) < DK_RET // 2

    def rotary(y):
        tiles = []
        for k in range(y.shape[1] // LANES):
            t = y[:, k * LANES:(k + 1) * LANES]
            swapped = jnp.where(first_half, pltpu.roll(t, LANES - DK_RET // 2, axis=1),
                                pltpu.roll(t, DK_RET // 2, axis=1))
            tiles.append(t * cos_ref[...] + swapped * sin_ref[...])
        return jnp.concatenate(tiles, axis=-1)

    nrot = N_RET_HEADS * DK_RET
    rq_ref[...] = rotary(seg(O_RQ, nrot)).astype(BF16)
    rk_ref[...] = (rotary(seg(O_RK, nrot)) * DK_RET ** -0.5).astype(BF16)
    rv_ref[...] = seg(O_RV, D_RET).astype(BF16)
    gr_ref[...] = _silu(seg(O_GR, D_RET))


def _prep_proj_weights(w_in, g_q, g_ks, g_kw):
    n_br = 3 * N_HEADS
    w = jnp.concatenate([w_in[:, :O_BR], jnp.pad(w_in[:, O_BR:O_BR + n_br], ((0, 0), (0, LANES - n_br))),
                         w_in[:, O_BR + n_br:]], axis=1).astype(BF16)
    assert w.shape[1] == D_IN_PAD
    gz = jnp.zeros((HEAD_DIM,), F32)
    gq = g_q * (HEAD_DIM ** -0.5 * math.log2(math.e))
    gq_slots = jnp.stack([jnp.concatenate([gq, gz]) if s < HPG else jnp.concatenate([gz, gq])
                          for s in range(N_HEADS)]).reshape(N_HEADS, 1, LANES)
    return w, gq_slots, jnp.tile(g_ks, 2).reshape(1, LANES), jnp.tile(g_kw, 2).reshape(1, LANES)


def _projection(x2, scale, shift, g_norm, w, gq_slots, gks2, gkw2, cos_t, sin_t, *, n_batch, per_row_mod,
                pos_minor):
    m = x2.shape[0]
    seq = m // n_batch
    tm = min(seq, 512)
    nt = seq // tm
    if per_row_mod:
        mod_spec = pl.BlockSpec((1, tm, D_MODEL), lambda i: (0, i, 0))
    else:
        mod_spec = pl.BlockSpec((1, 1, D_MODEL), lambda i: (i // nt, 0, 0))

    def rows_spec(width):
        return pl.BlockSpec((tm, width), lambda i: (i, 0))

    def const_spec(shape):
        return pl.BlockSpec(shape, lambda i: (0,) * len(shape))

    kv_out = "kv rows"
    widths = [(KV_ROW, kv_out), (KV_ROW, kv_out), (KV_ROW, BF16), (KV_ROW, kv_out), (KV_ROW, BF16), (LANES, F32),
              (D_NSA, F32), (N_RET_HEADS * DK_RET, BF16), (N_RET_HEADS * DK_RET, BF16), (D_RET, BF16),
              (D_RET, F32)]
    out_shape = [jax.ShapeDtypeStruct((n_batch, N_HEADS, seq, LANES), BF16)]
    out_specs = [pl.BlockSpec((1, N_HEADS, tm, LANES), lambda i: (i // nt, 0, i % nt, 0))]
    for width, dt in widths:
        if pos_minor and dt is kv_out:
            out_shape.append(jax.ShapeDtypeStruct((n_batch, KV_ROW, seq), F32))
            out_specs.append(pl.BlockSpec((1, KV_ROW, tm), lambda i: (i // nt, 0, i % nt)))
        else:
            out_shape.append(jax.ShapeDtypeStruct((m, width), F32 if dt is kv_out else dt))
            out_specs.append(rows_spec(width))
    return pl.pallas_call(
        functools.partial(_proj_kernel, pos_minor=pos_minor),
        grid=(m // tm,),
        in_specs=[rows_spec(D_MODEL), mod_spec, mod_spec, const_spec((1, D_MODEL)),
                  const_spec((D_MODEL, D_IN_PAD)), const_spec((N_HEADS, 1, LANES)), const_spec((1, LANES)),
                  const_spec((1, LANES)), pl.BlockSpec((tm, LANES), lambda i: (i % nt, 0)),
                  pl.BlockSpec((tm, LANES), lambda i: (i % nt, 0))],
        out_specs=out_specs,
        out_shape=out_shape,
        compiler_params=_cparams("arbitrary"),
        name="input_projection",
    )(x2, scale, shift, g_norm.reshape(1, D_MODEL), w, gq_slots, gks2, gkw2, cos_t, sin_t)


def _compress_kernel(pt_ref, *refs, n_in):
    x_refs, next_ref = refs[:n_in], refs[n_in]
    new_ref, perm_ref, w1_ref, pe_ref, w2_ref, gkc_ref, kc_ref, vc_ref = refs[n_in + 1:]
    cpp = PAGE_SIZE // CMP_STRIDE
    n = n_in * cpp
    last_step = pl.program_id(1) == pl.num_programs(1) - 1
    perm = perm_ref[...]

    def chunk_rows(page_ref):
        return _dot_nt(perm, page_ref[0].reshape(KV_ROW, PAGE_SIZE).astype(BF16))

    def flat_chunks(moved, kv):
        return jnp.concatenate(
            [jnp.concatenate([r[cpp * l:cpp * (l + 1), kv * LANES:(kv + 1) * LANES] for l in range(CMP_STRIDE)],
                             axis=1) for r in moved], axis=0).astype(BF16)

    moved = [chunk_rows(r) for r in x_refs]
    moved_next = [chunk_rows(next_ref)]
    new8 = jnp.broadcast_to(new_ref[0], (8, KV_ROW)).astype(BF16)
    zeros = jnp.zeros((8, (CMP_STRIDE - 1) * LANES), BF16)
    row = lax.broadcasted_iota(jnp.int32, (n, LANES), 0)
    pre = []
    for kv in range(2):
        new_chunk = jnp.concatenate([new8[:, kv * LANES:(kv + 1) * LANES], zeros], axis=1)
        a_all = _dot(jnp.concatenate([flat_chunks(moved, kv), flat_chunks(moved_next, kv), pe_ref[kv], new_chunk],
                                     axis=0), w1_ref[kv])
        a = a_all[:n]
        a_next = a_all[n:n + 1, LANES:]
        bias = a_all[n + cpp:n + cpp + 1, :LANES] + a_all[n + cpp + 1:n + cpp + 2, LANES:]
        a_new = a_all[n + cpp + 8:n + cpp + 9, LANES:]
        hi = pltpu.roll(a[:, LANES:], n - 1, axis=0)
        hi = jnp.where(row == n - 1, jnp.where(last_step, a_new, a_next), hi)
        pre.append(a[:, :LANES] + hi + bias)
    out = _dot(_silu(jnp.concatenate(pre, axis=1)).astype(BF16), w2_ref[...])
    kc_ref[0] = _pair_rmsnorm(out[:, :LANES], gkc_ref[...]).astype(BF16)
    vc_ref[0] = out[:, LANES:].astype(BF16)


def _compress(rows_t, page_table, new_rows, weights, *, n_batch, n_pages):
    w1kv, w2x, pe8kv, gkc2 = weights
    b = n_batch
    cpp = PAGE_SIZE // CMP_STRIDE
    p = math.gcd(n_pages, 128)
    row = jnp.arange(PAGE_SIZE, dtype=jnp.int32)
    perm = ((row % cpp) * CMP_STRIDE + row // cpp).reshape(PAGE_SIZE, 1) == row.reshape(1, PAGE_SIZE)
    page_shape = (1,) + rows_t.shape[1:4] + (PAGE_SIZE,)

    def page_spec(page_of_step):
        if page_table is None:
            return pl.BlockSpec(page_shape, lambda i, j, pt: (i, 0, 0, 0, page_of_step(j)))
        return pl.BlockSpec(page_shape, lambda i, j, pt: (pt[i, page_of_step(j)], 0, 0, 0, 0))

    in_specs = [page_spec(lambda j, k=k: j * p + k) for k in range(p)]
    in_specs.append(page_spec(lambda j: jnp.minimum((j + 1) * p, n_pages - 1)))

    def const_spec(shape):
        return pl.BlockSpec(shape, lambda i, j, pt: (0,) * len(shape))

    in_specs += [pl.BlockSpec((1, 1, KV_ROW), lambda i, j, pt: (i, 0, 0)), const_spec((PAGE_SIZE, PAGE_SIZE)),
                 const_spec(w1kv.shape), const_spec(pe8kv.shape), const_spec(w2x.shape), const_spec((1, LANES))]
    if page_table is None:
        page_table = jnp.zeros((1, 1), jnp.int32)
    return pl.pallas_call(
        functools.partial(_compress_kernel, n_in=p),
        grid_spec=pltpu.PrefetchScalarGridSpec(
            num_scalar_prefetch=1,
            grid=(b, n_pages // p),
            in_specs=in_specs,
            out_specs=[pl.BlockSpec((1, p * cpp, LANES), lambda i, j, pt: (i, j, 0))] * 2,
        ),
        out_shape=[jax.ShapeDtypeStruct((b, n_pages * cpp, LANES), BF16)] * 2,
        compiler_params=_cparams("arbitrary", "arbitrary"),
        name="compress",
    )(page_table, *([rows_t] * (p + 1)), new_rows, perm.astype(BF16), w1kv, pe8kv, w2x, gkc2)


def _prep_compress_weights(pe_ck, w_ck1, w_ck2, pe_cv, w_cv1, w_cv2, g_kc):
    eye = jnp.eye(N_KV, dtype=F32)
    w1 = jnp.stack([w_ck1, w_cv1]).reshape(2, 2, CMP_STRIDE, HEAD_DIM, HEAD_DIM)
    w1kv = jnp.einsum('khldf,gp->klgdhpf', w1, eye).reshape(2, CMP_STRIDE * LANES, KV_ROW).astype(BF16)
    w2 = jnp.stack([w_ck2, w_cv2])
    w2x = jnp.einsum('kfd,kq,gp->kgfqpd', w2, eye, eye).reshape(KV_ROW, KV_ROW).astype(BF16)
    pe = jnp.stack([pe_ck, pe_cv]).reshape(2, 2, CMP_STRIDE, 1, HEAD_DIM)
    pe_rows = jnp.broadcast_to(pe, (2, 2, CMP_STRIDE, N_KV, HEAD_DIM)).reshape(2, 2, CMP_STRIDE * LANES)
    pe8kv = jnp.zeros((2, 8, CMP_STRIDE * LANES), F32).at[:, :2].set(pe_rows).astype(BF16)
    return w1kv, w2x, pe8kv, jnp.tile(g_kc, 2).reshape(1, LANES)


def _importance_matrix(n_cmp_pad, n_cmp, n_sel, n_sel_pad):
    ratio = SEL_BLOCK // CMP_STRIDE
    span = CMP_BLOCK // CMP_STRIDE
    m = np.zeros((n_cmp_pad, n_sel_pad), np.float32)
    for j in range(n_sel):
        for r in range(ratio):
            for s in range(span):
                n = ratio * j + r - s
                if 0 <= n < n_cmp:
                    m[n, j] += 1.0
    return jnp.asarray(m, BF16)


def _nsa_select_kernel(q_ref, kc_ref, vc_ref, mimp_ref, oc_ref, bias_ref, *, tq, n_cmp, n_sel):
    qs = pl.program_id(1) * tq
    rows = HPG * tq
    ncp = kc_ref.shape[1]

    def q_pos(width):
        return qs + lax.broadcasted_iota(jnp.int32, (tq, width), 0)

    def per_head(x):
        return jnp.concatenate([x] * HPG, axis=0)

    col = lax.broadcasted_iota(jnp.int32, (tq, ncp), 1)
    cmp_mask = per_head(jnp.where((col * CMP_STRIDE + (CMP_BLOCK - 1) <= q_pos(ncp)) & (col < n_cmp), 0.0, NEG))
    cmp_any = per_head(jnp.where(q_pos(1) >= CMP_BLOCK - 1, 1.0, 0.0))
    blk = lax.broadcasted_iota(jnp.int32, (tq, LANES), 1)
    t = q_pos(LANES)
    jt = t // SEL_BLOCK
    forced = (blk == 0) | (blk == jt) | (blk == jt - 1)
    valid = blk * SEL_BLOCK <= t
    n_rank = ((n_sel + 7) // 8) * 8
    j_row = lax.broadcasted_iota(jnp.int32, (n_rank, tq), 0)

    for g in range(N_KV):
        q = q_ref[0, HPG * g:HPG * (g + 1)].reshape(rows, LANES)
        s = _dot_nt(q, kc_ref[0]) + cmp_mask
        e = jnp.exp2(s - jnp.max(s, axis=-1, keepdims=True))
        p = e * (cmp_any / jnp.sum(e, axis=-1, keepdims=True))
        oc_ref[0, HPG * g:HPG * (g + 1)] = _dot(p.astype(BF16), vc_ref[0]).astype(BF16).reshape(HPG, tq, LANES)
        p_sum = p[0:tq] + p[tq:2 * tq] + p[2 * tq:3 * tq] + p[3 * tq:4 * tq]
        imp = _split3_dot(p_sum, mimp_ref[...])

        score = jnp.where(valid, jnp.where(forced, FORCE, imp), NEG)
        score = jnp.where(blk < n_sel, score, BELOW_NEG)
        left = score.T[:n_rank]
        sel_t = jnp.zeros((n_rank, tq), F32)
        for _ in range(min(TOP_N, n_sel)):
            top = jnp.max(left, axis=0, keepdims=True)
            first = jnp.min(jnp.where(left == top, j_row, n_rank), axis=0, keepdims=True)
            taken = j_row == first
            sel_t = jnp.where(taken, 1.0, sel_t)
            left = jnp.where(taken, TAKEN, left)
        if n_rank < LANES:
            sel_t = jnp.concatenate([sel_t, jnp.zeros((LANES - n_rank, tq), F32)], axis=0)
        bias_ref[0, g] = jnp.where(sel_t.T > 0.5, 0.0, NEG).astype(BF16)


def _nsa_select(q_slots, kc, vc, *, n_batch, seq):
    tq = min(512, seq)
    n_cmp = seq // CMP_STRIDE - (CMP_BLOCK // CMP_STRIDE) + 1
    n_sel = seq // SEL_BLOCK
    assert n_sel <= LANES
    ncp = kc.shape[1]
    mimp = _importance_matrix(ncp, n_cmp, n_sel, LANES)
    return pl.pallas_call(
        functools.partial(_nsa_select_kernel, tq=tq, n_cmp=n_cmp, n_sel=n_sel),
        grid=(n_batch, seq // tq),
        in_specs=[pl.BlockSpec((1, N_HEADS, tq, LANES), lambda b, i: (b, 0, i, 0)),
                  pl.BlockSpec((1, ncp, LANES), lambda b, i: (b, 0, 0)),
                  pl.BlockSpec((1, ncp, LANES), lambda b, i: (b, 0, 0)),
                  pl.BlockSpec(mimp.shape, lambda b, i: (0, 0))],
        out_specs=[pl.BlockSpec((1, N_HEADS, tq, LANES), lambda b, i: (b, 0, i, 0)),
                   pl.BlockSpec((1, N_KV, tq, LANES), lambda b, i: (b, 0, i, 0))],
        out_shape=[jax.ShapeDtypeStruct((n_batch, N_HEADS, seq, LANES), BF16),
                   jax.ShapeDtypeStruct((n_batch, N_KV, seq, LANES), BF16)],
        compiler_params=_cparams("arbitrary", "arbitrary"),
        name="nsa_select",
    )(q_slots, kc, vc, mimp)


def _nsa_prompt_kernel(q_ref, oc_ref, bias_ref, slc_ref, win_ref, br_ref, ga_ref, expt_ref, o_ref,
                       m_sc, acc_sc, *, tq, tk):
    qs = pl.program_id(1) * tq
    rows = HPG * tq
    wlen = WINDOW + tq
    br = br_ref[...]
    lane_lo = lax.broadcasted_iota(jnp.int32, (tq, LANES), 1) < HEAD_DIM

    def q_pos(width):
        return qs + lax.broadcasted_iota(jnp.int32, (tq, width), 0)

    def per_head(x):
        return jnp.concatenate([x] * HPG, axis=0)

    w0 = pl.multiple_of(jnp.maximum(qs - WINDOW, 0), tq)
    kpos = w0 + lax.broadcasted_iota(jnp.int32, (tq, wlen), 1)
    win_mask = per_head(jnp.where((kpos <= q_pos(wlen)) & (kpos > q_pos(wlen) - WINDOW), 0.0, NEG))
    kt_diag = qs // tk
    kpos = kt_diag * tk + lax.broadcasted_iota(jnp.int32, (tq, tk), 1)
    diag_mask = per_head(jnp.where(kpos <= q_pos(tk), 0.0, NEG))

    def with_ones(v, g):
        own = (lax.broadcasted_iota(jnp.int32, v.shape, 1) // HEAD_DIM) == g
        return jnp.where(own, v, jnp.ones_like(v))

    def normalised(acc):
        return acc * (1.0 / pltpu.roll(acc, HEAD_DIM, axis=1))

    groups = range(N_KV)
    q = [q_ref[0, HPG * g:HPG * (g + 1)].reshape(rows, LANES) for g in groups]
    q_ext = jnp.concatenate([jnp.concatenate([q[g], per_head(bias_ref[0, g])], axis=1) for g in groups], axis=0)

    m_sc[...] = jnp.full((N_KV * rows, LANES), NEG, F32)
    acc_sc[...] = jnp.zeros((N_KV * rows, LANES), F32)

    def key_tile(kt, causal):
        k0 = pl.multiple_of(kt * tk, tk)
        k_ext = jnp.concatenate([slc_ref[0, pl.ds(k0, tk), 0:LANES], expt_ref[kt]], axis=1)
        v = slc_ref[0, pl.ds(k0, tk), LANES:2 * LANES]
        s_all = _dot_nt(q_ext, k_ext)
        for g in groups:
            rs = slice(g * rows, (g + 1) * rows)
            s = s_all[rs]
            if causal:
                s = s + diag_mask
            m_prev = m_sc[rs]
            m_next = jnp.maximum(m_prev, jnp.max(s, axis=-1, keepdims=True))
            p = jnp.exp2(s - jnp.concatenate([m_next] * (tk // LANES), axis=1))
            acc_sc[rs] = jnp.exp2(m_prev - m_next) * acc_sc[rs] + _dot(p.astype(BF16), with_ones(v, g))
            m_sc[rs] = m_next

    def full_tile(kt, carry):
        key_tile(kt, False)
        return carry

    lax.fori_loop(0, kt_diag, full_tile, 0)
    key_tile(kt_diag, True)

    k = win_ref[0, pl.ds(w0, wlen), 0:LANES]
    v = win_ref[0, pl.ds(w0, wlen), LANES:2 * LANES]
    s_all = _dot_nt(jnp.concatenate(q, axis=0), k)
    for g in groups:
        rs = slice(g * rows, (g + 1) * rows)
        s = s_all[rs] + win_mask
        e = jnp.exp2(s - jnp.max(s, axis=-1, keepdims=True))
        o_win = normalised(_dot(e.astype(BF16), with_ones(v, g)))
        o_slc = normalised(acc_sc[rs])

        heads = []
        for h in range(HPG):
            hs = slice(h * tq, (h + 1) * tq)
            c = 3 * (HPG * g + h)
            heads.append(br[:, c:c + 1] * oc_ref[0, HPG * g + h].astype(F32) + br[:, c + 1:c + 2] * o_slc[hs]
                         + br[:, c + 2:c + 3] * o_win[hs])
        for hh in range(HPG // 2):
            even, odd = heads[2 * hh], heads[2 * hh + 1]
            if g == 0:
                odd = pltpu.roll(odd, HEAD_DIM, axis=1)
            else:
                even = pltpu.roll(even, HEAD_DIM, axis=1)
            c0 = (2 * g + hh) * LANES
            o_ref[:, c0:c0 + LANES] = (jnp.where(lane_lo, even, odd) * ga_ref[:, c0:c0 + LANES]).astype(BF16)


def _nsa_prompt(q_slots, o_cmp, block_bias, slc_b, win_b, br, ga, *, n_batch, seq):
    tq = min(256, seq)
    tk = min(512, seq)
    assert seq % tk == 0 and tk % tq == 0 and seq >= WINDOW + tq
    key_blk = (jnp.arange(seq, dtype=jnp.int32) // SEL_BLOCK).reshape(seq // tk, tk, 1)
    expand = (key_blk == jnp.arange(LANES, dtype=jnp.int32).reshape(1, 1, LANES)).astype(BF16)
    nq = seq // tq
    rows = HPG * tq
    return pl.pallas_call(
        functools.partial(_nsa_prompt_kernel, tq=tq, tk=tk),
        grid=(n_batch, nq),
        in_specs=[pl.BlockSpec((1, N_HEADS, tq, LANES), lambda b, i: (b, 0, i, 0)),
                  pl.BlockSpec((1, N_HEADS, tq, LANES), lambda b, i: (b, 0, i, 0)),
                  pl.BlockSpec((1, N_KV, tq, LANES), lambda b, i: (b, 0, i, 0)),
                  pl.BlockSpec((1, seq, KV_ROW), lambda b, i: (b, 0, 0)),
                  pl.BlockSpec((1, seq, KV_ROW), lambda b, i: (b, 0, 0)),
                  pl.BlockSpec((tq, LANES), lambda b, i: (b * nq + i, 0)),
                  pl.BlockSpec((tq, D_NSA), lambda b, i: (b * nq + i, 0)),
                  pl.BlockSpec(expand.shape, lambda b, i: (0, 0, 0))],
        out_specs=pl.BlockSpec((tq, D_NSA), lambda b, i: (b * nq + i, 0)),
        out_shape=jax.ShapeDtypeStruct((n_batch * seq, D_NSA), BF16),
        scratch_shapes=[pltpu.VMEM((N_KV * rows, LANES), F32)] * 2,
        compiler_params=_cparams("arbitrary", "arbitrary"),
        name="nsa_prompt",
    )(q_slots, o_cmp, block_bias, slc_b.reshape(n_batch, seq, KV_ROW), win_b.reshape(n_batch, seq, KV_ROW), br, ga,
      expand)


def _retention_prompt_kernel(rq_ref, rk_ref, rv_ref, o_ref, s_ref, *, cl):
    @pl.when(pl.program_id(0) == 0)
    def _():
        s_ref[...] = jnp.zeros(s_ref.shape, F32)

    i_pos = lax.broadcasted_iota(jnp.int32, (cl, cl), 0)
    j_pos = lax.broadcasted_iota(jnp.int32, (cl, cl), 1)
    diff = (i_pos - j_pos).astype(F32)
    i_col = lax.broadcasted_iota(jnp.int32, (cl, 1), 0).astype(F32)
    lane_lo = lax.broadcasted_iota(jnp.int32, (cl, LANES), 1) < DK_RET
    for h in range(N_RET_HEADS):
        pair, odd = divmod(h, 2)
        log_g = math.log(1.0 - 2.0 ** (-5.0 - h))
        keep = lane_lo if odd == 0 else jnp.logical_not(lane_lo)
        decay = jnp.where(diff >= 0, jnp.exp(log_g * jnp.maximum(diff, 0.0)), 0.0)
        xi = jnp.exp(log_g * (i_col + 1.0))
        zeta = jnp.exp(log_g * (cl - 1.0 - i_col))
        for b in range(rq_ref.shape[0]):
            q2 = rq_ref[b, :, pair * LANES:(pair + 1) * LANES]
            k2 = rk_ref[b, :, pair * LANES:(pair + 1) * LANES]
            s_pair = jnp.concatenate([s_ref[b, 2 * pair], s_ref[b, 2 * pair + 1]], axis=0)
            qh = jnp.where(keep, q2, jnp.zeros_like(q2))
            v = rv_ref[b, :, h * DV_RET:(h + 1) * DV_RET]
            a = _dot_nt(qh, k2) * decay
            o_ref[b, :, h * DV_RET:(h + 1) * DV_RET] = (_dot(a.astype(BF16), v)
                                                         + _dot(qh, s_pair.astype(BF16)) * xi)
            kz = (k2.astype(F32) * zeta).astype(BF16)
            u = _dot_tn(kz, v)[odd * DK_RET:(odd + 1) * DK_RET]
            s_ref[b, h] = s_ref[b, h] * math.exp(log_g * cl) + u


def _retention_prompt(rq, rk, rv, *, n_batch, seq):
    cl = math.gcd(seq, 2 * RET_CHUNK)
    nrot = N_RET_HEADS * DK_RET
    return pl.pallas_call(
        functools.partial(_retention_prompt_kernel, cl=cl),
        grid=(seq // cl,),
        in_specs=[pl.BlockSpec((n_batch, cl, nrot), lambda i: (0, i, 0)),
                  pl.BlockSpec((n_batch, cl, nrot), lambda i: (0, i, 0)),
                  pl.BlockSpec((n_batch, cl, D_RET), lambda i: (0, i, 0))],
        out_specs=[pl.BlockSpec((n_batch, cl, D_RET), lambda i: (0, i, 0)),
                   pl.BlockSpec((n_batch, N_RET_HEADS, DK_RET, DV_RET), lambda i: (0, 0, 0, 0))],
        out_shape=[jax.ShapeDtypeStruct((n_batch, seq, D_RET), F32),
                   jax.ShapeDtypeStruct((n_batch, N_RET_HEADS, DK_RET, DV_RET), F32)],
        compiler_params=_cparams("arbitrary"),
        name="retention_prompt",
    )(rq.reshape(n_batch, seq, nrot), rk.reshape(n_batch, seq, nrot), rv.reshape(n_batch, seq, D_RET))


def _retention_step_kernel(rq_ref, rk_ref, rv_ref, s0_ref, o_ref, s_ref):
    lane_lo = lax.broadcasted_iota(jnp.int32, (8, LANES), 1) < DK_RET
    diag = (lax.broadcasted_iota(jnp.int32, (LANES, LANES), 0)
            == lax.broadcasted_iota(jnp.int32, (LANES, LANES), 1))
    for pair in range(N_RET_HEADS // 2):
        q2 = jnp.broadcast_to(rq_ref[0, :, pair * LANES:(pair + 1) * LANES], (8, LANES))
        k_row = rk_ref[0, :, pair * LANES:(pair + 1) * LANES]
        k2 = jnp.broadcast_to(k_row, (8, LANES))
        k_col = jnp.sum(jnp.where(diag, jnp.broadcast_to(k_row, (LANES, LANES)), 0.0), axis=-1, keepdims=True)
        s_pair = jnp.concatenate([s0_ref[0, 2 * pair], s0_ref[0, 2 * pair + 1]], axis=0)
        for odd in range(2):
            h = 2 * pair + odd
            g = math.exp(math.log(1.0 - 2.0 ** (-5.0 - h)))
            keep = lane_lo if odd == 0 else jnp.logical_not(lane_lo)
            qh = jnp.where(keep, q2, 0.0)
            v_row = rv_ref[0, :, h * DV_RET:(h + 1) * DV_RET]
            qk = jnp.sum(qh * k2, axis=-1, keepdims=True)
            o = qk * v_row + _dot(qh, s_pair) * g
            o_ref[0, :, h * DV_RET:(h + 1) * DV_RET] = o[0:1]
            u = (k_col * v_row)[odd * DK_RET:(odd + 1) * DK_RET]
            s_ref[0, h] = s0_ref[0, h] * g + u


def _nsa_step1_kernel(q_ref, kc_ref, vc_ref, win_ref, new_ref, mimp_ref, rq_ref, rk_ref, rv_ref, s0_ref,
                      oc_ref, ow_ref, idx_ref, oret_ref, sret_ref, *, t, n_cmp, n_sel):
    _retention_step_kernel(rq_ref, rk_ref, rv_ref, s0_ref, oret_ref, sret_ref)
    q = q_ref[0]
    nq = q.shape[0]
    ncp = kc_ref.shape[1]
    nsp = mimp_ref.shape[1]

    s = _dot_nt(q, kc_ref[0])
    col = lax.broadcasted_iota(jnp.int32, (nq, ncp), 1)
    mask = (col * CMP_STRIDE + (CMP_BLOCK - 1) <= t) & (col < n_cmp)
    s = jnp.where(mask, s, NEG)
    e = jnp.exp2(s - jnp.max(s, axis=-1, keepdims=True))
    p = jnp.where(mask, e, 0.0) * (1.0 / jnp.sum(e, axis=-1, keepdims=True))
    oc_ref[0] = _dot(p.astype(BF16), vc_ref[0])
    imp_rows = _split3_dot(p, mimp_ref[...])

    blk = lax.broadcasted_iota(jnp.int32, (1, nsp), 1)
    jt = t // SEL_BLOCK
    forced = (blk == 0) | (blk == jt) | (blk == jt - 1)
    scores = []
    for g in range(N_KV):
        imp = jnp.sum(imp_rows[HPG * g:HPG * (g + 1)], axis=0, keepdims=True)
        sc = jnp.where(blk * SEL_BLOCK <= t, jnp.where(forced, FORCE, imp), NEG)
        scores.append(jnp.where(blk < n_sel, sc, BELOW_NEG))
    s_rows = jnp.concatenate(scores + [jnp.full((LANES - N_KV, nsp), BELOW_NEG, F32)], axis=0)
    s_cols = s_rows.T
    a_idx = lax.broadcasted_iota(jnp.int32, (nsp, nsp), 0)
    c_idx = lax.broadcasted_iota(jnp.int32, (nsp, nsp), 1)
    rank_lane = lax.broadcasted_iota(jnp.int32, (nsp, LANES), 1).astype(F32)
    a_val = lax.broadcasted_iota(jnp.int32, (nsp, LANES), 0).astype(F32)
    rows_out = []
    for g in range(N_KV):
        s_col = s_cols[:, g:g + 1]
        s_row = s_rows[g:g + 1, :]
        beats = (s_row > s_col) | ((s_row == s_col) & (c_idx < a_idx))
        rank = jnp.sum(jnp.where(beats, 1.0, 0.0), axis=-1, keepdims=True)
        rows_out.append(jnp.sum(jnp.where(rank == rank_lane, a_val, 0.0), axis=0, keepdims=True))
    idx = jnp.concatenate(rows_out + [jnp.zeros((8 - N_KV, LANES), F32)], axis=0)
    idx_ref[0] = idx.astype(jnp.int32)

    wb = win_ref.shape[-1]
    kw_t = win_ref[0, 0].reshape(LANES, wb).astype(BF16)
    vw_t = win_ref[0, 1].reshape(LANES, wb).astype(BF16)
    k_new = new_ref[0, :, 0:LANES]
    v_new = new_ref[0, :, LANES:2 * LANES]
    s = _dot(q, kw_t)
    kpos = t - wb + lax.broadcasted_iota(jnp.int32, (nq, wb), 1)
    s = jnp.where((kpos > t - WINDOW) & (kpos >= 0), s, NEG)
    s_new = jnp.sum(q.astype(F32) * k_new, axis=-1, keepdims=True)
    m = jnp.maximum(jnp.max(s, axis=-1, keepdims=True), s_new)
    e = jnp.exp2(s - m)
    e_new = jnp.exp2(s_new - m)
    ow_ref[0] = ((_dot_nt(e.astype(BF16), vw_t) + e_new * v_new)
                 * (1.0 / (jnp.sum(e, axis=-1, keepdims=True) + e_new)))


def _nsa_step1(q16, kc, vc, state_win_t, new_win, mimp, rq, rk, rv, s0, *, t, n_cmp, n_sel):
    b, nq, _ = q16.shape
    ncp = kc.shape[1]
    nrot = N_RET_HEADS * DK_RET

    def per_b(shape):
        return pl.BlockSpec((1,) + shape, lambda i: (i,) + (0,) * len(shape))

    return pl.pallas_call(
        functools.partial(_nsa_step1_kernel, t=t, n_cmp=n_cmp, n_sel=n_sel),
        grid=(b,),
        in_specs=[per_b((nq, LANES)), per_b((ncp, LANES)), per_b((ncp, LANES)), per_b(state_win_t.shape[1:]),
                  per_b((1, KV_ROW)), pl.BlockSpec(mimp.shape, lambda i: (0, 0)),
                  per_b((1, nrot)), per_b((1, nrot)), per_b((1, D_RET)), per_b(s0.shape[1:])],
        out_specs=[per_b((nq, LANES)), per_b((nq, LANES)), per_b((8, LANES)), per_b((1, D_RET)),
                   per_b(s0.shape[1:])],
        out_shape=[jax.ShapeDtypeStruct((b, nq, LANES), F32), jax.ShapeDtypeStruct((b, nq, LANES), F32),
                   jax.ShapeDtypeStruct((b, 8, LANES), jnp.int32), jax.ShapeDtypeStruct((b, 1, D_RET), F32),
                   jax.ShapeDtypeStruct(s0.shape, F32)],
        compiler_params=_cparams("arbitrary"),
        name="nsa_step_compressed_window",
    )(q16, kc, vc, state_win_t, new_win, mimp, rq.astype(F32).reshape(b, 1, nrot), rk.astype(F32).reshape(b, 1, nrot),
      rv.astype(F32).reshape(b, 1, D_RET), s0)


def _nsa_step2_kernel(idx_sm, pt_sm, q_ref, *refs, n_cache_blocks, k_eff):
    page_refs = refs[:N_KV * k_eff]
    new_ref, oc_ref, ow_ref, br_ref, o_ref = refs[N_KV * k_eff:]
    b = pl.program_id(0)
    q = q_ref[0]
    nq = q.shape[0]
    bpp = PAGE_SIZE // SEL_BLOCK
    nk = k_eff * PAGE_SIZE
    q32 = q.astype(F32)
    lane_k = lax.broadcasted_iota(jnp.int32, (nq, nk), 1)
    slot = lane_k // PAGE_SIZE
    half = (lane_k % PAGE_SIZE) // SEL_BLOCK
    outs = []
    for g in range(N_KV):
        own = slice(g * HEAD_DIM, (g + 1) * HEAD_DIM)
        q_g = q32[:, own]
        k_new = new_ref[0, :, own]
        v_new = new_ref[0, :, LANES + g * HEAD_DIM:LANES + (g + 1) * HEAD_DIM]
        s_new_all = jnp.sum(q_g * k_new, axis=-1, keepdims=True)
        pages = [page_refs[g * k_eff + k][0] for k in range(k_eff)]
        keys_t = jnp.concatenate([p[0, 0] for p in pages], axis=-1).astype(BF16)
        vals_t = jnp.concatenate([p[1, 0] for p in pages], axis=-1).astype(BF16)
        allowed = jnp.zeros((nq, nk), jnp.int32)
        new_selected = jnp.int32(0)
        for k in range(k_eff):
            j = idx_sm[b, g * k_eff + k]
            in_block = (slot == k) & (half == j % bpp)
            allowed = jnp.where(in_block, (j < n_cache_blocks).astype(jnp.int32), allowed)
            new_selected = new_selected | (j == n_cache_blocks).astype(jnp.int32)
        s = jnp.where(allowed > 0, _dot(q_g.astype(BF16), keys_t), NEG)
        s_new = jnp.where(new_selected > 0, s_new_all, NEG)
        m = jnp.maximum(jnp.max(s, axis=-1, keepdims=True), s_new)
        e = jnp.exp2(s - m)
        e_new = jnp.exp2(s_new - m)
        outs.append((_dot_nt(e.astype(BF16), vals_t) + e_new * v_new)
                    * (1.0 / (jnp.sum(e, axis=-1, keepdims=True) + e_new)))
    row = lax.broadcasted_iota(jnp.int32, (nq, LANES), 0)
    lane = lax.broadcasted_iota(jnp.int32, (nq, LANES), 1)
    o_slc = jnp.concatenate(outs, axis=1)
    br = br_ref[0]
    gates = [jnp.sum(jnp.where(lane == 3 * row + j, br, 0.0), axis=-1, keepdims=True) for j in range(3)]
    o = gates[0] * oc_ref[0] + gates[1] * o_slc + gates[2] * ow_ref[0]
    lane_lo = lax.broadcasted_iota(jnp.int32, (1, LANES), 1) < HEAD_DIM
    for pair in range(N_HEADS // 2):
        even, odd = o[2 * pair:2 * pair + 1], o[2 * pair + 1:2 * pair + 2]
        if 2 * pair < HPG:
            odd = pltpu.roll(odd, HEAD_DIM, axis=1)
        else:
            even = pltpu.roll(even, HEAD_DIM, axis=1)
        o_ref[0, :, pair * LANES:(pair + 1) * LANES] = jnp.where(lane_lo, even, odd)


def _nsa_step2(idx, page_table, q16, cache_t, new_slc, o_cmp, o_win, br, *, n_cache_blocks, k_eff):
    b, nq, _ = q16.shape
    bpp = PAGE_SIZE // SEL_BLOCK

    def page_map(i, idx_sm, pt_sm, c):
        j = jnp.minimum(idx_sm[i, c], n_cache_blocks - 1)
        return (pt_sm[i, j // bpp], 0, c // k_eff, 0, 0)

    def per_b(shape):
        return pl.BlockSpec((1,) + shape, lambda i, idx_sm, pt_sm: (i,) + (0,) * len(shape))

    n_blk = N_KV * k_eff
    in_specs = [per_b((nq, LANES))]
    slab_shape = (1, 2, 1) + cache_t.shape[3:]
    in_specs += [pl.BlockSpec(slab_shape, functools.partial(page_map, c=c)) for c in range(n_blk)]
    in_specs += [per_b((1, KV_ROW)), per_b((nq, LANES)), per_b((nq, LANES)), per_b((1, LANES))]
    return pl.pallas_call(
        functools.partial(_nsa_step2_kernel, n_cache_blocks=n_cache_blocks, k_eff=k_eff),
        grid_spec=pltpu.PrefetchScalarGridSpec(
            num_scalar_prefetch=2,
            grid=(b,),
            in_specs=in_specs,
            out_specs=per_b((1, D_NSA)),
        ),
        out_shape=jax.ShapeDtypeStruct((b, 1, D_NSA), F32),
        compiler_params=_cparams("arbitrary"),
        name="nsa_step_selected",
    )(idx, page_table, q16, *([cache_t] * n_blk), new_slc, o_cmp, o_win, br)


def _out_kernel(*refs, mul_ga):
    if mul_ga:
        x_ref, gate_ref, ya_ref, ga_ref, oret_ref, gr_ref, gret_ref, w_ref, y_ref = refs
        ya = (ya_ref[...] * ga_ref[...]).astype(BF16)
    else:
        x_ref, gate_ref, ya_ref, oret_ref, gr_ref, gret_ref, w_ref, y_ref = refs
        ya = ya_ref[...]
    o = oret_ref[...]
    parts = []
    for h in range(N_RET_HEADS):
        t = o[:, h * DV_RET:(h + 1) * DV_RET]
        ms = jnp.mean(t * t, axis=-1, keepdims=True)
        parts.append(t * lax.rsqrt(ms + EPS) * gret_ref[...])
    yr = (jnp.concatenate(parts, axis=-1) * gr_ref[...]).astype(BF16)
    y_ref[...] = x_ref[...] + gate_ref[0] * _dot(jnp.concatenate([ya, yr], axis=-1), w_ref[...])


def _output(x2, gate, ya, ga, o_ret, gr, g_ret, w_out, *, n_batch, per_row_mod):
    m = x2.shape[0]
    seq = m // n_batch
    tm = min(seq, 1024)
    nt = seq // tm
    if per_row_mod:
        gate_spec = pl.BlockSpec((1, tm, D_MODEL), lambda i: (0, i, 0))
    else:
        gate_spec = pl.BlockSpec((1, 1, D_MODEL), lambda i: (i // nt, 0, 0))

    def rows_spec(width):
        return pl.BlockSpec((tm, width), lambda i: (i, 0))

    mul_ga = ga is not None
    args = [x2, gate, ya] + ([ga] if mul_ga else []) + [o_ret, gr, g_ret.reshape(1, DV_RET), w_out.astype(BF16)]
    in_specs = [rows_spec(D_MODEL), gate_spec, rows_spec(D_NSA)] + ([rows_spec(D_NSA)] if mul_ga else [])
    in_specs += [rows_spec(D_RET), rows_spec(D_RET), pl.BlockSpec((1, DV_RET), lambda i: (0, 0)),
                 pl.BlockSpec((D_MODEL, D_MODEL), lambda i: (0, 0))]
    return pl.pallas_call(
        functools.partial(_out_kernel, mul_ga=mul_ga),
        grid=(m // tm,),
        in_specs=in_specs,
        out_specs=rows_spec(D_MODEL),
        out_shape=jax.ShapeDtypeStruct((m, D_MODEL), F32),
        compiler_params=_cparams("arbitrary"),
        name="output_projection",
    )(*args)


def kernel(x_prompt, x_sample, c_prompt, c_sample, cache_cmp, cache_slc, state_win, state_ret, page_table, g_norm, w_ada, b_ada, w_in, g_q, g_kc, g_ks, g_kw, pe_ck, w_ck1, w_ck2, pe_cv, w_cv1, w_cv2, g_ret, w_out):
    assert g_norm.shape[0] == 1, "one layer"
    bp, seq, _ = x_prompt.shape
    bs, dec_seq, _ = x_sample.shape
    assert dec_seq == 1
    n_pages = page_table.shape[1]
    past = n_pages * PAGE_SIZE
    assert state_win.shape[2] == WINDOW and past >= WINDOW

    w_proj, gq_slots, gks2, gkw2 = _prep_proj_weights(w_in[0], g_q[0], g_ks[0], g_kw[0])
    cmp_weights = _prep_compress_weights(pe_ck[0], w_ck1[0], w_ck2[0], pe_cv[0], w_cv1[0], w_cv2[0], g_kc[0])

    pad = (-(bp + bs)) % 8
    c_all = jnp.concatenate([c_prompt, c_sample, jnp.zeros((pad, D_MODEL), F32)], axis=0)
    mod = _modulation(c_all, w_ada[0], b_ada[0])
    shift_p, scale_p, gate_p = (mod[:bp, k * D_MODEL:(k + 1) * D_MODEL].reshape(bp, 1, D_MODEL) for k in range(3))
    shift_s, scale_s, gate_s = (mod[bp:bp + bs, k * D_MODEL:(k + 1) * D_MODEL].reshape(1, bs, D_MODEL)
                                for k in range(3))

    xp2 = x_prompt.reshape(bp * seq, D_MODEL)
    cos_p, sin_p = _rope_tables(seq, 0, 1)
    (q_p, cmp_t, slc_t, slcb_p, win_t, winb_p, br_p, ga_p, rq_p, rk_p, rv_p, gr_p) = _projection(
        xp2, scale_p, shift_p, g_norm[0], w_proj, gq_slots, gks2, gkw2, cos_p, sin_p, n_batch=bp, per_row_mod=False,
        pos_minor=True)

    kc_p, vc_p = _compress(cmp_t.reshape(bp, 2, N_KV, HEAD_DIM, seq), None, jnp.zeros((bp, 1, KV_ROW), F32),
                           cmp_weights, n_batch=bp, n_pages=seq // PAGE_SIZE)
    ocmp_p, bias_p = _nsa_select(q_p, kc_p, vc_p, n_batch=bp, seq=seq)
    ya_p = _nsa_prompt(q_p, ocmp_p, bias_p, slcb_p, winb_p, br_p, ga_p, n_batch=bp, seq=seq)
    oret_p, sret_p = _retention_prompt(rq_p, rk_p, rv_p, n_batch=bp, seq=seq)
    y_p = _output(xp2, gate_p, ya_p, None, oret_p.reshape(bp * seq, D_RET), gr_p, g_ret[0], w_out[0],
                  n_batch=bp, per_row_mod=False)

    xs2 = x_sample.reshape(bs, D_MODEL)
    cos_s, sin_s = _rope_tables(bs, past, 0)
    (q_s, cmp_s, slc_s, _, win_s, _, br_s, ga_s, rq_s, rk_s, rv_s, gr_s) = _projection(
        xs2, scale_s, shift_s, g_norm[0], w_proj, gq_slots, gks2, gkw2, cos_s, sin_s, n_batch=1, per_row_mod=True,
        pos_minor=False)

    n_chunk_cache = past // CMP_STRIDE
    n_cmp_s = n_chunk_cache
    n_sel_s = past // SEL_BLOCK + 1
    nsp = ((n_sel_s + LANES - 1) // LANES) * LANES
    to_native = (0, 2, 3, 4, 1)
    kc_s, vc_s = _compress(jnp.transpose(cache_cmp[0], to_native), page_table, cmp_s.reshape(bs, 1, KV_ROW),
                           cmp_weights, n_batch=bs, n_pages=n_pages)

    q16 = jnp.pad(q_s[0].transpose(1, 0, 2), ((0, 0), (0, 16 - N_HEADS), (0, 0)))
    mimp_s = _importance_matrix(n_cmp_s, n_cmp_s, n_sel_s, nsp)
    o_cmp_s, o_win_s, idx_s, oret_s, sret_s = _nsa_step1(
        q16, kc_s, vc_s, jnp.transpose(state_win[0], to_native), win_s.reshape(bs, 1, KV_ROW), mimp_s,
        rq_s, rk_s, rv_s, state_ret[0], t=past, n_cmp=n_cmp_s, n_sel=n_sel_s)
    k_eff = min(TOP_N, n_sel_s)
    idx_flat = idx_s[:, :N_KV, :k_eff].reshape(bs, N_KV * k_eff)
    o_nsa_s = _nsa_step2(idx_flat, page_table, q16, jnp.transpose(cache_slc[0], to_native),
                         slc_s.reshape(bs, 1, KV_ROW), o_cmp_s, o_win_s, br_s.reshape(bs, 1, LANES),
                         n_cache_blocks=past // SEL_BLOCK, k_eff=k_eff)

    y_s = _output(xs2, gate_s, o_nsa_s.reshape(bs, D_NSA), ga_s, oret_s.reshape(bs, D_RET), gr_s, g_ret[0], w_out[0],
                  n_batch=1, per_row_mod=True)

    def kv_rows(a, b, l):
        return a.reshape(1, b, l, 2, N_KV, HEAD_DIM)

    def kv_rows_t(a, b, l):
        return jnp.transpose(a.reshape(b, 2, N_KV, HEAD_DIM, l), (0, 4, 1, 2, 3))[None]

    keep_p = min(WINDOW, seq)
    s_win = jnp.concatenate([state_win[0][:, 1:].reshape(bs, WINDOW - 1, KV_ROW), win_s.reshape(bs, 1, KV_ROW)],
                            axis=1)
    return (y_p.reshape(bp, seq, D_MODEL), y_s.reshape(bs, 1, D_MODEL),
            kv_rows_t(cmp_t, bp, seq), kv_rows_t(slc_t, bp, seq),
            kv_rows_t(win_t[:, :, seq - keep_p:], bp, keep_p), sret_p[None],
            kv_rows(cmp_s, bs, 1), kv_rows(slc_s, bs, 1), kv_rows(s_win, bs, WINDOW), sret_s[None])
```

```python
import functools
import math

import jax
import jax.numpy as jnp
import numpy as np
from jax import lax
from jax.experimental import pallas as pl
from jax.experimental.pallas import tpu as pltpu

D_MODEL = 1024
HEAD_DIM = 64
D_NSA = 512
N_HEADS = 8
N_KV = 2
HPG = 4
CMP_BLOCK = 32
CMP_STRIDE = 16
SEL_BLOCK = 64
TOP_N = 16
WINDOW = 512
D_RET = 512
N_RET_HEADS = 4
DV_RET = 128
DK_RET = 64
RET_CHUNK = 128
PAGE_SIZE = 128
ROPE_BASE = 10000.0
EPS = 1e-6
NEG = -1e30
FORCE = 1e4
BELOW_NEG = -2e38
TAKEN = -3e38

KV_ROW = 2 * N_KV * HEAD_DIM
LANES = 128
VMEM_LIMIT_BYTES = 56 * 1024 * 1024

F32 = jnp.float32
BF16 = jnp.bfloat16

O_Q = 0
O_CMP = O_Q + D_NSA
O_SLC = O_CMP + KV_ROW
O_WIN = O_SLC + KV_ROW
O_BR = O_WIN + KV_ROW
O_GA = O_BR + LANES
O_RQ = O_GA + D_NSA
O_RK = O_RQ + N_RET_HEADS * DK_RET
O_RV = O_RK + N_RET_HEADS * DK_RET
O_GR = O_RV + D_RET
D_IN_PAD = O_GR + D_RET


def _cparams(*sem):
    return pltpu.CompilerParams(dimension_semantics=sem, vmem_limit_bytes=VMEM_LIMIT_BYTES)


def _dot(a, b):
    return jnp.dot(a, b, preferred_element_type=F32)


def _dot_nt(a, b):
    return lax.dot_general(a, b, (((1,), (1,)), ((), ())), preferred_element_type=F32)


def _dot_tn(a, b):
    return lax.dot_general(a, b, (((0,), (0,)), ((), ())), preferred_element_type=F32)


def _silu(x):
    return x * jax.nn.sigmoid(x)


def _split3_dot(p, m):
    p1 = p.astype(BF16)
    r1 = p - p1.astype(F32)
    p2 = r1.astype(BF16)
    p3 = (r1 - p2.astype(F32)).astype(BF16)
    return _dot(p1, m) + _dot(p2, m) + _dot(p3, m)


def _pair_rmsnorm(k, gain2):
    lo = lax.broadcasted_iota(jnp.int32, k.shape, 1) < HEAD_DIM
    sq = k * k
    s0 = jnp.sum(jnp.where(lo, sq, 0.0), axis=-1, keepdims=True)
    s1 = jnp.sum(jnp.where(lo, 0.0, sq), axis=-1, keepdims=True)
    ms = jnp.where(lo, s0, s1) * (1.0 / HEAD_DIM)
    return k * lax.rsqrt(ms + EPS) * gain2


def _mod_kernel(c_ref, w_ref, b_ref, o_ref):
    o_ref[...] = _dot(_silu(c_ref[...]).astype(BF16), w_ref[...]) + b_ref[...]


def _modulation(c_all, w_ada, b_ada):
    rows = c_all.shape[0]
    n = w_ada.shape[1]
    tn = 1024
    return pl.pallas_call(
        _mod_kernel,
        grid=(n // tn,),
        in_specs=[pl.BlockSpec((rows, D_MODEL), lambda j: (0, 0)),
                  pl.BlockSpec((D_MODEL, tn), lambda j: (0, j)),
                  pl.BlockSpec((1, tn), lambda j: (0, j))],
        out_specs=pl.BlockSpec((rows, tn), lambda j: (0, j)),
        out_shape=jax.ShapeDtypeStruct((rows, n), F32),
        compiler_params=_cparams("arbitrary"),
        name="adaln_modulation",
    )(c_all, w_ada.astype(BF16), b_ada.reshape(1, n))


def _rope_kernel(freq_ref, cos_ref, sin_ref, *, base, stride, rows):
    shape = (rows, LANES)
    r = lax.broadcasted_iota(jnp.int32, shape, 0)
    pos = (base + stride * (pl.program_id(0) * rows + r)).astype(F32)
    ang = pos * freq_ref[...]
    lane = lax.broadcasted_iota(jnp.int32, shape, 1)
    sign = jnp.where((lane & (DK_RET - 1)) < DK_RET // 2, -1.0, 1.0)
    cos_ref[...] = jnp.cos(ang)
    sin_ref[...] = jnp.sin(ang) * sign


def _rope_tables(n_rows, base, stride):
    half = DK_RET // 2
    freqs = ROPE_BASE ** (-jnp.arange(half, dtype=F32) / half)
    freq_row = jnp.tile(freqs, LANES // half).reshape(1, LANES)
    rows = min(n_rows, 512)
    return pl.pallas_call(
        functools.partial(_rope_kernel, base=base, stride=stride, rows=rows),
        grid=(n_rows // rows,),
        in_specs=[pl.BlockSpec((1, LANES), lambda i: (0, 0))],
        out_specs=[pl.BlockSpec((rows, LANES), lambda i: (i, 0))] * 2,
        out_shape=[jax.ShapeDtypeStruct((n_rows, LANES), F32)] * 2,
        compiler_params=_cparams("arbitrary"),
        name="rope_tables",
    )(freq_row)


def _proj_kernel(x_ref, scale_ref, shift_ref, gn_ref, w_ref, gq_ref, gks_ref, gkw_ref, cos_ref, sin_ref,
                 q_ref, cmp_ref, slc_ref, slcb_ref, win_ref, winb_ref, br_ref, ga_ref, rq_ref, rk_ref,
                 rv_ref, gr_ref, *, pos_minor):
    x = x_ref[...]
    ms = jnp.mean(x * x, axis=-1, keepdims=True)
    xn = x * lax.rsqrt(ms + EPS) * gn_ref[...]
    hb = (xn * (1.0 + scale_ref[0]) + shift_ref[0]).astype(BF16)

    def seg(start, width):
        return _dot(hb, w_ref[:, start:start + width])

    yq = seg(O_Q, D_NSA)
    lane_lo = lax.broadcasted_iota(jnp.int32, (yq.shape[0], LANES), 1) < HEAD_DIM
    for s in range(N_HEADS):
        t = yq[:, (s // 2) * LANES:(s // 2 + 1) * LANES]
        if s % 2 != s // HPG:
            t = pltpu.roll(t, HEAD_DIM, axis=1)
        t = jnp.where(lane_lo == (s < HPG), t, 0.0)
        ms = jnp.sum(t * t, axis=-1, keepdims=True) * (1.0 / HEAD_DIM)
        q_ref[0, s] = (t * lax.rsqrt(ms + EPS) * gq_ref[s]).astype(BF16)

    y_cmp = seg(O_CMP, KV_ROW)
    y = seg(O_SLC, KV_ROW)
    y_slc = jnp.concatenate([_pair_rmsnorm(y[:, :LANES], gks_ref[...]), y[:, LANES:]], axis=-1)
    slcb_ref[...] = y_slc.astype(BF16)
    y = seg(O_WIN, KV_ROW)
    y_win = jnp.concatenate([_pair_rmsnorm(y[:, :LANES], gkw_ref[...]), y[:, LANES:]], axis=-1)
    winb_ref[...] = y_win.astype(BF16)
    for ref, rows in ((cmp_ref, y_cmp), (slc_ref, y_slc), (win_ref, y_win)):
        if pos_minor:
            ref[0] = rows.T
        else:
            ref[...] = rows

    br_ref[...] = jax.nn.sigmoid(seg(O_BR, LANES))
    ga_ref[...] = _silu(seg(O_GA, D_NSA))

    first_half = (lax.broadcasted_iota(jnp.int32, (yq.shape[0], LANES), 1) & (DK_RET - 1)) < DK_RET // 2

    def rotary(y):
        tiles = []
        for k in range(y.shape[1] // LANES):
            t = y[:, k * LANES:(k + 1) * LANES]
            swapped = jnp.where(first_half, pltpu.roll(t, LANES - DK_RET // 2, axis=1),
                                pltpu.roll(t, DK_RET // 2, axis=1))
            tiles.append(t * cos_ref[...] + swapped * sin_ref[...])
        return jnp.concatenate(tiles, axis=-1)

    nrot = N_RET_HEADS * DK_RET
    rq_ref[...] = rotary(seg(O_RQ, nrot)).astype(BF16)
    rk_ref[...] = (rotary(seg(O_RK, nrot)) * DK_RET ** -0.5).astype(BF16)
    rv_ref[...] = seg(O_RV, D_RET).astype(BF16)
    gr_ref[...] = _silu(seg(O_GR, D_RET))


def _prep_proj_weights(w_in, g_q, g_ks, g_kw):
    n_br = 3 * N_HEADS
    w = jnp.concatenate([w_in[:, :O_BR], jnp.pad(w_in[:, O_BR:O_BR + n_br], ((0, 0), (0, LANES - n_br))),
                         w_in[:, O_BR + n_br:]], axis=1).astype(BF16)
    assert w.shape[1] == D_IN_PAD
    gz = jnp.zeros((HEAD_DIM,), F32)
    gq = g_q * (HEAD_DIM ** -0.5 * math.log2(math.e))
    gq_slots = jnp.stack([jnp.concatenate([gq, gz]) if s < HPG else jnp.concatenate([gz, gq])
                          for s in range(N_HEADS)]).reshape(N_HEADS, 1, LANES)
    return w, gq_slots, jnp.tile(g_ks, 2).reshape(1, LANES), jnp.tile(g_kw, 2).reshape(1, LANES)


def _projection(x2, scale, shift, g_norm, w, gq_slots, gks2, gkw2, cos_t, sin_t, *, n_batch, per_row_mod,
                pos_minor):
    m = x2.shape[0]
    seq = m // n_batch
    tm = min(seq, 512)
    nt = seq // tm
    if per_row_mod:
        mod_spec = pl.BlockSpec((1, tm, D_MODEL), lambda i: (0, i, 0))
    else:
        mod_spec = pl.BlockSpec((1, 1, D_MODEL), lambda i: (i // nt, 0, 0))

    def rows_spec(width):
        return pl.BlockSpec((tm, width), lambda i: (i, 0))

    def const_spec(shape):
        return pl.BlockSpec(shape, lambda i: (0,) * len(shape))

    kv_out = "kv rows"
    widths = [(KV_ROW, kv_out), (KV_ROW, kv_out), (KV_ROW, BF16), (KV_ROW, kv_out), (KV_ROW, BF16), (LANES, F32),
              (D_NSA, F32), (N_RET_HEADS * DK_RET, BF16), (N_RET_HEADS * DK_RET, BF16), (D_RET, BF16),
              (D_RET, F32)]
    out_shape = [jax.ShapeDtypeStruct((n_batch, N_HEADS, seq, LANES), BF16)]
    out_specs = [pl.BlockSpec((1, N_HEADS, tm, LANES), lambda i: (i // nt, 0, i % nt, 0))]
    for width, dt in widths:
        if pos_minor and dt is kv_out:
            out_shape.append(jax.ShapeDtypeStruct((n_batch, KV_ROW, seq), F32))
            out_specs.append(pl.BlockSpec((1, KV_ROW, tm), lambda i: (i // nt, 0, i % nt)))
        else:
            out_shape.append(jax.ShapeDtypeStruct((m, width), F32 if dt is kv_out else dt))
            out_specs.append(rows_spec(width))
    return pl.pallas_call(
        functools.partial(_proj_kernel, pos_minor=pos_minor),
        grid=(m // tm,),
        in_specs=[rows_spec(D_MODEL), mod_spec, mod_spec, const_spec((1, D_MODEL)),
                  const_spec((D_MODEL, D_IN_PAD)), const_spec((N_HEADS, 1, LANES)), const_spec((1, LANES)),
                  const_spec((1, LANES)), pl.BlockSpec((tm, LANES), lambda i: (i % nt, 0)),
                  pl.BlockSpec((tm, LANES), lambda i: (i % nt, 0))],
        out_specs=out_specs,
        out_shape=out_shape,
        compiler_params=_cparams("arbitrary"),
        name="input_projection",
    )(x2, scale, shift, g_norm.reshape(1, D_MODEL), w, gq_slots, gks2, gkw2, cos_t, sin_t)


def _compress_kernel(pt_ref, *refs, n_in):
    x_refs, next_ref = refs[:n_in], refs[n_in]
    new_ref, perm_ref, w1_ref, pe_ref, w2_ref, gkc_ref, kc_ref, vc_ref = refs[n_in + 1:]
    cpp = PAGE_SIZE // CMP_STRIDE
    n = n_in * cpp
    last_step = pl.program_id(1) == pl.num_programs(1) - 1
    perm = perm_ref[...]

    def chunk_rows(page_ref):
        return _dot_nt(perm, page_ref[0].reshape(KV_ROW, PAGE_SIZE).astype(BF16))

    def flat_chunks(moved, kv):
        return jnp.concatenate(
            [jnp.concatenate([r[cpp * l:cpp * (l + 1), kv * LANES:(kv + 1) * LANES] for l in range(CMP_STRIDE)],
                             axis=1) for r in moved], axis=0).astype(BF16)

    moved = [chunk_rows(r) for r in x_refs]
    moved_next = [chunk_rows(next_ref)]
    new8 = jnp.broadcast_to(new_ref[0], (8, KV_ROW)).astype(BF16)
    zeros = jnp.zeros((8, (CMP_STRIDE - 1) * LANES), BF16)
    row = lax.broadcasted_iota(jnp.int32, (n, LANES), 0)
    pre = []
    for kv in range(2):
        new_chunk = jnp.concatenate([new8[:, kv * LANES:(kv + 1) * LANES], zeros], axis=1)
        a_all = _dot(jnp.concatenate([flat_chunks(moved, kv), flat_chunks(moved_next, kv), pe_ref[kv], new_chunk],
                                     axis=0), w1_ref[kv])
        a = a_all[:n]
        a_next = a_all[n:n + 1, LANES:]
        bias = a_all[n + cpp:n + cpp + 1, :LANES] + a_all[n + cpp + 1:n + cpp + 2, LANES:]
        a_new = a_all[n + cpp + 8:n + cpp + 9, LANES:]
        hi = pltpu.roll(a[:, LANES:], n - 1, axis=0)
        hi = jnp.where(row == n - 1, jnp.where(last_step, a_new, a_next), hi)
        pre.append(a[:, :LANES] + hi + bias)
    out = _dot(_silu(jnp.concatenate(pre, axis=1)).astype(BF16), w2_ref[...])
    kc_ref[0] = _pair_rmsnorm(out[:, :LANES], gkc_ref[...]).astype(BF16)
    vc_ref[0] = out[:, LANES:].astype(BF16)


def _compress(rows_t, page_table, new_rows, weights, *, n_batch, n_pages):
    w1kv, w2x, pe8kv, gkc2 = weights
    b = n_batch
    cpp = PAGE_SIZE // CMP_STRIDE
    p = math.gcd(n_pages, 128)
    row = jnp.arange(PAGE_SIZE, dtype=jnp.int32)
    perm = ((row % cpp) * CMP_STRIDE + row // cpp).reshape(PAGE_SIZE, 1) == row.reshape(1, PAGE_SIZE)
    page_shape = (1,) + rows_t.shape[1:4] + (PAGE_SIZE,)

    def page_spec(page_of_step):
        if page_table is None:
            return pl.BlockSpec(page_shape, lambda i, j, pt: (i, 0, 0, 0, page_of_step(j)))
        return pl.BlockSpec(page_shape, lambda i, j, pt: (pt[i, page_of_step(j)], 0, 0, 0, 0))

    in_specs = [page_spec(lambda j, k=k: j * p + k) for k in range(p)]
    in_specs.append(page_spec(lambda j: jnp.minimum((j + 1) * p, n_pages - 1)))

    def const_spec(shape):
        return pl.BlockSpec(shape, lambda i, j, pt: (0,) * len(shape))

    in_specs += [pl.BlockSpec((1, 1, KV_ROW), lambda i, j, pt: (i, 0, 0)), const_spec((PAGE_SIZE, PAGE_SIZE)),
                 const_spec(w1kv.shape), const_spec(pe8kv.shape), const_spec(w2x.shape), const_spec((1, LANES))]
    if page_table is None:
        page_table = jnp.zeros((1, 1), jnp.int32)
    return pl.pallas_call(
        functools.partial(_compress_kernel, n_in=p),
        grid_spec=pltpu.PrefetchScalarGridSpec(
            num_scalar_prefetch=1,
            grid=(b, n_pages // p),
            in_specs=in_specs,
            out_specs=[pl.BlockSpec((1, p * cpp, LANES), lambda i, j, pt: (i, j, 0))] * 2,
        ),
        out_shape=[jax.ShapeDtypeStruct((b, n_pages * cpp, LANES), BF16)] * 2,
        compiler_params=_cparams("arbitrary", "arbitrary"),
        name="compress",
    )(page_table, *([rows_t] * (p + 1)), new_rows, perm.astype(BF16), w1kv, pe8kv, w2x, gkc2)


def _prep_compress_weights(pe_ck, w_ck1, w_ck2, pe_cv, w_cv1, w_cv2, g_kc):
    eye = jnp.eye(N_KV, dtype=F32)
    w1 = jnp.stack([w_ck1, w_cv1]).reshape(2, 2, CMP_STRIDE, HEAD_DIM, HEAD_DIM)
    w1kv = jnp.einsum('khldf,gp->klgdhpf', w1, eye).reshape(2, CMP_STRIDE * LANES, KV_ROW).astype(BF16)
    w2 = jnp.stack([w_ck2, w_cv2])
    w2x = jnp.einsum('kfd,kq,gp->kgfqpd', w2, eye, eye).reshape(KV_ROW, KV_ROW).astype(BF16)
    pe = jnp.stack([pe_ck, pe_cv]).reshape(2, 2, CMP_STRIDE, 1, HEAD_DIM)
    pe_rows = jnp.broadcast_to(pe, (2, 2, CMP_STRIDE, N_KV, HEAD_DIM)).reshape(2, 2, CMP_STRIDE * LANES)
    pe8kv = jnp.zeros((2, 8, CMP_STRIDE * LANES), F32).at[:, :2].set(pe_rows).astype(BF16)
    return w1kv, w2x, pe8kv, jnp.tile(g_kc, 2).reshape(1, LANES)


def _importance_matrix(n_cmp_pad, n_cmp, n_sel, n_sel_pad):
    ratio = SEL_BLOCK // CMP_STRIDE
    span = CMP_BLOCK // CMP_STRIDE
    m = np.zeros((n_cmp_pad, n_sel_pad), np.float32)
    for j in range(n_sel):
        for r in range(ratio):
            for s in range(span):
                n = ratio * j + r - s
                if 0 <= n < n_cmp:
                    m[n, j] += 1.0
    return jnp.asarray(m, BF16)


def _nsa_select_kernel(q_ref, kc_ref, vc_ref, mimp_ref, oc_ref, bias_ref, *, tq, n_cmp, n_sel):
    qs = pl.program_id(1) * tq
    rows = HPG * tq
    ncp = kc_ref.shape[1]

    def q_pos(width):
        return qs + lax.broadcasted_iota(jnp.int32, (tq, width), 0)

    def per_head(x):
        return jnp.concatenate([x] * HPG, axis=0)

    col = lax.broadcasted_iota(jnp.int32, (tq, ncp), 1)
    cmp_mask = per_head(jnp.where((col * CMP_STRIDE + (CMP_BLOCK - 1) <= q_pos(ncp)) & (col < n_cmp), 0.0, NEG))
    cmp_any = per_head(jnp.where(q_pos(1) >= CMP_BLOCK - 1, 1.0, 0.0))
    blk = lax.broadcasted_iota(jnp.int32, (tq, LANES), 1)
    t = q_pos(LANES)
    jt = t // SEL_BLOCK
    forced = (blk == 0) | (blk == jt) | (blk == jt - 1)
    valid = blk * SEL_BLOCK <= t
    n_rank = ((n_sel + 7) // 8) * 8
    j_row = lax.broadcasted_iota(jnp.int32, (n_rank, tq), 0)

    for g in range(N_KV):
        q = q_ref[0, HPG * g:HPG * (g + 1)].reshape(rows, LANES)
        s = _dot_nt(q, kc_ref[0]) + cmp_mask
        e = jnp.exp2(s - jnp.max(s, axis=-1, keepdims=True))
        p = e * (cmp_any / jnp.sum(e, axis=-1, keepdims=True))
        oc_ref[0, HPG * g:HPG * (g + 1)] = _dot(p.astype(BF16), vc_ref[0]).astype(BF16).reshape(HPG, tq, LANES)
        p_sum = p[0:tq] + p[tq:2 * tq] + p[2 * tq:3 * tq] + p[3 * tq:4 * tq]
        imp = _split3_dot(p_sum, mimp_ref[...])

        score = jnp.where(valid, jnp.where(forced, FORCE, imp), NEG)
        score = jnp.where(blk < n_sel, score, BELOW_NEG)
        left = score.T[:n_rank]
        sel_t = jnp.zeros((n_rank, tq), F32)
        for _ in range(min(TOP_N, n_sel)):
            top = jnp.max(left, axis=0, keepdims=True)
            first = jnp.min(jnp.where(left == top, j_row, n_rank), axis=0, keepdims=True)
            taken = j_row == first
            sel_t = jnp.where(taken, 1.0, sel_t)
            left = jnp.where(taken, TAKEN, left)
        if n_rank < LANES:
            sel_t = jnp.concatenate([sel_t, jnp.zeros((LANES - n_rank, tq), F32)], axis=0)
        bias_ref[0, g] = jnp.where(sel_t.T > 0.5, 0.0, NEG).astype(BF16)


def _nsa_select(q_slots, kc, vc, *, n_batch, seq):
    tq = min(512, seq)
    n_cmp = seq // CMP_STRIDE - (CMP_BLOCK // CMP_STRIDE) + 1
    n_sel = seq // SEL_BLOCK
    assert n_sel <= LANES
    ncp = kc.shape[1]
    mimp = _importance_matrix(ncp, n_cmp, n_sel, LANES)
    return pl.pallas_call(
        functools.partial(_nsa_select_kernel, tq=tq, n_cmp=n_cmp, n_sel=n_sel),
        grid=(n_batch, seq // tq),
        in_specs=[pl.BlockSpec((1, N_HEADS, tq, LANES), lambda b, i: (b, 0, i, 0)),
                  pl.BlockSpec((1, ncp, LANES), lambda b, i: (b, 0, 0)),
                  pl.BlockSpec((1, ncp, LANES), lambda b, i: (b, 0, 0)),
                  pl.BlockSpec(mimp.shape, lambda b, i: (0, 0))],
        out_specs=[pl.BlockSpec((1, N_HEADS, tq, LANES), lambda b, i: (b, 0, i, 0)),
                   pl.BlockSpec((1, N_KV, tq, LANES), lambda b, i: (b, 0, i, 0))],
        out_shape=[jax.ShapeDtypeStruct((n_batch, N_HEADS, seq, LANES), BF16),
                   jax.ShapeDtypeStruct((n_batch, N_KV, seq, LANES), BF16)],
        compiler_params=_cparams("arbitrary", "arbitrary"),
        name="nsa_select",
    )(q_slots, kc, vc, mimp)


def _nsa_prompt_kernel(q_ref, oc_ref, bias_ref, slc_ref, win_ref, br_ref, ga_ref, expt_ref, o_ref,
                       m_sc, acc_sc, *, tq, tk):
    qs = pl.program_id(1) * tq
    rows = HPG * tq
    wlen = WINDOW + tq
    br = br_ref[...]
    lane_lo = lax.broadcasted_iota(jnp.int32, (tq, LANES), 1) < HEAD_DIM

    def q_pos(width):
        return qs + lax.broadcasted_iota(jnp.int32, (tq, width), 0)

    def per_head(x):
        return jnp.concatenate([x] * HPG, axis=0)

    w0 = pl.multiple_of(jnp.maximum(qs - WINDOW, 0), tq)
    kpos = w0 + lax.broadcasted_iota(jnp.int32, (tq, wlen), 1)
    win_mask = per_head(jnp.where((kpos <= q_pos(wlen)) & (kpos > q_pos(wlen) - WINDOW), 0.0, NEG))
    kt_diag = qs // tk
    kpos = kt_diag * tk + lax.broadcasted_iota(jnp.int32, (tq, tk), 1)
    diag_mask = per_head(jnp.where(kpos <= q_pos(tk), 0.0, NEG))

    def with_ones(v, g):
        own = (lax.broadcasted_iota(jnp.int32, v.shape, 1) // HEAD_DIM) == g
        return jnp.where(own, v, jnp.ones_like(v))

    def normalised(acc):
        return acc * (1.0 / pltpu.roll(acc, HEAD_DIM, axis=1))

    groups = range(N_KV)
    q = [q_ref[0, HPG * g:HPG * (g + 1)].reshape(rows, LANES) for g in groups]
    q_ext = jnp.concatenate([jnp.concatenate([q[g], per_head(bias_ref[0, g])], axis=1) for g in groups], axis=0)

    m_sc[...] = jnp.full((N_KV * rows, LANES), NEG, F32)
    acc_sc[...] = jnp.zeros((N_KV * rows, LANES), F32)

    def key_tile(kt, causal):
        k0 = pl.multiple_of(kt * tk, tk)
        k_ext = jnp.concatenate([slc_ref[0, pl.ds(k0, tk), 0:LANES], expt_ref[kt]], axis=1)
        v = slc_ref[0, pl.ds(k0, tk), LANES:2 * LANES]
        s_all = _dot_nt(q_ext, k_ext)
        for g in groups:
            rs = slice(g * rows, (g + 1) * rows)
            s = s_all[rs]
            if causal:
                s = s + diag_mask
            m_prev = m_sc[rs]
            m_next = jnp.maximum(m_prev, jnp.max(s, axis=-1, keepdims=True))
            p = jnp.exp2(s - jnp.concatenate([m_next] * (tk // LANES), axis=1))
            acc_sc[rs] = jnp.exp2(m_prev - m_next) * acc_sc[rs] + _dot(p.astype(BF16), with_ones(v, g))
            m_sc[rs] = m_next

    def full_tile(kt, carry):
        key_tile(kt, False)
        return carry

    lax.fori_loop(0, kt_diag, full_tile, 0)
    key_tile(kt_diag, True)

    k = win_ref[0, pl.ds(w0, wlen), 0:LANES]
    v = win_ref[0, pl.ds(w0, wlen), LANES:2 * LANES]
    s_all = _dot_nt(jnp.concatenate(q, axis=0), k)
    for g in groups:
        rs = slice(g * rows, (g + 1) * rows)
        s = s_all[rs] + win_mask
        e = jnp.exp2(s - jnp.max(s, axis=-1, keepdims=True))
        o_win = normalised(_dot(e.astype(BF16), with_ones(v, g)))
        o_slc = normalised(acc_sc[rs])

        heads = []
        for h in range(HPG):
            hs = slice(h * tq, (h + 1) * tq)
            c = 3 * (HPG * g + h)
            heads.append(br[:, c:c + 1] * oc_ref[0, HPG * g + h].astype(F32) + br[:, c + 1:c + 2] * o_slc[hs]
                         + br[:, c + 2:c + 3] * o_win[hs])
        for hh in range(HPG // 2):
            even, odd = heads[2 * hh], heads[2 * hh + 1]
            if g == 0:
                odd = pltpu.roll(odd, HEAD_DIM, axis=1)
            else:
                even = pltpu.roll(even, HEAD_DIM, axis=1)
            c0 = (2 * g + hh) * LANES
            o_ref[:, c0:c0 + LANES] = (jnp.where(lane_lo, even, odd) * ga_ref[:, c0:c0 + LANES]).astype(BF16)


def _nsa_prompt(q_slots, o_cmp, block_bias, slc_b, win_b, br, ga, *, n_batch, seq):
    tq = min(256, seq)
    tk = min(512, seq)
    assert seq % tk == 0 and tk % tq == 0 and seq >= WINDOW + tq
    key_blk = (jnp.arange(seq, dtype=jnp.int32) // SEL_BLOCK).reshape(seq // tk, tk, 1)
    expand = (key_blk == jnp.arange(LANES, dtype=jnp.int32).reshape(1, 1, LANES)).astype(BF16)
    nq = seq // tq
    rows = HPG * tq
    return pl.pallas_call(
        functools.partial(_nsa_prompt_kernel, tq=tq, tk=tk),
        grid=(n_batch, nq),
        in_specs=[pl.BlockSpec((1, N_HEADS, tq, LANES), lambda b, i: (b, 0, i, 0)),
                  pl.BlockSpec((1, N_HEADS, tq, LANES), lambda b, i: (b, 0, i, 0)),
                  pl.BlockSpec((1, N_KV, tq, LANES), lambda b, i: (b, 0, i, 0)),
                  pl.BlockSpec((1, seq, KV_ROW), lambda b, i: (b, 0, 0)),
                  pl.BlockSpec((1, seq, KV_ROW), lambda b, i: (b, 0, 0)),
                  pl.BlockSpec((tq, LANES), lambda b, i: (b * nq + i, 0)),
                  pl.BlockSpec((tq, D_NSA), lambda b, i: (b * nq + i, 0)),
                  pl.BlockSpec(expand.shape, lambda b, i: (0, 0, 0))],
        out_specs=pl.BlockSpec((tq, D_NSA), lambda b, i: (b * nq + i, 0)),
        out_shape=jax.ShapeDtypeStruct((n_batch * seq, D_NSA), BF16),
        scratch_shapes=[pltpu.VMEM((N_KV * rows, LANES), F32)] * 2,
        compiler_params=_cparams("arbitrary", "arbitrary"),
        name="nsa_prompt",
    )(q_slots, o_cmp, block_bias, slc_b.reshape(n_batch, seq, KV_ROW), win_b.reshape(n_batch, seq, KV_ROW), br, ga,
      expand)


def _retention_prompt_kernel(rq_ref, rk_ref, rv_ref, o_ref, s_ref, *, cl):
    @pl.when(pl.program_id(0) == 0)
    def _():
        s_ref[...] = jnp.zeros(s_ref.shape, F32)

    i_pos = lax.broadcasted_iota(jnp.int32, (cl, cl), 0)
    j_pos = lax.broadcasted_iota(jnp.int32, (cl, cl), 1)
    diff = (i_pos - j_pos).astype(F32)
    i_col = lax.broadcasted_iota(jnp.int32, (cl, 1), 0).astype(F32)
    lane_lo = lax.broadcasted_iota(jnp.int32, (cl, LANES), 1) < DK_RET
    for h in range(N_RET_HEADS):
        pair, odd = divmod(h, 2)
        log_g = math.log(1.0 - 2.0 ** (-5.0 - h))
        keep = lane_lo if odd == 0 else jnp.logical_not(lane_lo)
        decay = jnp.where(diff >= 0, jnp.exp(log_g * jnp.maximum(diff, 0.0)), 0.0)
        xi = jnp.exp(log_g * (i_col + 1.0))
        zeta = jnp.exp(log_g * (cl - 1.0 - i_col))
        for b in range(rq_ref.shape[0]):
            q2 = rq_ref[b, :, pair * LANES:(pair + 1) * LANES]
            k2 = rk_ref[b, :, pair * LANES:(pair + 1) * LANES]
            s_pair = jnp.concatenate([s_ref[b, 2 * pair], s_ref[b, 2 * pair + 1]], axis=0)
            qh = jnp.where(keep, q2, jnp.zeros_like(q2))
            v = rv_ref[b, :, h * DV_RET:(h + 1) * DV_RET]
            a = _dot_nt(qh, k2) * decay
            o_ref[b, :, h * DV_RET:(h + 1) * DV_RET] = (_dot(a.astype(BF16), v)
                                                         + _dot(qh, s_pair.astype(BF16)) * xi)
            kz = (k2.astype(F32) * zeta).astype(BF16)
            u = _dot_tn(kz, v)[odd * DK_RET:(odd + 1) * DK_RET]
            s_ref[b, h] = s_ref[b, h] * math.exp(log_g * cl) + u


def _retention_prompt(rq, rk, rv, *, n_batch, seq):
    cl = math.gcd(seq, 2 * RET_CHUNK)
    nrot = N_RET_HEADS * DK_RET
    return pl.pallas_call(
        functools.partial(_retention_prompt_kernel, cl=cl),
        grid=(seq // cl,),
        in_specs=[pl.BlockSpec((n_batch, cl, nrot), lambda i: (0, i, 0)),
                  pl.BlockSpec((n_batch, cl, nrot), lambda i: (0, i, 0)),
                  pl.BlockSpec((n_batch, cl, D_RET), lambda i: (0, i, 0))],
        out_specs=[pl.BlockSpec((n_batch, cl, D_RET), lambda i: (0, i, 0)),
                   pl.BlockSpec((n_batch, N_RET_HEADS, DK_RET, DV_RET), lambda i: (0, 0, 0, 0))],
        out_shape=[jax.ShapeDtypeStruct((n_batch, seq, D_RET), F32),
                   jax.ShapeDtypeStruct((n_batch, N_RET_HEADS, DK_RET, DV_RET), F32)],
        compiler_params=_cparams("arbitrary"),
        name="retention_prompt",
    )(rq.reshape(n_batch, seq, nrot), rk.reshape(n_batch, seq, nrot), rv.reshape(n_batch, seq, D_RET))


def _retention_step_kernel(rq_ref, rk_ref, rv_ref, s0_ref, o_ref, s_ref):
    lane_lo = lax.broadcasted_iota(jnp.int32, (8, LANES), 1) < DK_RET
    diag = (lax.broadcasted_iota(jnp.int32, (LANES, LANES), 0)
            == lax.broadcasted_iota(jnp.int32, (LANES, LANES), 1))
    for pair in range(N_RET_HEADS // 2):
        q2 = jnp.broadcast_to(rq_ref[0, :, pair * LANES:(pair + 1) * LANES], (8, LANES))
        k_row = rk_ref[0, :, pair * LANES:(pair + 1) * LANES]
        k2 = jnp.broadcast_to(k_row, (8, LANES))
        k_col = jnp.sum(jnp.where(diag, jnp.broadcast_to(k_row, (LANES, LANES)), 0.0), axis=-1, keepdims=True)
        s_pair = jnp.concatenate([s0_ref[0, 2 * pair], s0_ref[0, 2 * pair + 1]], axis=0)
        for odd in range(2):
            h = 2 * pair + odd
            g = math.exp(math.log(1.0 - 2.0 ** (-5.0 - h)))
            keep = lane_lo if odd == 0 else jnp.logical_not(lane_lo)
            qh = jnp.where(keep, q2, 0.0)
            v_row = rv_ref[0, :, h * DV_RET:(h + 1) * DV_RET]
            qk = jnp.sum(qh * k2, axis=-1, keepdims=True)
            o = qk * v_row + _dot(qh, s_pair) * g
            o_ref[0, :, h * DV_RET:(h + 1) * DV_RET] = o[0:1]
            u = (k_col * v_row)[odd * DK_RET:(odd + 1) * DK_RET]
            s_ref[0, h] = s0_ref[0, h] * g + u


def _nsa_step1_kernel(q_ref, kc_ref, vc_ref, win_ref, new_ref, mimp_ref, rq_ref, rk_ref, rv_ref, s0_ref,
                      oc_ref, ow_ref, idx_ref, oret_ref, sret_ref, *, t, n_cmp, n_sel):
    _retention_step_kernel(rq_ref, rk_ref, rv_ref, s0_ref, oret_ref, sret_ref)
    q = q_ref[0]
    nq = q.shape[0]
    ncp = kc_ref.shape[1]
    nsp = mimp_ref.shape[1]

    s = _dot_nt(q, kc_ref[0])
    col = lax.broadcasted_iota(jnp.int32, (nq, ncp), 1)
    mask = (col * CMP_STRIDE + (CMP_BLOCK - 1) <= t) & (col < n_cmp)
    s = jnp.where(mask, s, NEG)
    e = jnp.exp2(s - jnp.max(s, axis=-1, keepdims=True))
    p = jnp.where(mask, e, 0.0) * (1.0 / jnp.sum(e, axis=-1, keepdims=True))
    oc_ref[0] = _dot(p.astype(BF16), vc_ref[0])
    imp_rows = _split3_dot(p, mimp_ref[...])

    blk = lax.broadcasted_iota(jnp.int32, (1, nsp), 1)
    jt = t // SEL_BLOCK
    forced = (blk == 0) | (blk == jt) | (blk == jt - 1)
    scores = []
    for g in range(N_KV):
        imp = jnp.sum(imp_rows[HPG * g:HPG * (g + 1)], axis=0, keepdims=True)
        sc = jnp.where(blk * SEL_BLOCK <= t, jnp.where(forced, FORCE, imp), NEG)
        scores.append(jnp.where(blk < n_sel, sc, BELOW_NEG))
    s_rows = jnp.concatenate(scores + [jnp.full((LANES - N_KV, nsp), BELOW_NEG, F32)], axis=0)
    s_cols = s_rows.T
    a_idx = lax.broadcasted_iota(jnp.int32, (nsp, nsp), 0)
    c_idx = lax.broadcasted_iota(jnp.int32, (nsp, nsp), 1)
    rank_lane = lax.broadcasted_iota(jnp.int32, (nsp, LANES), 1).astype(F32)
    a_val = lax.broadcasted_iota(jnp.int32, (nsp, LANES), 0).astype(F32)
    rows_out = []
    for g in range(N_KV):
        s_col = s_cols[:, g:g + 1]
        s_row = s_rows[g:g + 1, :]
        beats = (s_row > s_col) | ((s_row == s_col) & (c_idx < a_idx))
        rank = jnp.sum(jnp.where(beats, 1.0, 0.0), axis=-1, keepdims=True)
        rows_out.append(jnp.sum(jnp.where(rank == rank_lane, a_val, 0.0), axis=0, keepdims=True))
    idx = jnp.concatenate(rows_out + [jnp.zeros((8 - N_KV, LANES), F32)], axis=0)
    idx_ref[0] = idx.astype(jnp.int32)

    wb = win_ref.shape[-1]
    kw_t = win_ref[0, 0].reshape(LANES, wb).astype(BF16)
    vw_t = win_ref[0, 1].reshape(LANES, wb).astype(BF16)
    k_new = new_ref[0, :, 0:LANES]
    v_new = new_ref[0, :, LANES:2 * LANES]
    s = _dot(q, kw_t)
    kpos = t - wb + lax.broadcasted_iota(jnp.int32, (nq, wb), 1)
    s = jnp.where((kpos > t - WINDOW) & (kpos >= 0), s, NEG)
    s_new = jnp.sum(q.astype(F32) * k_new, axis=-1, keepdims=True)
    m = jnp.maximum(jnp.max(s, axis=-1, keepdims=True), s_new)
    e = jnp.exp2(s - m)
    e_new = jnp.exp2(s_new - m)
    ow_ref[0] = ((_dot_nt(e.astype(BF16), vw_t) + e_new * v_new)
                 * (1.0 / (jnp.sum(e, axis=-1, keepdims=True) + e_new)))


def _nsa_step1(q16, kc, vc, state_win_t, new_win, mimp, rq, rk, rv, s0, *, t, n_cmp, n_sel):
    b, nq, _ = q16.shape
    ncp = kc.shape[1]
    nrot = N_RET_HEADS * DK_RET

    def per_b(shape):
        return pl.BlockSpec((1,) + shape, lambda i: (i,) + (0,) * len(shape))

    return pl.pallas_call(
        functools.partial(_nsa_step1_kernel, t=t, n_cmp=n_cmp, n_sel=n_sel),
        grid=(b,),
        in_specs=[per_b((nq, LANES)), per_b((ncp, LANES)), per_b((ncp, LANES)), per_b(state_win_t.shape[1:]),
                  per_b((1, KV_ROW)), pl.BlockSpec(mimp.shape, lambda i: (0, 0)),
                  per_b((1, nrot)), per_b((1, nrot)), per_b((1, D_RET)), per_b(s0.shape[1:])],
        out_specs=[per_b((nq, LANES)), per_b((nq, LANES)), per_b((8, LANES)), per_b((1, D_RET)),
                   per_b(s0.shape[1:])],
        out_shape=[jax.ShapeDtypeStruct((b, nq, LANES), F32), jax.ShapeDtypeStruct((b, nq, LANES), F32),
                   jax.ShapeDtypeStruct((b, 8, LANES), jnp.int32), jax.ShapeDtypeStruct((b, 1, D_RET), F32),
                   jax.ShapeDtypeStruct(s0.shape, F32)],
        compiler_params=_cparams("arbitrary"),
        name="nsa_step_compressed_window",
    )(q16, kc, vc, state_win_t, new_win, mimp, rq.astype(F32).reshape(b, 1, nrot), rk.astype(F32).reshape(b, 1, nrot),
      rv.astype(F32).reshape(b, 1, D_RET), s0)


def _nsa_step2_kernel(idx_sm, pt_sm, q_ref, *refs, n_cache_blocks, k_eff):
    page_refs = refs[:N_KV * k_eff]
    new_ref, oc_ref, ow_ref, br_ref, o_ref = refs[N_KV * k_eff:]
    b = pl.program_id(0)
    q = q_ref[0]
    nq = q.shape[0]
    bpp = PAGE_SIZE // SEL_BLOCK
    nk = k_eff * PAGE_SIZE
    k_new = new_ref[0, :, 0:LANES]
    v_new = new_ref[0, :, LANES:2 * LANES]
    s_new_all = jnp.sum(q.astype(F32) * k_new, axis=-1, keepdims=True)
    lane_k = lax.broadcasted_iota(jnp.int32, (nq, nk), 1)
    slot = lane_k // PAGE_SIZE
    half = (lane_k % PAGE_SIZE) // SEL_BLOCK
    outs = []
    for g in range(N_KV):
        pages = [page_refs[g * k_eff + k][0] for k in range(k_eff)]
        keys_t = jnp.concatenate([p[0].reshape(LANES, PAGE_SIZE) for p in pages], axis=-1).astype(BF16)
        vals_t = jnp.concatenate([p[1].reshape(LANES, PAGE_SIZE) for p in pages], axis=-1).astype(BF16)
        allowed = jnp.zeros((nq, nk), jnp.int32)
        new_selected = jnp.int32(0)
        for k in range(k_eff):
            j = idx_sm[b, g * k_eff + k]
            in_block = (slot == k) & (half == j % bpp)
            allowed = jnp.where(in_block, (j < n_cache_blocks).astype(jnp.int32), allowed)
            new_selected = new_selected | (j == n_cache_blocks).astype(jnp.int32)
        s = jnp.where(allowed > 0, _dot(q, keys_t), NEG)
        s_new = jnp.where(new_selected > 0, s_new_all, NEG)
        m = jnp.maximum(jnp.max(s, axis=-1, keepdims=True), s_new)
        e = jnp.exp2(s - m)
        e_new = jnp.exp2(s_new - m)
        outs.append((_dot_nt(e.astype(BF16), vals_t) + e_new * v_new)
                    * (1.0 / (jnp.sum(e, axis=-1, keepdims=True) + e_new)))
    row = lax.broadcasted_iota(jnp.int32, (nq, LANES), 0)
    lane = lax.broadcasted_iota(jnp.int32, (nq, LANES), 1)
    o_slc = jnp.where(row < HPG, outs[0], outs[1])
    br = br_ref[0]
    gates = [jnp.sum(jnp.where(lane == 3 * row + j, br, 0.0), axis=-1, keepdims=True) for j in range(3)]
    o = gates[0] * oc_ref[0] + gates[1] * o_slc + gates[2] * ow_ref[0]
    lane_lo = lax.broadcasted_iota(jnp.int32, (1, LANES), 1) < HEAD_DIM
    for pair in range(N_HEADS // 2):
        even, odd = o[2 * pair:2 * pair + 1], o[2 * pair + 1:2 * pair + 2]
        if 2 * pair < HPG:
            odd = pltpu.roll(odd, HEAD_DIM, axis=1)
        else:
            even = pltpu.roll(even, HEAD_DIM, axis=1)
        o_ref[0, :, pair * LANES:(pair + 1) * LANES] = jnp.where(lane_lo, even, odd)


def _nsa_step2(idx, page_table, q16, cache_t, new_slc, o_cmp, o_win, br, *, n_cache_blocks, k_eff):
    b, nq, _ = q16.shape
    bpp = PAGE_SIZE // SEL_BLOCK

    def page_map(i, idx_sm, pt_sm, c):
        j = jnp.minimum(idx_sm[i, c], n_cache_blocks - 1)
        return (pt_sm[i, j // bpp], 0, 0, 0, 0)

    def per_b(shape):
        return pl.BlockSpec((1,) + shape, lambda i, idx_sm, pt_sm: (i,) + (0,) * len(shape))

    n_blk = N_KV * k_eff
    in_specs = [per_b((nq, LANES))]
    in_specs += [pl.BlockSpec((1,) + cache_t.shape[1:], functools.partial(page_map, c=c)) for c in range(n_blk)]
    in_specs += [per_b((1, KV_ROW)), per_b((nq, LANES)), per_b((nq, LANES)), per_b((1, LANES))]
    return pl.pallas_call(
        functools.partial(_nsa_step2_kernel, n_cache_blocks=n_cache_blocks, k_eff=k_eff),
        grid_spec=pltpu.PrefetchScalarGridSpec(
            num_scalar_prefetch=2,
            grid=(b,),
            in_specs=in_specs,
            out_specs=per_b((1, D_NSA)),
        ),
        out_shape=jax.ShapeDtypeStruct((b, 1, D_NSA), F32),
        compiler_params=_cparams("arbitrary"),
        name="nsa_step_selected",
    )(idx, page_table, q16, *([cache_t] * n_blk), new_slc, o_cmp, o_win, br)


def _out_kernel(*refs, mul_ga):
    if mul_ga:
        x_ref, gate_ref, ya_ref, ga_ref, oret_ref, gr_ref, gret_ref, w_ref, y_ref = refs
        ya = (ya_ref[...] * ga_ref[...]).astype(BF16)
    else:
        x_ref, gate_ref, ya_ref, oret_ref, gr_ref, gret_ref, w_ref, y_ref = refs
        ya = ya_ref[...]
    o = oret_ref[...]
    parts = []
    for h in range(N_RET_HEADS):
        t = o[:, h * DV_RET:(h + 1) * DV_RET]
        ms = jnp.mean(t * t, axis=-1, keepdims=True)
        parts.append(t * lax.rsqrt(ms + EPS) * gret_ref[...])
    yr = (jnp.concatenate(parts, axis=-1) * gr_ref[...]).astype(BF16)
    y_ref[...] = x_ref[...] + gate_ref[0] * _dot(jnp.concatenate([ya, yr], axis=-1), w_ref[...])


def _output(x2, gate, ya, ga, o_ret, gr, g_ret, w_out, *, n_batch, per_row_mod):
    m = x2.shape[0]
    seq = m // n_batch
    tm = min(seq, 1024)
    nt = seq // tm
    if per_row_mod:
        gate_spec = pl.BlockSpec((1, tm, D_MODEL), lambda i: (0, i, 0))
    else:
        gate_spec = pl.BlockSpec((1, 1, D_MODEL), lambda i: (i // nt, 0, 0))

    def rows_spec(width):
        return pl.BlockSpec((tm, width), lambda i: (i, 0))

    mul_ga = ga is not None
    args = [x2, gate, ya] + ([ga] if mul_ga else []) + [o_ret, gr, g_ret.reshape(1, DV_RET), w_out.astype(BF16)]
    in_specs = [rows_spec(D_MODEL), gate_spec, rows_spec(D_NSA)] + ([rows_spec(D_NSA)] if mul_ga else [])
    in_specs += [rows_spec(D_RET), rows_spec(D_RET), pl.BlockSpec((1, DV_RET), lambda i: (0, 0)),
                 pl.BlockSpec((D_MODEL, D_MODEL), lambda i: (0, 0))]
    return pl.pallas_call(
        functools.partial(_out_kernel, mul_ga=mul_ga),
        grid=(m // tm,),
        in_specs=in_specs,
        out_specs=rows_spec(D_MODEL),
        out_shape=jax.ShapeDtypeStruct((m, D_MODEL), F32),
        compiler_params=_cparams("arbitrary"),
        name="output_projection",
    )(*args)


def kernel(x_prompt, x_sample, c_prompt, c_sample, cache_cmp, cache_slc, state_win, state_ret, page_table, g_norm, w_ada, b_ada, w_in, g_q, g_kc, g_ks, g_kw, pe_ck, w_ck1, w_ck2, pe_cv, w_cv1, w_cv2, g_ret, w_out):
    assert g_norm.shape[0] == 1, "one layer"
    bp, seq, _ = x_prompt.shape
    bs, dec_seq, _ = x_sample.shape
    assert dec_seq == 1
    n_pages = page_table.shape[1]
    past = n_pages * PAGE_SIZE
    assert state_win.shape[2] == WINDOW and past >= WINDOW

    w_proj, gq_slots, gks2, gkw2 = _prep_proj_weights(w_in[0], g_q[0], g_ks[0], g_kw[0])
    cmp_weights = _prep_compress_weights(pe_ck[0], w_ck1[0], w_ck2[0], pe_cv[0], w_cv1[0], w_cv2[0], g_kc[0])

    pad = (-(bp + bs)) % 8
    c_all = jnp.concatenate([c_prompt, c_sample, jnp.zeros((pad, D_MODEL), F32)], axis=0)
    mod = _modulation(c_all, w_ada[0], b_ada[0])
    shift_p, scale_p, gate_p = (mod[:bp, k * D_MODEL:(k + 1) * D_MODEL].reshape(bp, 1, D_MODEL) for k in range(3))
    shift_s, scale_s, gate_s = (mod[bp:bp + bs, k * D_MODEL:(k + 1) * D_MODEL].reshape(1, bs, D_MODEL)
                                for k in range(3))

    xp2 = x_prompt.reshape(bp * seq, D_MODEL)
    cos_p, sin_p = _rope_tables(seq, 0, 1)
    (q_p, cmp_t, slc_t, slcb_p, win_t, winb_p, br_p, ga_p, rq_p, rk_p, rv_p, gr_p) = _projection(
        xp2, scale_p, shift_p, g_norm[0], w_proj, gq_slots, gks2, gkw2, cos_p, sin_p, n_batch=bp, per_row_mod=False,
        pos_minor=True)

    kc_p, vc_p = _compress(cmp_t.reshape(bp, 2, N_KV, HEAD_DIM, seq), None, jnp.zeros((bp, 1, KV_ROW), F32),
                           cmp_weights, n_batch=bp, n_pages=seq // PAGE_SIZE)
    ocmp_p, bias_p = _nsa_select(q_p, kc_p, vc_p, n_batch=bp, seq=seq)
    ya_p = _nsa_prompt(q_p, ocmp_p, bias_p, slcb_p, winb_p, br_p, ga_p, n_batch=bp, seq=seq)
    oret_p, sret_p = _retention_prompt(rq_p, rk_p, rv_p, n_batch=bp, seq=seq)
    y_p = _output(xp2, gate_p, ya_p, None, oret_p.reshape(bp * seq, D_RET), gr_p, g_ret[0], w_out[0],
                  n_batch=bp, per_row_mod=False)

    xs2 = x_sample.reshape(bs, D_MODEL)
    cos_s, sin_s = _rope_tables(bs, past, 0)
    (q_s, cmp_s, slc_s, _, win_s, _, br_s, ga_s, rq_s, rk_s, rv_s, gr_s) = _projection(
        xs2, scale_s, shift_s, g_norm[0], w_proj, gq_slots, gks2, gkw2, cos_s, sin_s, n_batch=1, per_row_mod=True,
        pos_minor=False)

    n_chunk_cache = past // CMP_STRIDE
    n_cmp_s = n_chunk_cache
    n_sel_s = past // SEL_BLOCK + 1
    nsp = ((n_sel_s + LANES - 1) // LANES) * LANES
    to_native = (0, 2, 3, 4, 1)
    kc_s, vc_s = _compress(jnp.transpose(cache_cmp[0], to_native), page_table, cmp_s.reshape(bs, 1, KV_ROW),
                           cmp_weights, n_batch=bs, n_pages=n_pages)

    q16 = jnp.pad(q_s[0].transpose(1, 0, 2), ((0, 0), (0, 16 - N_HEADS), (0, 0)))
    mimp_s = _importance_matrix(n_cmp_s, n_cmp_s, n_sel_s, nsp)
    o_cmp_s, o_win_s, idx_s, oret_s, sret_s = _nsa_step1(
        q16, kc_s, vc_s, jnp.transpose(state_win[0], to_native), win_s.reshape(bs, 1, KV_ROW), mimp_s,
        rq_s, rk_s, rv_s, state_ret[0], t=past, n_cmp=n_cmp_s, n_sel=n_sel_s)
    k_eff = min(TOP_N, n_sel_s)
    idx_flat = idx_s[:, :N_KV, :k_eff].reshape(bs, N_KV * k_eff)
    o_nsa_s = _nsa_step2(idx_flat, page_table, q16, jnp.transpose(cache_slc[0], to_native),
                         slc_s.reshape(bs, 1, KV_ROW), o_cmp_s, o_win_s, br_s.reshape(bs, 1, LANES),
                         n_cache_blocks=past // SEL_BLOCK, k_eff=k_eff)

    y_s = _output(xs2, gate_s, o_nsa_s.reshape(bs, D_NSA), ga_s, oret_s.reshape(bs, D_RET), gr_s, g_ret[0], w_out[0],
                  n_batch=1, per_row_mod=True)

    def kv_rows(a, b, l):
        return a.reshape(1, b, l, 2, N_KV, HEAD_DIM)

    def kv_rows_t(a, b, l):
        return jnp.transpose(a.reshape(b, 2, N_KV, HEAD_DIM, l), (0, 4, 1, 2, 3))[None]

    keep_p = min(WINDOW, seq)
    s_win = jnp.concatenate([state_win[0][:, 1:].reshape(bs, WINDOW - 1, KV_ROW), win_s.reshape(bs, 1, KV_ROW)],
                            axis=1)
    return (y_p.reshape(bp, seq, D_MODEL), y_s.reshape(bs, 1, D_MODEL),
            kv_rows_t(cmp_t, bp, seq), kv_rows_t(slc_t, bp, seq),
            kv_rows_t(win_t[:, :, seq - keep_p:], bp, keep_p), sret_p[None],
            kv_rows(cmp_s, bs, 1), kv_rows(slc_s, bs, 1), kv_rows(s_win, bs, WINDOW), sret_s[None])
```
